```python
import math
import jax, jax.numpy as jnp
from jax import lax
import numpy as np

D_MODEL = 1024
BATCH = 8
SEQ = 2048
DEPTH = 2
DEC_BATCH = 128
DEC_SEQ = 8
PAST_LEN = 16384
PAGE_SIZE = 128

RET_HEADS = 4
RET_DK = 128
RET_DV = 128
RET_CHUNK = 128
ROPE_BASE = 10000.0
HG_HEADS = 4
HG_DK = 128
HG_DV = 128
HG_CHUNK = 16
XA_HEADS = 4
XA_DH = 128
N_MEM = 256
N_BRANCH = 3
N_EXPERTS = 64
N_GROUPS = 8
TOPK_GROUPS = 4
TOP_K = 6
D_EXPERT = 256
D_SHARED = 256
ROUTED_SCALE = 2.5
LN_EPS = 1e-5
DN_ALPHA = (2 * DEPTH) ** 0.25
DN_BETA = (8 * DEPTH) ** -0.25

RET_QKW = RET_HEADS * RET_DK
RET_VW = RET_HEADS * RET_DV
HG_KW = HG_HEADS * HG_DK
HG_VW = HG_HEADS * HG_DV
XA_W = XA_HEADS * XA_DH
IN_SPLITS = (RET_QKW, RET_QKW, RET_VW, RET_VW, HG_KW, HG_KW, HG_VW, HG_VW, XA_W, N_BRANCH * D_MODEL)
N_IN = sum(IN_SPLITS)

kernel_name = 'hybrid_retention_hgrn2_memxattn_moe_deepnorm_step'


def layer_norm(x, g, b):
    xf = x.astype(jnp.float32)
    mu = xf.mean(-1, keepdims=True)
    var = jnp.square(xf - mu).mean(-1, keepdims=True)
    return ((xf - mu) * lax.rsqrt(var + LN_EPS) * g.astype(jnp.float32) + b.astype(jnp.float32)).astype(x.dtype)


def rotary(x, pos0):
    T, d = x.shape[1], x.shape[-1]
    inv = 1.0 / (ROPE_BASE ** (jnp.arange(0, d, 2, dtype=jnp.float32) / d))
    ang = (jnp.arange(T, dtype=jnp.float32) + pos0)[:, None] * inv[None, :]
    cos = jnp.cos(ang)[None, :, None, :]
    sin = jnp.sin(ang)[None, :, None, :]
    x1, x2 = x[..., : d // 2], x[..., d // 2:]
    return jnp.concatenate([x1 * cos - x2 * sin, x1 * sin + x2 * cos], axis=-1)


def retention_chunkwise(q, k, v, state):
    B, T, H, _ = q.shape
    dv = v.shape[-1]
    C = math.gcd(T, RET_CHUNK)
    n = T // C
    gl = jnp.log1p(-jnp.exp2(-5.0 - jnp.arange(H, dtype=jnp.float32)))
    idx = jnp.arange(C, dtype=jnp.float32)
    rel = idx[:, None] - idx[None, :]
    intra = jnp.where(rel >= 0, jnp.exp(gl[:, None, None] * rel), 0.0)
    q_dec = jnp.exp(gl[None, :] * (idx[:, None] + 1.0))
    k_dec = jnp.exp(gl[None, :] * (C - 1.0 - idx[:, None]))
    c_dec = jnp.exp(gl * C)

    def step(s, inp):
        qc, kc, vc = inp
        att = jnp.einsum('bthd,bshd->bhts', qc, kc) * intra[None]
        o = (jnp.einsum('bhts,bshv->bthv', att, vc)
             + jnp.einsum('bthd,bhdv->bthv', qc, s) * q_dec[None, :, :, None])
        s = s * c_dec[None, :, None, None] + jnp.einsum('bshd,bshv->bhdv', kc * k_dec[None, :, :, None], vc)
        return s, o

    def to_chunks(a):
        return a.reshape(B, n, C, H, a.shape[-1]).swapaxes(0, 1)

    s, o = lax.scan(step, state, (to_chunks(q), to_chunks(k), to_chunks(v)))
    return o.swapaxes(0, 1).reshape(B, T, H, dv), s


def hgrn2_chunkwise(q, k, v, logf, state):
    B, T, H, _ = q.shape
    dv = v.shape[-1]
    C = math.gcd(T, HG_CHUNK)
    n = T // C
    causal = jnp.tril(jnp.ones((C, C), dtype=bool))

    def step(s, inp):
        qc, kc, vc, gc = inp
        L = jnp.cumsum(gc, axis=1)
        Lend = L[:, -1]
        qi = qc * jnp.exp(L)
        ki = kc * jnp.exp(-L)
        att = jnp.where(causal[None, None], jnp.einsum('bthd,bshd->bhts', qi, ki), 0.0)
        o = jnp.einsum('bhts,bshv->bthv', att, vc) + jnp.einsum('bthd,bhdv->bthv', qi, s)
        s = (s * jnp.exp(Lend)[..., None]
             + jnp.einsum('bshd,bshv->bhdv', kc * jnp.exp(Lend[:, None] - L), vc))
        return s, o

    def to_chunks(a):
        return a.reshape(B, n, C, H, a.shape[-1]).swapaxes(0, 1)

    s, o = lax.scan(step, state, (to_chunks(q), to_chunks(k), to_chunks(v), to_chunks(logf)))
    return o.swapaxes(0, 1).reshape(B, T, H, dv), s


def token_mixer(x, pos0, ret_state, hg_state, mem_k, mem_v, lb, p):
    f32 = jnp.float32
    B, T, _ = x.shape
    points = np.cumsum(IN_SPLITS)[:-1].tolist()
    proj = x @ p['w_in']
    rq, rk, rv, rg, hq, hf, hi, hgt, xq, gates = jnp.split(proj, points, axis=-1)

    q_r = rotary(rq.reshape(B, T, RET_HEADS, RET_DK).astype(f32), pos0)
    k_r = rotary(rk.reshape(B, T, RET_HEADS, RET_DK).astype(f32), pos0) * RET_DK ** -0.5
    v_r = rv.reshape(B, T, RET_HEADS, RET_DV).astype(f32)
    o_r, ret_new = retention_chunkwise(q_r, k_r, v_r, ret_state.astype(f32))
    mu = o_r.mean(-1, keepdims=True)
    var = jnp.square(o_r - mu).mean(-1, keepdims=True)
    o_r = (o_r - mu) * lax.rsqrt(var + LN_EPS) * p['ret_norm_g'].astype(f32).reshape(RET_HEADS, RET_DV)
    y_r = (o_r.reshape(B, T, RET_VW) * jax.nn.silu(rg.astype(f32))).astype(x.dtype)

    z = hf.reshape(B, T, HG_HEADS, HG_DK).astype(f32)
    lbh = lb.astype(f32).reshape(HG_HEADS, HG_DK)
    logf = jnp.logaddexp(jnp.log(lbh), jnp.log1p(-lbh) + jax.nn.log_sigmoid(z))
    k_h = -jnp.expm1(logf)
    q_h = jax.nn.silu(hq.reshape(B, T, HG_HEADS, HG_DK).astype(f32)) * HG_DK ** -0.5
    v_h = hi.reshape(B, T, HG_HEADS, HG_DV).astype(f32)
    o_h, hg_new = hgrn2_chunkwise(q_h, k_h, v_h, logf, hg_state.astype(f32))
    o_h = o_h * lax.rsqrt(jnp.square(o_h).mean(-1, keepdims=True) + LN_EPS)
    o_h = o_h * p['hgrn_norm_g'].astype(f32).reshape(HG_HEADS, HG_DV)
    y_h = (o_h.reshape(B, T, HG_VW) * jax.nn.silu(hgt.astype(f32))).astype(x.dtype)

    q_x = xq.reshape(B, T, XA_HEADS, XA_DH).astype(f32) * XA_DH ** -0.5
    sc = jnp.einsum('bthd,bmhd->bhtm', q_x, mem_k.astype(f32))
    a = jax.nn.softmax(sc, axis=-1)
    y_x = jnp.einsum('bhtm,bmhd->bthd', a, mem_v.astype(f32)).reshape(B, T, XA_W).astype(x.dtype)

    g = jax.nn.sigmoid(gates.astype(f32)).reshape(B, T, N_BRANCH, D_MODEL)
    u_r = (y_r @ p['w_up_ret']).astype(f32)
    u_h = (y_h @ p['w_up_hgrn']).astype(f32)
    u_x = (y_x @ p['w_up_xattn']).astype(f32)
    m = (g[:, :, 0] * u_r + g[:, :, 1] * u_h + g[:, :, 2] * u_x).astype(x.dtype)
    return m @ p['w_out'], ret_new.astype(ret_state.dtype), hg_new.astype(hg_state.dtype)


def moe_ffn(x, p):
    f32 = jnp.float32
    B, T, D = x.shape
    N = B * T
    xt = x.reshape(N, D)
    s = jax.nn.sigmoid((xt @ p['w_router']).astype(f32))
    sel = s + p['b_router'].astype(f32)
    grp = sel.reshape(N, N_GROUPS, N_EXPERTS // N_GROUPS)
    gscore = lax.top_k(grp, 2)[0].sum(-1)
    _, gidx = lax.top_k(gscore, TOPK_GROUPS)
    gmask = (gidx[..., None] == jnp.arange(N_GROUPS)).any(axis=-2)
    emask = jnp.repeat(gmask, N_EXPERTS // N_GROUPS, axis=-1)
    _, eidx = lax.top_k(jnp.where(emask, sel, -jnp.inf), TOP_K)
    w = jnp.take_along_axis(s, eidx, axis=-1)
    w = w / w.sum(-1, keepdims=True) * ROUTED_SCALE
    flat_e = eidx.reshape(-1)
    order = jnp.argsort(flat_e)
    tok = order // TOP_K
    sizes = jnp.bincount(flat_e, length=N_EXPERTS).astype(jnp.int32)
    xs = xt[tok]
    h = jax.nn.silu(lax.ragged_dot(xs, p['w_e_gate'], sizes)) * lax.ragged_dot(xs, p['w_e_up'], sizes)
    ye = lax.ragged_dot(h, p['w_e_down'], sizes)
    ye = ye * w.reshape(-1)[order][:, None].astype(ye.dtype)
    routed = jnp.zeros_like(xt).at[tok].add(ye)
    shared = (jax.nn.silu(xt @ p['w_s_gate']) * (xt @ p['w_s_up'])) @ p['w_s_down']
    return (routed + shared).reshape(B, T, D)


def decoder_layer(x, pos0, ret_state, hg_state, mem_k, mem_v, lb, p):
    h, ret_new, hg_new = token_mixer(x, pos0, ret_state, hg_state, mem_k, mem_v, lb, p)
    x = layer_norm(DN_ALPHA * x + h, p['ln1_g'], p['ln1_b'])
    x = layer_norm(DN_ALPHA * x + moe_ffn(x, p), p['ln2_g'], p['ln2_b'])
    return x, ret_new, hg_new


def setup_inputs(seed: int = 0) -> dict:
    key = jax.random.key(seed)
    ks = jax.random.split(key, 32)
    f32 = jnp.float32

    def nrm(k, shape, scale):
        return jax.random.normal(k, shape, f32) * scale

    in_scales = (1.0, 1.0, DN_BETA, 1.0, 1.0, 1.0, DN_BETA, 1.0, 1.0, 1.0)
    col_scale = jnp.asarray(np.concatenate([np.full(w, s, np.float32) for w, s in zip(IN_SPLITS, in_scales)]))
    kv_scale = jnp.asarray(np.concatenate([np.full(XA_W, 1.0, np.float32), np.full(XA_W, DN_BETA, np.float32)]))
    di = D_MODEL ** -0.5
    return {
        'x_prompt': nrm(ks[0], (BATCH, SEQ, D_MODEL), 1.0),
        'x_sample': nrm(ks[1], (DEC_BATCH, DEC_SEQ, D_MODEL), 1.0),
        'mem_prompt': nrm(ks[2], (BATCH, N_MEM, D_MODEL), 1.0),
        'state_ret': nrm(ks[3], (DEPTH, DEC_BATCH, RET_HEADS, RET_DK, RET_DV), 0.5),
        'state_hgrn': nrm(ks[4], (DEPTH, DEC_BATCH, HG_HEADS, HG_DK, HG_DV), 0.5),
        'cache_mem_k': nrm(ks[5], (DEPTH, DEC_BATCH, N_MEM, XA_HEADS, XA_DH), 1.0),
        'cache_mem_v': nrm(ks[6], (DEPTH, DEC_BATCH, N_MEM, XA_HEADS, XA_DH), DN_BETA),
        'w_in': nrm(ks[7], (DEPTH, D_MODEL, N_IN), di) * col_scale,
        'w_up_ret': nrm(ks[8], (DEPTH, RET_VW, D_MODEL), RET_VW ** -0.5 * DN_BETA),
        'w_up_hgrn': nrm(ks[9], (DEPTH, HG_VW, D_MODEL), HG_VW ** -0.5 * DN_BETA),
        'w_up_xattn': nrm(ks[10], (DEPTH, XA_W, D_MODEL), XA_W ** -0.5 * DN_BETA),
        'w_out': nrm(ks[11], (DEPTH, D_MODEL, D_MODEL), di * DN_BETA),
        'w_mem_kv': nrm(ks[12], (DEPTH, D_MODEL, 2 * XA_W), di) * kv_scale,
        'ret_norm_g': 1.0 + nrm(ks[13], (DEPTH, RET_VW), 0.02),
        'hgrn_norm_g': 1.0 + nrm(ks[14], (DEPTH, HG_VW), 0.02),
        'lb_logits': nrm(ks[15], (DEPTH, HG_KW), 0.5),
        'ln1_g': 1.0 + nrm(ks[16], (DEPTH, D_MODEL), 0.02),
        'ln1_b': nrm(ks[17], (DEPTH, D_MODEL), 0.02),
        'ln2_g': 1.0 + nrm(ks[18], (DEPTH, D_MODEL), 0.02),
        'ln2_b': nrm(ks[19], (DEPTH, D_MODEL), 0.02),
        'w_router': nrm(ks[20], (DEPTH, D_MODEL, N_EXPERTS), di),
        'b_router': nrm(ks[21], (DEPTH, N_EXPERTS), 0.01),
        'w_e_gate': nrm(ks[22], (DEPTH, N_EXPERTS, D_MODEL, D_EXPERT), di * DN_BETA),
        'w_e_up': nrm(ks[23], (DEPTH, N_EXPERTS, D_MODEL, D_EXPERT), di * DN_BETA),
        'w_e_down': nrm(ks[24], (DEPTH, N_EXPERTS, D_EXPERT, D_MODEL), D_EXPERT ** -0.5 * DN_BETA),
        'w_s_gate': nrm(ks[25], (DEPTH, D_MODEL, D_SHARED), di * DN_BETA),
        'w_s_up': nrm(ks[26], (DEPTH, D_MODEL, D_SHARED), di * DN_BETA),
        'w_s_down': nrm(ks[27], (DEPTH, D_SHARED, D_MODEL), D_SHARED ** -0.5 * DN_BETA),
    }


def reference(x_prompt, x_sample, mem_prompt, state_ret, state_hgrn, cache_mem_k, cache_mem_v,
              w_in, w_up_ret, w_up_hgrn, w_up_xattn, w_out, w_mem_kv, ret_norm_g, hgrn_norm_g,
              lb_logits, ln1_g, ln1_b, ln2_g, ln2_b, w_router, b_router,
              w_e_gate, w_e_up, w_e_down, w_s_gate, w_s_up, w_s_down):
    lb_cum = jnp.cumsum(jax.nn.softmax(lb_logits.astype(jnp.float32), axis=0), axis=0)
    lower_bounds = lb_cum - lb_cum[0:1]
    bp = x_prompt.shape[0]
    nm = mem_prompt.shape[1]
    yp, ys = x_prompt, x_sample
    ret_p, hg_p, mk_p, mv_p, ret_s, hg_s = [], [], [], [], [], []
    for l in range(DEPTH):
        p = {
            'w_in': w_in[l], 'w_up_ret': w_up_ret[l], 'w_up_hgrn': w_up_hgrn[l],
            'w_up_xattn': w_up_xattn[l], 'w_out': w_out[l], 'ret_norm_g': ret_norm_g[l],
            'hgrn_norm_g': hgrn_norm_g[l], 'ln1_g': ln1_g[l], 'ln1_b': ln1_b[l],
            'ln2_g': ln2_g[l], 'ln2_b': ln2_b[l], 'w_router': w_router[l], 'b_router': b_router[l],
            'w_e_gate': w_e_gate[l], 'w_e_up': w_e_up[l], 'w_e_down': w_e_down[l],
            'w_s_gate': w_s_gate[l], 'w_s_up': w_s_up[l], 'w_s_down': w_s_down[l],
        }
        kv = mem_prompt @ w_mem_kv[l]
        mk = kv[..., :XA_W].reshape(bp, nm, XA_HEADS, XA_DH)
        mv = kv[..., XA_W:].reshape(bp, nm, XA_HEADS, XA_DH)
        r0 = jnp.zeros((bp,) + state_ret.shape[2:], state_ret.dtype)
        h0 = jnp.zeros((bp,) + state_hgrn.shape[2:], state_hgrn.dtype)
        yp, rp, hp = decoder_layer(yp, 0, r0, h0, mk, mv, lower_bounds[l], p)
        ys, rs, hs = decoder_layer(ys, PAST_LEN, state_ret[l], state_hgrn[l],
                                   cache_mem_k[l], cache_mem_v[l], lower_bounds[l], p)
        ret_p.append(rp); hg_p.append(hp); mk_p.append(mk); mv_p.append(mv)
        ret_s.append(rs); hg_s.append(hs)
    return (yp, ys, jnp.stack(ret_p), jnp.stack(hg_p), jnp.stack(mk_p), jnp.stack(mv_p),
            jnp.stack(ret_s), jnp.stack(hg_s))
```

```python
import functools

import jax
import jax.numpy as jnp
from jax import lax
from jax.experimental import pallas as pl
from jax.experimental.pallas import tpu as pltpu

F32 = jnp.float32
BF16 = jnp.bfloat16
I32 = jnp.int32
HIGHEST = lax.Precision.HIGHEST

D_MODEL = 1024
DEPTH = 2
PAST_LEN = 16384
HEADS = 4
DH = 128
HW = HEADS * DH
RET_CHUNK = 128
HG_CHUNK = 16
ROPE_BASE = 10000.0
N_EXPERTS = 64
N_GROUPS = 8
GROUP_SIZE = N_EXPERTS // N_GROUPS
TOPK_GROUPS = 4
TOP_K = 6
D_EXPERT = 256
ROUTED_SCALE = 2.5
LN_EPS = 1e-5
DN_ALPHA = (2 * DEPTH) ** 0.25
N_IN = 9 * HW + 3 * D_MODEL
COL_RET_Q, COL_RET_K, COL_RET_V, COL_RET_G = 0, 1, 2, 3
COL_HG_Q, COL_HG_F, COL_HG_I, COL_HG_G = 4, 5, 6, 7
COL_XA_Q = 8
COL_GATES = 9
LANES = 128
SUBLANES = 8
ROW_TILES = D_MODEL // LANES
SAMPLE_BB = 8
EXPERT_TILE = 256
VMEM_LIMIT = 56 * 1024 * 1024


def _params(sem):
    return pltpu.CompilerParams(dimension_semantics=sem, vmem_limit_bytes=VMEM_LIMIT)


def _bdot(a, b):
    return jnp.dot(a.astype(BF16), b.astype(BF16), preferred_element_type=F32)


def _bdot_nt(a, b):
    return lax.dot_general(a.astype(BF16), b.astype(BF16), (((1,), (1,)), ((), ())),
                           preferred_element_type=F32)


def _bdot_tn(a, b):
    return lax.dot_general(a.astype(BF16), b.astype(BF16), (((0,), (0,)), ((), ())),
                           preferred_element_type=F32)


def _silu(x):
    return x * jax.nn.sigmoid(x)


def _pick(n, prefs):
    for p in prefs:
        if n % p == 0:
            return p
    raise ValueError(f"no tile for {n}")


def _mm_body(x_ref, w_ref, o_ref, wb_ref):
    @pl.when(pl.program_id(1) == 0)
    def _():
        wb_ref[...] = w_ref[...].astype(BF16)

    o_ref[...] = jnp.dot(x_ref[...].astype(BF16), wb_ref[...],
                         preferred_element_type=F32).astype(o_ref.dtype)


def _matmul(x, w, layer, tm, tn):
    m, k = x.shape
    n = w.shape[2]
    return pl.pallas_call(
        _mm_body,
        grid=(n // tn, m // tm),
        in_specs=[pl.BlockSpec((tm, k), lambda j, i: (i, 0)),
                  pl.BlockSpec((None, k, tn), lambda j, i: (layer, 0, j))],
        out_specs=pl.BlockSpec((tm, tn), lambda j, i: (i, j)),
        out_shape=jax.ShapeDtypeStruct((m, n), F32),
        scratch_shapes=[pltpu.VMEM((k, tn), BF16)],
        compiler_params=_params(("arbitrary", "arbitrary")),
        name="dense_matmul",
    )(x, w)


def _rotary(x, cos, sin_signed):
    return x * cos + pltpu.roll(x, DH // 2, 1) * sin_signed


def _group_norm_gate(o, gain, gate):
    mu = jnp.mean(o, axis=-1, keepdims=True)
    var = jnp.mean(jnp.square(o - mu), axis=-1, keepdims=True)
    return (o - mu) * lax.rsqrt(var + LN_EPS) * gain * _silu(gate)


def _ret_prompt_body(q_ref, k_ref, v_ref, g_ref, cos_ref, sin_ref, gl_ref, gain_ref,
                     y_ref, st_ref, s_scr):
    c = pl.program_id(1)

    @pl.when(c == 0)
    def _():
        s_scr[...] = jnp.zeros_like(s_scr)

    ch = RET_CHUNK
    cos = cos_ref[...]
    sin = sin_ref[...]
    ri = lax.broadcasted_iota(I32, (ch, ch), 0)
    ci = lax.broadcasted_iota(I32, (ch, ch), 1)
    rel = (ri - ci).astype(F32)
    idx = lax.broadcasted_iota(I32, (ch, DH), 0).astype(F32)
    for h in range(HEADS):
        sl = slice(h * DH, (h + 1) * DH)
        gl = gl_ref[h:h + 1, :]
        qr = _rotary(q_ref[:, sl], cos, sin)
        kr = _rotary(k_ref[:, sl], cos, sin) * (DH ** -0.5)
        v = v_ref[:, sl]
        intra = jnp.where(rel >= 0, jnp.exp(gl * rel), 0.0)
        att = _bdot_nt(qr, kr) * intra
        s = s_scr[h]
        o = _bdot(att, v) + _bdot(qr, s) * jnp.exp(gl * (idx + 1.0))
        s_scr[h] = s * jnp.exp(gl * float(ch)) + _bdot_tn(kr * jnp.exp(gl * (ch - 1.0 - idx)), v)
        y_ref[:, sl] = _group_norm_gate(o, gain_ref[:, sl], g_ref[:, sl]).astype(BF16)

    @pl.when(c == pl.num_programs(1) - 1)
    def _():
        st_ref[0] = s_scr[...]


def _ret_sample_body(q_ref, k_ref, v_ref, g_ref, cos_ref, sin_ref, gl_ref, gain_ref, sin_ref_state,
                     y_ref, st_ref, *, ts):
    rows = SAMPLE_BB * ts
    shift = ts.bit_length() - 1
    cos = cos_ref[...]
    sin = sin_ref[...]
    ri = lax.broadcasted_iota(I32, (rows, rows), 0)
    ci = lax.broadcasted_iota(I32, (rows, rows), 1)
    rel = (ri - ci).astype(F32)
    mask = ((ri >> shift) == (ci >> shift)) & (ri >= ci)
    idx = (lax.broadcasted_iota(I32, (rows, DH), 0) & (ts - 1)).astype(F32)
    for h in range(HEADS):
        sl = slice(h * DH, (h + 1) * DH)
        gl = gl_ref[h:h + 1, :]
        qr = _rotary(q_ref[:, sl], cos, sin)
        kr = _rotary(k_ref[:, sl], cos, sin) * (DH ** -0.5)
        v = v_ref[:, sl]
        intra = jnp.where(mask, jnp.exp(gl[:, :rows] * rel), 0.0)
        o_intra = _bdot(_bdot_nt(qr, kr) * intra, v)
        q_dec = jnp.exp(gl * (idx + 1.0))
        kd = kr * jnp.exp(gl * (ts - 1.0 - idx))
        c_dec = jnp.exp(gl * float(ts))
        outs = []
        for j in range(SAMPLE_BB):
            rs = slice(j * ts, (j + 1) * ts)
            s = sin_ref_state[j, h]
            outs.append(o_intra[rs] + _bdot(qr[rs], s) * q_dec[rs])
            st_ref[j, h] = s * c_dec + _bdot_tn(kd[rs], v[rs])
        o = jnp.concatenate(outs, axis=0)
        y_ref[:, sl] = _group_norm_gate(o, gain_ref[:, sl], g_ref[:, sl]).astype(BF16)


def _proj_spec(rows, col, row_map):
    return pl.BlockSpec((rows, HW), lambda *a: (row_map(*a), col))


def _retention(proj, state, layer, cos_p, sin_p, cos_s, sin_s, gl, gain, b, t, nb, ts):
    n_p = b * t
    nc = t // RET_CHUNK
    prow = lambda bi, c: bi * nc + c
    const2 = lambda *a: (0, 0)
    y_p, st_p = pl.pallas_call(
        _ret_prompt_body,
        grid=(b, nc),
        in_specs=[_proj_spec(RET_CHUNK, COL_RET_Q, prow), _proj_spec(RET_CHUNK, COL_RET_K, prow),
                  _proj_spec(RET_CHUNK, COL_RET_V, prow), _proj_spec(RET_CHUNK, COL_RET_G, prow),
                  pl.BlockSpec((RET_CHUNK, DH), lambda bi, c: (c, 0)),
                  pl.BlockSpec((RET_CHUNK, DH), lambda bi, c: (c, 0)),
                  pl.BlockSpec((HEADS, DH), const2),
                  pl.BlockSpec((None, 1, HW), lambda bi, c: (layer, 0, 0))],
        out_specs=[pl.BlockSpec((RET_CHUNK, HW), lambda bi, c: (prow(bi, c), 0)),
                   pl.BlockSpec((1, HEADS, DH, DH), lambda bi, c: (bi, 0, 0, 0))],
        out_shape=[jax.ShapeDtypeStruct((n_p, HW), BF16),
                   jax.ShapeDtypeStruct((b, HEADS, DH, DH), F32)],
        scratch_shapes=[pltpu.VMEM((HEADS, DH, DH), F32)],
        compiler_params=_params(("arbitrary", "arbitrary")),
        name="retention_prompt",
    )(proj, proj, proj, proj, cos_p, sin_p, gl, gain)

    rows = SAMPLE_BB * ts
    base = n_p // rows
    srow = lambda i: base + i
    y_s, st_s = pl.pallas_call(
        functools.partial(_ret_sample_body, ts=ts),
        grid=(nb // SAMPLE_BB,),
        in_specs=[_proj_spec(rows, COL_RET_Q, srow), _proj_spec(rows, COL_RET_K, srow),
                  _proj_spec(rows, COL_RET_V, srow), _proj_spec(rows, COL_RET_G, srow),
                  pl.BlockSpec((rows, DH), const2), pl.BlockSpec((rows, DH), const2),
                  pl.BlockSpec((HEADS, DH), const2),
                  pl.BlockSpec((None, 1, HW), lambda i: (layer, 0, 0)),
                  pl.BlockSpec((None, SAMPLE_BB, HEADS, DH, DH), lambda i: (layer, i, 0, 0, 0))],
        out_specs=[pl.BlockSpec((rows, HW), lambda i: (i, 0)),
                   pl.BlockSpec((SAMPLE_BB, HEADS, DH, DH), lambda i: (i, 0, 0, 0))],
        out_shape=[jax.ShapeDtypeStruct((nb * ts, HW), BF16),
                   jax.ShapeDtypeStruct((nb, HEADS, DH, DH), F32)],
        compiler_params=_params(("arbitrary",)),
        name="retention_sample",
    )(proj, proj, proj, proj, cos_s, sin_s, gl, gain, state)
    return jnp.concatenate([y_p, y_s], axis=0), st_p, st_s


def _hg_prepare(hq_ref, hf_ref, lbt_ref, rows, chunk):
    shift = chunk.bit_length() - 1
    ri = lax.broadcasted_iota(I32, (rows, rows), 0)
    ci = lax.broadcasted_iota(I32, (rows, rows), 1)
    same = (ri >> shift) == (ci >> shift)
    causal = same & (ci <= ri)
    z = hf_ref[...]
    log_lb = lbt_ref[0:1, :]
    log_1m_lb = lbt_ref[1:2, :]
    one_m_lb = lbt_ref[2:3, :]
    log_sig = jnp.minimum(z, 0.0) - jnp.log1p(jnp.exp(-jnp.abs(z)))
    bterm = log_1m_lb + log_sig
    logf = jnp.maximum(log_lb, bterm) + jnp.log1p(jnp.exp(-jnp.abs(log_lb - bterm)))
    kh = one_m_lb * jax.nn.sigmoid(-z)
    qh = _silu(hq_ref[...]) * (DH ** -0.5)
    cum = jnp.dot(causal.astype(F32), logf, precision=HIGHEST, preferred_element_type=F32)
    tot = jnp.dot(same.astype(F32), logf, precision=HIGHEST, preferred_element_type=F32)
    qi = qh * jnp.exp(cum)
    ki = kh * jnp.exp(-cum)
    ke = kh * jnp.exp(tot - cum)
    return causal, qi, ki, ke, jnp.exp(tot)


def _rms_norm_gate(o, gain, gate):
    return o * lax.rsqrt(jnp.mean(jnp.square(o), axis=-1, keepdims=True) + LN_EPS) * gain * _silu(gate)


def _hg_prompt_body(hq_ref, hf_ref, hi_ref, hg_ref, lbt_ref, gain_ref, y_ref, st_ref, s_scr):
    c = pl.program_id(1)

    @pl.when(c == 0)
    def _():
        s_scr[...] = jnp.zeros_like(s_scr)

    rows = RET_CHUNK
    causal, qi, ki, ke, etot = _hg_prepare(hq_ref, hf_ref, lbt_ref, rows, HG_CHUNK)
    v = hi_ref[...]
    for h in range(HEADS):
        sl = slice(h * DH, (h + 1) * DH)
        att = jnp.where(causal, _bdot_nt(qi[:, sl], ki[:, sl]), 0.0)
        o_intra = _bdot(att, v[:, sl])
        st = s_scr[h]
        outs = []
        for j in range(rows // HG_CHUNK):
            rs = slice(j * HG_CHUNK, (j + 1) * HG_CHUNK)
            outs.append(o_intra[rs] + _bdot_nt(qi[rs, sl], st))
            st = st * etot[j * HG_CHUNK:j * HG_CHUNK + 1, sl] + _bdot_tn(v[rs, sl], ke[rs, sl])
        s_scr[h] = st
        o = jnp.concatenate(outs, axis=0)
        y_ref[:, sl] = _rms_norm_gate(o, gain_ref[:, sl], hg_ref[:, sl]).astype(BF16)

    @pl.when(c == pl.num_programs(1) - 1)
    def _():
        for h in range(HEADS):
            st_ref[0, h] = s_scr[h].T


def _hg_sample_body(hq_ref, hf_ref, hi_ref, hg_ref, lbt_ref, gain_ref, sin_ref_state,
                    y_ref, st_ref, *, ts):
    rows = SAMPLE_BB * ts
    causal, qi, ki, ke, etot = _hg_prepare(hq_ref, hf_ref, lbt_ref, rows, ts)
    v = hi_ref[...]
    for h in range(HEADS):
        sl = slice(h * DH, (h + 1) * DH)
        att = jnp.where(causal, _bdot_nt(qi[:, sl], ki[:, sl]), 0.0)
        o_intra = _bdot(att, v[:, sl])
        outs = []
        for j in range(SAMPLE_BB):
            rs = slice(j * ts, (j + 1) * ts)
            s = sin_ref_state[j, h]
            outs.append(o_intra[rs] + _bdot(qi[rs, sl], s))
            scale = jnp.broadcast_to(etot[j * ts:j * ts + 1, sl], (DH, DH)).T
            st_ref[j, h] = s * scale + _bdot_tn(ke[rs, sl], v[rs, sl])
        o = jnp.concatenate(outs, axis=0)
        y_ref[:, sl] = _rms_norm_gate(o, gain_ref[:, sl], hg_ref[:, sl]).astype(BF16)


def _hgrn(proj, state, layer, lbt, gain, b, t, nb, ts):
    n_p = b * t
    nc = t // RET_CHUNK
    prow = lambda bi, c: bi * nc + c
    y_p, st_p = pl.pallas_call(
        _hg_prompt_body,
        grid=(b, nc),
        in_specs=[_proj_spec(RET_CHUNK, COL_HG_Q, prow), _proj_spec(RET_CHUNK, COL_HG_F, prow),
                  _proj_spec(RET_CHUNK, COL_HG_I, prow), _proj_spec(RET_CHUNK, COL_HG_G, prow),
                  pl.BlockSpec((None, SUBLANES, HW), lambda bi, c: (layer, 0, 0)),
                  pl.BlockSpec((None, 1, HW), lambda bi, c: (layer, 0, 0))],
        out_specs=[pl.BlockSpec((RET_CHUNK, HW), lambda bi, c: (prow(bi, c), 0)),
                   pl.BlockSpec((1, HEADS, DH, DH), lambda bi, c: (bi, 0, 0, 0))],
        out_shape=[jax.ShapeDtypeStruct((n_p, HW), BF16),
                   jax.ShapeDtypeStruct((b, HEADS, DH, DH), F32)],
        scratch_shapes=[pltpu.VMEM((HEADS, DH, DH), F32)],
        compiler_params=_params(("arbitrary", "arbitrary")),
        name="hgrn_prompt",
    )(proj, proj, proj, proj, lbt, gain)

    rows = SAMPLE_BB * ts
    base = n_p // rows
    srow = lambda i: base + i
    y_s, st_s = pl.pallas_call(
        functools.partial(_hg_sample_body, ts=ts),
        grid=(nb // SAMPLE_BB,),
        in_specs=[_proj_spec(rows, COL_HG_Q, srow), _proj_spec(rows, COL_HG_F, srow),
                  _proj_spec(rows, COL_HG_I, srow), _proj_spec(rows, COL_HG_G, srow),
                  pl.BlockSpec((None, SUBLANES, HW), lambda i: (layer, 0, 0)),
                  pl.BlockSpec((None, 1, HW), lambda i: (layer, 0, 0)),
                  pl.BlockSpec((None, SAMPLE_BB, HEADS, DH, DH), lambda i: (layer, i, 0, 0, 0))],
        out_specs=[pl.BlockSpec((rows, HW), lambda i: (i, 0)),
                   pl.BlockSpec((SAMPLE_BB, HEADS, DH, DH), lambda i: (i, 0, 0, 0))],
        out_shape=[jax.ShapeDtypeStruct((nb * ts, HW), BF16),
                   jax.ShapeDtypeStruct((nb, HEADS, DH, DH), F32)],
        compiler_params=_params(("arbitrary",)),
        name="hgrn_sample",
    )(proj, proj, proj, proj, lbt, gain, state)
    return jnp.concatenate([y_p, y_s], axis=0), st_p, st_s


def _softmax_rows(s):
    e = jnp.exp(s - jnp.max(s, axis=-1, keepdims=True))
    return e / jnp.sum(e, axis=-1, keepdims=True)


def _xa_prompt_body(q_ref, k_ref, v_ref, y_ref):
    for h in range(HEADS):
        sl = slice(h * DH, (h + 1) * DH)
        a = _softmax_rows(_bdot_nt(q_ref[:, sl] * (DH ** -0.5), k_ref[:, sl]))
        y_ref[:, sl] = _bdot(a, v_ref[:, sl]).astype(BF16)


def _xa_sample_body(q_ref, k_ref, v_ref, y_ref, *, ts):
    for j in range(SAMPLE_BB):
        rs = slice(j * ts, (j + 1) * ts)
        for h in range(HEADS):
            sl = slice(h * DH, (h + 1) * DH)
            a = _softmax_rows(_bdot_nt(q_ref[rs, sl] * (DH ** -0.5), k_ref[j, :, h, :]))
            y_ref[rs, sl] = _bdot(a, v_ref[j, :, h, :]).astype(BF16)


def _cross_attention(proj, kv_p, cache_k, cache_v, layer, b, t, nb, ts):
    n_p = b * t
    n_mem = kv_p.shape[0] // b
    tq = _pick(t, (512, 256, 128))
    nq = t // tq
    y_p = pl.pallas_call(
        _xa_prompt_body,
        grid=(b, nq),
        in_specs=[_proj_spec(tq, COL_XA_Q, lambda bi, qi: bi * nq + qi),
                  pl.BlockSpec((n_mem, HW), lambda bi, qi: (bi, 0)),
                  pl.BlockSpec((n_mem, HW), lambda bi, qi: (bi, 1))],
        out_specs=pl.BlockSpec((tq, HW), lambda bi, qi: (bi * nq + qi, 0)),
        out_shape=jax.ShapeDtypeStruct((n_p, HW), BF16),
        compiler_params=_params(("arbitrary", "arbitrary")),
        name="xattn_prompt",
    )(proj, kv_p, kv_p)

    rows = SAMPLE_BB * ts
    base = n_p // rows
    kv_spec = pl.BlockSpec((None, SAMPLE_BB, n_mem, HEADS, DH), lambda i: (layer, i, 0, 0, 0))
    y_s = pl.pallas_call(
        functools.partial(_xa_sample_body, ts=ts),
        grid=(nb // SAMPLE_BB,),
        in_specs=[_proj_spec(rows, COL_XA_Q, lambda i: base + i), kv_spec, kv_spec],
        out_specs=pl.BlockSpec((rows, HW), lambda i: (i, 0)),
        out_shape=jax.ShapeDtypeStruct((nb * ts, HW), BF16),
        compiler_params=_params(("arbitrary",)),
        name="xattn_sample",
    )(proj, cache_k, cache_v)
    return jnp.concatenate([y_p, y_s], axis=0)


def _layer_norm(tv, g, b):
    mu = jnp.mean(tv, axis=-1, keepdims=True)
    var = jnp.mean(jnp.square(tv - mu), axis=-1, keepdims=True)
    return (tv - mu) * lax.rsqrt(var + LN_EPS) * g + b


def _merge_body(yr_ref, yh_ref, yx_ref, g0a, g0b, g1a, g1b, g2a, g2b, x_ref,
                wr_ref, wh_ref, wx_ref, wo_ref, lg_ref, lb_ref,
                x1_ref, x1b_ref, x1t_ref, wr_s, wh_s, wx_s, wo_s):
    @pl.when(pl.program_id(0) == 0)
    def _():
        wr_s[...] = wr_ref[...].astype(BF16)
        wh_s[...] = wh_ref[...].astype(BF16)
        wx_s[...] = wx_ref[...].astype(BF16)
        wo_s[...] = wo_ref[...].astype(BF16)

    def branch(y_ref, w_s, ga, gb):
        gate = jax.nn.sigmoid(jnp.concatenate([ga[...], gb[...]], axis=-1))
        return gate * jnp.dot(y_ref[...], w_s[...], preferred_element_type=F32)

    m = branch(yr_ref, wr_s, g0a, g0b) + branch(yh_ref, wh_s, g1a, g1b) + branch(yx_ref, wx_s, g2a, g2b)
    hmix = jnp.dot(m.astype(BF16), wo_s[...], preferred_element_type=F32)
    x1 = _layer_norm(DN_ALPHA * x_ref[...] + hmix, lg_ref[...], lb_ref[...])
    x1_ref[...] = x1
    x1b_ref[...] = x1.astype(BF16)
    for s in range(ROW_TILES):
        x1t_ref[:, s, :] = x1[:, s * LANES:(s + 1) * LANES]


def _merge(yr, yh, yx, proj, x, w_up_ret, w_up_hgrn, w_up_xattn, w_out, ln_g, ln_b, layer):
    nt = x.shape[0]
    tm = _pick(nt, (256, 128))
    row = lambda i: (i, 0)
    wspec = lambda k: pl.BlockSpec((None, k, D_MODEL), lambda i: (layer, 0, 0))
    vec = pl.BlockSpec((None, 1, D_MODEL), lambda i: (layer, 0, 0))
    gate_specs = [pl.BlockSpec((tm, HW), lambda i, c=c: (i, COL_GATES + c)) for c in range(6)]
    return pl.pallas_call(
        _merge_body,
        grid=(nt // tm,),
        in_specs=[pl.BlockSpec((tm, HW), row)] * 3 + gate_specs + [pl.BlockSpec((tm, D_MODEL), row),
                  wspec(HW), wspec(HW), wspec(HW), wspec(D_MODEL), vec, vec],
        out_specs=[pl.BlockSpec((tm, D_MODEL), row), pl.BlockSpec((tm, D_MODEL), row),
                   pl.BlockSpec((tm, ROW_TILES, LANES), lambda i: (i, 0, 0))],
        out_shape=[jax.ShapeDtypeStruct((nt, D_MODEL), F32),
                   jax.ShapeDtypeStruct((nt, D_MODEL), BF16),
                   jax.ShapeDtypeStruct((nt, ROW_TILES, LANES), F32)],
        scratch_shapes=[pltpu.VMEM((HW, D_MODEL), BF16)] * 3 + [pltpu.VMEM((D_MODEL, D_MODEL), BF16)],
        compiler_params=_params(("arbitrary",)),
        name="merge_out_ln1",
    )(yr, yh, yx, *([proj] * 6), x, w_up_ret, w_up_hgrn, w_up_xattn, w_out, ln_g, ln_b)


def _router_body(x_ref, wt_ref, b_ref, eidx_ref, wn_ref):
    tm = x_ref.shape[0]
    logits = lax.dot_general(wt_ref[...], x_ref[...], (((1,), (1,)), ((), ())),
                             precision=HIGHEST, preferred_element_type=F32)
    s = jax.nn.sigmoid(logits)
    sel = s + b_ref[...]
    neg = -jnp.inf
    groups = [sel[g * GROUP_SIZE:(g + 1) * GROUP_SIZE, :] for g in range(N_GROUPS)]
    ie = lax.broadcasted_iota(I32, (GROUP_SIZE, tm), 0).astype(F32)
    rows = []
    for blk in groups:
        m1 = jnp.max(blk, axis=0, keepdims=True)
        first = jnp.min(jnp.where(blk == m1, ie, float(GROUP_SIZE)), axis=0, keepdims=True)
        rows.append(m1 + jnp.max(jnp.where(ie == first, neg, blk), axis=0, keepdims=True))
    gscore = jnp.concatenate(rows, axis=0)
    ig = lax.broadcasted_iota(I32, gscore.shape, 0).astype(F32)
    gmask = jnp.zeros(gscore.shape, F32)
    for _ in range(TOPK_GROUPS):
        m = jnp.max(gscore, axis=0, keepdims=True)
        gi = jnp.min(jnp.where(gscore == m, ig, float(N_GROUPS)), axis=0, keepdims=True)
        hit = ig == gi
        gmask = jnp.where(hit, 1.0, gmask)
        gscore = jnp.where(hit, neg, gscore)
    masked = jnp.concatenate([jnp.where(gmask[g:g + 1, :] > 0.5, blk, neg)
                              for g, blk in enumerate(groups)], axis=0)
    ix = lax.broadcasted_iota(I32, masked.shape, 0).astype(F32)
    idxs, ws = [], []
    for _ in range(TOP_K):
        m = jnp.max(masked, axis=0, keepdims=True)
        ei = jnp.min(jnp.where(masked == m, ix, float(N_EXPERTS)), axis=0, keepdims=True)
        hit = ix == ei
        idxs.append(ei)
        ws.append(jnp.sum(jnp.where(hit, s, 0.0), axis=0, keepdims=True))
        masked = jnp.where(hit, neg, masked)
    wsum = ws[0]
    for w in ws[1:]:
        wsum = wsum + w
    pad = [jnp.zeros((1, tm), F32)] * (SUBLANES - TOP_K)
    eidx_ref[...] = jnp.concatenate(idxs + pad, axis=0).astype(I32)
    wn_ref[...] = jnp.concatenate([w / wsum * ROUTED_SCALE for w in ws] + pad, axis=0)


def _router(x1, w_router_t, b_router, layer):
    nt = x1.shape[0]
    tm = _pick(nt, (512, 256, 128))
    return pl.pallas_call(
        _router_body,
        grid=(nt // tm,),
        in_specs=[pl.BlockSpec((tm, D_MODEL), lambda i: (i, 0)),
                  pl.BlockSpec((None, N_EXPERTS, D_MODEL), lambda i: (layer, 0, 0)),
                  pl.BlockSpec((None, N_EXPERTS, 1), lambda i: (layer, 0, 0))],
        out_specs=[pl.BlockSpec((SUBLANES, tm), lambda i: (0, i))] * 2,
        out_shape=[jax.ShapeDtypeStruct((SUBLANES, nt), I32),
                   jax.ShapeDtypeStruct((SUBLANES, nt), F32)],
        compiler_params=_params(("arbitrary",)),
        name="moe_router",
    )(x1, w_router_t, b_router)


def _positions_body(eidx_ref, pos_ref, cnt_ref, off_ref, base_scr, off_scr):
    phase = pl.program_id(0)
    i = pl.program_id(1)
    tp = eidx_ref.shape[1]
    ix = lax.broadcasted_iota(I32, (N_EXPERTS, tp), 0)
    eidx = eidx_ref[...]
    member = jnp.zeros((N_EXPERTS, tp), F32)
    for k in range(TOP_K):
        member = member + (ix == eidx[k:k + 1, :]).astype(F32)
    tile_cnt = jnp.sum(member, axis=1, keepdims=True)

    @pl.when((phase == 0) & (i == 0))
    def _():
        base_scr[...] = jnp.zeros_like(base_scr)

    @pl.when((phase == 1) & (i == 0))
    def _():
        cnt = base_scr[...]
        er = lax.broadcasted_iota(I32, (N_EXPERTS, N_EXPERTS), 0)
        ec = lax.broadcasted_iota(I32, (N_EXPERTS, N_EXPERTS), 1)
        off = jnp.dot((ec < er).astype(F32), cnt, precision=HIGHEST, preferred_element_type=F32)
        off_scr[...] = off
        cnt_ref[...] = cnt
        off_ref[...] = off
        base_scr[...] = jnp.zeros_like(base_scr)

    @pl.when(phase == 1)
    def _():
        tr = lax.broadcasted_iota(I32, (tp, tp), 0)
        tc = lax.broadcasted_iota(I32, (tp, tp), 1)
        before = jnp.dot(member.astype(BF16), (tr < tc).astype(BF16), preferred_element_type=F32)
        where_to = before + (off_scr[...] + base_scr[...])[:, 0:1]
        rows = [jnp.sum(jnp.where(ix == eidx[k:k + 1, :], where_to, 0.0), axis=0, keepdims=True)
                for k in range(TOP_K)]
        rows += [jnp.zeros((1, tp), F32)] * (SUBLANES - TOP_K)
        pos_ref[...] = jnp.concatenate(rows, axis=0).astype(I32)

    base_scr[...] = base_scr[...] + tile_cnt


def _positions(eidx):
    nt = eidx.shape[1]
    tp = _pick(nt, (512, 256, 128))
    const = lambda p, i: (0, 0)
    return pl.pallas_call(
        _positions_body,
        grid=(2, nt // tp),
        in_specs=[pl.BlockSpec((SUBLANES, tp), lambda p, i: (0, i))],
        out_specs=[pl.BlockSpec((SUBLANES, tp), lambda p, i: (0, i * p)),
                   pl.BlockSpec((N_EXPERTS, LANES), const), pl.BlockSpec((N_EXPERTS, LANES), const)],
        out_shape=[jax.ShapeDtypeStruct((SUBLANES, nt), I32),
                   jax.ShapeDtypeStruct((N_EXPERTS, LANES), F32),
                   jax.ShapeDtypeStruct((N_EXPERTS, LANES), F32)],
        scratch_shapes=[pltpu.VMEM((N_EXPERTS, LANES), F32), pltpu.VMEM((N_EXPERTS, LANES), F32)],
        compiler_params=_params(("arbitrary", "arbitrary")),
        name="moe_positions",
    )(eidx)


def _step_table(cnt_f, off_f, n_rows):
    te = EXPERT_TILE
    n_steps = n_rows // te + N_EXPERTS
    cnt = cnt_f[:, 0].astype(I32)
    off = off_f[:, 0].astype(I32)
    first = off // te
    last = (off + cnt - 1) // te
    nst = jnp.where(cnt > 0, last - first + 1, 0)
    s_end = jnp.cumsum(nst)
    s_beg = s_end - nst
    total = s_end[-1]
    s = jnp.minimum(jnp.arange(n_steps, dtype=I32), total - 1)
    e = jnp.sum((s_end[None, :] <= s[:, None]).astype(I32), axis=1)
    tile = first[e] + s - s_beg[e]
    valid = jnp.arange(n_steps, dtype=I32) < total
    lo = jnp.where(valid, jnp.maximum(off[e], tile * te), 0)
    hi = jnp.where(valid, jnp.minimum(off[e] + cnt[e], (tile + 1) * te), 0)
    fresh = jnp.concatenate([jnp.ones((1,), I32), (tile[1:] != tile[:-1]).astype(I32)])
    return tile, e, lo, hi, fresh


def _dispatch_body(pos_ref, xt_ref, xs_ref, pos_s, sem_p, sem):
    i = pl.program_id(0)
    td = pos_ref.shape[1]
    cp = pltpu.make_async_copy(pos_ref, pos_s, sem_p)
    cp.start()
    cp.wait()

    def issue(r, carry):
        for k in range(TOP_K):
            pltpu.make_async_copy(xt_ref.at[i * td + r], xs_ref.at[pos_s[k, r]], sem).start()
        return carry

    lax.fori_loop(0, td, issue, 0)
    for k in range(TOP_K):
        pltpu.make_async_copy(xt_ref.at[pl.ds(0, td)], xs_ref.at[pl.ds(0, td)], sem).wait()


def _dispatch(pos, x1t):
    nt = x1t.shape[0]
    td = _pick(nt, (512, 256, 128))
    return pl.pallas_call(
        _dispatch_body,
        grid=(nt // td,),
        in_specs=[pl.BlockSpec((SUBLANES, td), lambda i: (0, i)),
                  pl.BlockSpec(memory_space=pl.ANY)],
        out_specs=pl.BlockSpec(memory_space=pl.ANY),
        out_shape=jax.ShapeDtypeStruct((nt * TOP_K, ROW_TILES, LANES), F32),
        scratch_shapes=[pltpu.SMEM((SUBLANES, td), I32), pltpu.SemaphoreType.DMA, pltpu.SemaphoreType.DMA],
        compiler_params=_params(("arbitrary",)),
        name="moe_dispatch",
    )(pos, x1t)


def _experts_body(tile_ref, exp_ref, lo_ref, hi_ref, fresh_ref, xs_ref, wg_ref, wu_ref, wd_ref, ye_ref):
    s = pl.program_id(0)
    te = EXPERT_TILE
    lo = lo_ref[s]
    hi = hi_ref[s]

    @pl.when(fresh_ref[s] == 1)
    def _():
        ye_ref[...] = jnp.zeros_like(ye_ref)

    @pl.when(hi > lo)
    def _():
        x = jnp.concatenate([xs_ref[:, t, :] for t in range(ROW_TILES)], axis=-1).astype(BF16)
        g = jnp.dot(x, wg_ref[...].astype(BF16), preferred_element_type=F32)
        u = jnp.dot(x, wu_ref[...].astype(BF16), preferred_element_type=F32)
        y = jnp.dot((_silu(g) * u).astype(BF16), wd_ref[...].astype(BF16), preferred_element_type=F32)
        row = tile_ref[s] * te + lax.broadcasted_iota(I32, (te, LANES), 0)
        mine = (row >= lo) & (row < hi)
        for t in range(ROW_TILES):
            ye_ref[:, t, :] = jnp.where(mine, y[:, t * LANES:(t + 1) * LANES], ye_ref[:, t, :])


def _experts(table, xs, w_gate, w_up, w_down, layer):
    n_rows = xs.shape[0]
    te = EXPERT_TILE
    n_steps = table[0].shape[0]
    tile_map = lambda s, tile, e, lo, hi, fr: (tile[s], 0, 0)
    w_in_spec = pl.BlockSpec((None, None, D_MODEL, D_EXPERT), lambda s, tile, e, lo, hi, fr: (layer, e[s], 0, 0))
    w_dn_spec = pl.BlockSpec((None, None, D_EXPERT, D_MODEL), lambda s, tile, e, lo, hi, fr: (layer, e[s], 0, 0))
    return pl.pallas_call(
        _experts_body,
        grid_spec=pltpu.PrefetchScalarGridSpec(
            num_scalar_prefetch=5,
            grid=(n_steps,),
            in_specs=[pl.BlockSpec((te, ROW_TILES, LANES), tile_map), w_in_spec, w_in_spec, w_dn_spec],
            out_specs=pl.BlockSpec((te, ROW_TILES, LANES), tile_map)),
        out_shape=jax.ShapeDtypeStruct((n_rows, ROW_TILES, LANES), F32),
        compiler_params=_params(("arbitrary",)),
        name="moe_experts",
    )(*table, xs, w_gate, w_up, w_down)


def _combine_body(pos_ref, wn_ref, ye_ref, x1_ref, x1b_ref, wsg_ref, wsu_ref, wsd_ref, lg_ref, lb_ref,
                  x2_ref, x2b_ref, pos_s, wn_s, buf, acc, wsg_s, wsu_s, wsd_s, sem_p, sem_w, sem):
    tc = pos_ref.shape[1]

    @pl.when(pl.program_id(0) == 0)
    def _():
        wsg_s[...] = wsg_ref[...].astype(BF16)
        wsu_s[...] = wsu_ref[...].astype(BF16)
        wsd_s[...] = wsd_ref[...].astype(BF16)

    cp = pltpu.make_async_copy(pos_ref, pos_s, sem_p)
    cw = pltpu.make_async_copy(wn_ref, wn_s, sem_w)
    cp.start()
    cw.start()
    cp.wait()
    cw.wait()

    def issue(r, carry):
        for k in range(TOP_K):
            pltpu.make_async_copy(ye_ref.at[pos_s[k, r]], buf.at[k, r], sem).start()
        return carry

    lax.fori_loop(0, tc, issue, 0)
    xb = x1b_ref[...]
    hs = _silu(jnp.dot(xb, wsg_s[...], preferred_element_type=F32)) * jnp.dot(xb, wsu_s[...], preferred_element_type=F32)
    shared = jnp.dot(hs.astype(BF16), wsd_s[...], preferred_element_type=F32)
    for k in range(TOP_K):
        pltpu.make_async_copy(ye_ref.at[pl.ds(0, tc)], buf.at[k], sem).wait()

    def reduce(r, carry):
        tot = buf[0, r] * wn_s[0, r]
        for k in range(1, TOP_K):
            tot = tot + buf[k, r] * wn_s[k, r]
        acc[r] = tot
        return carry

    lax.fori_loop(0, tc, reduce, 0)
    routed = jnp.concatenate([acc[:, t, :] for t in range(ROW_TILES)], axis=-1)
    x2 = _layer_norm(DN_ALPHA * x1_ref[...] + (routed + shared), lg_ref[...], lb_ref[...])
    x2_ref[...] = x2
    x2b_ref[...] = x2.astype(BF16)


def _combine(pos, wn, ye, x1, x1b, w_s_gate, w_s_up, w_s_down, ln_g, ln_b, layer):
    nt = x1.shape[0]
    tc = _pick(nt, (256, 128))
    d_sh = w_s_gate.shape[2]
    row = lambda i: (i, 0)
    vec = pl.BlockSpec((None, 1, D_MODEL), lambda i: (layer, 0, 0))
    return pl.pallas_call(
        _combine_body,
        grid=(nt // tc,),
        in_specs=[pl.BlockSpec((SUBLANES, tc), lambda i: (0, i)), pl.BlockSpec((SUBLANES, tc), lambda i: (0, i)),
                  pl.BlockSpec(memory_space=pl.ANY),
                  pl.BlockSpec((tc, D_MODEL), row), pl.BlockSpec((tc, D_MODEL), row),
                  pl.BlockSpec((None, D_MODEL, d_sh), lambda i: (layer, 0, 0)),
                  pl.BlockSpec((None, D_MODEL, d_sh), lambda i: (layer, 0, 0)),
                  pl.BlockSpec((None, d_sh, D_MODEL), lambda i: (layer, 0, 0)), vec, vec],
        out_specs=[pl.BlockSpec((tc, D_MODEL), row), pl.BlockSpec((tc, D_MODEL), row)],
        out_shape=[jax.ShapeDtypeStruct((nt, D_MODEL), F32), jax.ShapeDtypeStruct((nt, D_MODEL), BF16)],
        scratch_shapes=[pltpu.SMEM((SUBLANES, tc), I32), pltpu.SMEM((SUBLANES, tc), F32),
                        pltpu.VMEM((TOP_K, tc, ROW_TILES, LANES), F32), pltpu.VMEM((tc, ROW_TILES, LANES), F32),
                        pltpu.VMEM((D_MODEL, d_sh), BF16), pltpu.VMEM((D_MODEL, d_sh), BF16),
                        pltpu.VMEM((d_sh, D_MODEL), BF16),
                        pltpu.SemaphoreType.DMA, pltpu.SemaphoreType.DMA, pltpu.SemaphoreType.DMA],
        compiler_params=_params(("arbitrary",)),
        name="moe_combine_ln2",
    )(pos, wn, ye, x1, x1b, w_s_gate, w_s_up, w_s_down, ln_g, ln_b)


def _rope_tables(t, pos0):
    inv = 1.0 / (ROPE_BASE ** (jnp.arange(0, DH, 2, dtype=F32) / DH))
    ang = (jnp.arange(t, dtype=F32) + pos0)[:, None] * inv[None, :]
    cos, sin = jnp.cos(ang), jnp.sin(ang)
    return jnp.concatenate([cos, cos], axis=-1), jnp.concatenate([-sin, sin], axis=-1)


def kernel(x_prompt, x_sample, mem_prompt, state_ret, state_hgrn, cache_mem_k, cache_mem_v, w_in, w_up_ret, w_up_hgrn, w_up_xattn, w_out, w_mem_kv, ret_norm_g, hgrn_norm_g, lb_logits, ln1_g, ln1_b, ln2_g, ln2_b, w_router, b_router, w_e_gate, w_e_up, w_e_down, w_s_gate, w_s_up, w_s_down):
    b, t, d = x_prompt.shape
    nb, ts, _ = x_sample.shape
    n_mem = mem_prompt.shape[1]
    assert d == D_MODEL and t % RET_CHUNK == 0 and nb % SAMPLE_BB == 0
    assert ts & (ts - 1) == 0 and HG_CHUNK % ts == 0 and RET_CHUNK % ts == 0
    n_p, n_s = b * t, nb * ts
    nt = n_p + n_s
    assert n_p % (SAMPLE_BB * ts) == 0 and (nt * TOP_K) % EXPERT_TILE == 0

    lb_cum = jnp.cumsum(jax.nn.softmax(lb_logits.astype(F32), axis=0), axis=0)
    lbs = lb_cum - lb_cum[0:1]
    lbt = jnp.stack([jnp.log(lbs), jnp.log1p(-lbs), 1.0 - lbs] + [jnp.zeros_like(lbs)] * (SUBLANES - 3), axis=1)
    gl = jnp.broadcast_to(jnp.log1p(-jnp.exp2(-5.0 - jnp.arange(HEADS, dtype=F32)))[:, None], (HEADS, DH))
    cos_p, sin_p = _rope_tables(t, 0)
    cos_s, sin_s = _rope_tables(ts, PAST_LEN)
    cos_s, sin_s = jnp.tile(cos_s, (SAMPLE_BB, 1)), jnp.tile(sin_s, (SAMPLE_BB, 1))
    vec3 = lambda a: a.reshape(DEPTH, 1, -1)
    w_router_t = jnp.swapaxes(w_router, 1, 2)
    b_router3 = b_router.reshape(DEPTH, N_EXPERTS, 1)
    mem2 = mem_prompt.reshape(b * n_mem, d)

    x = jnp.concatenate([x_prompt.reshape(n_p, d), x_sample.reshape(n_s, d)], axis=0)
    xb = x.astype(BF16)
    tm_proj = _pick(nt, (1024, 512, 128))
    outs = {k: [] for k in ("ret_p", "hg_p", "mk", "mv", "ret_s", "hg_s")}
    for l in range(DEPTH):
        proj = _matmul(xb, w_in, l, tm_proj, 1280)
        kv_p = _matmul(mem2, w_mem_kv, l, _pick(b * n_mem, (1024, 512, 256)), 2 * HW)
        yr, ret_p, ret_s = _retention(proj, state_ret, l, cos_p, sin_p, cos_s, sin_s, gl,
                                      vec3(ret_norm_g), b, t, nb, ts)
        yh, hg_p, hg_s = _hgrn(proj, state_hgrn, l, lbt, vec3(hgrn_norm_g), b, t, nb, ts)
        yx = _cross_attention(proj, kv_p, cache_mem_k, cache_mem_v, l, b, t, nb, ts)
        x1, x1b, x1t = _merge(yr, yh, yx, proj, x, w_up_ret, w_up_hgrn, w_up_xattn, w_out,
                              vec3(ln1_g), vec3(ln1_b), l)
        eidx, wn = _router(x1, w_router_t, b_router3, l)
        pos, cnt, off = _positions(eidx)
        table = _step_table(cnt, off, nt * TOP_K)
        xs = _dispatch(pos, x1t)
        ye = _experts(table, xs, w_e_gate, w_e_up, w_e_down, l)
        x, xb = _combine(pos, wn, ye, x1, x1b, w_s_gate, w_s_up, w_s_down, vec3(ln2_g), vec3(ln2_b), l)
        outs["ret_p"].append(ret_p)
        outs["hg_p"].append(hg_p)
        outs["mk"].append(kv_p[:, :HW].reshape(b, n_mem, HEADS, DH))
        outs["mv"].append(kv_p[:, HW:].reshape(b, n_mem, HEADS, DH))
        outs["ret_s"].append(ret_s)
        outs["hg_s"].append(hg_s)
    return (x[:n_p].reshape(b, t, d), x[n_p:].reshape(nb, ts, d),
            jnp.stack(outs["ret_p"]), jnp.stack(outs["hg_p"]), jnp.stack(outs["mk"]), jnp.stack(outs["mv"]),
            jnp.stack(outs["ret_s"]), jnp.stack(outs["hg_s"]))
```

```python
import functools

import jax
import jax.numpy as jnp
from jax import lax
from jax.experimental import pallas as pl
from jax.experimental.pallas import tpu as pltpu

F32 = jnp.float32
BF16 = jnp.bfloat16
I32 = jnp.int32
HIGHEST = lax.Precision.HIGHEST

D_MODEL = 1024
DEPTH = 2
PAST_LEN = 16384
HEADS = 4
DH = 128
HW = HEADS * DH
RET_CHUNK = 128
HG_CHUNK = 16
ROPE_BASE = 10000.0
N_EXPERTS = 64
N_GROUPS = 8
GROUP_SIZE = N_EXPERTS // N_GROUPS
TOPK_GROUPS = 4
TOP_K = 6
D_EXPERT = 256
ROUTED_SCALE = 2.5
LN_EPS = 1e-5
DN_ALPHA = (2 * DEPTH) ** 0.25
N_IN = 9 * HW + 3 * D_MODEL
COL_RET_Q, COL_RET_K, COL_RET_V, COL_RET_G = 0, 1, 2, 3
COL_HG_Q, COL_HG_F, COL_HG_I, COL_HG_G = 4, 5, 6, 7
COL_XA_Q = 8
COL_GATES = 9
LANES = 128
SUBLANES = 8
ROW_TILES = D_MODEL // LANES
SAMPLE_BB = 8
EXPERT_TILE = 256
VMEM_LIMIT = 56 * 1024 * 1024


def _params(sem):
    return pltpu.CompilerParams(dimension_semantics=sem, vmem_limit_bytes=VMEM_LIMIT)


def _bdot(a, b):
    return jnp.dot(a.astype(BF16), b.astype(BF16), preferred_element_type=F32)


def _bdot_nt(a, b):
    return lax.dot_general(a.astype(BF16), b.astype(BF16), (((1,), (1,)), ((), ())),
                           preferred_element_type=F32)


def _bdot_tn(a, b):
    return lax.dot_general(a.astype(BF16), b.astype(BF16), (((0,), (0,)), ((), ())),
                           preferred_element_type=F32)


def _silu(x):
    return x * jax.nn.sigmoid(x)


def _pick(n, prefs):
    for p in prefs:
        if n % p == 0:
            return p
    raise ValueError(f"no tile for {n}")


def _mm_body(x_ref, w_ref, o_ref, wb_ref):
    @pl.when(pl.program_id(1) == 0)
    def _():
        wb_ref[...] = w_ref[...].astype(BF16)

    o_ref[...] = jnp.dot(x_ref[...].astype(BF16), wb_ref[...],
                         preferred_element_type=F32).astype(o_ref.dtype)


def _matmul(x, w, layer, tm, tn):
    m, k = x.shape
    n = w.shape[2]
    return pl.pallas_call(
        _mm_body,
        grid=(n // tn, m // tm),
        in_specs=[pl.BlockSpec((tm, k), lambda j, i: (i, 0)),
                  pl.BlockSpec((None, k, tn), lambda j, i: (layer, 0, j))],
        out_specs=pl.BlockSpec((tm, tn), lambda j, i: (i, j)),
        out_shape=jax.ShapeDtypeStruct((m, n), F32),
        scratch_shapes=[pltpu.VMEM((k, tn), BF16)],
        compiler_params=_params(("arbitrary", "arbitrary")),
        name="dense_matmul",
    )(x, w)


def _rotary(x, cos, sin_signed):
    return x * cos + pltpu.roll(x, DH // 2, 1) * sin_signed


def _group_norm_gate(o, gain, gate):
    mu = jnp.mean(o, axis=-1, keepdims=True)
    var = jnp.mean(jnp.square(o - mu), axis=-1, keepdims=True)
    return (o - mu) * lax.rsqrt(var + LN_EPS) * gain * _silu(gate)


def _ret_prompt_body(q_ref, k_ref, v_ref, g_ref, cos_ref, sin_ref, gl_ref, gain_ref,
                     y_ref, st_ref, s_scr):
    c = pl.program_id(1)

    @pl.when(c == 0)
    def _():
        s_scr[...] = jnp.zeros_like(s_scr)

    ch = RET_CHUNK
    cos = cos_ref[...]
    sin = sin_ref[...]
    ri = lax.broadcasted_iota(I32, (ch, ch), 0)
    ci = lax.broadcasted_iota(I32, (ch, ch), 1)
    rel = (ri - ci).astype(F32)
    idx = lax.broadcasted_iota(I32, (ch, DH), 0).astype(F32)
    for h in range(HEADS):
        sl = slice(h * DH, (h + 1) * DH)
        gl = gl_ref[h:h + 1, :]
        qr = _rotary(q_ref[:, sl], cos, sin)
        kr = _rotary(k_ref[:, sl], cos, sin) * (DH ** -0.5)
        v = v_ref[:, sl]
        intra = jnp.where(rel >= 0, jnp.exp(gl * rel), 0.0)
        att = _bdot_nt(qr, kr) * intra
        s = s_scr[h]
        o = _bdot(att, v) + _bdot(qr, s) * jnp.exp(gl * (idx + 1.0))
        s_scr[h] = s * jnp.exp(gl * float(ch)) + _bdot_tn(kr * jnp.exp(gl * (ch - 1.0 - idx)), v)
        y_ref[:, sl] = _group_norm_gate(o, gain_ref[:, sl], g_ref[:, sl]).astype(BF16)

    @pl.when(c == pl.num_programs(1) - 1)
    def _():
        st_ref[0] = s_scr[...]


def _ret_sample_body(q_ref, k_ref, v_ref, g_ref, cos_ref, sin_ref, gl_ref, gain_ref, sin_ref_state,
                     y_ref, st_ref, *, ts):
    rows = SAMPLE_BB * ts
    shift = ts.bit_length() - 1
    cos = cos_ref[...]
    sin = sin_ref[...]
    ri = lax.broadcasted_iota(I32, (rows, rows), 0)
    ci = lax.broadcasted_iota(I32, (rows, rows), 1)
    rel = (ri - ci).astype(F32)
    mask = ((ri >> shift) == (ci >> shift)) & (ri >= ci)
    idx = (lax.broadcasted_iota(I32, (rows, DH), 0) & (ts - 1)).astype(F32)
    for h in range(HEADS):
        sl = slice(h * DH, (h + 1) * DH)
        gl = gl_ref[h:h + 1, :]
        qr = _rotary(q_ref[:, sl], cos, sin)
        kr = _rotary(k_ref[:, sl], cos, sin) * (DH ** -0.5)
        v = v_ref[:, sl]
        intra = jnp.where(mask, jnp.exp(gl[:, :rows] * rel), 0.0)
        o_intra = _bdot(_bdot_nt(qr, kr) * intra, v)
        q_dec = jnp.exp(gl * (idx + 1.0))
        kd = kr * jnp.exp(gl * (ts - 1.0 - idx))
        c_dec = jnp.exp(gl * float(ts))
        outs = []
        for j in range(SAMPLE_BB):
            rs = slice(j * ts, (j + 1) * ts)
            s = sin_ref_state[j, h]
            outs.append(o_intra[rs] + _bdot(qr[rs], s) * q_dec[rs])
            st_ref[j, h] = s * c_dec + _bdot_tn(kd[rs], v[rs])
        o = jnp.concatenate(outs, axis=0)
        y_ref[:, sl] = _group_norm_gate(o, gain_ref[:, sl], g_ref[:, sl]).astype(BF16)


def _proj_spec(rows, col, row_map):
    return pl.BlockSpec((rows, HW), lambda *a: (row_map(*a), col))


def _retention(proj, state, layer, cos_p, sin_p, cos_s, sin_s, gl, gain, b, t, nb, ts):
    n_p = b * t
    nc = t // RET_CHUNK
    prow = lambda bi, c: bi * nc + c
    const2 = lambda *a: (0, 0)
    y_p, st_p = pl.pallas_call(
        _ret_prompt_body,
        grid=(b, nc),
        in_specs=[_proj_spec(RET_CHUNK, COL_RET_Q, prow), _proj_spec(RET_CHUNK, COL_RET_K, prow),
                  _proj_spec(RET_CHUNK, COL_RET_V, prow), _proj_spec(RET_CHUNK, COL_RET_G, prow),
                  pl.BlockSpec((RET_CHUNK, DH), lambda bi, c: (c, 0)),
                  pl.BlockSpec((RET_CHUNK, DH), lambda bi, c: (c, 0)),
                  pl.BlockSpec((HEADS, DH), const2),
                  pl.BlockSpec((None, 1, HW), lambda bi, c: (layer, 0, 0))],
        out_specs=[pl.BlockSpec((RET_CHUNK, HW), lambda bi, c: (prow(bi, c), 0)),
                   pl.BlockSpec((1, HEADS, DH, DH), lambda bi, c: (bi, 0, 0, 0))],
        out_shape=[jax.ShapeDtypeStruct((n_p, HW), BF16),
                   jax.ShapeDtypeStruct((b, HEADS, DH, DH), F32)],
        scratch_shapes=[pltpu.VMEM((HEADS, DH, DH), F32)],
        compiler_params=_params(("arbitrary", "arbitrary")),
        name="retention_prompt",
    )(proj, proj, proj, proj, cos_p, sin_p, gl, gain)

    rows = SAMPLE_BB * ts
    base = n_p // rows
    srow = lambda i: base + i
    y_s, st_s = pl.pallas_call(
        functools.partial(_ret_sample_body, ts=ts),
        grid=(nb // SAMPLE_BB,),
        in_specs=[_proj_spec(rows, COL_RET_Q, srow), _proj_spec(rows, COL_RET_K, srow),
                  _proj_spec(rows, COL_RET_V, srow), _proj_spec(rows, COL_RET_G, srow),
                  pl.BlockSpec((rows, DH), const2), pl.BlockSpec((rows, DH), const2),
                  pl.BlockSpec((HEADS, DH), const2),
                  pl.BlockSpec((None, 1, HW), lambda i: (layer, 0, 0)),
                  pl.BlockSpec((None, SAMPLE_BB, HEADS, DH, DH), lambda i: (layer, i, 0, 0, 0))],
        out_specs=[pl.BlockSpec((rows, HW), lambda i: (i, 0)),
                   pl.BlockSpec((SAMPLE_BB, HEADS, DH, DH), lambda i: (i, 0, 0, 0))],
        out_shape=[jax.ShapeDtypeStruct((nb * ts, HW), BF16),
                   jax.ShapeDtypeStruct((nb, HEADS, DH, DH), F32)],
        compiler_params=_params(("arbitrary",)),
        name="retention_sample",
    )(proj, proj, proj, proj, cos_s, sin_s, gl, gain, state)
    return jnp.concatenate([y_p, y_s], axis=0), st_p, st_s


def _hg_prepare(hq_ref, hf_ref, lbt_ref, rows, chunk):
    shift = chunk.bit_length() - 1
    ri = lax.broadcasted_iota(I32, (rows, rows), 0)
    ci = lax.broadcasted_iota(I32, (rows, rows), 1)
    same = (ri >> shift) == (ci >> shift)
    causal = same & (ci <= ri)
    z = hf_ref[...]
    log_lb = lbt_ref[0:1, :]
    log_1m_lb = lbt_ref[1:2, :]
    one_m_lb = lbt_ref[2:3, :]
    log_sig = jnp.minimum(z, 0.0) - jnp.log1p(jnp.exp(-jnp.abs(z)))
    bterm = log_1m_lb + log_sig
    logf = jnp.maximum(log_lb, bterm) + jnp.log1p(jnp.exp(-jnp.abs(log_lb - bterm)))
    kh = one_m_lb * jax.nn.sigmoid(-z)
    qh = _silu(hq_ref[...]) * (DH ** -0.5)
    cum = jnp.dot(causal.astype(F32), logf, precision=HIGHEST, preferred_element_type=F32)
    tot = jnp.dot(same.astype(F32), logf, precision=HIGHEST, preferred_element_type=F32)
    qi = qh * jnp.exp(cum)
    ki = kh * jnp.exp(-cum)
    ke = kh * jnp.exp(tot - cum)
    return causal, qi, ki, ke, jnp.exp(tot)


def _rms_norm_gate(o, gain, gate):
    return o * lax.rsqrt(jnp.mean(jnp.square(o), axis=-1, keepdims=True) + LN_EPS) * gain * _silu(gate)


def _hg_prompt_body(hq_ref, hf_ref, hi_ref, hg_ref, lbt_ref, gain_ref, y_ref, st_ref, s_scr):
    c = pl.program_id(1)

    @pl.when(c == 0)
    def _():
        s_scr[...] = jnp.zeros_like(s_scr)

    rows = RET_CHUNK
    causal, qi, ki, ke, etot = _hg_prepare(hq_ref, hf_ref, lbt_ref, rows, HG_CHUNK)
    v = hi_ref[...]
    for h in range(HEADS):
        sl = slice(h * DH, (h + 1) * DH)
        att = jnp.where(causal, _bdot_nt(qi[:, sl], ki[:, sl]), 0.0)
        o_intra = _bdot(att, v[:, sl])
        st = s_scr[h]
        outs = []
        for j in range(rows // HG_CHUNK):
            rs = slice(j * HG_CHUNK, (j + 1) * HG_CHUNK)
            outs.append(o_intra[rs] + _bdot_nt(qi[rs, sl], st))
            st = st * etot[j * HG_CHUNK:j * HG_CHUNK + 1, sl] + _bdot_tn(v[rs, sl], ke[rs, sl])
        s_scr[h] = st
        o = jnp.concatenate(outs, axis=0)
        y_ref[:, sl] = _rms_norm_gate(o, gain_ref[:, sl], hg_ref[:, sl]).astype(BF16)

    @pl.when(c == pl.num_programs(1) - 1)
    def _():
        for h in range(HEADS):
            st_ref[0, h] = s_scr[h].T


def _hg_sample_body(hq_ref, hf_ref, hi_ref, hg_ref, lbt_ref, gain_ref, sin_ref_state,
                    y_ref, st_ref, *, ts):
    rows = SAMPLE_BB * ts
    causal, qi, ki, ke, etot = _hg_prepare(hq_ref, hf_ref, lbt_ref, rows, ts)
    v = hi_ref[...]
    for h in range(HEADS):
        sl = slice(h * DH, (h + 1) * DH)
        att = jnp.where(causal, _bdot_nt(qi[:, sl], ki[:, sl]), 0.0)
        o_intra = _bdot(att, v[:, sl])
        outs = []
        for j in range(SAMPLE_BB):
            rs = slice(j * ts, (j + 1) * ts)
            s = sin_ref_state[j, h]
            outs.append(o_intra[rs] + _bdot(qi[rs, sl], s))
            scale = jnp.broadcast_to(etot[j * ts:j * ts + 1, sl], (DH, DH)).T
            st_ref[j, h] = s * scale + _bdot_tn(ke[rs, sl], v[rs, sl])
        o = jnp.concatenate(outs, axis=0)
        y_ref[:, sl] = _rms_norm_gate(o, gain_ref[:, sl], hg_ref[:, sl]).astype(BF16)


def _hgrn(proj, state, layer, lbt, gain, b, t, nb, ts):
    n_p = b * t
    nc = t // RET_CHUNK
    prow = lambda bi, c: bi * nc + c
    y_p, st_p = pl.pallas_call(
        _hg_prompt_body,
        grid=(b, nc),
        in_specs=[_proj_spec(RET_CHUNK, COL_HG_Q, prow), _proj_spec(RET_CHUNK, COL_HG_F, prow),
                  _proj_spec(RET_CHUNK, COL_HG_I, prow), _proj_spec(RET_CHUNK, COL_HG_G, prow),
                  pl.BlockSpec((None, SUBLANES, HW), lambda bi, c: (layer, 0, 0)),
                  pl.BlockSpec((None, 1, HW), lambda bi, c: (layer, 0, 0))],
        out_specs=[pl.BlockSpec((RET_CHUNK, HW), lambda bi, c: (prow(bi, c), 0)),
                   pl.BlockSpec((1, HEADS, DH, DH), lambda bi, c: (bi, 0, 0, 0))],
        out_shape=[jax.ShapeDtypeStruct((n_p, HW), BF16),
                   jax.ShapeDtypeStruct((b, HEADS, DH, DH), F32)],
        scratch_shapes=[pltpu.VMEM((HEADS, DH, DH), F32)],
        compiler_params=_params(("arbitrary", "arbitrary")),
        name="hgrn_prompt",
    )(proj, proj, proj, proj, lbt, gain)

    rows = SAMPLE_BB * ts
    base = n_p // rows
    srow = lambda i: base + i
    y_s, st_s = pl.pallas_call(
        functools.partial(_hg_sample_body, ts=ts),
        grid=(nb // SAMPLE_BB,),
        in_specs=[_proj_spec(rows, COL_HG_Q, srow), _proj_spec(rows, COL_HG_F, srow),
                  _proj_spec(rows, COL_HG_I, srow), _proj_spec(rows, COL_HG_G, srow),
                  pl.BlockSpec((None, SUBLANES, HW), lambda i: (layer, 0, 0)),
                  pl.BlockSpec((None, 1, HW), lambda i: (layer, 0, 0)),
                  pl.BlockSpec((None, SAMPLE_BB, HEADS, DH, DH), lambda i: (layer, i, 0, 0, 0))],
        out_specs=[pl.BlockSpec((rows, HW), lambda i: (i, 0)),
                   pl.BlockSpec((SAMPLE_BB, HEADS, DH, DH), lambda i: (i, 0, 0, 0))],
        out_shape=[jax.ShapeDtypeStruct((nb * ts, HW), BF16),
                   jax.ShapeDtypeStruct((nb, HEADS, DH, DH), F32)],
        compiler_params=_params(("arbitrary",)),
        name="hgrn_sample",
    )(proj, proj, proj, proj, lbt, gain, state)
    return jnp.concatenate([y_p, y_s], axis=0), st_p, st_s


def _softmax_rows(s):
    e = jnp.exp(s - jnp.max(s, axis=-1, keepdims=True))
    return e / jnp.sum(e, axis=-1, keepdims=True)


def _xa_prompt_body(q_ref, k_ref, v_ref, y_ref):
    for h in range(HEADS):
        sl = slice(h * DH, (h + 1) * DH)
        a = _softmax_rows(_bdot_nt(q_ref[:, sl] * (DH ** -0.5), k_ref[:, sl]))
        y_ref[:, sl] = _bdot(a, v_ref[:, sl]).astype(BF16)


def _xa_sample_body(q_ref, k_ref, v_ref, y_ref, *, ts):
    for j in range(SAMPLE_BB):
        rs = slice(j * ts, (j + 1) * ts)
        for h in range(HEADS):
            sl = slice(h * DH, (h + 1) * DH)
            a = _softmax_rows(_bdot_nt(q_ref[rs, sl] * (DH ** -0.5), k_ref[j, :, h, :]))
            y_ref[rs, sl] = _bdot(a, v_ref[j, :, h, :]).astype(BF16)


def _cross_attention(proj, kv_p, cache_k, cache_v, layer, b, t, nb, ts):
    n_p = b * t
    n_mem = kv_p.shape[0] // b
    tq = _pick(t, (512, 256, 128))
    nq = t // tq
    y_p = pl.pallas_call(
        _xa_prompt_body,
        grid=(b, nq),
        in_specs=[_proj_spec(tq, COL_XA_Q, lambda bi, qi: bi * nq + qi),
                  pl.BlockSpec((n_mem, HW), lambda bi, qi: (bi, 0)),
                  pl.BlockSpec((n_mem, HW), lambda bi, qi: (bi, 1))],
        out_specs=pl.BlockSpec((tq, HW), lambda bi, qi: (bi * nq + qi, 0)),
        out_shape=jax.ShapeDtypeStruct((n_p, HW), BF16),
        compiler_params=_params(("arbitrary", "arbitrary")),
        name="xattn_prompt",
    )(proj, kv_p, kv_p)

    rows = SAMPLE_BB * ts
    base = n_p // rows
    kv_spec = pl.BlockSpec((None, SAMPLE_BB, n_mem, HEADS, DH), lambda i: (layer, i, 0, 0, 0))
    y_s = pl.pallas_call(
        functools.partial(_xa_sample_body, ts=ts),
        grid=(nb // SAMPLE_BB,),
        in_specs=[_proj_spec(rows, COL_XA_Q, lambda i: base + i), kv_spec, kv_spec],
        out_specs=pl.BlockSpec((rows, HW), lambda i: (i, 0)),
        out_shape=jax.ShapeDtypeStruct((nb * ts, HW), BF16),
        compiler_params=_params(("arbitrary",)),
        name="xattn_sample",
    )(proj, cache_k, cache_v)
    return jnp.concatenate([y_p, y_s], axis=0)


def _layer_norm(tv, g, b):
    mu = jnp.mean(tv, axis=-1, keepdims=True)
    var = jnp.mean(jnp.square(tv - mu), axis=-1, keepdims=True)
    return (tv - mu) * lax.rsqrt(var + LN_EPS) * g + b


def _merge_body(yr_ref, yh_ref, yx_ref, g0a, g0b, g1a, g1b, g2a, g2b, x_ref,
                wr_ref, wh_ref, wx_ref, wo_ref, lg_ref, lb_ref,
                x1_ref, x1b_ref, x1t_ref, wr_s, wh_s, wx_s, wo_s):
    @pl.when(pl.program_id(0) == 0)
    def _():
        wr_s[...] = wr_ref[...].astype(BF16)
        wh_s[...] = wh_ref[...].astype(BF16)
        wx_s[...] = wx_ref[...].astype(BF16)
        wo_s[...] = wo_ref[...].astype(BF16)

    def branch(y_ref, w_s, ga, gb):
        gate = jax.nn.sigmoid(jnp.concatenate([ga[...], gb[...]], axis=-1))
        return gate * jnp.dot(y_ref[...], w_s[...], preferred_element_type=F32)

    m = branch(yr_ref, wr_s, g0a, g0b) + branch(yh_ref, wh_s, g1a, g1b) + branch(yx_ref, wx_s, g2a, g2b)
    hmix = jnp.dot(m.astype(BF16), wo_s[...], preferred_element_type=F32)
    x1 = _layer_norm(DN_ALPHA * x_ref[...] + hmix, lg_ref[...], lb_ref[...])
    x1_ref[...] = x1
    x1b_ref[...] = x1.astype(BF16)
    tm = x1.shape[0]
    for s in range(ROW_TILES):
        x1t_ref[pl.ds(s, tm, stride=ROW_TILES), :] = x1[:, s * LANES:(s + 1) * LANES]


def _merge(yr, yh, yx, proj, x, w_up_ret, w_up_hgrn, w_up_xattn, w_out, ln_g, ln_b, layer):
    nt = x.shape[0]
    tm = _pick(nt, (256, 128))
    row = lambda i: (i, 0)
    wspec = lambda k: pl.BlockSpec((None, k, D_MODEL), lambda i: (layer, 0, 0))
    vec = pl.BlockSpec((None, 1, D_MODEL), lambda i: (layer, 0, 0))
    gate_specs = [pl.BlockSpec((tm, HW), lambda i, c=c: (i, COL_GATES + c)) for c in range(6)]
    return pl.pallas_call(
        _merge_body,
        grid=(nt // tm,),
        in_specs=[pl.BlockSpec((tm, HW), row)] * 3 + gate_specs + [pl.BlockSpec((tm, D_MODEL), row),
                  wspec(HW), wspec(HW), wspec(HW), wspec(D_MODEL), vec, vec],
        out_specs=[pl.BlockSpec((tm, D_MODEL), row), pl.BlockSpec((tm, D_MODEL), row),
                   pl.BlockSpec((tm * ROW_TILES, LANES), row)],
        out_shape=[jax.ShapeDtypeStruct((nt, D_MODEL), F32),
                   jax.ShapeDtypeStruct((nt, D_MODEL), BF16),
                   jax.ShapeDtypeStruct((nt * ROW_TILES, LANES), F32)],
        scratch_shapes=[pltpu.VMEM((HW, D_MODEL), BF16)] * 3 + [pltpu.VMEM((D_MODEL, D_MODEL), BF16)],
        compiler_params=_params(("arbitrary",)),
        name="merge_out_ln1",
    )(yr, yh, yx, *([proj] * 6), x, w_up_ret, w_up_hgrn, w_up_xattn, w_out, ln_g, ln_b)


def _router_body(x_ref, wt_ref, b_ref, eidx_ref, wn_ref):
    tm = x_ref.shape[0]
    logits = lax.dot_general(wt_ref[...], x_ref[...], (((1,), (1,)), ((), ())),
                             precision=HIGHEST, preferred_element_type=F32)
    s = jax.nn.sigmoid(logits)
    sel = s + b_ref[...]
    neg = -jnp.inf
    groups = [sel[g * GROUP_SIZE:(g + 1) * GROUP_SIZE, :] for g in range(N_GROUPS)]
    ie = lax.broadcasted_iota(I32, (GROUP_SIZE, tm), 0).astype(F32)
    rows = []
    for blk in groups:
        m1 = jnp.max(blk, axis=0, keepdims=True)
        first = jnp.min(jnp.where(blk == m1, ie, float(GROUP_SIZE)), axis=0, keepdims=True)
        rows.append(m1 + jnp.max(jnp.where(ie == first, neg, blk), axis=0, keepdims=True))
    gscore = jnp.concatenate(rows, axis=0)
    ig = lax.broadcasted_iota(I32, gscore.shape, 0).astype(F32)
    gmask = jnp.zeros(gscore.shape, F32)
    for _ in range(TOPK_GROUPS):
        m = jnp.max(gscore, axis=0, keepdims=True)
        gi = jnp.min(jnp.where(gscore == m, ig, float(N_GROUPS)), axis=0, keepdims=True)
        hit = ig == gi
        gmask = jnp.where(hit, 1.0, gmask)
        gscore = jnp.where(hit, neg, gscore)
    masked = jnp.concatenate([jnp.where(gmask[g:g + 1, :] > 0.5, blk, neg)
                              for g, blk in enumerate(groups)], axis=0)
    ix = lax.broadcasted_iota(I32, masked.shape, 0).astype(F32)
    idxs, ws = [], []
    for _ in range(TOP_K):
        m = jnp.max(masked, axis=0, keepdims=True)
        ei = jnp.min(jnp.where(masked == m, ix, float(N_EXPERTS)), axis=0, keepdims=True)
        hit = ix == ei
        idxs.append(ei)
        ws.append(jnp.sum(jnp.where(hit, s, 0.0), axis=0, keepdims=True))
        masked = jnp.where(hit, neg, masked)
    wsum = ws[0]
    for w in ws[1:]:
        wsum = wsum + w
    pad = [jnp.zeros((1, tm), F32)] * (SUBLANES - TOP_K)
    eidx_ref[...] = jnp.concatenate(idxs + pad, axis=0).astype(I32)
    wn_ref[...] = jnp.concatenate([w / wsum * ROUTED_SCALE for w in ws] + pad, axis=0)


def _router(x1, w_router_t, b_router, layer):
    nt = x1.shape[0]
    tm = _pick(nt, (512, 256, 128))
    return pl.pallas_call(
        _router_body,
        grid=(nt // tm,),
        in_specs=[pl.BlockSpec((tm, D_MODEL), lambda i: (i, 0)),
                  pl.BlockSpec((None, N_EXPERTS, D_MODEL), lambda i: (layer, 0, 0)),
                  pl.BlockSpec((None, N_EXPERTS, 1), lambda i: (layer, 0, 0))],
        out_specs=[pl.BlockSpec((SUBLANES, tm), lambda i: (0, i))] * 2,
        out_shape=[jax.ShapeDtypeStruct((SUBLANES, nt), I32),
                   jax.ShapeDtypeStruct((SUBLANES, nt), F32)],
        compiler_params=_params(("arbitrary",)),
        name="moe_router",
    )(x1, w_router_t, b_router)


def _positions_body(eidx_ref, pos_ref, cnt_ref, off_ref, base_scr, off_scr):
    phase = pl.program_id(0)
    i = pl.program_id(1)
    tp = eidx_ref.shape[1]
    ix = lax.broadcasted_iota(I32, (N_EXPERTS, tp), 0)
    eidx = eidx_ref[...]
    member = jnp.zeros((N_EXPERTS, tp), F32)
    for k in range(TOP_K):
        member = member + (ix == eidx[k:k + 1, :]).astype(F32)
    tile_cnt = jnp.sum(member, axis=1, keepdims=True)

    @pl.when((phase == 0) & (i == 0))
    def _():
        base_scr[...] = jnp.zeros_like(base_scr)

    @pl.when((phase == 1) & (i == 0))
    def _():
        cnt = base_scr[...]
        er = lax.broadcasted_iota(I32, (N_EXPERTS, N_EXPERTS), 0)
        ec = lax.broadcasted_iota(I32, (N_EXPERTS, N_EXPERTS), 1)
        off = jnp.dot((ec < er).astype(F32), cnt, precision=HIGHEST, preferred_element_type=F32)
        off_scr[...] = off
        cnt_ref[...] = cnt
        off_ref[...] = off
        base_scr[...] = jnp.zeros_like(base_scr)

    @pl.when(phase == 1)
    def _():
        tr = lax.broadcasted_iota(I32, (tp, tp), 0)
        tc = lax.broadcasted_iota(I32, (tp, tp), 1)
        before = jnp.dot(member.astype(BF16), (tr < tc).astype(BF16), preferred_element_type=F32)
        where_to = before + (off_scr[...] + base_scr[...])[:, 0:1]
        rows = [jnp.sum(jnp.where(ix == eidx[k:k + 1, :], where_to, 0.0), axis=0, keepdims=True)
                for k in range(TOP_K)]
        rows += [jnp.zeros((1, tp), F32)] * (SUBLANES - TOP_K)
        pos_ref[...] = jnp.concatenate(rows, axis=0).astype(I32)

    base_scr[...] = base_scr[...] + tile_cnt


def _positions(eidx):
    nt = eidx.shape[1]
    tp = _pick(nt, (512, 256, 128))
    const = lambda p, i: (0, 0)
    return pl.pallas_call(
        _positions_body,
        grid=(2, nt // tp),
        in_specs=[pl.BlockSpec((SUBLANES, tp), lambda p, i: (0, i))],
        out_specs=[pl.BlockSpec((SUBLANES, tp), lambda p, i: (0, i * p)),
                   pl.BlockSpec((N_EXPERTS, LANES), const), pl.BlockSpec((N_EXPERTS, LANES), const)],
        out_shape=[jax.ShapeDtypeStruct((SUBLANES, nt), I32),
                   jax.ShapeDtypeStruct((N_EXPERTS, LANES), F32),
                   jax.ShapeDtypeStruct((N_EXPERTS, LANES), F32)],
        scratch_shapes=[pltpu.VMEM((N_EXPERTS, LANES), F32), pltpu.VMEM((N_EXPERTS, LANES), F32)],
        compiler_params=_params(("arbitrary", "arbitrary")),
        name="moe_positions",
    )(eidx)


def _step_table(cnt_f, off_f, n_rows):
    te = EXPERT_TILE
    n_steps = n_rows // te + N_EXPERTS
    cnt = cnt_f[:, 0].astype(I32)
    off = off_f[:, 0].astype(I32)
    first = off // te
    last = (off + cnt - 1) // te
    nst = jnp.where(cnt > 0, last - first + 1, 0)
    s_end = jnp.cumsum(nst)
    s_beg = s_end - nst
    total = s_end[-1]
    s = jnp.minimum(jnp.arange(n_steps, dtype=I32), total - 1)
    e = jnp.sum((s_end[None, :] <= s[:, None]).astype(I32), axis=1)
    tile = first[e] + s - s_beg[e]
    valid = jnp.arange(n_steps, dtype=I32) < total
    lo = jnp.where(valid, jnp.maximum(off[e], tile * te), 0)
    hi = jnp.where(valid, jnp.minimum(off[e] + cnt[e], (tile + 1) * te), 0)
    fresh = jnp.concatenate([jnp.ones((1,), I32), (tile[1:] != tile[:-1]).astype(I32)])
    return tile, e, lo, hi, fresh


def _dispatch_body(pos_ref, xt_ref, xs_ref, pos_s, sem_p, sem):
    td = pos_ref.shape[1]
    cp = pltpu.make_async_copy(pos_ref, pos_s, sem_p)
    cp.start()
    cp.wait()

    def issue(r, carry):
        for k in range(TOP_K):
            pltpu.make_async_copy(xt_ref.at[r], xs_ref.at[pos_s[k, r]], sem).start()
        return carry

    lax.fori_loop(0, td, issue, 0)
    for k in range(TOP_K):
        pltpu.make_async_copy(xt_ref, xs_ref.at[pl.ds(0, td)], sem).wait()


def _dispatch(pos, x1t):
    nt = x1t.shape[0]
    td = _pick(nt, (512, 256, 128))
    return pl.pallas_call(
        _dispatch_body,
        grid=(nt // td,),
        in_specs=[pl.BlockSpec((SUBLANES, td), lambda i: (0, i)),
                  pl.BlockSpec((td, ROW_TILES, LANES), lambda i: (i, 0, 0))],
        out_specs=pl.BlockSpec(memory_space=pl.ANY),
        out_shape=jax.ShapeDtypeStruct((nt * TOP_K, ROW_TILES, LANES), F32),
        scratch_shapes=[pltpu.SMEM((SUBLANES, td), I32), pltpu.SemaphoreType.DMA, pltpu.SemaphoreType.DMA],
        compiler_params=_params(("arbitrary",)),
        name="moe_dispatch",
    )(pos, x1t)


def _experts_body(tile_ref, exp_ref, lo_ref, hi_ref, fresh_ref, xs_ref, wg_ref, wu_ref, wd_ref, ye_ref):
    s = pl.program_id(0)
    te = EXPERT_TILE
    lo = lo_ref[s]
    hi = hi_ref[s]

    @pl.when(fresh_ref[s] == 1)
    def _():
        ye_ref[...] = jnp.zeros_like(ye_ref)

    @pl.when(hi > lo)
    def _():
        x = jnp.concatenate([xs_ref[pl.ds(t, te, stride=ROW_TILES), :].astype(BF16)
                             for t in range(ROW_TILES)], axis=-1)
        g = jnp.dot(x, wg_ref[...].astype(BF16), preferred_element_type=F32)
        u = jnp.dot(x, wu_ref[...].astype(BF16), preferred_element_type=F32)
        y = jnp.dot((_silu(g) * u).astype(BF16), wd_ref[...].astype(BF16), preferred_element_type=F32)
        row = tile_ref[s] * te + lax.broadcasted_iota(I32, (te, LANES), 0)
        mine = (row >= lo) & (row < hi)
        for t in range(ROW_TILES):
            sl = pl.ds(t, te, stride=ROW_TILES)
            ye_ref[sl, :] = jnp.where(mine, y[:, t * LANES:(t + 1) * LANES], ye_ref[sl, :])


def _experts(table, xs, w_gate, w_up, w_down, layer):
    n_rows = xs.shape[0] // ROW_TILES
    te = EXPERT_TILE
    n_steps = table[0].shape[0]
    tile_map = lambda s, tile, e, lo, hi, fr: (tile[s], 0)
    w_in_spec = pl.BlockSpec((None, None, D_MODEL, D_EXPERT), lambda s, tile, e, lo, hi, fr: (layer, e[s], 0, 0))
    w_dn_spec = pl.BlockSpec((None, None, D_EXPERT, D_MODEL), lambda s, tile, e, lo, hi, fr: (layer, e[s], 0, 0))
    return pl.pallas_call(
        _experts_body,
        grid_spec=pltpu.PrefetchScalarGridSpec(
            num_scalar_prefetch=5,
            grid=(n_steps,),
            in_specs=[pl.BlockSpec((te * ROW_TILES, LANES), tile_map), w_in_spec, w_in_spec, w_dn_spec],
            out_specs=pl.BlockSpec((te * ROW_TILES, LANES), tile_map)),
        out_shape=jax.ShapeDtypeStruct((n_rows * ROW_TILES, LANES), F32),
        compiler_params=_params(("arbitrary",)),
        name="moe_experts",
    )(*table, xs, w_gate, w_up, w_down)


def _combine_body(pos_ref, wn_ref, ye_ref, x1_ref, x1b_ref, wsg_ref, wsu_ref, wsd_ref, lg_ref, lb_ref,
                  x2_ref, x2b_ref, pos_s, wn_s, buf, acc, wsg_s, wsu_s, wsd_s, sem_p, sem_w, sem):
    tc = pos_ref.shape[1]

    @pl.when(pl.program_id(0) == 0)
    def _():
        wsg_s[...] = wsg_ref[...].astype(BF16)
        wsu_s[...] = wsu_ref[...].astype(BF16)
        wsd_s[...] = wsd_ref[...].astype(BF16)

    cp = pltpu.make_async_copy(pos_ref, pos_s, sem_p)
    cw = pltpu.make_async_copy(wn_ref, wn_s, sem_w)
    cp.start()
    cw.start()
    cp.wait()
    cw.wait()

    def issue(r, carry):
        for k in range(TOP_K):
            pltpu.make_async_copy(ye_ref.at[pos_s[k, r]], buf.at[k, r], sem).start()
        return carry

    lax.fori_loop(0, tc, issue, 0)
    xb = x1b_ref[...]
    hs = _silu(jnp.dot(xb, wsg_s[...], preferred_element_type=F32)) * jnp.dot(xb, wsu_s[...], preferred_element_type=F32)
    shared = jnp.dot(hs.astype(BF16), wsd_s[...], preferred_element_type=F32)
    for k in range(TOP_K):
        pltpu.make_async_copy(ye_ref.at[pl.ds(0, tc)], buf.at[k], sem).wait()

    def reduce(r, carry):
        tot = buf[0, r] * wn_s[0, r]
        for k in range(1, TOP_K):
            tot = tot + buf[k, r] * wn_s[k, r]
        acc[pl.ds(pl.multiple_of(r * ROW_TILES, ROW_TILES), ROW_TILES), :] = tot
        return carry

    lax.fori_loop(0, tc, reduce, 0)
    routed = jnp.concatenate([acc[pl.ds(t, tc, stride=ROW_TILES), :] for t in range(ROW_TILES)], axis=-1)
    x2 = _layer_norm(DN_ALPHA * x1_ref[...] + (routed + shared), lg_ref[...], lb_ref[...])
    x2_ref[...] = x2
    x2b_ref[...] = x2.astype(BF16)


def _combine(pos, wn, ye, x1, x1b, w_s_gate, w_s_up, w_s_down, ln_g, ln_b, layer):
    nt = x1.shape[0]
    tc = _pick(nt, (256, 128))
    d_sh = w_s_gate.shape[2]
    row = lambda i: (i, 0)
    vec = pl.BlockSpec((None, 1, D_MODEL), lambda i: (layer, 0, 0))
    return pl.pallas_call(
        _combine_body,
        grid=(nt // tc,),
        in_specs=[pl.BlockSpec((SUBLANES, tc), lambda i: (0, i)), pl.BlockSpec((SUBLANES, tc), lambda i: (0, i)),
                  pl.BlockSpec(memory_space=pl.ANY),
                  pl.BlockSpec((tc, D_MODEL), row), pl.BlockSpec((tc, D_MODEL), row),
                  pl.BlockSpec((None, D_MODEL, d_sh), lambda i: (layer, 0, 0)),
                  pl.BlockSpec((None, D_MODEL, d_sh), lambda i: (layer, 0, 0)),
                  pl.BlockSpec((None, d_sh, D_MODEL), lambda i: (layer, 0, 0)), vec, vec],
        out_specs=[pl.BlockSpec((tc, D_MODEL), row), pl.BlockSpec((tc, D_MODEL), row)],
        out_shape=[jax.ShapeDtypeStruct((nt, D_MODEL), F32), jax.ShapeDtypeStruct((nt, D_MODEL), BF16)],
        scratch_shapes=[pltpu.SMEM((SUBLANES, tc), I32), pltpu.SMEM((SUBLANES, tc), F32),
                        pltpu.VMEM((TOP_K, tc, ROW_TILES, LANES), F32), pltpu.VMEM((tc * ROW_TILES, LANES), F32),
                        pltpu.VMEM((D_MODEL, d_sh), BF16), pltpu.VMEM((D_MODEL, d_sh), BF16),
                        pltpu.VMEM((d_sh, D_MODEL), BF16),
                        pltpu.SemaphoreType.DMA, pltpu.SemaphoreType.DMA, pltpu.SemaphoreType.DMA],
        compiler_params=_params(("arbitrary",)),
        name="moe_combine_ln2",
    )(pos, wn, ye, x1, x1b, w_s_gate, w_s_up, w_s_down, ln_g, ln_b)


def _rope_tables(t, pos0):
    inv = 1.0 / (ROPE_BASE ** (jnp.arange(0, DH, 2, dtype=F32) / DH))
    ang = (jnp.arange(t, dtype=F32) + pos0)[:, None] * inv[None, :]
    cos, sin = jnp.cos(ang), jnp.sin(ang)
    return jnp.concatenate([cos, cos], axis=-1), jnp.concatenate([-sin, sin], axis=-1)


def kernel(x_prompt, x_sample, mem_prompt, state_ret, state_hgrn, cache_mem_k, cache_mem_v, w_in, w_up_ret, w_up_hgrn, w_up_xattn, w_out, w_mem_kv, ret_norm_g, hgrn_norm_g, lb_logits, ln1_g, ln1_b, ln2_g, ln2_b, w_router, b_router, w_e_gate, w_e_up, w_e_down, w_s_gate, w_s_up, w_s_down):
    b, t, d = x_prompt.shape
    nb, ts, _ = x_sample.shape
    n_mem = mem_prompt.shape[1]
    assert d == D_MODEL and t % RET_CHUNK == 0 and nb % SAMPLE_BB == 0
    assert ts & (ts - 1) == 0 and HG_CHUNK % ts == 0 and RET_CHUNK % ts == 0
    n_p, n_s = b * t, nb * ts
    nt = n_p + n_s
    assert n_p % (SAMPLE_BB * ts) == 0 and (nt * TOP_K) % EXPERT_TILE == 0

    lb_cum = jnp.cumsum(jax.nn.softmax(lb_logits.astype(F32), axis=0), axis=0)
    lbs = lb_cum - lb_cum[0:1]
    lbt = jnp.stack([jnp.log(lbs), jnp.log1p(-lbs), 1.0 - lbs] + [jnp.zeros_like(lbs)] * (SUBLANES - 3), axis=1)
    gl = jnp.broadcast_to(jnp.log1p(-jnp.exp2(-5.0 - jnp.arange(HEADS, dtype=F32)))[:, None], (HEADS, DH))
    cos_p, sin_p = _rope_tables(t, 0)
    cos_s, sin_s = _rope_tables(ts, PAST_LEN)
    cos_s, sin_s = jnp.tile(cos_s, (SAMPLE_BB, 1)), jnp.tile(sin_s, (SAMPLE_BB, 1))
    vec3 = lambda a: a.reshape(DEPTH, 1, -1)
    w_router_t = jnp.swapaxes(w_router, 1, 2)
    b_router3 = b_router.reshape(DEPTH, N_EXPERTS, 1)
    mem2 = mem_prompt.reshape(b * n_mem, d)

    x = jnp.concatenate([x_prompt.reshape(n_p, d), x_sample.reshape(n_s, d)], axis=0)
    xb = x.astype(BF16)
    tm_proj = _pick(nt, (1024, 512, 128))
    outs = {k: [] for k in ("ret_p", "hg_p", "mk", "mv", "ret_s", "hg_s")}
    for l in range(DEPTH):
        proj = _matmul(xb, w_in, l, tm_proj, 1280)
        kv_p = _matmul(mem2, w_mem_kv, l, _pick(b * n_mem, (1024, 512, 256)), 2 * HW)
        yr, ret_p, ret_s = _retention(proj, state_ret, l, cos_p, sin_p, cos_s, sin_s, gl,
                                      vec3(ret_norm_g), b, t, nb, ts)
        yh, hg_p, hg_s = _hgrn(proj, state_hgrn, l, lbt, vec3(hgrn_norm_g), b, t, nb, ts)
        yx = _cross_attention(proj, kv_p, cache_mem_k, cache_mem_v, l, b, t, nb, ts)
        x1, x1b, x1t = _merge(yr, yh, yx, proj, x, w_up_ret, w_up_hgrn, w_up_xattn, w_out,
                              vec3(ln1_g), vec3(ln1_b), l)
        eidx, wn = _router(x1, w_router_t, b_router3, l)
        pos, cnt, off = _positions(eidx)
        table = _step_table(cnt, off, nt * TOP_K)
        xs = _dispatch(pos, x1t.reshape(nt, ROW_TILES, LANES))
        ye = _experts(table, xs.reshape(-1, LANES), w_e_gate, w_e_up, w_e_down, l)
        ye = ye.reshape(-1, ROW_TILES, LANES)
        x, xb = _combine(pos, wn, ye, x1, x1b, w_s_gate, w_s_up, w_s_down, vec3(ln2_g), vec3(ln2_b), l)
        outs["ret_p"].append(ret_p)
        outs["hg_p"].append(hg_p)
        outs["mk"].append(kv_p[:, :HW].reshape(b, n_mem, HEADS, DH))
        outs["mv"].append(kv_p[:, HW:].reshape(b, n_mem, HEADS, DH))
        outs["ret_s"].append(ret_s)
        outs["hg_s"].append(hg_s)
    return (x[:n_p].reshape(b, t, d), x[n_p:].reshape(nb, ts, d),
            jnp.stack(outs["ret_p"]), jnp.stack(outs["hg_p"]), jnp.stack(outs["mk"]), jnp.stack(outs["mv"]),
            jnp.stack(outs["ret_s"]), jnp.stack(outs["hg_s"]))
```

```python
import functools

import jax
import jax.numpy as jnp
from jax import lax
from jax.experimental import pallas as pl
from jax.experimental.pallas import tpu as pltpu

F32 = jnp.float32
BF16 = jnp.bfloat16
I32 = jnp.int32
HIGHEST = lax.Precision.HIGHEST

D_MODEL = 1024
DEPTH = 2
PAST_LEN = 16384
HEADS = 4
DH = 128
HW = HEADS * DH
RET_CHUNK = 128
HG_CHUNK = 16
ROPE_BASE = 10000.0
N_EXPERTS = 64
N_GROUPS = 8
GROUP_SIZE = N_EXPERTS // N_GROUPS
TOPK_GROUPS = 4
TOP_K = 6
D_EXPERT = 256
ROUTED_SCALE = 2.5
LN_EPS = 1e-5
DN_ALPHA = (2 * DEPTH) ** 0.25
N_IN = 9 * HW + 3 * D_MODEL
COL_RET_Q, COL_RET_K, COL_RET_V, COL_RET_G = 0, 1, 2, 3
COL_HG_Q, COL_HG_F, COL_HG_I, COL_HG_G = 4, 5, 6, 7
COL_XA_Q = 8
COL_GATES = 9
LANES = 128
SUBLANES = 8
ROW_TILES = D_MODEL // LANES
SAMPLE_BB = 8
EXPERT_TILE = 256
ISSUE_UNROLL = 8
VMEM_LIMIT = 56 * 1024 * 1024


def _params(sem):
    return pltpu.CompilerParams(dimension_semantics=sem, vmem_limit_bytes=VMEM_LIMIT)


def _bdot(a, b):
    return jnp.dot(a.astype(BF16), b.astype(BF16), preferred_element_type=F32)


def _bdot_nt(a, b):
    return lax.dot_general(a.astype(BF16), b.astype(BF16), (((1,), (1,)), ((), ())),
                           preferred_element_type=F32)


def _bdot_tn(a, b):
    return lax.dot_general(a.astype(BF16), b.astype(BF16), (((0,), (0,)), ((), ())),
                           preferred_element_type=F32)


def _silu(x):
    return x * jax.nn.sigmoid(x)


def _pick(n, prefs):
    for p in prefs:
        if n % p == 0:
            return p
    raise ValueError(f"no tile for {n}")


def _mm_body(x_ref, w_ref, o_ref, wb_ref):
    @pl.when(pl.program_id(1) == 0)
    def _():
        wb_ref[...] = w_ref[...].astype(BF16)

    o_ref[...] = jnp.dot(x_ref[...].astype(BF16), wb_ref[...],
                         preferred_element_type=F32).astype(o_ref.dtype)


def _matmul(x, w, layer, tm, tn):
    m, k = x.shape
    n = w.shape[2]
    return pl.pallas_call(
        _mm_body,
        grid=(n // tn, m // tm),
        in_specs=[pl.BlockSpec((tm, k), lambda j, i: (i, 0)),
                  pl.BlockSpec((None, k, tn), lambda j, i: (layer, 0, j))],
        out_specs=pl.BlockSpec((tm, tn), lambda j, i: (i, j)),
        out_shape=jax.ShapeDtypeStruct((m, n), F32),
        scratch_shapes=[pltpu.VMEM((k, tn), BF16)],
        compiler_params=_params(("arbitrary", "arbitrary")),
        name="dense_matmul",
    )(x, w)


def _rotary(x, cos, sin_signed):
    return x * cos + pltpu.roll(x, DH // 2, 1) * sin_signed


def _group_norm_gate(o, gain, gate):
    mu = jnp.mean(o, axis=-1, keepdims=True)
    var = jnp.mean(jnp.square(o - mu), axis=-1, keepdims=True)
    return (o - mu) * lax.rsqrt(var + LN_EPS) * gain * _silu(gate)


def _ret_prompt_body(q_ref, k_ref, v_ref, g_ref, cos_ref, sin_ref, gl_ref, gain_ref,
                     y_ref, st_ref, s_scr):
    c = pl.program_id(1)

    @pl.when(c == 0)
    def _():
        s_scr[...] = jnp.zeros_like(s_scr)

    ch = RET_CHUNK
    cos = cos_ref[...]
    sin = sin_ref[...]
    ri = lax.broadcasted_iota(I32, (ch, ch), 0)
    ci = lax.broadcasted_iota(I32, (ch, ch), 1)
    rel = (ri - ci).astype(F32)
    idx = lax.broadcasted_iota(I32, (ch, DH), 0).astype(F32)
    for h in range(HEADS):
        sl = slice(h * DH, (h + 1) * DH)
        gl = gl_ref[h:h + 1, :]
        qr = _rotary(q_ref[:, sl], cos, sin)
        kr = _rotary(k_ref[:, sl], cos, sin) * (DH ** -0.5)
        v = v_ref[:, sl]
        intra = jnp.where(rel >= 0, jnp.exp(gl * rel), 0.0)
        att = _bdot_nt(qr, kr) * intra
        s = s_scr[h]
        o = _bdot(att, v) + _bdot(qr, s) * jnp.exp(gl * (idx + 1.0))
        s_scr[h] = s * jnp.exp(gl * float(ch)) + _bdot_tn(kr * jnp.exp(gl * (ch - 1.0 - idx)), v)
        y_ref[:, sl] = _group_norm_gate(o, gain_ref[:, sl], g_ref[:, sl]).astype(BF16)

    @pl.when(c == pl.num_programs(1) - 1)
    def _():
        st_ref[0] = s_scr[...]


def _ret_sample_body(q_ref, k_ref, v_ref, g_ref, cos_ref, sin_ref, gl_ref, gain_ref, sin_ref_state,
                     y_ref, st_ref, *, ts):
    rows = SAMPLE_BB * ts
    shift = ts.bit_length() - 1
    cos = cos_ref[...]
    sin = sin_ref[...]
    ri = lax.broadcasted_iota(I32, (rows, rows), 0)
    ci = lax.broadcasted_iota(I32, (rows, rows), 1)
    rel = (ri - ci).astype(F32)
    mask = ((ri >> shift) == (ci >> shift)) & (ri >= ci)
    idx = (lax.broadcasted_iota(I32, (rows, DH), 0) & (ts - 1)).astype(F32)
    for h in range(HEADS):
        sl = slice(h * DH, (h + 1) * DH)
        gl = gl_ref[h:h + 1, :]
        qr = _rotary(q_ref[:, sl], cos, sin)
        kr = _rotary(k_ref[:, sl], cos, sin) * (DH ** -0.5)
        v = v_ref[:, sl]
        intra = jnp.where(mask, jnp.exp(gl[:, :rows] * rel), 0.0)
        o_intra = _bdot(_bdot_nt(qr, kr) * intra, v)
        q_dec = jnp.exp(gl * (idx + 1.0))
        kd = kr * jnp.exp(gl * (ts - 1.0 - idx))
        c_dec = jnp.exp(gl * float(ts))
        outs = []
        for j in range(SAMPLE_BB):
            rs = slice(j * ts, (j + 1) * ts)
            s = sin_ref_state[j, h]
            outs.append(o_intra[rs] + _bdot(qr[rs], s) * q_dec[rs])
            st_ref[j, h] = s * c_dec + _bdot_tn(kd[rs], v[rs])
        o = jnp.concatenate(outs, axis=0)
        y_ref[:, sl] = _group_norm_gate(o, gain_ref[:, sl], g_ref[:, sl]).astype(BF16)


def _proj_spec(rows, col, row_map):
    return pl.BlockSpec((rows, HW), lambda *a: (row_map(*a), col))


def _retention(proj, state, layer, cos_p, sin_p, cos_s, sin_s, gl, gain, b, t, nb, ts):
    n_p = b * t
    nc = t // RET_CHUNK
    prow = lambda bi, c: bi * nc + c
    const2 = lambda *a: (0, 0)
    y_p, st_p = pl.pallas_call(
        _ret_prompt_body,
        grid=(b, nc),
        in_specs=[_proj_spec(RET_CHUNK, COL_RET_Q, prow), _proj_spec(RET_CHUNK, COL_RET_K, prow),
                  _proj_spec(RET_CHUNK, COL_RET_V, prow), _proj_spec(RET_CHUNK, COL_RET_G, prow),
                  pl.BlockSpec((RET_CHUNK, DH), lambda bi, c: (c, 0)),
                  pl.BlockSpec((RET_CHUNK, DH), lambda bi, c: (c, 0)),
                  pl.BlockSpec((HEADS, DH), const2),
                  pl.BlockSpec((None, 1, HW), lambda bi, c: (layer, 0, 0))],
        out_specs=[pl.BlockSpec((RET_CHUNK, HW), lambda bi, c: (prow(bi, c), 0)),
                   pl.BlockSpec((1, HEADS, DH, DH), lambda bi, c: (bi, 0, 0, 0))],
        out_shape=[jax.ShapeDtypeStruct((n_p, HW), BF16),
                   jax.ShapeDtypeStruct((b, HEADS, DH, DH), F32)],
        scratch_shapes=[pltpu.VMEM((HEADS, DH, DH), F32)],
        compiler_params=_params(("arbitrary", "arbitrary")),
        name="retention_prompt",
    )(proj, proj, proj, proj, cos_p, sin_p, gl, gain)

    rows = SAMPLE_BB * ts
    base = n_p // rows
    srow = lambda i: base + i
    y_s, st_s = pl.pallas_call(
        functools.partial(_ret_sample_body, ts=ts),
        grid=(nb // SAMPLE_BB,),
        in_specs=[_proj_spec(rows, COL_RET_Q, srow), _proj_spec(rows, COL_RET_K, srow),
                  _proj_spec(rows, COL_RET_V, srow), _proj_spec(rows, COL_RET_G, srow),
                  pl.BlockSpec((rows, DH), const2), pl.BlockSpec((rows, DH), const2),
                  pl.BlockSpec((HEADS, DH), const2),
                  pl.BlockSpec((None, 1, HW), lambda i: (layer, 0, 0)),
                  pl.BlockSpec((None, SAMPLE_BB, HEADS, DH, DH), lambda i: (layer, i, 0, 0, 0))],
        out_specs=[pl.BlockSpec((rows, HW), lambda i: (i, 0)),
                   pl.BlockSpec((SAMPLE_BB, HEADS, DH, DH), lambda i: (i, 0, 0, 0))],
        out_shape=[jax.ShapeDtypeStruct((nb * ts, HW), BF16),
                   jax.ShapeDtypeStruct((nb, HEADS, DH, DH), F32)],
        compiler_params=_params(("arbitrary",)),
        name="retention_sample",
    )(proj, proj, proj, proj, cos_s, sin_s, gl, gain, state)
    return (y_p, y_s), st_p, st_s


def _hg_prepare(hq_ref, hf_ref, lbt_ref, rows, chunk):
    shift = chunk.bit_length() - 1
    ri = lax.broadcasted_iota(I32, (rows, rows), 0)
    ci = lax.broadcasted_iota(I32, (rows, rows), 1)
    same = (ri >> shift) == (ci >> shift)
    causal = same & (ci <= ri)
    z = hf_ref[...]
    log_lb = lbt_ref[0:1, :]
    log_1m_lb = lbt_ref[1:2, :]
    one_m_lb = lbt_ref[2:3, :]
    log_sig = jnp.minimum(z, 0.0) - jnp.log1p(jnp.exp(-jnp.abs(z)))
    bterm = log_1m_lb + log_sig
    logf = jnp.maximum(log_lb, bterm) + jnp.log1p(jnp.exp(-jnp.abs(log_lb - bterm)))
    kh = one_m_lb * jax.nn.sigmoid(-z)
    qh = _silu(hq_ref[...]) * (DH ** -0.5)
    cum = jnp.dot(causal.astype(F32), logf, precision=HIGHEST, preferred_element_type=F32)
    tot = jnp.dot(same.astype(F32), logf, precision=HIGHEST, preferred_element_type=F32)
    qi = qh * jnp.exp(cum)
    ki = kh * jnp.exp(-cum)
    ke = kh * jnp.exp(tot - cum)
    return causal, qi, ki, ke, jnp.exp(tot)


def _rms_norm_gate(o, gain, gate):
    return o * lax.rsqrt(jnp.mean(jnp.square(o), axis=-1, keepdims=True) + LN_EPS) * gain * _silu(gate)


def _hg_prompt_body(hq_ref, hf_ref, hi_ref, hg_ref, lbt_ref, gain_ref, y_ref, st_ref, s_scr):
    c = pl.program_id(1)

    @pl.when(c == 0)
    def _():
        s_scr[...] = jnp.zeros_like(s_scr)

    rows = RET_CHUNK
    causal, qi, ki, ke, etot = _hg_prepare(hq_ref, hf_ref, lbt_ref, rows, HG_CHUNK)
    v = hi_ref[...]
    for h in range(HEADS):
        sl = slice(h * DH, (h + 1) * DH)
        att = jnp.where(causal, _bdot_nt(qi[:, sl], ki[:, sl]), 0.0)
        o_intra = _bdot(att, v[:, sl])
        st = s_scr[h]
        outs = []
        for j in range(rows // HG_CHUNK):
            rs = slice(j * HG_CHUNK, (j + 1) * HG_CHUNK)
            outs.append(o_intra[rs] + _bdot_nt(qi[rs, sl], st))
            st = st * etot[j * HG_CHUNK:j * HG_CHUNK + 1, sl] + _bdot_tn(v[rs, sl], ke[rs, sl])
        s_scr[h] = st
        o = jnp.concatenate(outs, axis=0)
        y_ref[:, sl] = _rms_norm_gate(o, gain_ref[:, sl], hg_ref[:, sl]).astype(BF16)

    @pl.when(c == pl.num_programs(1) - 1)
    def _():
        for h in range(HEADS):
            st_ref[0, h] = s_scr[h].T


def _hg_sample_body(hq_ref, hf_ref, hi_ref, hg_ref, lbt_ref, gain_ref, sin_ref_state,
                    y_ref, st_ref, *, ts):
    rows = SAMPLE_BB * ts
    causal, qi, ki, ke, etot = _hg_prepare(hq_ref, hf_ref, lbt_ref, rows, ts)
    v = hi_ref[...]
    for h in range(HEADS):
        sl = slice(h * DH, (h + 1) * DH)
        att = jnp.where(causal, _bdot_nt(qi[:, sl], ki[:, sl]), 0.0)
        o_intra = _bdot(att, v[:, sl])
        outs = []
        for j in range(SAMPLE_BB):
            rs = slice(j * ts, (j + 1) * ts)
            s = sin_ref_state[j, h]
            outs.append(o_intra[rs] + _bdot(qi[rs, sl], s))
            scale = jnp.broadcast_to(etot[j * ts:j * ts + 1, sl], (DH, DH)).T
            st_ref[j, h] = s * scale + _bdot_tn(ke[rs, sl], v[rs, sl])
        o = jnp.concatenate(outs, axis=0)
        y_ref[:, sl] = _rms_norm_gate(o, gain_ref[:, sl], hg_ref[:, sl]).astype(BF16)


def _hgrn(proj, state, layer, lbt, gain, b, t, nb, ts):
    n_p = b * t
    nc = t // RET_CHUNK
    prow = lambda bi, c: bi * nc + c
    y_p, st_p = pl.pallas_call(
        _hg_prompt_body,
        grid=(b, nc),
        in_specs=[_proj_spec(RET_CHUNK, COL_HG_Q, prow), _proj_spec(RET_CHUNK, COL_HG_F, prow),
                  _proj_spec(RET_CHUNK, COL_HG_I, prow), _proj_spec(RET_CHUNK, COL_HG_G, prow),
                  pl.BlockSpec((None, SUBLANES, HW), lambda bi, c: (layer, 0, 0)),
                  pl.BlockSpec((None, 1, HW), lambda bi, c: (layer, 0, 0))],
        out_specs=[pl.BlockSpec((RET_CHUNK, HW), lambda bi, c: (prow(bi, c), 0)),
                   pl.BlockSpec((1, HEADS, DH, DH), lambda bi, c: (bi, 0, 0, 0))],
        out_shape=[jax.ShapeDtypeStruct((n_p, HW), BF16),
                   jax.ShapeDtypeStruct((b, HEADS, DH, DH), F32)],
        scratch_shapes=[pltpu.VMEM((HEADS, DH, DH), F32)],
        compiler_params=_params(("arbitrary", "arbitrary")),
        name="hgrn_prompt",
    )(proj, proj, proj, proj, lbt, gain)

    rows = SAMPLE_BB * ts
    base = n_p // rows
    srow = lambda i: base + i
    y_s, st_s = pl.pallas_call(
        functools.partial(_hg_sample_body, ts=ts),
        grid=(nb // SAMPLE_BB,),
        in_specs=[_proj_spec(rows, COL_HG_Q, srow), _proj_spec(rows, COL_HG_F, srow),
                  _proj_spec(rows, COL_HG_I, srow), _proj_spec(rows, COL_HG_G, srow),
                  pl.BlockSpec((None, SUBLANES, HW), lambda i: (layer, 0, 0)),
                  pl.BlockSpec((None, 1, HW), lambda i: (layer, 0, 0)),
                  pl.BlockSpec((None, SAMPLE_BB, HEADS, DH, DH), lambda i: (layer, i, 0, 0, 0))],
        out_specs=[pl.BlockSpec((rows, HW), lambda i: (i, 0)),
                   pl.BlockSpec((SAMPLE_BB, HEADS, DH, DH), lambda i: (i, 0, 0, 0))],
        out_shape=[jax.ShapeDtypeStruct((nb * ts, HW), BF16),
                   jax.ShapeDtypeStruct((nb, HEADS, DH, DH), F32)],
        compiler_params=_params(("arbitrary",)),
        name="hgrn_sample",
    )(proj, proj, proj, proj, lbt, gain, state)
    return (y_p, y_s), st_p, st_s


def _softmax_rows(s):
    e = jnp.exp(s - jnp.max(s, axis=-1, keepdims=True))
    return e / jnp.sum(e, axis=-1, keepdims=True)


def _xa_prompt_body(q_ref, k_ref, v_ref, y_ref):
    for h in range(HEADS):
        sl = slice(h * DH, (h + 1) * DH)
        a = _softmax_rows(_bdot_nt(q_ref[:, sl] * (DH ** -0.5), k_ref[:, sl]))
        y_ref[:, sl] = _bdot(a, v_ref[:, sl]).astype(BF16)


def _xa_sample_body(q_ref, k_ref, v_ref, y_ref, *, ts):
    n_mem = k_ref.shape[1] // HEADS
    for j in range(SAMPLE_BB):
        rs = slice(j * ts, (j + 1) * ts)
        for h in range(HEADS):
            sl = slice(h * DH, (h + 1) * DH)
            mem = pl.ds(h, n_mem, stride=HEADS)
            a = _softmax_rows(_bdot_nt(q_ref[rs, sl] * (DH ** -0.5), k_ref[j, mem, :]))
            y_ref[rs, sl] = _bdot(a, v_ref[j, mem, :]).astype(BF16)


def _cross_attention(proj, kv_p, cache_k, cache_v, layer, b, t, nb, ts):
    n_p = b * t
    n_mem = kv_p.shape[0] // b
    tq = _pick(t, (512, 256, 128))
    nq = t // tq
    y_p = pl.pallas_call(
        _xa_prompt_body,
        grid=(b, nq),
        in_specs=[_proj_spec(tq, COL_XA_Q, lambda bi, qi: bi * nq + qi),
                  pl.BlockSpec((n_mem, HW), lambda bi, qi: (bi, 0)),
                  pl.BlockSpec((n_mem, HW), lambda bi, qi: (bi, 1))],
        out_specs=pl.BlockSpec((tq, HW), lambda bi, qi: (bi * nq + qi, 0)),
        out_shape=jax.ShapeDtypeStruct((n_p, HW), BF16),
        compiler_params=_params(("arbitrary", "arbitrary")),
        name="xattn_prompt",
    )(proj, kv_p, kv_p)

    rows = SAMPLE_BB * ts
    base = n_p // rows
    cache_k = cache_k.reshape(DEPTH, nb, n_mem * HEADS, DH)
    cache_v = cache_v.reshape(DEPTH, nb, n_mem * HEADS, DH)
    kv_spec = pl.BlockSpec((None, SAMPLE_BB, n_mem * HEADS, DH), lambda i: (layer, i, 0, 0))
    y_s = pl.pallas_call(
        functools.partial(_xa_sample_body, ts=ts),
        grid=(nb // SAMPLE_BB,),
        in_specs=[_proj_spec(rows, COL_XA_Q, lambda i: base + i), kv_spec, kv_spec],
        out_specs=pl.BlockSpec((rows, HW), lambda i: (i, 0)),
        out_shape=jax.ShapeDtypeStruct((nb * ts, HW), BF16),
        compiler_params=_params(("arbitrary",)),
        name="xattn_sample",
    )(proj, cache_k, cache_v)
    return (y_p, y_s)


def _layer_norm(tv, g, b):
    mu = jnp.mean(tv, axis=-1, keepdims=True)
    var = jnp.mean(jnp.square(tv - mu), axis=-1, keepdims=True)
    return (tv - mu) * lax.rsqrt(var + LN_EPS) * g + b


def _merge_body(yrp_ref, yrs_ref, yhp_ref, yhs_ref, yxp_ref, yxs_ref, g0a, g0b, g1a, g1b, g2a, g2b, x_ref,
                wr_ref, wh_ref, wx_ref, wo_ref, lg_ref, lb_ref,
                x1_ref, x1b_ref, x1t_ref, wr_s, wh_s, wx_s, wo_s, *, prompt_tiles):
    @pl.when(pl.program_id(0) == 0)
    def _():
        wr_s[...] = wr_ref[...].astype(BF16)
        wh_s[...] = wh_ref[...].astype(BF16)
        wx_s[...] = wx_ref[...].astype(BF16)
        wo_s[...] = wo_ref[...].astype(BF16)

    is_prompt = pl.program_id(0) < prompt_tiles

    def branch(yp_ref, ys_ref, w_s, ga, gb):
        y = jnp.where(is_prompt, yp_ref[...], ys_ref[...])
        gate = jax.nn.sigmoid(jnp.concatenate([ga[...], gb[...]], axis=-1))
        return gate * jnp.dot(y, w_s[...], preferred_element_type=F32)

    m = (branch(yrp_ref, yrs_ref, wr_s, g0a, g0b) + branch(yhp_ref, yhs_ref, wh_s, g1a, g1b)
         + branch(yxp_ref, yxs_ref, wx_s, g2a, g2b))
    hmix = jnp.dot(m.astype(BF16), wo_s[...], preferred_element_type=F32)
    x1 = _layer_norm(DN_ALPHA * x_ref[...] + hmix, lg_ref[...], lb_ref[...])
    x1_ref[...] = x1
    x1b_ref[...] = x1.astype(BF16)
    tm = x1.shape[0]
    for s in range(ROW_TILES):
        x1t_ref[pl.ds(s, tm, stride=ROW_TILES), :] = x1[:, s * LANES:(s + 1) * LANES]


def _merge(ys, proj, x, w_up_ret, w_up_hgrn, w_up_xattn, w_out, ln_g, ln_b, layer):
    nt = x.shape[0]
    tm = _pick(nt, (256, 128))
    n_p = ys[0].shape[0]
    assert n_p % tm == 0 and ys[1].shape[0] % tm == 0
    p_tiles = n_p // tm
    row = lambda i: (i, 0)
    y_specs = [pl.BlockSpec((tm, HW), lambda i: (jnp.minimum(i, p_tiles - 1), 0)),
               pl.BlockSpec((tm, HW), lambda i: (jnp.maximum(i - p_tiles, 0), 0))] * 3
    wspec = lambda k: pl.BlockSpec((None, k, D_MODEL), lambda i: (layer, 0, 0))
    vec = pl.BlockSpec((None, 1, D_MODEL), lambda i: (layer, 0, 0))
    gate_specs = [pl.BlockSpec((tm, HW), lambda i, c=c: (i, COL_GATES + c)) for c in range(6)]
    return pl.pallas_call(
        functools.partial(_merge_body, prompt_tiles=p_tiles),
        grid=(nt // tm,),
        in_specs=y_specs + gate_specs + [pl.BlockSpec((tm, D_MODEL), row),
                  wspec(HW), wspec(HW), wspec(HW), wspec(D_MODEL), vec, vec],
        out_specs=[pl.BlockSpec((tm, D_MODEL), row), pl.BlockSpec((tm, D_MODEL), row),
                   pl.BlockSpec((tm * ROW_TILES, LANES), row)],
        out_shape=[jax.ShapeDtypeStruct((nt, D_MODEL), F32),
                   jax.ShapeDtypeStruct((nt, D_MODEL), BF16),
                   jax.ShapeDtypeStruct((nt * ROW_TILES, LANES), F32)],
        scratch_shapes=[pltpu.VMEM((HW, D_MODEL), BF16)] * 3 + [pltpu.VMEM((D_MODEL, D_MODEL), BF16)],
        compiler_params=_params(("arbitrary",)),
        name="merge_out_ln1",
    )(*ys, *([proj] * 6), x, w_up_ret, w_up_hgrn, w_up_xattn, w_out, ln_g, ln_b)


def _router_body(x_ref, wt_ref, b_ref, eidx_ref, wn_ref):
    tm = x_ref.shape[0]
    logits = lax.dot_general(wt_ref[...], x_ref[...], (((1,), (1,)), ((), ())),
                             precision=HIGHEST, preferred_element_type=F32)
    s = jax.nn.sigmoid(logits)
    sel = s + b_ref[...]
    neg = -jnp.inf
    groups = [sel[g * GROUP_SIZE:(g + 1) * GROUP_SIZE, :] for g in range(N_GROUPS)]
    ie = lax.broadcasted_iota(I32, (GROUP_SIZE, tm), 0).astype(F32)
    rows = []
    for blk in groups:
        m1 = jnp.max(blk, axis=0, keepdims=True)
        first = jnp.min(jnp.where(blk == m1, ie, float(GROUP_SIZE)), axis=0, keepdims=True)
        rows.append(m1 + jnp.max(jnp.where(ie == first, neg, blk), axis=0, keepdims=True))
    gscore = jnp.concatenate(rows, axis=0)
    ig = lax.broadcasted_iota(I32, gscore.shape, 0).astype(F32)
    gmask = jnp.zeros(gscore.shape, F32)
    for _ in range(TOPK_GROUPS):
        m = jnp.max(gscore, axis=0, keepdims=True)
        gi = jnp.min(jnp.where(gscore == m, ig, float(N_GROUPS)), axis=0, keepdims=True)
        hit = ig == gi
        gmask = jnp.where(hit, 1.0, gmask)
        gscore = jnp.where(hit, neg, gscore)
    masked = jnp.concatenate([jnp.where(gmask[g:g + 1, :] > 0.5, blk, neg)
                              for g, blk in enumerate(groups)], axis=0)
    ix = lax.broadcasted_iota(I32, masked.shape, 0).astype(F32)
    idxs, ws = [], []
    for _ in range(TOP_K):
        m = jnp.max(masked, axis=0, keepdims=True)
        ei = jnp.min(jnp.where(masked == m, ix, float(N_EXPERTS)), axis=0, keepdims=True)
        hit = ix == ei
        idxs.append(ei)
        ws.append(jnp.sum(jnp.where(hit, s, 0.0), axis=0, keepdims=True))
        masked = jnp.where(hit, neg, masked)
    wsum = ws[0]
    for w in ws[1:]:
        wsum = wsum + w
    pad = [jnp.zeros((1, tm), F32)] * (SUBLANES - TOP_K)
    eidx_ref[...] = jnp.concatenate(idxs + pad, axis=0).astype(I32)
    wn_ref[...] = jnp.concatenate([w / wsum * ROUTED_SCALE for w in ws] + pad, axis=0)


def _router(x1, w_router_t, b_router, layer):
    nt = x1.shape[0]
    tm = _pick(nt, (512, 256, 128))
    return pl.pallas_call(
        _router_body,
        grid=(nt // tm,),
        in_specs=[pl.BlockSpec((tm, D_MODEL), lambda i: (i, 0)),
                  pl.BlockSpec((None, N_EXPERTS, D_MODEL), lambda i: (layer, 0, 0)),
                  pl.BlockSpec((None, N_EXPERTS, 1), lambda i: (layer, 0, 0))],
        out_specs=[pl.BlockSpec((SUBLANES, tm), lambda i: (0, i))] * 2,
        out_shape=[jax.ShapeDtypeStruct((SUBLANES, nt), I32),
                   jax.ShapeDtypeStruct((SUBLANES, nt), F32)],
        compiler_params=_params(("arbitrary",)),
        name="moe_router",
    )(x1, w_router_t, b_router)


def _positions_body(eidx_ref, pos_ref, cnt_ref, off_ref, base_scr, off_scr):
    phase = pl.program_id(0)
    i = pl.program_id(1)
    tp = eidx_ref.shape[1]
    ix = lax.broadcasted_iota(I32, (N_EXPERTS, tp), 0)
    eidx = eidx_ref[...]
    member = jnp.zeros((N_EXPERTS, tp), F32)
    for k in range(TOP_K):
        member = member + (ix == eidx[k:k + 1, :]).astype(F32)
    tile_cnt = jnp.sum(member, axis=1, keepdims=True)

    @pl.when((phase == 0) & (i == 0))
    def _():
        base_scr[...] = jnp.zeros_like(base_scr)

    @pl.when((phase == 1) & (i == 0))
    def _():
        cnt = base_scr[...]
        er = lax.broadcasted_iota(I32, (N_EXPERTS, N_EXPERTS), 0)
        ec = lax.broadcasted_iota(I32, (N_EXPERTS, N_EXPERTS), 1)
        off = jnp.dot((ec < er).astype(F32), cnt, precision=HIGHEST, preferred_element_type=F32)
        off_scr[...] = off
        cnt_ref[...] = cnt
        off_ref[...] = off
        base_scr[...] = jnp.zeros_like(base_scr)

    @pl.when(phase == 1)
    def _():
        tr = lax.broadcasted_iota(I32, (tp, tp), 0)
        tc = lax.broadcasted_iota(I32, (tp, tp), 1)
        before = jnp.dot(member.astype(BF16), (tr < tc).astype(BF16), preferred_element_type=F32)
        where_to = before + (off_scr[...] + base_scr[...])[:, 0:1]
        rows = [jnp.sum(jnp.where(ix == eidx[k:k + 1, :], where_to, 0.0), axis=0, keepdims=True)
                for k in range(TOP_K)]
        rows += [jnp.zeros((1, tp), F32)] * (SUBLANES - TOP_K)
        pos_ref[...] = jnp.concatenate(rows, axis=0).astype(I32)

    base_scr[...] = base_scr[...] + tile_cnt


def _positions(eidx):
    nt = eidx.shape[1]
    tp = _pick(nt, (512, 256, 128))
    const = lambda p, i: (0, 0)
    return pl.pallas_call(
        _positions_body,
        grid=(2, nt // tp),
        in_specs=[pl.BlockSpec((SUBLANES, tp), lambda p, i: (0, i))],
        out_specs=[pl.BlockSpec((SUBLANES, tp), lambda p, i: (0, i * p)),
                   pl.BlockSpec((N_EXPERTS, LANES), const), pl.BlockSpec((N_EXPERTS, LANES), const)],
        out_shape=[jax.ShapeDtypeStruct((SUBLANES, nt), I32),
                   jax.ShapeDtypeStruct((N_EXPERTS, LANES), F32),
                   jax.ShapeDtypeStruct((N_EXPERTS, LANES), F32)],
        scratch_shapes=[pltpu.VMEM((N_EXPERTS, LANES), F32), pltpu.VMEM((N_EXPERTS, LANES), F32)],
        compiler_params=_params(("arbitrary", "arbitrary")),
        name="moe_positions",
    )(eidx)


T_TILE, T_EXPERT, T_LO, T_HI, T_FRESH, T_NEWEXP = range(6)


def _table_body(cnt_ref, off_ref, tbl_ref):
    te = float(EXPERT_TILE)
    n = tbl_ref.shape[1]
    cnt = cnt_ref[...]
    off = off_ref[...]
    first = jnp.floor(off * (1.0 / te))
    last = jnp.floor((off + cnt - 1.0) * (1.0 / te))
    nst = jnp.where(cnt > 0.0, last - first + 1.0, 0.0)
    er = lax.broadcasted_iota(I32, (N_EXPERTS, N_EXPERTS), 0)
    ec = lax.broadcasted_iota(I32, (N_EXPERTS, N_EXPERTS), 1)
    s_end = jnp.dot((ec <= er).astype(F32), nst, precision=HIGHEST, preferred_element_type=F32)
    s_beg = s_end - nst
    total = s_end[N_EXPERTS - 1:N_EXPERTS, 0:1]
    sidx = lax.broadcasted_iota(I32, (1, n), 1).astype(F32)
    s = jnp.minimum(sidx, total - 1.0)
    e_s = jnp.sum((s_end[:, 0:1] <= s).astype(F32), axis=0, keepdims=True)
    hot = lax.broadcasted_iota(I32, (N_EXPERTS, n), 0).astype(F32) == e_s

    def pick(col):
        return jnp.sum(jnp.where(hot, col[:, 0:1], 0.0), axis=0, keepdims=True)

    tile = pick(first) + s - pick(s_beg)
    valid = sidx < total
    o, c = pick(off), pick(cnt)
    lo = jnp.where(valid, jnp.maximum(o, tile * te), 0.0)
    hi = jnp.where(valid, jnp.minimum(o + c, (tile + 1.0) * te), 0.0)
    head = sidx == 0.0
    fresh = jnp.where((tile != pltpu.roll(tile, 1, 1)) | head, 1.0, 0.0)
    newexp = jnp.where((e_s != pltpu.roll(e_s, 1, 1)) | head, 1.0, 0.0)
    pad = [jnp.zeros((1, n), F32)] * (SUBLANES - 6)
    tbl_ref[...] = jnp.concatenate([tile, e_s, lo, hi, fresh, newexp] + pad, axis=0).astype(I32)


def _step_table(cnt, off, n_rows):
    n_steps = n_rows // EXPERT_TILE + N_EXPERTS
    width = -(-n_steps // LANES) * LANES
    tbl = pl.pallas_call(
        _table_body,
        out_shape=jax.ShapeDtypeStruct((SUBLANES, width), I32),
        name="moe_step_table",
    )(cnt, off)
    return tbl, n_steps


def _dispatch_body(pos_ref, xt_ref, xs_ref, pos_s, sem_p, sem):
    td = pos_ref.shape[1]
    cp = pltpu.make_async_copy(pos_ref, pos_s, sem_p)
    cp.start()
    cp.wait()

    def issue(g, carry):
        for u in range(ISSUE_UNROLL):
            r = g * ISSUE_UNROLL + u
            for k in range(TOP_K):
                pltpu.make_async_copy(xt_ref.at[r], xs_ref.at[pos_s[k, r]], sem).start(priority=k % 2)
        return carry

    lax.fori_loop(0, td // ISSUE_UNROLL, issue, 0)
    for k in range(TOP_K):
        pltpu.make_async_copy(xt_ref, xs_ref.at[pl.ds(0, td)], sem).wait()


def _dispatch(pos, x1t):
    nt = x1t.shape[0]
    td = _pick(nt, (512, 256, 128))
    return pl.pallas_call(
        _dispatch_body,
        grid=(nt // td,),
        in_specs=[pl.BlockSpec((SUBLANES, td), lambda i: (0, i)),
                  pl.BlockSpec((td, ROW_TILES, LANES), lambda i: (i, 0, 0))],
        out_specs=pl.BlockSpec(memory_space=pl.ANY),
        out_shape=jax.ShapeDtypeStruct((nt * TOP_K, ROW_TILES, LANES), F32),
        scratch_shapes=[pltpu.SMEM((SUBLANES, td), I32), pltpu.SemaphoreType.DMA, pltpu.SemaphoreType.DMA],
        compiler_params=_params(("arbitrary",)),
        name="moe_dispatch",
    )(pos, x1t)


def _experts_body(tbl_ref, xs_ref, wg_ref, wu_ref, wd_ref, ye_ref, wg_s, wu_s, wd_s):
    s = pl.program_id(0)
    te = EXPERT_TILE
    lo = tbl_ref[T_LO, s]
    hi = tbl_ref[T_HI, s]

    @pl.when(tbl_ref[T_NEWEXP, s] == 1)
    def _():
        wg_s[...] = wg_ref[...].astype(BF16)
        wu_s[...] = wu_ref[...].astype(BF16)
        wd_s[...] = wd_ref[...].astype(BF16)

    @pl.when(tbl_ref[T_FRESH, s] == 1)
    def _():
        ye_ref[...] = jnp.zeros_like(ye_ref)

    @pl.when(hi > lo)
    def _():
        x = jnp.concatenate([xs_ref[pl.ds(t, te, stride=ROW_TILES), :].astype(BF16)
                             for t in range(ROW_TILES)], axis=-1)
        g = jnp.dot(x, wg_s[...], preferred_element_type=F32)
        u = jnp.dot(x, wu_s[...], preferred_element_type=F32)
        y = jnp.dot((_silu(g) * u).astype(BF16), wd_s[...], preferred_element_type=F32)
        row = tbl_ref[T_TILE, s] * te + lax.broadcasted_iota(I32, (te, LANES), 0)
        mine = (row >= lo) & (row < hi)
        for t in range(ROW_TILES):
            sl = pl.ds(t, te, stride=ROW_TILES)
            ye_ref[sl, :] = jnp.where(mine, y[:, t * LANES:(t + 1) * LANES], ye_ref[sl, :])


def _experts(tbl, n_steps, xs, w_gate, w_up, w_down, layer):
    n_rows = xs.shape[0] // ROW_TILES
    te = EXPERT_TILE
    tile_map = lambda s, tbl: (tbl[T_TILE, s], 0)
    w_map = lambda s, tbl: (layer, tbl[T_EXPERT, s], 0, 0)
    w_in_spec = pl.BlockSpec((None, None, D_MODEL, D_EXPERT), w_map)
    w_dn_spec = pl.BlockSpec((None, None, D_EXPERT, D_MODEL), w_map)
    return pl.pallas_call(
        _experts_body,
        grid_spec=pltpu.PrefetchScalarGridSpec(
            num_scalar_prefetch=1,
            grid=(n_steps,),
            in_specs=[pl.BlockSpec((te * ROW_TILES, LANES), tile_map), w_in_spec, w_in_spec, w_dn_spec],
            out_specs=pl.BlockSpec((te * ROW_TILES, LANES), tile_map),
            scratch_shapes=[pltpu.VMEM((D_MODEL, D_EXPERT), BF16), pltpu.VMEM((D_MODEL, D_EXPERT), BF16),
                            pltpu.VMEM((D_EXPERT, D_MODEL), BF16)]),
        out_shape=jax.ShapeDtypeStruct((n_rows * ROW_TILES, LANES), F32),
        compiler_params=_params(("arbitrary",)),
        name="moe_experts",
    )(tbl, xs, w_gate, w_up, w_down)


def _combine_body(pos_ref, wn_ref, ye_ref, x1_ref, x1b_ref, wsg_ref, wsu_ref, wsd_ref, lg_ref, lb_ref,
                  x2_ref, x2b_ref, pos_s, wn_s, buf, acc, wsg_s, wsu_s, wsd_s, sem_p, sem_w, sem):
    tc = pos_ref.shape[1]

    @pl.when(pl.program_id(0) == 0)
    def _():
        wsg_s[...] = wsg_ref[...].astype(BF16)
        wsu_s[...] = wsu_ref[...].astype(BF16)
        wsd_s[...] = wsd_ref[...].astype(BF16)

    cp = pltpu.make_async_copy(pos_ref, pos_s, sem_p)
    cw = pltpu.make_async_copy(wn_ref, wn_s, sem_w)
    cp.start()
    cw.start()
    cp.wait()
    cw.wait()

    def issue(g, carry):
        for u in range(ISSUE_UNROLL):
            r = g * ISSUE_UNROLL + u
            for k in range(TOP_K):
                pltpu.make_async_copy(ye_ref.at[pos_s[k, r]], buf.at[k, r], sem).start(priority=k % 2)
        return carry

    lax.fori_loop(0, tc // ISSUE_UNROLL, issue, 0)
    xb = x1b_ref[...]
    hs = _silu(jnp.dot(xb, wsg_s[...], preferred_element_type=F32)) * jnp.dot(xb, wsu_s[...], preferred_element_type=F32)
    shared = jnp.dot(hs.astype(BF16), wsd_s[...], preferred_element_type=F32)
    for k in range(TOP_K):
        pltpu.make_async_copy(ye_ref.at[pl.ds(0, tc)], buf.at[k], sem).wait()

    def reduce(g, carry):
        for u in range(ISSUE_UNROLL):
            r = g * ISSUE_UNROLL + u
            tot = buf[0, r] * wn_s[0, r]
            for k in range(1, TOP_K):
                tot = tot + buf[k, r] * wn_s[k, r]
            acc[pl.ds(pl.multiple_of(r * ROW_TILES, ROW_TILES), ROW_TILES), :] = tot
        return carry

    lax.fori_loop(0, tc // ISSUE_UNROLL, reduce, 0)
    routed = jnp.concatenate([acc[pl.ds(t, tc, stride=ROW_TILES), :] for t in range(ROW_TILES)], axis=-1)
    x2 = _layer_norm(DN_ALPHA * x1_ref[...] + (routed + shared), lg_ref[...], lb_ref[...])
    x2_ref[...] = x2
    x2b_ref[...] = x2.astype(BF16)


def _combine(pos, wn, ye, x1, x1b, w_s_gate, w_s_up, w_s_down, ln_g, ln_b, layer):
    nt = x1.shape[0]
    tc = _pick(nt, (256, 128))
    d_sh = w_s_gate.shape[2]
    row = lambda i: (i, 0)
    vec = pl.BlockSpec((None, 1, D_MODEL), lambda i: (layer, 0, 0))
    return pl.pallas_call(
        _combine_body,
        grid=(nt // tc,),
        in_specs=[pl.BlockSpec((SUBLANES, tc), lambda i: (0, i)), pl.BlockSpec((SUBLANES, tc), lambda i: (0, i)),
                  pl.BlockSpec(memory_space=pl.ANY),
                  pl.BlockSpec((tc, D_MODEL), row), pl.BlockSpec((tc, D_MODEL), row),
                  pl.BlockSpec((None, D_MODEL, d_sh), lambda i: (layer, 0, 0)),
                  pl.BlockSpec((None, D_MODEL, d_sh), lambda i: (layer, 0, 0)),
                  pl.BlockSpec((None, d_sh, D_MODEL), lambda i: (layer, 0, 0)), vec, vec],
        out_specs=[pl.BlockSpec((tc, D_MODEL), row), pl.BlockSpec((tc, D_MODEL), row)],
        out_shape=[jax.ShapeDtypeStruct((nt, D_MODEL), F32), jax.ShapeDtypeStruct((nt, D_MODEL), BF16)],
        scratch_shapes=[pltpu.SMEM((SUBLANES, tc), I32), pltpu.SMEM((SUBLANES, tc), F32),
                        pltpu.VMEM((TOP_K, tc, ROW_TILES, LANES), F32), pltpu.VMEM((tc * ROW_TILES, LANES), F32),
                        pltpu.VMEM((D_MODEL, d_sh), BF16), pltpu.VMEM((D_MODEL, d_sh), BF16),
                        pltpu.VMEM((d_sh, D_MODEL), BF16),
                        pltpu.SemaphoreType.DMA, pltpu.SemaphoreType.DMA, pltpu.SemaphoreType.DMA],
        compiler_params=_params(("arbitrary",)),
        name="moe_combine_ln2",
    )(pos, wn, ye, x1, x1b, w_s_gate, w_s_up, w_s_down, ln_g, ln_b)


def _rope_tables(t, pos0):
    inv = 1.0 / (ROPE_BASE ** (jnp.arange(0, DH, 2, dtype=F32) / DH))
    ang = (jnp.arange(t, dtype=F32) + pos0)[:, None] * inv[None, :]
    cos, sin = jnp.cos(ang), jnp.sin(ang)
    return jnp.concatenate([cos, cos], axis=-1), jnp.concatenate([-sin, sin], axis=-1)


def kernel(x_prompt, x_sample, mem_prompt, state_ret, state_hgrn, cache_mem_k, cache_mem_v, w_in, w_up_ret, w_up_hgrn, w_up_xattn, w_out, w_mem_kv, ret_norm_g, hgrn_norm_g, lb_logits, ln1_g, ln1_b, ln2_g, ln2_b, w_router, b_router, w_e_gate, w_e_up, w_e_down, w_s_gate, w_s_up, w_s_down):
    b, t, d = x_prompt.shape
    nb, ts, _ = x_sample.shape
    n_mem = mem_prompt.shape[1]
    assert d == D_MODEL and t % RET_CHUNK == 0 and nb % SAMPLE_BB == 0
    assert ts & (ts - 1) == 0 and HG_CHUNK % ts == 0 and RET_CHUNK % ts == 0
    n_p, n_s = b * t, nb * ts
    nt = n_p + n_s
    assert n_p % (SAMPLE_BB * ts) == 0 and (nt * TOP_K) % EXPERT_TILE == 0

    lb_cum = jnp.cumsum(jax.nn.softmax(lb_logits.astype(F32), axis=0), axis=0)
    lbs = lb_cum - lb_cum[0:1]
    lbt = jnp.stack([jnp.log(lbs), jnp.log1p(-lbs), 1.0 - lbs] + [jnp.zeros_like(lbs)] * (SUBLANES - 3), axis=1)
    gl = jnp.broadcast_to(jnp.log1p(-jnp.exp2(-5.0 - jnp.arange(HEADS, dtype=F32)))[:, None], (HEADS, DH))
    cos_p, sin_p = _rope_tables(t, 0)
    cos_s, sin_s = _rope_tables(ts, PAST_LEN)
    cos_s, sin_s = jnp.tile(cos_s, (SAMPLE_BB, 1)), jnp.tile(sin_s, (SAMPLE_BB, 1))
    vec3 = lambda a: a.reshape(DEPTH, 1, -1)
    w_router_t = jnp.swapaxes(w_router, 1, 2)
    b_router3 = b_router.reshape(DEPTH, N_EXPERTS, 1)
    mem2 = mem_prompt.reshape(b * n_mem, d)

    x = jnp.concatenate([x_prompt.reshape(n_p, d), x_sample.reshape(n_s, d)], axis=0)
    xb = x.astype(BF16)
    tm_proj = _pick(nt, (1024, 512, 128))
    outs = {k: [] for k in ("ret_p", "hg_p", "mk", "mv", "ret_s", "hg_s")}
    for l in range(DEPTH):
        proj = _matmul(xb, w_in, l, tm_proj, 1280)
        kv_p = _matmul(mem2, w_mem_kv, l, _pick(b * n_mem, (1024, 512, 256)), 2 * HW)
        yr, ret_p, ret_s = _retention(proj, state_ret, l, cos_p, sin_p, cos_s, sin_s, gl,
                                      vec3(ret_norm_g), b, t, nb, ts)
        yh, hg_p, hg_s = _hgrn(proj, state_hgrn, l, lbt, vec3(hgrn_norm_g), b, t, nb, ts)
        yx = _cross_attention(proj, kv_p, cache_mem_k, cache_mem_v, l, b, t, nb, ts)
        x1, x1b, x1t = _merge((*yr, *yh, *yx), proj, x, w_up_ret, w_up_hgrn, w_up_xattn, w_out,
                              vec3(ln1_g), vec3(ln1_b), l)
        eidx, wn = _router(x1, w_router_t, b_router3, l)
        pos, cnt, off = _positions(eidx)
        tbl, n_steps = _step_table(cnt, off, nt * TOP_K)
        xs = _dispatch(pos, x1t.reshape(nt, ROW_TILES, LANES))
        ye = _experts(tbl, n_steps, xs.reshape(-1, LANES), w_e_gate, w_e_up, w_e_down, l)
        ye = ye.reshape(-1, ROW_TILES, LANES)
        x, xb = _combine(pos, wn, ye, x1, x1b, w_s_gate, w_s_up, w_s_down, vec3(ln2_g), vec3(ln2_b), l)
        outs["ret_p"].append(ret_p)
        outs["hg_p"].append(hg_p)
        outs["mk"].append(kv_p[:, :HW].reshape(b, n_mem, HEADS, DH))
        outs["mv"].append(kv_p[:, HW:].reshape(b, n_mem, HEADS, DH))
        outs["ret_s"].append(ret_s)
        outs["hg_s"].append(hg_s)
    return (x[:n_p].reshape(b, t, d), x[n_p:].reshape(nb, ts, d),
            jnp.stack(outs["ret_p"]), jnp.stack(outs["hg_p"]), jnp.stack(outs["mk"]), jnp.stack(outs["mv"]),
            jnp.stack(outs["ret_s"]), jnp.stack(outs["hg_s"]))
```

```python
import functools

import jax
import jax.numpy as jnp
from jax import lax
from jax.experimental import pallas as pl
from jax.experimental.pallas import tpu as pltpu

F32 = jnp.float32
BF16 = jnp.bfloat16
I32 = jnp.int32
HIGHEST = lax.Precision.HIGHEST

D_MODEL = 1024
DEPTH = 2
PAST_LEN = 16384
HEADS = 4
DH = 128
HW = HEADS * DH
RET_CHUNK = 128
HG_CHUNK = 16
ROPE_BASE = 10000.0
N_EXPERTS = 64
N_GROUPS = 8
GROUP_SIZE = N_EXPERTS // N_GROUPS
TOPK_GROUPS = 4
TOP_K = 6
D_EXPERT = 256
ROUTED_SCALE = 2.5
LN_EPS = 1e-5
DN_ALPHA = (2 * DEPTH) ** 0.25
N_IN = 9 * HW + 3 * D_MODEL
COL_RET_Q, COL_RET_K, COL_RET_V, COL_RET_G = 0, 1, 2, 3
COL_HG_Q, COL_HG_F, COL_HG_I, COL_HG_G = 4, 5, 6, 7
COL_XA_Q = 8
COL_GATES = 9
LANES = 128
SUBLANES = 8
ROW_TILES = D_MODEL // LANES
SAMPLE_BB = 8
EXPERT_TILES = (512, 256)
ISSUE_UNROLL = 8
VMEM_LIMIT = 56 * 1024 * 1024


def _params(sem):
    return pltpu.CompilerParams(dimension_semantics=sem, vmem_limit_bytes=VMEM_LIMIT)


def _bdot(a, b):
    return jnp.dot(a.astype(BF16), b.astype(BF16), preferred_element_type=F32)


def _bdot_nt(a, b):
    return lax.dot_general(a.astype(BF16), b.astype(BF16), (((1,), (1,)), ((), ())),
                           preferred_element_type=F32)


def _bdot_tn(a, b):
    return lax.dot_general(a.astype(BF16), b.astype(BF16), (((0,), (0,)), ((), ())),
                           preferred_element_type=F32)


def _silu(x):
    return x * jax.nn.sigmoid(x)


def _pick(n, prefs):
    for p in prefs:
        if n % p == 0:
            return p
    raise ValueError(f"no tile for {n}")


def _mm_body(x_ref, w_ref, o_ref, wb_ref):
    @pl.when(pl.program_id(1) == 0)
    def _():
        wb_ref[...] = w_ref[...].astype(BF16)

    o_ref[...] = jnp.dot(x_ref[...].astype(BF16), wb_ref[...],
                         preferred_element_type=F32).astype(o_ref.dtype)


def _matmul(x, w, layer, tm, tn):
    m, k = x.shape
    n = w.shape[2]
    return pl.pallas_call(
        _mm_body,
        grid=(n // tn, m // tm),
        in_specs=[pl.BlockSpec((tm, k), lambda j, i: (i, 0)),
                  pl.BlockSpec((None, k, tn), lambda j, i: (layer, 0, j))],
        out_specs=pl.BlockSpec((tm, tn), lambda j, i: (i, j)),
        out_shape=jax.ShapeDtypeStruct((m, n), F32),
        scratch_shapes=[pltpu.VMEM((k, tn), BF16)],
        compiler_params=_params(("arbitrary", "arbitrary")),
        name="dense_matmul",
    )(x, w)


def _rotary(x, cos, sin_signed):
    return x * cos + pltpu.roll(x, DH // 2, 1) * sin_signed


def _group_norm_gate(o, gain, gate):
    mu = jnp.mean(o, axis=-1, keepdims=True)
    var = jnp.mean(jnp.square(o - mu), axis=-1, keepdims=True)
    return (o - mu) * lax.rsqrt(var + LN_EPS) * gain * _silu(gate)


def _ret_prompt_body(q_ref, k_ref, v_ref, g_ref, cos_ref, sin_ref, gl_ref, gain_ref,
                     y_ref, st_ref, s_scr):
    c = pl.program_id(1)

    @pl.when(c == 0)
    def _():
        s_scr[...] = jnp.zeros_like(s_scr)

    ch = RET_CHUNK
    cos = cos_ref[...]
    sin = sin_ref[...]
    ri = lax.broadcasted_iota(I32, (ch, ch), 0)
    ci = lax.broadcasted_iota(I32, (ch, ch), 1)
    rel = (ri - ci).astype(F32)
    idx = lax.broadcasted_iota(I32, (ch, DH), 0).astype(F32)
    for h in range(HEADS):
        sl = slice(h * DH, (h + 1) * DH)
        gl = gl_ref[h:h + 1, :]
        qr = _rotary(q_ref[:, sl], cos, sin)
        kr = _rotary(k_ref[:, sl], cos, sin) * (DH ** -0.5)
        v = v_ref[:, sl]
        intra = jnp.where(rel >= 0, jnp.exp(gl * rel), 0.0)
        att = _bdot_nt(qr, kr) * intra
        s = s_scr[h]
        o = _bdot(att, v) + _bdot(qr, s) * jnp.exp(gl * (idx + 1.0))
        s_scr[h] = s * jnp.exp(gl * float(ch)) + _bdot_tn(kr * jnp.exp(gl * (ch - 1.0 - idx)), v)
        y_ref[:, sl] = _group_norm_gate(o, gain_ref[:, sl], g_ref[:, sl]).astype(BF16)

    @pl.when(c == pl.num_programs(1) - 1)
    def _():
        st_ref[0] = s_scr[...]


def _ret_sample_body(q_ref, k_ref, v_ref, g_ref, cos_ref, sin_ref, gl_ref, gain_ref, sin_ref_state,
                     y_ref, st_ref, *, ts):
    rows = SAMPLE_BB * ts
    shift = ts.bit_length() - 1
    cos = cos_ref[...]
    sin = sin_ref[...]
    ri = lax.broadcasted_iota(I32, (rows, rows), 0)
    ci = lax.broadcasted_iota(I32, (rows, rows), 1)
    rel = (ri - ci).astype(F32)
    mask = ((ri >> shift) == (ci >> shift)) & (ri >= ci)
    idx = (lax.broadcasted_iota(I32, (rows, DH), 0) & (ts - 1)).astype(F32)
    for h in range(HEADS):
        sl = slice(h * DH, (h + 1) * DH)
        gl = gl_ref[h:h + 1, :]
        qr = _rotary(q_ref[:, sl], cos, sin)
        kr = _rotary(k_ref[:, sl], cos, sin) * (DH ** -0.5)
        v = v_ref[:, sl]
        intra = jnp.where(mask, jnp.exp(gl[:, :rows] * rel), 0.0)
        o_intra = _bdot(_bdot_nt(qr, kr) * intra, v)
        q_dec = jnp.exp(gl * (idx + 1.0))
        kd = kr * jnp.exp(gl * (ts - 1.0 - idx))
        c_dec = jnp.exp(gl * float(ts))
        outs = []
        for j in range(SAMPLE_BB):
            rs = slice(j * ts, (j + 1) * ts)
            s = sin_ref_state[j, h]
            outs.append(o_intra[rs] + _bdot(qr[rs], s) * q_dec[rs])
            new_state = s * c_dec + _bdot_tn(kd[rs], v[rs])
            for slot in range(st_ref.shape[0]):
                st_ref[slot, j, h] = new_state
        o = jnp.concatenate(outs, axis=0)
        y_ref[:, sl] = _group_norm_gate(o, gain_ref[:, sl], g_ref[:, sl]).astype(BF16)


def _proj_spec(rows, col, row_map):
    return pl.BlockSpec((rows, HW), lambda *a: (row_map(*a), col))


def _sample_state_call(body, grid, in_specs, args, y_shape, y_spec, layer, nb, prev, name):
    st_shape = jax.ShapeDtypeStruct((DEPTH, nb, HEADS, DH, DH), F32)
    slots = DEPTH if prev is None else 1
    st_spec = pl.BlockSpec((slots, SAMPLE_BB, HEADS, DH, DH), lambda i: (layer, i, 0, 0, 0))
    aliases = {}
    if prev is not None:
        n_in = len(args)
        inner = body
        body = lambda *refs: inner(*refs[:n_in], *refs[n_in + 1:])
        in_specs = in_specs + [pl.BlockSpec(memory_space=pl.ANY)]
        args = args + (prev,)
        aliases = {n_in: 1}
    return pl.pallas_call(
        body, grid=grid, in_specs=in_specs, out_specs=[y_spec, st_spec], out_shape=[y_shape, st_shape],
        input_output_aliases=aliases, compiler_params=_params(("arbitrary",)), name=name,
    )(*args)


def _retention(proj, state, layer, cos_p, sin_p, cos_s, sin_s, gl, gain, b, t, nb, ts, prev_s):
    n_p = b * t
    nc = t // RET_CHUNK
    prow = lambda bi, c: bi * nc + c
    const2 = lambda *a: (0, 0)
    y_p, st_p = pl.pallas_call(
        _ret_prompt_body,
        grid=(b, nc),
        in_specs=[_proj_spec(RET_CHUNK, COL_RET_Q, prow), _proj_spec(RET_CHUNK, COL_RET_K, prow),
                  _proj_spec(RET_CHUNK, COL_RET_V, prow), _proj_spec(RET_CHUNK, COL_RET_G, prow),
                  pl.BlockSpec((RET_CHUNK, DH), lambda bi, c: (c, 0)),
                  pl.BlockSpec((RET_CHUNK, DH), lambda bi, c: (c, 0)),
                  pl.BlockSpec((HEADS, DH), const2),
                  pl.BlockSpec((None, 1, HW), lambda bi, c: (layer, 0, 0))],
        out_specs=[pl.BlockSpec((RET_CHUNK, HW), lambda bi, c: (prow(bi, c), 0)),
                   pl.BlockSpec((1, HEADS, DH, DH), lambda bi, c: (bi, 0, 0, 0))],
        out_shape=[jax.ShapeDtypeStruct((n_p, HW), BF16),
                   jax.ShapeDtypeStruct((b, HEADS, DH, DH), F32)],
        scratch_shapes=[pltpu.VMEM((HEADS, DH, DH), F32)],
        compiler_params=_params(("arbitrary", "arbitrary")),
        name="retention_prompt",
    )(proj, proj, proj, proj, cos_p, sin_p, gl, gain)

    rows = SAMPLE_BB * ts
    base = n_p // rows
    srow = lambda i: base + i
    y_s, st_s = _sample_state_call(
        functools.partial(_ret_sample_body, ts=ts), (nb // SAMPLE_BB,),
        [_proj_spec(rows, COL_RET_Q, srow), _proj_spec(rows, COL_RET_K, srow),
         _proj_spec(rows, COL_RET_V, srow), _proj_spec(rows, COL_RET_G, srow),
         pl.BlockSpec((rows, DH), const2), pl.BlockSpec((rows, DH), const2),
         pl.BlockSpec((HEADS, DH), const2),
         pl.BlockSpec((None, 1, HW), lambda i: (layer, 0, 0)),
         pl.BlockSpec((None, SAMPLE_BB, HEADS, DH, DH), lambda i: (layer, i, 0, 0, 0))],
        (proj, proj, proj, proj, cos_s, sin_s, gl, gain, state),
        jax.ShapeDtypeStruct((nb * ts, HW), BF16), pl.BlockSpec((rows, HW), lambda i: (i, 0)),
        layer, nb, prev_s, "retention_sample")
    return (y_p, y_s), st_p, st_s


def _hg_prepare(hq_ref, hf_ref, lbt_ref, rows, chunk):
    shift = chunk.bit_length() - 1
    ri = lax.broadcasted_iota(I32, (rows, rows), 0)
    ci = lax.broadcasted_iota(I32, (rows, rows), 1)
    same = (ri >> shift) == (ci >> shift)
    causal = same & (ci <= ri)
    z = hf_ref[...]
    log_lb = lbt_ref[0:1, :]
    log_1m_lb = lbt_ref[1:2, :]
    one_m_lb = lbt_ref[2:3, :]
    log_sig = jnp.minimum(z, 0.0) - jnp.log1p(jnp.exp(-jnp.abs(z)))
    bterm = log_1m_lb + log_sig
    logf = jnp.maximum(log_lb, bterm) + jnp.log1p(jnp.exp(-jnp.abs(log_lb - bterm)))
    kh = one_m_lb * jax.nn.sigmoid(-z)
    qh = _silu(hq_ref[...]) * (DH ** -0.5)
    cum = jnp.dot(causal.astype(F32), logf, precision=HIGHEST, preferred_element_type=F32)
    tot = jnp.dot(same.astype(F32), logf, precision=HIGHEST, preferred_element_type=F32)
    qi = qh * jnp.exp(cum)
    ki = kh * jnp.exp(-cum)
    ke = kh * jnp.exp(tot - cum)
    return causal, qi, ki, ke, jnp.exp(tot)


def _rms_norm_gate(o, gain, gate):
    return o * lax.rsqrt(jnp.mean(jnp.square(o), axis=-1, keepdims=True) + LN_EPS) * gain * _silu(gate)


def _hg_prompt_body(hq_ref, hf_ref, hi_ref, hg_ref, lbt_ref, gain_ref, y_ref, st_ref, s_scr):
    c = pl.program_id(1)

    @pl.when(c == 0)
    def _():
        s_scr[...] = jnp.zeros_like(s_scr)

    rows = RET_CHUNK
    causal, qi, ki, ke, etot = _hg_prepare(hq_ref, hf_ref, lbt_ref, rows, HG_CHUNK)
    v = hi_ref[...]
    for h in range(HEADS):
        sl = slice(h * DH, (h + 1) * DH)
        att = jnp.where(causal, _bdot_nt(qi[:, sl], ki[:, sl]), 0.0)
        o_intra = _bdot(att, v[:, sl])
        st = s_scr[h]
        outs = []
        for j in range(rows // HG_CHUNK):
            rs = slice(j * HG_CHUNK, (j + 1) * HG_CHUNK)
            outs.append(o_intra[rs] + _bdot_nt(qi[rs, sl], st))
            st = st * etot[j * HG_CHUNK:j * HG_CHUNK + 1, sl] + _bdot_tn(v[rs, sl], ke[rs, sl])
        s_scr[h] = st
        o = jnp.concatenate(outs, axis=0)
        y_ref[:, sl] = _rms_norm_gate(o, gain_ref[:, sl], hg_ref[:, sl]).astype(BF16)

    @pl.when(c == pl.num_programs(1) - 1)
    def _():
        for h in range(HEADS):
            st_ref[0, h] = s_scr[h].T


def _hg_sample_body(hq_ref, hf_ref, hi_ref, hg_ref, lbt_ref, gain_ref, sin_ref_state,
                    y_ref, st_ref, *, ts):
    rows = SAMPLE_BB * ts
    causal, qi, ki, ke, etot = _hg_prepare(hq_ref, hf_ref, lbt_ref, rows, ts)
    v = hi_ref[...]
    for h in range(HEADS):
        sl = slice(h * DH, (h + 1) * DH)
        att = jnp.where(causal, _bdot_nt(qi[:, sl], ki[:, sl]), 0.0)
        o_intra = _bdot(att, v[:, sl])
        outs = []
        for j in range(SAMPLE_BB):
            rs = slice(j * ts, (j + 1) * ts)
            s = sin_ref_state[j, h]
            outs.append(o_intra[rs] + _bdot(qi[rs, sl], s))
            scale = jnp.broadcast_to(etot[j * ts:j * ts + 1, sl], (DH, DH)).T
            new_state = s * scale + _bdot_tn(ke[rs, sl], v[rs, sl])
            for slot in range(st_ref.shape[0]):
                st_ref[slot, j, h] = new_state
        o = jnp.concatenate(outs, axis=0)
        y_ref[:, sl] = _rms_norm_gate(o, gain_ref[:, sl], hg_ref[:, sl]).astype(BF16)


def _hgrn(proj, state, layer, lbt, gain, b, t, nb, ts, prev_s):
    n_p = b * t
    nc = t // RET_CHUNK
    prow = lambda bi, c: bi * nc + c
    y_p, st_p = pl.pallas_call(
        _hg_prompt_body,
        grid=(b, nc),
        in_specs=[_proj_spec(RET_CHUNK, COL_HG_Q, prow), _proj_spec(RET_CHUNK, COL_HG_F, prow),
                  _proj_spec(RET_CHUNK, COL_HG_I, prow), _proj_spec(RET_CHUNK, COL_HG_G, prow),
                  pl.BlockSpec((None, SUBLANES, HW), lambda bi, c: (layer, 0, 0)),
                  pl.BlockSpec((None, 1, HW), lambda bi, c: (layer, 0, 0))],
        out_specs=[pl.BlockSpec((RET_CHUNK, HW), lambda bi, c: (prow(bi, c), 0)),
                   pl.BlockSpec((1, HEADS, DH, DH), lambda bi, c: (bi, 0, 0, 0))],
        out_shape=[jax.ShapeDtypeStruct((n_p, HW), BF16),
                   jax.ShapeDtypeStruct((b, HEADS, DH, DH), F32)],
        scratch_shapes=[pltpu.VMEM((HEADS, DH, DH), F32)],
        compiler_params=_params(("arbitrary", "arbitrary")),
        name="hgrn_prompt",
    )(proj, proj, proj, proj, lbt, gain)

    rows = SAMPLE_BB * ts
    base = n_p // rows
    srow = lambda i: base + i
    y_s, st_s = _sample_state_call(
        functools.partial(_hg_sample_body, ts=ts), (nb // SAMPLE_BB,),
        [_proj_spec(rows, COL_HG_Q, srow), _proj_spec(rows, COL_HG_F, srow),
         _proj_spec(rows, COL_HG_I, srow), _proj_spec(rows, COL_HG_G, srow),
         pl.BlockSpec((None, SUBLANES, HW), lambda i: (layer, 0, 0)),
         pl.BlockSpec((None, 1, HW), lambda i: (layer, 0, 0)),
         pl.BlockSpec((None, SAMPLE_BB, HEADS, DH, DH), lambda i: (layer, i, 0, 0, 0))],
        (proj, proj, proj, proj, lbt, gain, state),
        jax.ShapeDtypeStruct((nb * ts, HW), BF16), pl.BlockSpec((rows, HW), lambda i: (i, 0)),
        layer, nb, prev_s, "hgrn_sample")
    return (y_p, y_s), st_p, st_s


def _softmax_rows(s):
    e = jnp.exp(s - jnp.max(s, axis=-1, keepdims=True))
    return e / jnp.sum(e, axis=-1, keepdims=True)


def _xa_prompt_body(q_ref, k_ref, v_ref, y_ref):
    for h in range(HEADS):
        sl = slice(h * DH, (h + 1) * DH)
        a = _softmax_rows(_bdot_nt(q_ref[:, sl] * (DH ** -0.5), k_ref[:, sl]))
        y_ref[:, sl] = _bdot(a, v_ref[:, sl]).astype(BF16)


def _xa_sample_body(q_ref, k_ref, v_ref, y_ref, *, ts):
    n_mem = k_ref.shape[1] // HEADS
    for j in range(SAMPLE_BB):
        rs = slice(j * ts, (j + 1) * ts)
        for h in range(HEADS):
            sl = slice(h * DH, (h + 1) * DH)
            mem = pl.ds(h, n_mem, stride=HEADS)
            a = _softmax_rows(_bdot_nt(q_ref[rs, sl] * (DH ** -0.5), k_ref[j, mem, :]))
            y_ref[rs, sl] = _bdot(a, v_ref[j, mem, :]).astype(BF16)


def _cross_attention(proj, kv_p, cache_k, cache_v, layer, b, t, nb, ts):
    n_p = b * t
    n_mem = kv_p.shape[0] // b
    tq = _pick(t, (512, 256, 128))
    nq = t // tq
    y_p = pl.pallas_call(
        _xa_prompt_body,
        grid=(b, nq),
        in_specs=[_proj_spec(tq, COL_XA_Q, lambda bi, qi: bi * nq + qi),
                  pl.BlockSpec((n_mem, HW), lambda bi, qi: (bi, 0)),
                  pl.BlockSpec((n_mem, HW), lambda bi, qi: (bi, 1))],
        out_specs=pl.BlockSpec((tq, HW), lambda bi, qi: (bi * nq + qi, 0)),
        out_shape=jax.ShapeDtypeStruct((n_p, HW), BF16),
        compiler_params=_params(("arbitrary", "arbitrary")),
        name="xattn_prompt",
    )(proj, kv_p, kv_p)

    rows = SAMPLE_BB * ts
    base = n_p // rows
    cache_k = cache_k.reshape(DEPTH, nb, n_mem * HEADS, DH)
    cache_v = cache_v.reshape(DEPTH, nb, n_mem * HEADS, DH)
    kv_spec = pl.BlockSpec((None, SAMPLE_BB, n_mem * HEADS, DH), lambda i: (layer, i, 0, 0))
    y_s = pl.pallas_call(
        functools.partial(_xa_sample_body, ts=ts),
        grid=(nb // SAMPLE_BB,),
        in_specs=[_proj_spec(rows, COL_XA_Q, lambda i: base + i), kv_spec, kv_spec],
        out_specs=pl.BlockSpec((rows, HW), lambda i: (i, 0)),
        out_shape=jax.ShapeDtypeStruct((nb * ts, HW), BF16),
        compiler_params=_params(("arbitrary",)),
        name="xattn_sample",
    )(proj, cache_k, cache_v)
    return (y_p, y_s)


def _layer_norm(tv, g, b):
    mu = jnp.mean(tv, axis=-1, keepdims=True)
    var = jnp.mean(jnp.square(tv - mu), axis=-1, keepdims=True)
    return (tv - mu) * lax.rsqrt(var + LN_EPS) * g + b


def _merge_body(yrp_ref, yrs_ref, yhp_ref, yhs_ref, yxp_ref, yxs_ref, g0a, g0b, g1a, g1b, g2a, g2b, xp_ref, xs_ref,
                wr_ref, wh_ref, wx_ref, wo_ref, lg_ref, lb_ref,
                x1_ref, x1b_ref, x1t_ref, wr_s, wh_s, wx_s, wo_s, *, prompt_tiles):
    @pl.when(pl.program_id(0) == 0)
    def _():
        wr_s[...] = wr_ref[...].astype(BF16)
        wh_s[...] = wh_ref[...].astype(BF16)
        wx_s[...] = wx_ref[...].astype(BF16)
        wo_s[...] = wo_ref[...].astype(BF16)

    is_prompt = pl.program_id(0) < prompt_tiles

    def branch(yp_ref, ys_ref, w_s, ga, gb):
        y = jnp.where(is_prompt, yp_ref[...], ys_ref[...])
        gate = jax.nn.sigmoid(jnp.concatenate([ga[...], gb[...]], axis=-1))
        return gate * jnp.dot(y, w_s[...], preferred_element_type=F32)

    m = (branch(yrp_ref, yrs_ref, wr_s, g0a, g0b) + branch(yhp_ref, yhs_ref, wh_s, g1a, g1b)
         + branch(yxp_ref, yxs_ref, wx_s, g2a, g2b))
    hmix = jnp.dot(m.astype(BF16), wo_s[...], preferred_element_type=F32)
    x = jnp.where(is_prompt, xp_ref[...], xs_ref[...])
    x1 = _layer_norm(DN_ALPHA * x + hmix, lg_ref[...], lb_ref[...])
    x1_ref[...] = x1
    x1b_ref[...] = x1.astype(BF16)
    tm = x1.shape[0]
    for s in range(ROW_TILES):
        x1t_ref[pl.ds(s, tm, stride=ROW_TILES), :] = x1[:, s * LANES:(s + 1) * LANES]


def _merge(ys, proj, x, w_up_ret, w_up_hgrn, w_up_xattn, w_out, ln_g, ln_b, layer):
    n_p = ys[0].shape[0]
    nt = n_p + ys[1].shape[0]
    tm = _pick(nt, (256, 128))
    assert n_p % tm == 0 and ys[1].shape[0] % tm == 0
    p_tiles = n_p // tm
    row = lambda i: (i, 0)
    p_map = lambda i: (jnp.minimum(i, p_tiles - 1), 0)
    s_map = lambda i: (jnp.maximum(i - p_tiles, 0), 0)
    y_specs = [pl.BlockSpec((tm, HW), p_map), pl.BlockSpec((tm, HW), s_map)] * 3
    wspec = lambda k: pl.BlockSpec((None, k, D_MODEL), lambda i: (layer, 0, 0))
    vec = pl.BlockSpec((None, 1, D_MODEL), lambda i: (layer, 0, 0))
    gate_specs = [pl.BlockSpec((tm, HW), lambda i, c=c: (i, COL_GATES + c)) for c in range(6)]
    return pl.pallas_call(
        functools.partial(_merge_body, prompt_tiles=p_tiles),
        grid=(nt // tm,),
        in_specs=y_specs + gate_specs + [pl.BlockSpec((tm, D_MODEL), p_map), pl.BlockSpec((tm, D_MODEL), s_map),
                  wspec(HW), wspec(HW), wspec(HW), wspec(D_MODEL), vec, vec],
        out_specs=[pl.BlockSpec((tm, D_MODEL), row), pl.BlockSpec((tm, D_MODEL), row),
                   pl.BlockSpec((tm * ROW_TILES, LANES), row)],
        out_shape=[jax.ShapeDtypeStruct((nt, D_MODEL), F32),
                   jax.ShapeDtypeStruct((nt, D_MODEL), BF16),
                   jax.ShapeDtypeStruct((nt * ROW_TILES, LANES), F32)],
        scratch_shapes=[pltpu.VMEM((HW, D_MODEL), BF16)] * 3 + [pltpu.VMEM((D_MODEL, D_MODEL), BF16)],
        compiler_params=_params(("arbitrary",)),
        name="merge_out_ln1",
    )(*ys, *([proj] * 6), *x, w_up_ret, w_up_hgrn, w_up_xattn, w_out, ln_g, ln_b)


def _router_body(x_ref, wt_ref, b_ref, eidx_ref, wn_ref):
    tm = x_ref.shape[0]
    logits = lax.dot_general(wt_ref[...], x_ref[...], (((1,), (1,)), ((), ())),
                             precision=HIGHEST, preferred_element_type=F32)
    s = jax.nn.sigmoid(logits)
    sel = s + b_ref[...]
    neg = -jnp.inf
    groups = [sel[g * GROUP_SIZE:(g + 1) * GROUP_SIZE, :] for g in range(N_GROUPS)]
    ie = lax.broadcasted_iota(I32, (GROUP_SIZE, tm), 0).astype(F32)
    rows = []
    for blk in groups:
        m1 = jnp.max(blk, axis=0, keepdims=True)
        first = jnp.min(jnp.where(blk == m1, ie, float(GROUP_SIZE)), axis=0, keepdims=True)
        rows.append(m1 + jnp.max(jnp.where(ie == first, neg, blk), axis=0, keepdims=True))
    gscore = jnp.concatenate(rows, axis=0)
    ig = lax.broadcasted_iota(I32, gscore.shape, 0).astype(F32)
    gmask = jnp.zeros(gscore.shape, F32)
    for _ in range(TOPK_GROUPS):
        m = jnp.max(gscore, axis=0, keepdims=True)
        gi = jnp.min(jnp.where(gscore == m, ig, float(N_GROUPS)), axis=0, keepdims=True)
        hit = ig == gi
        gmask = jnp.where(hit, 1.0, gmask)
        gscore = jnp.where(hit, neg, gscore)
    masked = jnp.concatenate([jnp.where(gmask[g:g + 1, :] > 0.5, blk, neg)
                              for g, blk in enumerate(groups)], axis=0)
    ix = lax.broadcasted_iota(I32, masked.shape, 0).astype(F32)
    idxs, ws = [], []
    for _ in range(TOP_K):
        m = jnp.max(masked, axis=0, keepdims=True)
        ei = jnp.min(jnp.where(masked == m, ix, float(N_EXPERTS)), axis=0, keepdims=True)
        hit = ix == ei
        idxs.append(ei)
        ws.append(jnp.sum(jnp.where(hit, s, 0.0), axis=0, keepdims=True))
        masked = jnp.where(hit, neg, masked)
    wsum = ws[0]
    for w in ws[1:]:
        wsum = wsum + w
    pad = [jnp.zeros((1, tm), F32)] * (SUBLANES - TOP_K)
    eidx_ref[...] = jnp.concatenate(idxs + pad, axis=0).astype(I32)
    wn_ref[...] = jnp.concatenate([w / wsum * ROUTED_SCALE for w in ws] + pad, axis=0)


def _router(x1, w_router_t, b_router, layer):
    nt = x1.shape[0]
    tm = _pick(nt, (512, 256, 128))
    return pl.pallas_call(
        _router_body,
        grid=(nt // tm,),
        in_specs=[pl.BlockSpec((tm, D_MODEL), lambda i: (i, 0)),
                  pl.BlockSpec((None, N_EXPERTS, D_MODEL), lambda i: (layer, 0, 0)),
                  pl.BlockSpec((None, N_EXPERTS, 1), lambda i: (layer, 0, 0))],
        out_specs=[pl.BlockSpec((SUBLANES, tm), lambda i: (0, i))] * 2,
        out_shape=[jax.ShapeDtypeStruct((SUBLANES, nt), I32),
                   jax.ShapeDtypeStruct((SUBLANES, nt), F32)],
        compiler_params=_params(("arbitrary",)),
        name="moe_router",
    )(x1, w_router_t, b_router)


def _positions_body(eidx_ref, pos_ref, cnt_ref, off_ref, base_scr, off_scr):
    phase = pl.program_id(0)
    i = pl.program_id(1)
    tp = eidx_ref.shape[1]
    ix = lax.broadcasted_iota(I32, (N_EXPERTS, tp), 0)
    eidx = eidx_ref[...]
    member = jnp.zeros((N_EXPERTS, tp), F32)
    for k in range(TOP_K):
        member = member + (ix == eidx[k:k + 1, :]).astype(F32)
    tile_cnt = jnp.sum(member, axis=1, keepdims=True)

    @pl.when((phase == 0) & (i == 0))
    def _():
        base_scr[...] = jnp.zeros_like(base_scr)

    @pl.when((phase == 1) & (i == 0))
    def _():
        cnt = base_scr[...]
        er = lax.broadcasted_iota(I32, (N_EXPERTS, N_EXPERTS), 0)
        ec = lax.broadcasted_iota(I32, (N_EXPERTS, N_EXPERTS), 1)
        off = jnp.dot((ec < er).astype(F32), cnt, precision=HIGHEST, preferred_element_type=F32)
        off_scr[...] = off
        cnt_ref[...] = cnt
        off_ref[...] = off
        base_scr[...] = jnp.zeros_like(base_scr)

    @pl.when(phase == 1)
    def _():
        tr = lax.broadcasted_iota(I32, (tp, tp), 0)
        tc = lax.broadcasted_iota(I32, (tp, tp), 1)
        before = jnp.dot(member.astype(BF16), (tr < tc).astype(BF16), preferred_element_type=F32)
        where_to = before + (off_scr[...] + base_scr[...])[:, 0:1]
        rows = [jnp.sum(jnp.where(ix == eidx[k:k + 1, :], where_to, 0.0), axis=0, keepdims=True)
                for k in range(TOP_K)]
        rows += [jnp.zeros((1, tp), F32)] * (SUBLANES - TOP_K)
        pos_ref[...] = jnp.concatenate(rows, axis=0).astype(I32)

    base_scr[...] = base_scr[...] + tile_cnt


def _positions(eidx):
    nt = eidx.shape[1]
    tp = _pick(nt, (512, 256, 128))
    const = lambda p, i: (0, 0)
    return pl.pallas_call(
        _positions_body,
        grid=(2, nt // tp),
        in_specs=[pl.BlockSpec((SUBLANES, tp), lambda p, i: (0, i))],
        out_specs=[pl.BlockSpec((SUBLANES, tp), lambda p, i: (0, i * p)),
                   pl.BlockSpec((N_EXPERTS, LANES), const), pl.BlockSpec((N_EXPERTS, LANES), const)],
        out_shape=[jax.ShapeDtypeStruct((SUBLANES, nt), I32),
                   jax.ShapeDtypeStruct((N_EXPERTS, LANES), F32),
                   jax.ShapeDtypeStruct((N_EXPERTS, LANES), F32)],
        scratch_shapes=[pltpu.VMEM((N_EXPERTS, LANES), F32), pltpu.VMEM((N_EXPERTS, LANES), F32)],
        compiler_params=_params(("arbitrary", "arbitrary")),
        name="moe_positions",
    )(eidx)


T_TILE, T_EXPERT, T_LO, T_HI, T_FRESH, T_NEWEXP = range(6)


def _table_body(cnt_ref, off_ref, tbl_ref, *, tile_rows):
    te = float(tile_rows)
    n = tbl_ref.shape[1]
    cnt = cnt_ref[...]
    off = off_ref[...]
    first = jnp.floor(off * (1.0 / te))
    last = jnp.floor((off + cnt - 1.0) * (1.0 / te))
    nst = jnp.where(cnt > 0.0, last - first + 1.0, 0.0)
    er = lax.broadcasted_iota(I32, (N_EXPERTS, N_EXPERTS), 0)
    ec = lax.broadcasted_iota(I32, (N_EXPERTS, N_EXPERTS), 1)
    s_end = jnp.dot((ec <= er).astype(F32), nst, precision=HIGHEST, preferred_element_type=F32)
    s_beg = s_end - nst
    total = s_end[N_EXPERTS - 1:N_EXPERTS, 0:1]
    sidx = lax.broadcasted_iota(I32, (1, n), 1).astype(F32)
    s = jnp.minimum(sidx, total - 1.0)
    e_s = jnp.sum((s_end[:, 0:1] <= s).astype(F32), axis=0, keepdims=True)
    hot = lax.broadcasted_iota(I32, (N_EXPERTS, n), 0).astype(F32) == e_s

    def pick(col):
        return jnp.sum(jnp.where(hot, col[:, 0:1], 0.0), axis=0, keepdims=True)

    tile = pick(first) + s - pick(s_beg)
    valid = sidx < total
    o, c = pick(off), pick(cnt)
    lo = jnp.where(valid, jnp.maximum(o, tile * te), 0.0)
    hi = jnp.where(valid, jnp.minimum(o + c, (tile + 1.0) * te), 0.0)
    head = sidx == 0.0
    fresh = jnp.where((tile != pltpu.roll(tile, 1, 1)) | head, 1.0, 0.0)
    newexp = jnp.where((e_s != pltpu.roll(e_s, 1, 1)) | head, 1.0, 0.0)
    pad = [jnp.zeros((1, n), F32)] * (SUBLANES - 6)
    tbl_ref[...] = jnp.concatenate([tile, e_s, lo, hi, fresh, newexp] + pad, axis=0).astype(I32)


def _step_table(cnt, off, n_rows, te):
    n_steps = n_rows // te + N_EXPERTS
    width = -(-n_steps // LANES) * LANES
    tbl = pl.pallas_call(
        functools.partial(_table_body, tile_rows=te),
        out_shape=jax.ShapeDtypeStruct((SUBLANES, width), I32),
        name="moe_step_table",
    )(cnt, off)
    return tbl, n_steps


def _dispatch_body(pos_ref, xt_ref, xs_ref, pos_s, sem_p, sem):
    td = pos_ref.shape[1]
    cp = pltpu.make_async_copy(pos_ref, pos_s, sem_p)
    cp.start()
    cp.wait()

    def issue(g, carry):
        for u in range(ISSUE_UNROLL):
            r = g * ISSUE_UNROLL + u
            for k in range(TOP_K):
                pltpu.make_async_copy(xt_ref.at[r], xs_ref.at[pos_s[k, r]], sem).start(priority=k % 2)
        return carry

    lax.fori_loop(0, td // ISSUE_UNROLL, issue, 0)
    for k in range(TOP_K):
        pltpu.make_async_copy(xt_ref, xs_ref.at[pl.ds(0, td)], sem).wait()


def _dispatch(pos, x1t):
    nt = x1t.shape[0]
    td = _pick(nt, (512, 256, 128))
    return pl.pallas_call(
        _dispatch_body,
        grid=(nt // td,),
        in_specs=[pl.BlockSpec((SUBLANES, td), lambda i: (0, i)),
                  pl.BlockSpec((td, ROW_TILES, LANES), lambda i: (i, 0, 0))],
        out_specs=pl.BlockSpec(memory_space=pl.ANY),
        out_shape=jax.ShapeDtypeStruct((nt * TOP_K, ROW_TILES, LANES), F32),
        scratch_shapes=[pltpu.SMEM((SUBLANES, td), I32), pltpu.SemaphoreType.DMA, pltpu.SemaphoreType.DMA],
        compiler_params=_params(("arbitrary",)),
        name="moe_dispatch",
    )(pos, x1t)


def _experts_body(tbl_ref, xs_ref, wg_ref, wu_ref, wd_ref, ye_ref, wg_s, wu_s, wd_s):
    s = pl.program_id(0)
    te = xs_ref.shape[0] // ROW_TILES
    lo = tbl_ref[T_LO, s]
    hi = tbl_ref[T_HI, s]

    @pl.when(tbl_ref[T_NEWEXP, s] == 1)
    def _():
        wg_s[...] = wg_ref[...].astype(BF16)
        wu_s[...] = wu_ref[...].astype(BF16)
        wd_s[...] = wd_ref[...].astype(BF16)

    @pl.when(tbl_ref[T_FRESH, s] == 1)
    def _():
        ye_ref[...] = jnp.zeros_like(ye_ref)

    @pl.when(hi > lo)
    def _():
        x = jnp.concatenate([xs_ref[pl.ds(t, te, stride=ROW_TILES), :].astype(BF16)
                             for t in range(ROW_TILES)], axis=-1)
        g = jnp.dot(x, wg_s[...], preferred_element_type=F32)
        u = jnp.dot(x, wu_s[...], preferred_element_type=F32)
        y = jnp.dot((_silu(g) * u).astype(BF16), wd_s[...], preferred_element_type=F32)
        row = tbl_ref[T_TILE, s] * te + lax.broadcasted_iota(I32, (te, LANES), 0)
        mine = (row >= lo) & (row < hi)
        for t in range(ROW_TILES):
            sl = pl.ds(t, te, stride=ROW_TILES)
            ye_ref[sl, :] = jnp.where(mine, y[:, t * LANES:(t + 1) * LANES], ye_ref[sl, :])


def _experts(tbl, n_steps, te, xs, w_gate, w_up, w_down, layer):
    n_rows = xs.shape[0] // ROW_TILES
    tile_map = lambda s, tbl: (tbl[T_TILE, s], 0)
    w_map = lambda s, tbl: (layer, tbl[T_EXPERT, s], 0, 0)
    w_in_spec = pl.BlockSpec((None, None, D_MODEL, D_EXPERT), w_map)
    w_dn_spec = pl.BlockSpec((None, None, D_EXPERT, D_MODEL), w_map)
    return pl.pallas_call(
        _experts_body,
        grid_spec=pltpu.PrefetchScalarGridSpec(
            num_scalar_prefetch=1,
            grid=(n_steps,),
            in_specs=[pl.BlockSpec((te * ROW_TILES, LANES), tile_map), w_in_spec, w_in_spec, w_dn_spec],
            out_specs=pl.BlockSpec((te * ROW_TILES, LANES), tile_map),
            scratch_shapes=[pltpu.VMEM((D_MODEL, D_EXPERT), BF16), pltpu.VMEM((D_MODEL, D_EXPERT), BF16),
                            pltpu.VMEM((D_EXPERT, D_MODEL), BF16)]),
        out_shape=jax.ShapeDtypeStruct((n_rows * ROW_TILES, LANES), F32),
        compiler_params=_params(("arbitrary",)),
        name="moe_experts",
    )(tbl, xs, w_gate, w_up, w_down)


def _combine_body(pos_ref, wn_ref, ye_ref, x1_ref, x1b_ref, wsg_ref, wsu_ref, wsd_ref, lg_ref, lb_ref,
                  x2p_ref, x2s_ref, x2b_ref, pos_s, wn_s, buf, acc, wsg_s, wsu_s, wsd_s, sem_p, sem_w, sem,
                  *, prompt_tiles):
    i = pl.program_id(0)
    n = pl.num_programs(0)
    tc = wn_s.shape[1]
    slot = i % 2
    groups = tc // ISSUE_UNROLL

    def load_positions(tile, into):
        cp = pltpu.make_async_copy(pos_ref.at[tile], pos_s.at[into], sem_p)
        cp.start()
        cp.wait()

    def issue_rows(g, into):
        for u in range(ISSUE_UNROLL):
            r = g * ISSUE_UNROLL + u
            for k in range(TOP_K):
                pltpu.make_async_copy(ye_ref.at[pos_s[into, k, r]], buf.at[into, k, r],
                                      sem.at[into]).start(priority=k % 2)

    @pl.when(i == 0)
    def _():
        wsg_s[...] = wsg_ref[...].astype(BF16)
        wsu_s[...] = wsu_ref[...].astype(BF16)
        wsd_s[...] = wsd_ref[...].astype(BF16)
        load_positions(0, 0)

        def first(g, carry):
            issue_rows(g, 0)
            return carry

        lax.fori_loop(0, groups, first, 0)

    has_next = i + 1 < n

    @pl.when(has_next)
    def _():
        load_positions(i + 1, 1 - slot)

    cw = pltpu.make_async_copy(wn_ref.at[i], wn_s, sem_w)
    cw.start()
    xb = x1b_ref[...]
    hs = _silu(jnp.dot(xb, wsg_s[...], preferred_element_type=F32)) * jnp.dot(xb, wsu_s[...], preferred_element_type=F32)
    shared = jnp.dot(hs.astype(BF16), wsd_s[...], preferred_element_type=F32)
    cw.wait()
    for k in range(TOP_K):
        pltpu.make_async_copy(ye_ref.at[pl.ds(0, tc)], buf.at[slot, k], sem.at[slot]).wait()

    def reduce_rows(g):
        for u in range(ISSUE_UNROLL):
            r = g * ISSUE_UNROLL + u
            tot = buf[slot, 0, r] * wn_s[0, r]
            for k in range(1, TOP_K):
                tot = tot + buf[slot, k, r] * wn_s[k, r]
            acc[pl.ds(pl.multiple_of(r * ROW_TILES, ROW_TILES), ROW_TILES), :] = tot

    @pl.when(has_next)
    def _():
        def step(g, carry):
            issue_rows(g, 1 - slot)
            reduce_rows(g)
            return carry

        lax.fori_loop(0, groups, step, 0)

    @pl.when(jnp.logical_not(has_next))
    def _():
        def step(g, carry):
            reduce_rows(g)
            return carry

        lax.fori_loop(0, groups, step, 0)

    routed = jnp.concatenate([acc[pl.ds(t, tc, stride=ROW_TILES), :] for t in range(ROW_TILES)], axis=-1)
    x2 = _layer_norm(DN_ALPHA * x1_ref[...] + (routed + shared), lg_ref[...], lb_ref[...])
    x2b_ref[...] = x2.astype(BF16)

    @pl.when(i < prompt_tiles)
    def _():
        x2p_ref[...] = x2

    @pl.when(i >= prompt_tiles)
    def _():
        x2s_ref[...] = x2


def _combine(pos, wn, ye, x1, x1b, w_s_gate, w_s_up, w_s_down, ln_g, ln_b, layer, n_p):
    nt = x1.shape[0]
    tc = _pick(nt, (256, 128))
    assert n_p % tc == 0
    n_tiles = nt // tc
    p_tiles = n_p // tc
    d_sh = w_s_gate.shape[2]
    pos3 = pos.reshape(SUBLANES, n_tiles, tc).transpose(1, 0, 2)
    wn3 = wn.reshape(SUBLANES, n_tiles, tc).transpose(1, 0, 2)
    row = lambda i: (i, 0)
    whole = pl.BlockSpec((n_tiles, SUBLANES, tc), lambda i: (0, 0, 0))
    vec = pl.BlockSpec((None, 1, D_MODEL), lambda i: (layer, 0, 0))
    return pl.pallas_call(
        functools.partial(_combine_body, prompt_tiles=p_tiles),
        grid=(n_tiles,),
        in_specs=[whole, whole,
                  pl.BlockSpec(memory_space=pl.ANY),
                  pl.BlockSpec((tc, D_MODEL), row), pl.BlockSpec((tc, D_MODEL), row),
                  pl.BlockSpec((None, D_MODEL, d_sh), lambda i: (layer, 0, 0)),
                  pl.BlockSpec((None, D_MODEL, d_sh), lambda i: (layer, 0, 0)),
                  pl.BlockSpec((None, d_sh, D_MODEL), lambda i: (layer, 0, 0)), vec, vec],
        out_specs=[pl.BlockSpec((tc, D_MODEL), lambda i: (jnp.minimum(i, p_tiles - 1), 0)),
                   pl.BlockSpec((tc, D_MODEL), lambda i: (jnp.maximum(i - p_tiles, 0), 0)),
                   pl.BlockSpec((tc, D_MODEL), row)],
        out_shape=[jax.ShapeDtypeStruct((n_p, D_MODEL), F32), jax.ShapeDtypeStruct((nt - n_p, D_MODEL), F32),
                   jax.ShapeDtypeStruct((nt, D_MODEL), BF16)],
        scratch_shapes=[pltpu.SMEM((2, SUBLANES, tc), I32), pltpu.SMEM((SUBLANES, tc), F32),
                        pltpu.VMEM((2, TOP_K, tc, ROW_TILES, LANES), F32),
                        pltpu.VMEM((tc * ROW_TILES, LANES), F32),
                        pltpu.VMEM((D_MODEL, d_sh), BF16), pltpu.VMEM((D_MODEL, d_sh), BF16),
                        pltpu.VMEM((d_sh, D_MODEL), BF16),
                        pltpu.SemaphoreType.DMA, pltpu.SemaphoreType.DMA, pltpu.SemaphoreType.DMA((2,))],
        compiler_params=_params(("arbitrary",)),
        name="moe_combine_ln2",
    )(pos3, wn3, ye, x1, x1b, w_s_gate, w_s_up, w_s_down, ln_g, ln_b)


def _rope_tables(t, pos0):
    inv = 1.0 / (ROPE_BASE ** (jnp.arange(0, DH, 2, dtype=F32) / DH))
    ang = (jnp.arange(t, dtype=F32) + pos0)[:, None] * inv[None, :]
    cos, sin = jnp.cos(ang), jnp.sin(ang)
    return jnp.concatenate([cos, cos], axis=-1), jnp.concatenate([-sin, sin], axis=-1)


def kernel(x_prompt, x_sample, mem_prompt, state_ret, state_hgrn, cache_mem_k, cache_mem_v, w_in, w_up_ret, w_up_hgrn, w_up_xattn, w_out, w_mem_kv, ret_norm_g, hgrn_norm_g, lb_logits, ln1_g, ln1_b, ln2_g, ln2_b, w_router, b_router, w_e_gate, w_e_up, w_e_down, w_s_gate, w_s_up, w_s_down):
    b, t, d = x_prompt.shape
    nb, ts, _ = x_sample.shape
    n_mem = mem_prompt.shape[1]
    assert d == D_MODEL and t % RET_CHUNK == 0 and nb % SAMPLE_BB == 0
    assert ts & (ts - 1) == 0 and HG_CHUNK % ts == 0 and RET_CHUNK % ts == 0
    n_p, n_s = b * t, nb * ts
    nt = n_p + n_s
    assert n_p % (SAMPLE_BB * ts) == 0

    lb_cum = jnp.cumsum(jax.nn.softmax(lb_logits.astype(F32), axis=0), axis=0)
    lbs = lb_cum - lb_cum[0:1]
    lbt = jnp.stack([jnp.log(lbs), jnp.log1p(-lbs), 1.0 - lbs] + [jnp.zeros_like(lbs)] * (SUBLANES - 3), axis=1)
    gl = jnp.broadcast_to(jnp.log1p(-jnp.exp2(-5.0 - jnp.arange(HEADS, dtype=F32)))[:, None], (HEADS, DH))
    cos_p, sin_p = _rope_tables(t, 0)
    cos_s, sin_s = _rope_tables(ts, PAST_LEN)
    cos_s, sin_s = jnp.tile(cos_s, (SAMPLE_BB, 1)), jnp.tile(sin_s, (SAMPLE_BB, 1))
    vec3 = lambda a: a.reshape(DEPTH, 1, -1)
    w_router_t = jnp.swapaxes(w_router, 1, 2)
    b_router3 = b_router.reshape(DEPTH, N_EXPERTS, 1)
    mem2 = mem_prompt.reshape(b * n_mem, d)

    x = (x_prompt.reshape(n_p, d), x_sample.reshape(n_s, d))
    xb = jnp.concatenate([x[0].astype(BF16), x[1].astype(BF16)], axis=0)
    tm_proj = _pick(nt, (1024, 512, 128))
    te = _pick(nt * TOP_K, EXPERT_TILES)
    outs = {k: [] for k in ("ret_p", "hg_p", "mk", "mv")}
    ret_s = hg_s = None
    for l in range(DEPTH):
        proj = _matmul(xb, w_in, l, tm_proj, 1280)
        kv_p = _matmul(mem2, w_mem_kv, l, _pick(b * n_mem, (1024, 512, 256)), 2 * HW)
        yr, ret_p, ret_s = _retention(proj, state_ret, l, cos_p, sin_p, cos_s, sin_s, gl,
                                      vec3(ret_norm_g), b, t, nb, ts, ret_s)
        yh, hg_p, hg_s = _hgrn(proj, state_hgrn, l, lbt, vec3(hgrn_norm_g), b, t, nb, ts, hg_s)
        yx = _cross_attention(proj, kv_p, cache_mem_k, cache_mem_v, l, b, t, nb, ts)
        x1, x1b, x1t = _merge((*yr, *yh, *yx), proj, x, w_up_ret, w_up_hgrn, w_up_xattn, w_out,
                              vec3(ln1_g), vec3(ln1_b), l)
        eidx, wn = _router(x1, w_router_t, b_router3, l)
        pos, cnt, off = _positions(eidx)
        tbl, n_steps = _step_table(cnt, off, nt * TOP_K, te)
        xs = _dispatch(pos, x1t.reshape(nt, ROW_TILES, LANES))
        ye = _experts(tbl, n_steps, te, xs.reshape(-1, LANES), w_e_gate, w_e_up, w_e_down, l)
        ye = ye.reshape(-1, ROW_TILES, LANES)
        x_p, x_s, xb = _combine(pos, wn, ye, x1, x1b, w_s_gate, w_s_up, w_s_down,
                                vec3(ln2_g), vec3(ln2_b), l, n_p)
        x = (x_p, x_s)
        outs["ret_p"].append(ret_p)
        outs["hg_p"].append(hg_p)
        outs["mk"].append(kv_p[:, :HW].reshape(b, n_mem, HEADS, DH))
        outs["mv"].append(kv_p[:, HW:].reshape(b, n_mem, HEADS, DH))
    return (x[0].reshape(b, t, d), x[1].reshape(nb, ts, d),
            jnp.stack(outs["ret_p"]), jnp.stack(outs["hg_p"]), jnp.stack(outs["mk"]), jnp.stack(outs["mv"]),
            ret_s, hg_s)
```

```python
import functools

import jax
import jax.numpy as jnp
from jax import lax
from jax.experimental import pallas as pl
from jax.experimental.pallas import tpu as pltpu

F32 = jnp.float32
BF16 = jnp.bfloat16
I32 = jnp.int32
HIGHEST = lax.Precision.HIGHEST

D_MODEL = 1024
DEPTH = 2
PAST_LEN = 16384
HEADS = 4
DH = 128
HW = HEADS * DH
RET_CHUNK = 128
HG_CHUNK = 16
ROPE_BASE = 10000.0
N_EXPERTS = 64
N_GROUPS = 8
GROUP_SIZE = N_EXPERTS // N_GROUPS
TOPK_GROUPS = 4
TOP_K = 6
D_EXPERT = 256
ROUTED_SCALE = 2.5
LN_EPS = 1e-5
DN_ALPHA = (2 * DEPTH) ** 0.25
N_IN = 9 * HW + 3 * D_MODEL
COL_RET_Q, COL_RET_K, COL_RET_V, COL_RET_G = 0, 1, 2, 3
COL_HG_Q, COL_HG_F, COL_HG_I, COL_HG_G = 4, 5, 6, 7
COL_XA_Q = 8
COL_GATES = 9
LANES = 128
SUBLANES = 8
ROW_TILES = D_MODEL // LANES
SAMPLE_BB = 8
EXPERT_TILES = (512, 256)
ISSUE_UNROLL = 8
VMEM_LIMIT = 56 * 1024 * 1024


def _params(sem):
    return pltpu.CompilerParams(dimension_semantics=sem, vmem_limit_bytes=VMEM_LIMIT)


def _bdot(a, b):
    return jnp.dot(a.astype(BF16), b.astype(BF16), preferred_element_type=F32)


def _bdot_nt(a, b):
    return lax.dot_general(a.astype(BF16), b.astype(BF16), (((1,), (1,)), ((), ())),
                           preferred_element_type=F32)


def _bdot_tn(a, b):
    return lax.dot_general(a.astype(BF16), b.astype(BF16), (((0,), (0,)), ((), ())),
                           preferred_element_type=F32)


def _silu(x):
    return x * jax.nn.sigmoid(x)


def _pick(n, prefs):
    for p in prefs:
        if n % p == 0:
            return p
    raise ValueError(f"no tile for {n}")


def _mm_body(x_ref, w_ref, o_ref, wb_ref):
    @pl.when(pl.program_id(1) == 0)
    def _():
        wb_ref[...] = w_ref[...].astype(BF16)

    o_ref[...] = jnp.dot(x_ref[...].astype(BF16), wb_ref[...],
                         preferred_element_type=F32).astype(o_ref.dtype)


def _matmul(x, w, layer, tm, tn):
    m, k = x.shape
    n = w.shape[2]
    return pl.pallas_call(
        _mm_body,
        grid=(n // tn, m // tm),
        in_specs=[pl.BlockSpec((tm, k), lambda j, i: (i, 0)),
                  pl.BlockSpec((None, k, tn), lambda j, i: (layer, 0, j))],
        out_specs=pl.BlockSpec((tm, tn), lambda j, i: (i, j)),
        out_shape=jax.ShapeDtypeStruct((m, n), F32),
        scratch_shapes=[pltpu.VMEM((k, tn), BF16)],
        compiler_params=_params(("arbitrary", "arbitrary")),
        name="dense_matmul",
    )(x, w)


def _rotary(x, cos, sin_signed):
    return x * cos + pltpu.roll(x, DH // 2, 1) * sin_signed


def _group_norm_gate(o, gain, gate):
    mu = jnp.mean(o, axis=-1, keepdims=True)
    var = jnp.mean(jnp.square(o - mu), axis=-1, keepdims=True)
    return (o - mu) * lax.rsqrt(var + LN_EPS) * gain * _silu(gate)


def _ret_prompt_body(q_ref, k_ref, v_ref, g_ref, cos_ref, sin_ref, gl_ref, gain_ref,
                     y_ref, st_ref, s_scr):
    c = pl.program_id(1)

    @pl.when(c == 0)
    def _():
        s_scr[...] = jnp.zeros_like(s_scr)

    ch = RET_CHUNK
    cos = cos_ref[...]
    sin = sin_ref[...]
    ri = lax.broadcasted_iota(I32, (ch, ch), 0)
    ci = lax.broadcasted_iota(I32, (ch, ch), 1)
    rel = (ri - ci).astype(F32)
    idx = lax.broadcasted_iota(I32, (ch, DH), 0).astype(F32)
    for h in range(HEADS):
        sl = slice(h * DH, (h + 1) * DH)
        gl = gl_ref[h:h + 1, :]
        qr = _rotary(q_ref[:, sl], cos, sin)
        kr = _rotary(k_ref[:, sl], cos, sin) * (DH ** -0.5)
        v = v_ref[:, sl]
        intra = jnp.where(rel >= 0, jnp.exp(gl * rel), 0.0)
        att = _bdot_nt(qr, kr) * intra
        s = s_scr[h]
        o = _bdot(att, v) + _bdot(qr, s) * jnp.exp(gl * (idx + 1.0))
        s_scr[h] = s * jnp.exp(gl * float(ch)) + _bdot_tn(kr * jnp.exp(gl * (ch - 1.0 - idx)), v)
        y_ref[:, sl] = _group_norm_gate(o, gain_ref[:, sl], g_ref[:, sl]).astype(BF16)

    @pl.when(c == pl.num_programs(1) - 1)
    def _():
        st_ref[0] = s_scr[...]


def _ret_sample_body(q_ref, k_ref, v_ref, g_ref, cos_ref, sin_ref, gl_ref, gain_ref, sin_ref_state,
                     y_ref, st_ref, *, ts):
    rows = SAMPLE_BB * ts
    shift = ts.bit_length() - 1
    cos = cos_ref[...]
    sin = sin_ref[...]
    ri = lax.broadcasted_iota(I32, (rows, rows), 0)
    ci = lax.broadcasted_iota(I32, (rows, rows), 1)
    rel = (ri - ci).astype(F32)
    mask = ((ri >> shift) == (ci >> shift)) & (ri >= ci)
    idx = (lax.broadcasted_iota(I32, (rows, DH), 0) & (ts - 1)).astype(F32)
    for h in range(HEADS):
        sl = slice(h * DH, (h + 1) * DH)
        gl = gl_ref[h:h + 1, :]
        qr = _rotary(q_ref[:, sl], cos, sin)
        kr = _rotary(k_ref[:, sl], cos, sin) * (DH ** -0.5)
        v = v_ref[:, sl]
        intra = jnp.where(mask, jnp.exp(gl[:, :rows] * rel), 0.0)
        o_intra = _bdot(_bdot_nt(qr, kr) * intra, v)
        q_dec = jnp.exp(gl * (idx + 1.0))
        kd = kr * jnp.exp(gl * (ts - 1.0 - idx))
        c_dec = jnp.exp(gl * float(ts))
        outs = []
        for j in range(SAMPLE_BB):
            rs = slice(j * ts, (j + 1) * ts)
            s = sin_ref_state[j, h]
            outs.append(o_intra[rs] + _bdot(qr[rs], s) * q_dec[rs])
            new_state = s * c_dec + _bdot_tn(kd[rs], v[rs])
            for slot in range(st_ref.shape[0]):
                st_ref[slot, j, h] = new_state
        o = jnp.concatenate(outs, axis=0)
        y_ref[:, sl] = _group_norm_gate(o, gain_ref[:, sl], g_ref[:, sl]).astype(BF16)


def _proj_spec(rows, col, row_map):
    return pl.BlockSpec((rows, HW), lambda *a: (row_map(*a), col))


def _sample_state_call(body, grid, in_specs, args, y_shape, y_spec, layer, nb, prev, name):
    st_shape = jax.ShapeDtypeStruct((DEPTH, nb, HEADS, DH, DH), F32)
    slots = DEPTH if prev is None else 1
    st_spec = pl.BlockSpec((slots, SAMPLE_BB, HEADS, DH, DH), lambda i: (layer, i, 0, 0, 0))
    aliases = {}
    if prev is not None:
        n_in = len(args)
        inner = body
        body = lambda *refs: inner(*refs[:n_in], *refs[n_in + 1:])
        in_specs = in_specs + [pl.BlockSpec(memory_space=pl.ANY)]
        args = args + (prev,)
        aliases = {n_in: 1}
    return pl.pallas_call(
        body, grid=grid, in_specs=in_specs, out_specs=[y_spec, st_spec], out_shape=[y_shape, st_shape],
        input_output_aliases=aliases, compiler_params=_params(("arbitrary",)), name=name,
    )(*args)


def _retention(proj, state, layer, cos_p, sin_p, cos_s, sin_s, gl, gain, b, t, nb, ts, prev_s):
    n_p = b * t
    nc = t // RET_CHUNK
    prow = lambda bi, c: bi * nc + c
    const2 = lambda *a: (0, 0)
    y_p, st_p = pl.pallas_call(
        _ret_prompt_body,
        grid=(b, nc),
        in_specs=[_proj_spec(RET_CHUNK, COL_RET_Q, prow), _proj_spec(RET_CHUNK, COL_RET_K, prow),
                  _proj_spec(RET_CHUNK, COL_RET_V, prow), _proj_spec(RET_CHUNK, COL_RET_G, prow),
                  pl.BlockSpec((RET_CHUNK, DH), lambda bi, c: (c, 0)),
                  pl.BlockSpec((RET_CHUNK, DH), lambda bi, c: (c, 0)),
                  pl.BlockSpec((HEADS, DH), const2),
                  pl.BlockSpec((None, 1, HW), lambda bi, c: (layer, 0, 0))],
        out_specs=[pl.BlockSpec((RET_CHUNK, HW), lambda bi, c: (prow(bi, c), 0)),
                   pl.BlockSpec((1, HEADS, DH, DH), lambda bi, c: (bi, 0, 0, 0))],
        out_shape=[jax.ShapeDtypeStruct((n_p, HW), BF16),
                   jax.ShapeDtypeStruct((b, HEADS, DH, DH), F32)],
        scratch_shapes=[pltpu.VMEM((HEADS, DH, DH), F32)],
        compiler_params=_params(("arbitrary", "arbitrary")),
        name="retention_prompt",
    )(proj, proj, proj, proj, cos_p, sin_p, gl, gain)

    rows = SAMPLE_BB * ts
    base = n_p // rows
    srow = lambda i: base + i
    y_s, st_s = _sample_state_call(
        functools.partial(_ret_sample_body, ts=ts), (nb // SAMPLE_BB,),
        [_proj_spec(rows, COL_RET_Q, srow), _proj_spec(rows, COL_RET_K, srow),
         _proj_spec(rows, COL_RET_V, srow), _proj_spec(rows, COL_RET_G, srow),
         pl.BlockSpec((rows, DH), const2), pl.BlockSpec((rows, DH), const2),
         pl.BlockSpec((HEADS, DH), const2),
         pl.BlockSpec((None, 1, HW), lambda i: (layer, 0, 0)),
         pl.BlockSpec((None, SAMPLE_BB, HEADS, DH, DH), lambda i: (layer, i, 0, 0, 0))],
        (proj, proj, proj, proj, cos_s, sin_s, gl, gain, state),
        jax.ShapeDtypeStruct((nb * ts, HW), BF16), pl.BlockSpec((rows, HW), lambda i: (i, 0)),
        layer, nb, prev_s, "retention_sample")
    return (y_p, y_s), st_p, st_s


def _hg_prepare(hq_ref, hf_ref, lbt_ref, rows, chunk):
    shift = chunk.bit_length() - 1
    ri = lax.broadcasted_iota(I32, (rows, rows), 0)
    ci = lax.broadcasted_iota(I32, (rows, rows), 1)
    same = (ri >> shift) == (ci >> shift)
    causal = same & (ci <= ri)
    z = hf_ref[...]
    log_lb = lbt_ref[0:1, :]
    log_1m_lb = lbt_ref[1:2, :]
    one_m_lb = lbt_ref[2:3, :]
    log_sig = jnp.minimum(z, 0.0) - jnp.log1p(jnp.exp(-jnp.abs(z)))
    bterm = log_1m_lb + log_sig
    logf = jnp.maximum(log_lb, bterm) + jnp.log1p(jnp.exp(-jnp.abs(log_lb - bterm)))
    kh = one_m_lb * jax.nn.sigmoid(-z)
    qh = _silu(hq_ref[...]) * (DH ** -0.5)
    cum = jnp.dot(causal.astype(F32), logf, precision=HIGHEST, preferred_element_type=F32)
    tot = jnp.dot(same.astype(F32), logf, precision=HIGHEST, preferred_element_type=F32)
    qi = qh * jnp.exp(cum)
    ki = kh * jnp.exp(-cum)
    ke = kh * jnp.exp(tot - cum)
    return causal, qi, ki, ke, tot, logf


def _rms_norm_gate(o, gain, gate):
    return o * lax.rsqrt(jnp.mean(jnp.square(o), axis=-1, keepdims=True) + LN_EPS) * gain * _silu(gate)


def _hg_prompt_body(hq_ref, hf_ref, hi_ref, hg_ref, lbt_ref, gain_ref, y_ref, st_ref, s_scr):
    c = pl.program_id(1)

    @pl.when(c == 0)
    def _():
        s_scr[...] = jnp.zeros_like(s_scr)

    rows = RET_CHUNK
    n_sub = rows // HG_CHUNK
    shift = HG_CHUNK.bit_length() - 1
    causal, qi, ki, ke, tot, logf = _hg_prepare(hq_ref, hf_ref, lbt_ref, rows, HG_CHUNK)
    ri = lax.broadcasted_iota(I32, (rows, rows), 0)
    ci = lax.broadcasted_iota(I32, (rows, rows), 1)
    pre = jnp.dot(((ci >> shift) < (ri >> shift)).astype(F32), logf, precision=HIGHEST,
                  preferred_element_type=F32)
    sub = lax.broadcasted_iota(I32, (rows, DH), 0) >> shift
    v = hi_ref[...]
    for h in range(HEADS):
        sl = slice(h * DH, (h + 1) * DH)
        q_h, ke_h, v_h, pre_h = qi[:, sl], ke[:, sl], v[:, sl], pre[:, sl]
        att = jnp.where(causal, _bdot_nt(q_h, ki[:, sl]), 0.0)
        st0 = s_scr[h]
        o = _bdot(att, v_h) + _bdot_nt(q_h * jnp.exp(pre_h), st0)
        end_last = pre_h[rows - 1:rows] + tot[rows - 1:rows, sl]
        st = st0 * jnp.exp(end_last)
        for i in range(n_sub):
            rs = slice(i * HG_CHUNK, (i + 1) * HG_CHUNK)
            u_t = _bdot_tn(v_h[rs], ke_h[rs])
            if i + 1 < n_sub:
                end_i = pre_h[(i + 1) * HG_CHUNK:(i + 1) * HG_CHUNK + 1]
                later = q_h * jnp.exp(jnp.where(sub > i, pre_h - end_i, -jnp.inf))
                o = o + _bdot_nt(later, u_t)
                st = st + u_t * jnp.exp(end_last - end_i)
            else:
                st = st + u_t
        s_scr[h] = st
        y_ref[:, sl] = _rms_norm_gate(o, gain_ref[:, sl], hg_ref[:, sl]).astype(BF16)

    @pl.when(c == pl.num_programs(1) - 1)
    def _():
        for h in range(HEADS):
            st_ref[0, h] = s_scr[h].T


def _hg_sample_body(hq_ref, hf_ref, hi_ref, hg_ref, lbt_ref, gain_ref, sin_ref_state,
                    y_ref, st_ref, *, ts):
    rows = SAMPLE_BB * ts
    causal, qi, ki, ke, tot, _ = _hg_prepare(hq_ref, hf_ref, lbt_ref, rows, ts)
    etot = jnp.exp(tot)
    v = hi_ref[...]
    for h in range(HEADS):
        sl = slice(h * DH, (h + 1) * DH)
        att = jnp.where(causal, _bdot_nt(qi[:, sl], ki[:, sl]), 0.0)
        o_intra = _bdot(att, v[:, sl])
        outs = []
        for j in range(SAMPLE_BB):
            rs = slice(j * ts, (j + 1) * ts)
            s = sin_ref_state[j, h]
            outs.append(o_intra[rs] + _bdot(qi[rs, sl], s))
            scale = jnp.broadcast_to(etot[j * ts:j * ts + 1, sl], (DH, DH)).T
            new_state = s * scale + _bdot_tn(ke[rs, sl], v[rs, sl])
            for slot in range(st_ref.shape[0]):
                st_ref[slot, j, h] = new_state
        o = jnp.concatenate(outs, axis=0)
        y_ref[:, sl] = _rms_norm_gate(o, gain_ref[:, sl], hg_ref[:, sl]).astype(BF16)


def _hgrn(proj, state, layer, lbt, gain, b, t, nb, ts, prev_s):
    n_p = b * t
    nc = t // RET_CHUNK
    prow = lambda bi, c: bi * nc + c
    y_p, st_p = pl.pallas_call(
        _hg_prompt_body,
        grid=(b, nc),
        in_specs=[_proj_spec(RET_CHUNK, COL_HG_Q, prow), _proj_spec(RET_CHUNK, COL_HG_F, prow),
                  _proj_spec(RET_CHUNK, COL_HG_I, prow), _proj_spec(RET_CHUNK, COL_HG_G, prow),
                  pl.BlockSpec((None, SUBLANES, HW), lambda bi, c: (layer, 0, 0)),
                  pl.BlockSpec((None, 1, HW), lambda bi, c: (layer, 0, 0))],
        out_specs=[pl.BlockSpec((RET_CHUNK, HW), lambda bi, c: (prow(bi, c), 0)),
                   pl.BlockSpec((1, HEADS, DH, DH), lambda bi, c: (bi, 0, 0, 0))],
        out_shape=[jax.ShapeDtypeStruct((n_p, HW), BF16),
                   jax.ShapeDtypeStruct((b, HEADS, DH, DH), F32)],
        scratch_shapes=[pltpu.VMEM((HEADS, DH, DH), F32)],
        compiler_params=_params(("arbitrary", "arbitrary")),
        name="hgrn_prompt",
    )(proj, proj, proj, proj, lbt, gain)

    rows = SAMPLE_BB * ts
    base = n_p // rows
    srow = lambda i: base + i
    y_s, st_s = _sample_state_call(
        functools.partial(_hg_sample_body, ts=ts), (nb // SAMPLE_BB,),
        [_proj_spec(rows, COL_HG_Q, srow), _proj_spec(rows, COL_HG_F, srow),
         _proj_spec(rows, COL_HG_I, srow), _proj_spec(rows, COL_HG_G, srow),
         pl.BlockSpec((None, SUBLANES, HW), lambda i: (layer, 0, 0)),
         pl.BlockSpec((None, 1, HW), lambda i: (layer, 0, 0)),
         pl.BlockSpec((None, SAMPLE_BB, HEADS, DH, DH), lambda i: (layer, i, 0, 0, 0))],
        (proj, proj, proj, proj, lbt, gain, state),
        jax.ShapeDtypeStruct((nb * ts, HW), BF16), pl.BlockSpec((rows, HW), lambda i: (i, 0)),
        layer, nb, prev_s, "hgrn_sample")
    return (y_p, y_s), st_p, st_s


def _softmax_rows(s):
    e = jnp.exp(s - jnp.max(s, axis=-1, keepdims=True))
    return e / jnp.sum(e, axis=-1, keepdims=True)


def _xa_prompt_body(q_ref, k_ref, v_ref, y_ref):
    for h in range(HEADS):
        sl = slice(h * DH, (h + 1) * DH)
        a = _softmax_rows(_bdot_nt(q_ref[:, sl] * (DH ** -0.5), k_ref[:, sl]))
        y_ref[:, sl] = _bdot(a, v_ref[:, sl]).astype(BF16)


def _xa_sample_body(q_ref, k_ref, v_ref, y_ref, *, ts):
    n_mem = k_ref.shape[1] // HEADS
    pairs = [(j, h) for j in range(SAMPLE_BB) for h in range(HEADS)]
    q = q_ref[...] * (DH ** -0.5)
    scores = [_bdot_nt(q[j * ts:(j + 1) * ts, h * DH:(h + 1) * DH], k_ref[j, pl.ds(h, n_mem, stride=HEADS), :])
              for j, h in pairs]
    a = _softmax_rows(jnp.concatenate(scores, axis=0))
    for n, (j, h) in enumerate(pairs):
        y = _bdot(a[n * ts:(n + 1) * ts], v_ref[j, pl.ds(h, n_mem, stride=HEADS), :])
        y_ref[j * ts:(j + 1) * ts, h * DH:(h + 1) * DH] = y.astype(BF16)


def _cross_attention(proj, kv_p, cache_k, cache_v, layer, b, t, nb, ts):
    n_p = b * t
    n_mem = kv_p.shape[0] // b
    tq = _pick(t, (512, 256, 128))
    nq = t // tq
    y_p = pl.pallas_call(
        _xa_prompt_body,
        grid=(b, nq),
        in_specs=[_proj_spec(tq, COL_XA_Q, lambda bi, qi: bi * nq + qi),
                  pl.BlockSpec((n_mem, HW), lambda bi, qi: (bi, 0)),
                  pl.BlockSpec((n_mem, HW), lambda bi, qi: (bi, 1))],
        out_specs=pl.BlockSpec((tq, HW), lambda bi, qi: (bi * nq + qi, 0)),
        out_shape=jax.ShapeDtypeStruct((n_p, HW), BF16),
        compiler_params=_params(("arbitrary", "arbitrary")),
        name="xattn_prompt",
    )(proj, kv_p, kv_p)

    rows = SAMPLE_BB * ts
    base = n_p // rows
    cache_k = cache_k.reshape(DEPTH, nb, n_mem * HEADS, DH)
    cache_v = cache_v.reshape(DEPTH, nb, n_mem * HEADS, DH)
    kv_spec = pl.BlockSpec((None, SAMPLE_BB, n_mem * HEADS, DH), lambda i: (layer, i, 0, 0))
    y_s = pl.pallas_call(
        functools.partial(_xa_sample_body, ts=ts),
        grid=(nb // SAMPLE_BB,),
        in_specs=[_proj_spec(rows, COL_XA_Q, lambda i: base + i), kv_spec, kv_spec],
        out_specs=pl.BlockSpec((rows, HW), lambda i: (i, 0)),
        out_shape=jax.ShapeDtypeStruct((nb * ts, HW), BF16),
        compiler_params=_params(("arbitrary",)),
        name="xattn_sample",
    )(proj, cache_k, cache_v)
    return (y_p, y_s)


def _layer_norm(tv, g, b):
    mu = jnp.mean(tv, axis=-1, keepdims=True)
    var = jnp.mean(jnp.square(tv - mu), axis=-1, keepdims=True)
    return (tv - mu) * lax.rsqrt(var + LN_EPS) * g + b


def _merge_body(yrp_ref, yrs_ref, yhp_ref, yhs_ref, yxp_ref, yxs_ref, g0a, g0b, g1a, g1b, g2a, g2b, xp_ref, xs_ref,
                wr_ref, wh_ref, wx_ref, wo_ref, lg_ref, lb_ref,
                x1_ref, x1b_ref, x1t_ref, wr_s, wh_s, wx_s, wo_s, *, prompt_tiles):
    @pl.when(pl.program_id(0) == 0)
    def _():
        wr_s[...] = wr_ref[...].astype(BF16)
        wh_s[...] = wh_ref[...].astype(BF16)
        wx_s[...] = wx_ref[...].astype(BF16)
        wo_s[...] = wo_ref[...].astype(BF16)

    is_prompt = pl.program_id(0) < prompt_tiles

    def branch(yp_ref, ys_ref, w_s, ga, gb):
        y = jnp.where(is_prompt, yp_ref[...], ys_ref[...])
        gate = jax.nn.sigmoid(jnp.concatenate([ga[...], gb[...]], axis=-1))
        return gate * jnp.dot(y, w_s[...], preferred_element_type=F32)

    m = (branch(yrp_ref, yrs_ref, wr_s, g0a, g0b) + branch(yhp_ref, yhs_ref, wh_s, g1a, g1b)
         + branch(yxp_ref, yxs_ref, wx_s, g2a, g2b))
    hmix = jnp.dot(m.astype(BF16), wo_s[...], preferred_element_type=F32)
    x = jnp.where(is_prompt, xp_ref[...], xs_ref[...])
    x1 = _layer_norm(DN_ALPHA * x + hmix, lg_ref[...], lb_ref[...])
    x1_ref[...] = x1
    x1b_ref[...] = x1.astype(BF16)
    tm = x1.shape[0]
    for s in range(ROW_TILES):
        x1t_ref[pl.ds(s, tm, stride=ROW_TILES), :] = x1[:, s * LANES:(s + 1) * LANES]


def _merge(ys, proj, x, w_up_ret, w_up_hgrn, w_up_xattn, w_out, ln_g, ln_b, layer):
    n_p = ys[0].shape[0]
    nt = n_p + ys[1].shape[0]
    tm = _pick(nt, (256, 128))
    assert n_p % tm == 0 and ys[1].shape[0] % tm == 0
    p_tiles = n_p // tm
    row = lambda i: (i, 0)
    p_map = lambda i: (jnp.minimum(i, p_tiles - 1), 0)
    s_map = lambda i: (jnp.maximum(i - p_tiles, 0), 0)
    y_specs = [pl.BlockSpec((tm, HW), p_map), pl.BlockSpec((tm, HW), s_map)] * 3
    wspec = lambda k: pl.BlockSpec((None, k, D_MODEL), lambda i: (layer, 0, 0))
    vec = pl.BlockSpec((None, 1, D_MODEL), lambda i: (layer, 0, 0))
    gate_specs = [pl.BlockSpec((tm, HW), lambda i, c=c: (i, COL_GATES + c)) for c in range(6)]
    return pl.pallas_call(
        functools.partial(_merge_body, prompt_tiles=p_tiles),
        grid=(nt // tm,),
        in_specs=y_specs + gate_specs + [pl.BlockSpec((tm, D_MODEL), p_map), pl.BlockSpec((tm, D_MODEL), s_map),
                  wspec(HW), wspec(HW), wspec(HW), wspec(D_MODEL), vec, vec],
        out_specs=[pl.BlockSpec((tm, D_MODEL), row), pl.BlockSpec((tm, D_MODEL), row),
                   pl.BlockSpec((tm * ROW_TILES, LANES), row)],
        out_shape=[jax.ShapeDtypeStruct((nt, D_MODEL), F32),
                   jax.ShapeDtypeStruct((nt, D_MODEL), BF16),
                   jax.ShapeDtypeStruct((nt * ROW_TILES, LANES), F32)],
        scratch_shapes=[pltpu.VMEM((HW, D_MODEL), BF16)] * 3 + [pltpu.VMEM((D_MODEL, D_MODEL), BF16)],
        compiler_params=_params(("arbitrary",)),
        name="merge_out_ln1",
    )(*ys, *([proj] * 6), *x, w_up_ret, w_up_hgrn, w_up_xattn, w_out, ln_g, ln_b)


def _router_body(x_ref, wt_ref, b_ref, eidx_ref, wn_ref):
    tm = x_ref.shape[0]
    x = x_ref[...]
    w = wt_ref[...]
    xh = x.astype(BF16)
    xl = (x - xh.astype(F32)).astype(BF16)
    wh = w.astype(BF16)
    wl = (w - wh.astype(F32)).astype(BF16)
    logits = _bdot_nt(wh, xh) + (_bdot_nt(wh, xl) + _bdot_nt(wl, xh))
    s = jax.nn.sigmoid(logits)
    sel = s + b_ref[...]
    neg = -jnp.inf
    groups = [sel[g * GROUP_SIZE:(g + 1) * GROUP_SIZE, :] for g in range(N_GROUPS)]
    ie = lax.broadcasted_iota(I32, (GROUP_SIZE, tm), 0).astype(F32)
    rows = []
    for blk in groups:
        m1 = jnp.max(blk, axis=0, keepdims=True)
        first = jnp.min(jnp.where(blk == m1, ie, float(GROUP_SIZE)), axis=0, keepdims=True)
        rows.append(m1 + jnp.max(jnp.where(ie == first, neg, blk), axis=0, keepdims=True))
    gscore = jnp.concatenate(rows, axis=0)
    ig = lax.broadcasted_iota(I32, gscore.shape, 0).astype(F32)
    gmask = jnp.zeros(gscore.shape, F32)
    for _ in range(TOPK_GROUPS):
        m = jnp.max(gscore, axis=0, keepdims=True)
        gi = jnp.min(jnp.where(gscore == m, ig, float(N_GROUPS)), axis=0, keepdims=True)
        hit = ig == gi
        gmask = jnp.where(hit, 1.0, gmask)
        gscore = jnp.where(hit, neg, gscore)
    masked = jnp.concatenate([jnp.where(gmask[g:g + 1, :] > 0.5, blk, neg)
                              for g, blk in enumerate(groups)], axis=0)
    ix = lax.broadcasted_iota(I32, masked.shape, 0).astype(F32)
    idxs, ws = [], []
    for _ in range(TOP_K):
        m = jnp.max(masked, axis=0, keepdims=True)
        ei = jnp.min(jnp.where(masked == m, ix, float(N_EXPERTS)), axis=0, keepdims=True)
        hit = ix == ei
        idxs.append(ei)
        ws.append(jnp.sum(jnp.where(hit, s, 0.0), axis=0, keepdims=True))
        masked = jnp.where(hit, neg, masked)
    wsum = ws[0]
    for w in ws[1:]:
        wsum = wsum + w
    pad = [jnp.zeros((1, tm), F32)] * (SUBLANES - TOP_K)
    eidx_ref[...] = jnp.concatenate(idxs + pad, axis=0).astype(I32)
    wn_ref[...] = jnp.concatenate([w / wsum * ROUTED_SCALE for w in ws] + pad, axis=0)


def _router(x1, w_router_t, b_router, layer):
    nt = x1.shape[0]
    tm = _pick(nt, (512, 256, 128))
    return pl.pallas_call(
        _router_body,
        grid=(nt // tm,),
        in_specs=[pl.BlockSpec((tm, D_MODEL), lambda i: (i, 0)),
                  pl.BlockSpec((None, N_EXPERTS, D_MODEL), lambda i: (layer, 0, 0)),
                  pl.BlockSpec((None, N_EXPERTS, 1), lambda i: (layer, 0, 0))],
        out_specs=[pl.BlockSpec((SUBLANES, tm), lambda i: (0, i))] * 2,
        out_shape=[jax.ShapeDtypeStruct((SUBLANES, nt), I32),
                   jax.ShapeDtypeStruct((SUBLANES, nt), F32)],
        compiler_params=_params(("arbitrary",)),
        name="moe_router",
    )(x1, w_router_t, b_router)


def _positions_body(eidx_ref, pos_ref, cnt_ref, off_ref, base_scr, off_scr):
    phase = pl.program_id(0)
    i = pl.program_id(1)
    tp = eidx_ref.shape[1]
    ix = lax.broadcasted_iota(I32, (N_EXPERTS, tp), 0)
    eidx = eidx_ref[...]
    member = jnp.zeros((N_EXPERTS, tp), F32)
    for k in range(TOP_K):
        member = member + (ix == eidx[k:k + 1, :]).astype(F32)
    tile_cnt = jnp.sum(member, axis=1, keepdims=True)

    @pl.when((phase == 0) & (i == 0))
    def _():
        base_scr[...] = jnp.zeros_like(base_scr)

    @pl.when((phase == 1) & (i == 0))
    def _():
        cnt = base_scr[...]
        er = lax.broadcasted_iota(I32, (N_EXPERTS, N_EXPERTS), 0)
        ec = lax.broadcasted_iota(I32, (N_EXPERTS, N_EXPERTS), 1)
        off = jnp.dot((ec < er).astype(F32), cnt, precision=HIGHEST, preferred_element_type=F32)
        off_scr[...] = off
        cnt_ref[...] = cnt
        off_ref[...] = off
        base_scr[...] = jnp.zeros_like(base_scr)

    @pl.when(phase == 1)
    def _():
        tr = lax.broadcasted_iota(I32, (tp, tp), 0)
        tc = lax.broadcasted_iota(I32, (tp, tp), 1)
        before = jnp.dot(member.astype(BF16), (tr < tc).astype(BF16), preferred_element_type=F32)
        where_to = before + (off_scr[...] + base_scr[...])[:, 0:1]
        rows = [jnp.sum(jnp.where(ix == eidx[k:k + 1, :], where_to, 0.0), axis=0, keepdims=True)
                for k in range(TOP_K)]
        rows += [jnp.zeros((1, tp), F32)] * (SUBLANES - TOP_K)
        pos_ref[...] = jnp.concatenate(rows, axis=0).astype(I32)

    base_scr[...] = base_scr[...] + tile_cnt


def _positions(eidx):
    nt = eidx.shape[1]
    tp = _pick(nt, (512, 256, 128))
    const = lambda p, i: (0, 0)
    return pl.pallas_call(
        _positions_body,
        grid=(2, nt // tp),
        in_specs=[pl.BlockSpec((SUBLANES, tp), lambda p, i: (0, i))],
        out_specs=[pl.BlockSpec((SUBLANES, tp), lambda p, i: (0, i * p)),
                   pl.BlockSpec((N_EXPERTS, LANES), const), pl.BlockSpec((N_EXPERTS, LANES), const)],
        out_shape=[jax.ShapeDtypeStruct((SUBLANES, nt), I32),
                   jax.ShapeDtypeStruct((N_EXPERTS, LANES), F32),
                   jax.ShapeDtypeStruct((N_EXPERTS, LANES), F32)],
        scratch_shapes=[pltpu.VMEM((N_EXPERTS, LANES), F32), pltpu.VMEM((N_EXPERTS, LANES), F32)],
        compiler_params=_params(("arbitrary", "arbitrary")),
        name="moe_positions",
    )(eidx)


T_TILE, T_EXPERT, T_LO, T_HI, T_FRESH, T_NEWEXP = range(6)


def _table_body(cnt_ref, off_ref, tbl_ref, *, tile_rows):
    te = float(tile_rows)
    n = tbl_ref.shape[1]
    cnt = cnt_ref[...]
    off = off_ref[...]
    first = jnp.floor(off * (1.0 / te))
    last = jnp.floor((off + cnt - 1.0) * (1.0 / te))
    nst = jnp.where(cnt > 0.0, last - first + 1.0, 0.0)
    er = lax.broadcasted_iota(I32, (N_EXPERTS, N_EXPERTS), 0)
    ec = lax.broadcasted_iota(I32, (N_EXPERTS, N_EXPERTS), 1)
    s_end = jnp.dot((ec <= er).astype(F32), nst, precision=HIGHEST, preferred_element_type=F32)
    s_beg = s_end - nst
    total = s_end[N_EXPERTS - 1:N_EXPERTS, 0:1]
    sidx = lax.broadcasted_iota(I32, (1, n), 1).astype(F32)
    s = jnp.minimum(sidx, total - 1.0)
    e_s = jnp.sum((s_end[:, 0:1] <= s).astype(F32), axis=0, keepdims=True)
    hot = lax.broadcasted_iota(I32, (N_EXPERTS, n), 0).astype(F32) == e_s

    def pick(col):
        return jnp.sum(jnp.where(hot, col[:, 0:1], 0.0), axis=0, keepdims=True)

    tile = pick(first) + s - pick(s_beg)
    valid = sidx < total
    o, c = pick(off), pick(cnt)
    lo = jnp.where(valid, jnp.maximum(o, tile * te), 0.0)
    hi = jnp.where(valid, jnp.minimum(o + c, (tile + 1.0) * te), 0.0)
    head = sidx == 0.0
    fresh = jnp.where((tile != pltpu.roll(tile, 1, 1)) | head, 1.0, 0.0)
    newexp = jnp.where((e_s != pltpu.roll(e_s, 1, 1)) | head, 1.0, 0.0)
    pad = [jnp.zeros((1, n), F32)] * (SUBLANES - 6)
    tbl_ref[...] = jnp.concatenate([tile, e_s, lo, hi, fresh, newexp] + pad, axis=0).astype(I32)


def _step_table(cnt, off, n_rows, te):
    n_steps = n_rows // te + N_EXPERTS
    width = -(-n_steps // LANES) * LANES
    tbl = pl.pallas_call(
        functools.partial(_table_body, tile_rows=te),
        out_shape=jax.ShapeDtypeStruct((SUBLANES, width), I32),
        name="moe_step_table",
    )(cnt, off)
    return tbl, n_steps


def _dispatch_body(pos_ref, xt_ref, xs_ref, pos_s, sem_p, sem):
    td = pos_ref.shape[1]
    cp = pltpu.make_async_copy(pos_ref, pos_s, sem_p)
    cp.start()
    cp.wait()

    def issue(g, carry):
        for u in range(ISSUE_UNROLL):
            r = g * ISSUE_UNROLL + u
            for k in range(TOP_K):
                pltpu.make_async_copy(xt_ref.at[r], xs_ref.at[pos_s[k, r]], sem).start(priority=k % 2)
        return carry

    lax.fori_loop(0, td // ISSUE_UNROLL, issue, 0)
    for k in range(TOP_K):
        pltpu.make_async_copy(xt_ref, xs_ref.at[pl.ds(0, td)], sem).wait()


def _dispatch(pos, x1t):
    nt = x1t.shape[0]
    td = _pick(nt, (512, 256, 128))
    return pl.pallas_call(
        _dispatch_body,
        grid=(nt // td,),
        in_specs=[pl.BlockSpec((SUBLANES, td), lambda i: (0, i)),
                  pl.BlockSpec((td, ROW_TILES, LANES), lambda i: (i, 0, 0))],
        out_specs=pl.BlockSpec(memory_space=pl.ANY),
        out_shape=jax.ShapeDtypeStruct((nt * TOP_K, ROW_TILES, LANES), F32),
        scratch_shapes=[pltpu.SMEM((SUBLANES, td), I32), pltpu.SemaphoreType.DMA, pltpu.SemaphoreType.DMA],
        compiler_params=_params(("arbitrary",)),
        name="moe_dispatch",
    )(pos, x1t)


def _experts_body(tbl_ref, xs_ref, wg_ref, wu_ref, wd_ref, ye_ref, wg_s, wu_s, wd_s):
    s = pl.program_id(0)
    te = xs_ref.shape[0] // ROW_TILES
    lo = tbl_ref[T_LO, s]
    hi = tbl_ref[T_HI, s]

    @pl.when(tbl_ref[T_NEWEXP, s] == 1)
    def _():
        wg_s[...] = wg_ref[...].astype(BF16)
        wu_s[...] = wu_ref[...].astype(BF16)
        wd_s[...] = wd_ref[...].astype(BF16)

    @pl.when(tbl_ref[T_FRESH, s] == 1)
    def _():
        ye_ref[...] = jnp.zeros_like(ye_ref)

    @pl.when(hi > lo)
    def _():
        x = jnp.concatenate([xs_ref[pl.ds(t, te, stride=ROW_TILES), :].astype(BF16)
                             for t in range(ROW_TILES)], axis=-1)
        g = jnp.dot(x, wg_s[...], preferred_element_type=F32)
        u = jnp.dot(x, wu_s[...], preferred_element_type=F32)
        y = jnp.dot((_silu(g) * u).astype(BF16), wd_s[...], preferred_element_type=F32)
        row = tbl_ref[T_TILE, s] * te + lax.broadcasted_iota(I32, (te, LANES), 0)
        mine = (row >= lo) & (row < hi)
        for t in range(ROW_TILES):
            sl = pl.ds(t, te, stride=ROW_TILES)
            ye_ref[sl, :] = jnp.where(mine, y[:, t * LANES:(t + 1) * LANES], ye_ref[sl, :])


def _experts(tbl, n_steps, te, xs, w_gate, w_up, w_down, layer):
    n_rows = xs.shape[0] // ROW_TILES
    tile_map = lambda s, tbl: (tbl[T_TILE, s], 0)
    w_map = lambda s, tbl: (layer, tbl[T_EXPERT, s], 0, 0)
    w_in_spec = pl.BlockSpec((None, None, D_MODEL, D_EXPERT), w_map)
    w_dn_spec = pl.BlockSpec((None, None, D_EXPERT, D_MODEL), w_map)
    return pl.pallas_call(
        _experts_body,
        grid_spec=pltpu.PrefetchScalarGridSpec(
            num_scalar_prefetch=1,
            grid=(n_steps,),
            in_specs=[pl.BlockSpec((te * ROW_TILES, LANES), tile_map), w_in_spec, w_in_spec, w_dn_spec],
            out_specs=pl.BlockSpec((te * ROW_TILES, LANES), tile_map),
            scratch_shapes=[pltpu.VMEM((D_MODEL, D_EXPERT), BF16), pltpu.VMEM((D_MODEL, D_EXPERT), BF16),
                            pltpu.VMEM((D_EXPERT, D_MODEL), BF16)]),
        out_shape=jax.ShapeDtypeStruct((n_rows * ROW_TILES, LANES), F32),
        compiler_params=_params(("arbitrary",)),
        name="moe_experts",
    )(tbl, xs, w_gate, w_up, w_down)


def _combine_body(pos_ref, wn_ref, ye_ref, x1_ref, x1b_ref, wsg_ref, wsu_ref, wsd_ref, lg_ref, lb_ref,
                  x2p_ref, x2s_ref, x2b_ref, pos_s, wn_s, buf, acc, wsg_s, wsu_s, wsd_s, sem_p, sem_w, sem,
                  *, prompt_tiles):
    i = pl.program_id(0)
    n = pl.num_programs(0)
    tc = wn_s.shape[1]
    slot = i % 2
    groups = tc // ISSUE_UNROLL

    def load_positions(tile, into):
        cp = pltpu.make_async_copy(pos_ref.at[tile], pos_s.at[into], sem_p)
        cp.start()
        cp.wait()

    def issue_rows(g, into):
        for u in range(ISSUE_UNROLL):
            r = g * ISSUE_UNROLL + u
            for k in range(TOP_K):
                pltpu.make_async_copy(ye_ref.at[pos_s[into, k, r]], buf.at[into, k, r],
                                      sem.at[into]).start(priority=k % 2)

    @pl.when(i == 0)
    def _():
        wsg_s[...] = wsg_ref[...].astype(BF16)
        wsu_s[...] = wsu_ref[...].astype(BF16)
        wsd_s[...] = wsd_ref[...].astype(BF16)
        load_positions(0, 0)

        def first(g, carry):
            issue_rows(g, 0)
            return carry

        lax.fori_loop(0, groups, first, 0)

    has_next = i + 1 < n

    @pl.when(has_next)
    def _():
        load_positions(i + 1, 1 - slot)

        def ahead(g, carry):
            issue_rows(g, 1 - slot)
            return carry

        lax.fori_loop(0, groups, ahead, 0)

    cw = pltpu.make_async_copy(wn_ref.at[i], wn_s, sem_w)
    cw.start()
    xb = x1b_ref[...]
    hs = _silu(jnp.dot(xb, wsg_s[...], preferred_element_type=F32)) * jnp.dot(xb, wsu_s[...], preferred_element_type=F32)
    shared = jnp.dot(hs.astype(BF16), wsd_s[...], preferred_element_type=F32)
    cw.wait()
    for k in range(TOP_K):
        pltpu.make_async_copy(ye_ref.at[pl.ds(0, tc)], buf.at[slot, k], sem.at[slot]).wait()

    def reduce_rows(g, carry):
        for u in range(ISSUE_UNROLL):
            r = g * ISSUE_UNROLL + u
            tot = buf[slot, 0, r] * wn_s[0, r]
            for k in range(1, TOP_K):
                tot = tot + buf[slot, k, r] * wn_s[k, r]
            acc[pl.ds(pl.multiple_of(r * ROW_TILES, ROW_TILES), ROW_TILES), :] = tot
        return carry

    lax.fori_loop(0, groups, reduce_rows, 0)
    routed = jnp.concatenate([acc[pl.ds(t, tc, stride=ROW_TILES), :] for t in range(ROW_TILES)], axis=-1)
    x2 = _layer_norm(DN_ALPHA * x1_ref[...] + (routed + shared), lg_ref[...], lb_ref[...])
    x2b_ref[...] = x2.astype(BF16)

    @pl.when(i < prompt_tiles)
    def _():
        x2p_ref[...] = x2

    @pl.when(i >= prompt_tiles)
    def _():
        x2s_ref[...] = x2


def _combine(pos, wn, ye, x1, x1b, w_s_gate, w_s_up, w_s_down, ln_g, ln_b, layer, n_p):
    nt = x1.shape[0]
    tc = _pick(nt, (256, 128))
    assert n_p % tc == 0
    n_tiles = nt // tc
    p_tiles = n_p // tc
    d_sh = w_s_gate.shape[2]
    pos3 = pos.reshape(SUBLANES, n_tiles, tc).transpose(1, 0, 2)
    wn3 = wn.reshape(SUBLANES, n_tiles, tc).transpose(1, 0, 2)
    row = lambda i: (i, 0)
    whole = pl.BlockSpec((n_tiles, SUBLANES, tc), lambda i: (0, 0, 0))
    vec = pl.BlockSpec((None, 1, D_MODEL), lambda i: (layer, 0, 0))
    return pl.pallas_call(
        functools.partial(_combine_body, prompt_tiles=p_tiles),
        grid=(n_tiles,),
        in_specs=[whole, whole,
                  pl.BlockSpec(memory_space=pl.ANY),
                  pl.BlockSpec((tc, D_MODEL), row), pl.BlockSpec((tc, D_MODEL), row),
                  pl.BlockSpec((None, D_MODEL, d_sh), lambda i: (layer, 0, 0)),
                  pl.BlockSpec((None, D_MODEL, d_sh), lambda i: (layer, 0, 0)),
                  pl.BlockSpec((None, d_sh, D_MODEL), lambda i: (layer, 0, 0)), vec, vec],
        out_specs=[pl.BlockSpec((tc, D_MODEL), lambda i: (jnp.minimum(i, p_tiles - 1), 0)),
                   pl.BlockSpec((tc, D_MODEL), lambda i: (jnp.maximum(i - p_tiles, 0), 0)),
                   pl.BlockSpec((tc, D_MODEL), row)],
        out_shape=[jax.ShapeDtypeStruct((n_p, D_MODEL), F32), jax.ShapeDtypeStruct((nt - n_p, D_MODEL), F32),
                   jax.ShapeDtypeStruct((nt, D_MODEL), BF16)],
        scratch_shapes=[pltpu.SMEM((2, SUBLANES, tc), I32), pltpu.SMEM((SUBLANES, tc), F32),
                        pltpu.VMEM((2, TOP_K, tc, ROW_TILES, LANES), F32),
                        pltpu.VMEM((tc * ROW_TILES, LANES), F32),
                        pltpu.VMEM((D_MODEL, d_sh), BF16), pltpu.VMEM((D_MODEL, d_sh), BF16),
                        pltpu.VMEM((d_sh, D_MODEL), BF16),
                        pltpu.SemaphoreType.DMA, pltpu.SemaphoreType.DMA, pltpu.SemaphoreType.DMA((2,))],
        compiler_params=_params(("arbitrary",)),
        name="moe_combine_ln2",
    )(pos3, wn3, ye, x1, x1b, w_s_gate, w_s_up, w_s_down, ln_g, ln_b)


def _rope_tables(t, pos0):
    inv = 1.0 / (ROPE_BASE ** (jnp.arange(0, DH, 2, dtype=F32) / DH))
    ang = (jnp.arange(t, dtype=F32) + pos0)[:, None] * inv[None, :]
    cos, sin = jnp.cos(ang), jnp.sin(ang)
    return jnp.concatenate([cos, cos], axis=-1), jnp.concatenate([-sin, sin], axis=-1)


def kernel(x_prompt, x_sample, mem_prompt, state_ret, state_hgrn, cache_mem_k, cache_mem_v, w_in, w_up_ret, w_up_hgrn, w_up_xattn, w_out, w_mem_kv, ret_norm_g, hgrn_norm_g, lb_logits, ln1_g, ln1_b, ln2_g, ln2_b, w_router, b_router, w_e_gate, w_e_up, w_e_down, w_s_gate, w_s_up, w_s_down):
    b, t, d = x_prompt.shape
    nb, ts, _ = x_sample.shape
    n_mem = mem_prompt.shape[1]
    assert d == D_MODEL and t % RET_CHUNK == 0 and nb % SAMPLE_BB == 0
    assert ts & (ts - 1) == 0 and HG_CHUNK % ts == 0 and RET_CHUNK % ts == 0
    n_p, n_s = b * t, nb * ts
    nt = n_p + n_s
    assert n_p % (SAMPLE_BB * ts) == 0

    lb_cum = jnp.cumsum(jax.nn.softmax(lb_logits.astype(F32), axis=0), axis=0)
    lbs = lb_cum - lb_cum[0:1]
    lbt = jnp.stack([jnp.log(lbs), jnp.log1p(-lbs), 1.0 - lbs] + [jnp.zeros_like(lbs)] * (SUBLANES - 3), axis=1)
    gl = jnp.broadcast_to(jnp.log1p(-jnp.exp2(-5.0 - jnp.arange(HEADS, dtype=F32)))[:, None], (HEADS, DH))
    cos_p, sin_p = _rope_tables(t, 0)
    cos_s, sin_s = _rope_tables(ts, PAST_LEN)
    cos_s, sin_s = jnp.tile(cos_s, (SAMPLE_BB, 1)), jnp.tile(sin_s, (SAMPLE_BB, 1))
    vec3 = lambda a: a.reshape(DEPTH, 1, -1)
    w_router_t = jnp.swapaxes(w_router, 1, 2)
    b_router3 = b_router.reshape(DEPTH, N_EXPERTS, 1)
    mem2 = mem_prompt.reshape(b * n_mem, d)

    x = (x_prompt.reshape(n_p, d), x_sample.reshape(n_s, d))
    xb = jnp.concatenate([x[0].astype(BF16), x[1].astype(BF16)], axis=0)
    tm_proj = _pick(nt, (1024, 512, 128))
    te = _pick(nt * TOP_K, EXPERT_TILES)
    outs = {k: [] for k in ("ret_p", "hg_p", "mk", "mv")}
    ret_s = hg_s = None
    for l in range(DEPTH):
        proj = _matmul(xb, w_in, l, tm_proj, 1280)
        kv_p = _matmul(mem2, w_mem_kv, l, _pick(b * n_mem, (1024, 512, 256)), 2 * HW)
        yr, ret_p, ret_s = _retention(proj, state_ret, l, cos_p, sin_p, cos_s, sin_s, gl,
                                      vec3(ret_norm_g), b, t, nb, ts, ret_s)
        yh, hg_p, hg_s = _hgrn(proj, state_hgrn, l, lbt, vec3(hgrn_norm_g), b, t, nb, ts, hg_s)
        yx = _cross_attention(proj, kv_p, cache_mem_k, cache_mem_v, l, b, t, nb, ts)
        x1, x1b, x1t = _merge((*yr, *yh, *yx), proj, x, w_up_ret, w_up_hgrn, w_up_xattn, w_out,
                              vec3(ln1_g), vec3(ln1_b), l)
        eidx, wn = _router(x1, w_router_t, b_router3, l)
        pos, cnt, off = _positions(eidx)
        tbl, n_steps = _step_table(cnt, off, nt * TOP_K, te)
        xs = _dispatch(pos, x1t.reshape(nt, ROW_TILES, LANES))
        ye = _experts(tbl, n_steps, te, xs.reshape(-1, LANES), w_e_gate, w_e_up, w_e_down, l)
        ye = ye.reshape(-1, ROW_TILES, LANES)
        x_p, x_s, xb = _combine(pos, wn, ye, x1, x1b, w_s_gate, w_s_up, w_s_down,
                                vec3(ln2_g), vec3(ln2_b), l, n_p)
        x = (x_p, x_s)
        outs["ret_p"].append(ret_p)
        outs["hg_p"].append(hg_p)
        outs["mk"].append(kv_p[:, :HW].reshape(b, n_mem, HEADS, DH))
        outs["mv"].append(kv_p[:, HW:].reshape(b, n_mem, HEADS, DH))
    return (x[0].reshape(b, t, d), x[1].reshape(nb, ts, d),
            jnp.stack(outs["ret_p"]), jnp.stack(outs["hg_p"]), jnp.stack(outs["mk"]), jnp.stack(outs["mv"]),
            ret_s, hg_s)
```

```python
import functools

import jax
import jax.numpy as jnp
from jax import lax
from jax.experimental import pallas as pl
from jax.experimental.pallas import tpu as pltpu

F32 = jnp.float32
BF16 = jnp.bfloat16
I32 = jnp.int32
HIGHEST = lax.Precision.HIGHEST

D_MODEL = 1024
DEPTH = 2
PAST_LEN = 16384
HEADS = 4
DH = 128
HW = HEADS * DH
RET_CHUNK = 128
HG_CHUNK = 16
ROPE_BASE = 10000.0
N_EXPERTS = 64
N_GROUPS = 8
GROUP_SIZE = N_EXPERTS // N_GROUPS
TOPK_GROUPS = 4
TOP_K = 6
D_EXPERT = 256
ROUTED_SCALE = 2.5
LN_EPS = 1e-5
DN_ALPHA = (2 * DEPTH) ** 0.25
N_IN = 9 * HW + 3 * D_MODEL
COL_RET_Q, COL_RET_K, COL_RET_V, COL_RET_G = 0, 1, 2, 3
COL_HG_Q, COL_HG_F, COL_HG_I, COL_HG_G = 4, 5, 6, 7
COL_XA_Q = 8
COL_GATES = 9
LANES = 128
SUBLANES = 8
ROW_TILES = D_MODEL // LANES
PACKED_TILES = ROW_TILES // 2
U32 = jnp.uint32
SAMPLE_BB = 8
EXPERT_TILES = (512, 256)
ISSUE_UNROLL = 8
VMEM_LIMIT = 56 * 1024 * 1024


def _params(sem):
    return pltpu.CompilerParams(dimension_semantics=sem, vmem_limit_bytes=VMEM_LIMIT)


def _bdot(a, b):
    return jnp.dot(a.astype(BF16), b.astype(BF16), preferred_element_type=F32)


def _bdot_nt(a, b):
    return lax.dot_general(a.astype(BF16), b.astype(BF16), (((1,), (1,)), ((), ())),
                           preferred_element_type=F32)


def _bdot_tn(a, b):
    return lax.dot_general(a.astype(BF16), b.astype(BF16), (((0,), (0,)), ((), ())),
                           preferred_element_type=F32)


def _silu(x):
    return x * jax.nn.sigmoid(x)


def _pick(n, prefs):
    for p in prefs:
        if n % p == 0:
            return p
    raise ValueError(f"no tile for {n}")


def _mm_body(x_ref, w_ref, o_ref, wb_ref):
    @pl.when(pl.program_id(1) == 0)
    def _():
        wb_ref[...] = w_ref[...].astype(BF16)

    o_ref[...] = jnp.dot(x_ref[...].astype(BF16), wb_ref[...],
                         preferred_element_type=F32).astype(o_ref.dtype)


def _matmul(x, w, layer, tm, tn):
    m, k = x.shape
    n = w.shape[2]
    return pl.pallas_call(
        _mm_body,
        grid=(n // tn, m // tm),
        in_specs=[pl.BlockSpec((tm, k), lambda j, i: (i, 0)),
                  pl.BlockSpec((None, k, tn), lambda j, i: (layer, 0, j))],
        out_specs=pl.BlockSpec((tm, tn), lambda j, i: (i, j)),
        out_shape=jax.ShapeDtypeStruct((m, n), F32),
        scratch_shapes=[pltpu.VMEM((k, tn), BF16)],
        compiler_params=_params(("arbitrary", "arbitrary")),
        name="dense_matmul",
    )(x, w)


def _rotary(x, cos, sin_signed):
    return x * cos + pltpu.roll(x, DH // 2, 1) * sin_signed


def _group_norm_gate(o, gain, gate):
    mu = jnp.mean(o, axis=-1, keepdims=True)
    var = jnp.mean(jnp.square(o - mu), axis=-1, keepdims=True)
    return (o - mu) * lax.rsqrt(var + LN_EPS) * gain * _silu(gate)


def _ret_prompt_body(q_ref, k_ref, v_ref, g_ref, cos_ref, sin_ref, gl_ref, gain_ref,
                     y_ref, st_ref, s_scr):
    c = pl.program_id(1)

    @pl.when(c == 0)
    def _():
        s_scr[...] = jnp.zeros_like(s_scr)

    ch = RET_CHUNK
    cos = cos_ref[...]
    sin = sin_ref[...]
    ri = lax.broadcasted_iota(I32, (ch, ch), 0)
    ci = lax.broadcasted_iota(I32, (ch, ch), 1)
    rel = (ri - ci).astype(F32)
    idx = lax.broadcasted_iota(I32, (ch, DH), 0).astype(F32)
    for h in range(HEADS):
        sl = slice(h * DH, (h + 1) * DH)
        gl = gl_ref[h:h + 1, :]
        qr = _rotary(q_ref[:, sl], cos, sin)
        kr = _rotary(k_ref[:, sl], cos, sin) * (DH ** -0.5)
        v = v_ref[:, sl]
        intra = jnp.where(rel >= 0, jnp.exp(gl * rel), 0.0)
        att = _bdot_nt(qr, kr) * intra
        s = s_scr[h]
        o = _bdot(att, v) + _bdot(qr, s) * jnp.exp(gl * (idx + 1.0))
        s_scr[h] = s * jnp.exp(gl * float(ch)) + _bdot_tn(kr * jnp.exp(gl * (ch - 1.0 - idx)), v)
        y_ref[:, sl] = _group_norm_gate(o, gain_ref[:, sl], g_ref[:, sl]).astype(BF16)

    @pl.when(c == pl.num_programs(1) - 1)
    def _():
        st_ref[0] = s_scr[...]


def _ret_sample_body(q_ref, k_ref, v_ref, g_ref, cos_ref, sin_ref, gl_ref, gain_ref, sin_ref_state,
                     y_ref, st_ref, *, ts):
    rows = SAMPLE_BB * ts
    shift = ts.bit_length() - 1
    cos = cos_ref[...]
    sin = sin_ref[...]
    ri = lax.broadcasted_iota(I32, (rows, rows), 0)
    ci = lax.broadcasted_iota(I32, (rows, rows), 1)
    rel = (ri - ci).astype(F32)
    mask = ((ri >> shift) == (ci >> shift)) & (ri >= ci)
    idx = (lax.broadcasted_iota(I32, (rows, DH), 0) & (ts - 1)).astype(F32)
    for h in range(HEADS):
        sl = slice(h * DH, (h + 1) * DH)
        gl = gl_ref[h:h + 1, :]
        qr = _rotary(q_ref[:, sl], cos, sin)
        kr = _rotary(k_ref[:, sl], cos, sin) * (DH ** -0.5)
        v = v_ref[:, sl]
        intra = jnp.where(mask, jnp.exp(gl[:, :rows] * rel), 0.0)
        o_intra = _bdot(_bdot_nt(qr, kr) * intra, v)
        q_dec = jnp.exp(gl * (idx + 1.0))
        kd = kr * jnp.exp(gl * (ts - 1.0 - idx))
        c_dec = jnp.exp(gl * float(ts))
        outs = []
        for j in range(SAMPLE_BB):
            rs = slice(j * ts, (j + 1) * ts)
            s = sin_ref_state[j, h]
            outs.append(o_intra[rs] + _bdot(qr[rs], s) * q_dec[rs])
            new_state = s * c_dec + _bdot_tn(kd[rs], v[rs])
            for slot in range(st_ref.shape[0]):
                st_ref[slot, j, h] = new_state
        o = jnp.concatenate(outs, axis=0)
        y_ref[:, sl] = _group_norm_gate(o, gain_ref[:, sl], g_ref[:, sl]).astype(BF16)


def _proj_spec(rows, col, row_map):
    return pl.BlockSpec((rows, HW), lambda *a: (row_map(*a), col))


def _sample_state_call(body, grid, in_specs, args, y_shape, y_spec, layer, nb, prev, name):
    st_shape = jax.ShapeDtypeStruct((DEPTH, nb, HEADS, DH, DH), F32)
    slots = DEPTH if prev is None else 1
    st_spec = pl.BlockSpec((slots, SAMPLE_BB, HEADS, DH, DH), lambda i: (layer, i, 0, 0, 0))
    aliases = {}
    if prev is not None:
        n_in = len(args)
        inner = body
        body = lambda *refs: inner(*refs[:n_in], *refs[n_in + 1:])
        in_specs = in_specs + [pl.BlockSpec(memory_space=pl.ANY)]
        args = args + (prev,)
        aliases = {n_in: 1}
    return pl.pallas_call(
        body, grid=grid, in_specs=in_specs, out_specs=[y_spec, st_spec], out_shape=[y_shape, st_shape],
        input_output_aliases=aliases, compiler_params=_params(("arbitrary",)), name=name,
    )(*args)


def _retention(proj, state, layer, cos_p, sin_p, cos_s, sin_s, gl, gain, b, t, nb, ts, prev_s):
    n_p = b * t
    nc = t // RET_CHUNK
    prow = lambda bi, c: bi * nc + c
    const2 = lambda *a: (0, 0)
    y_p, st_p = pl.pallas_call(
        _ret_prompt_body,
        grid=(b, nc),
        in_specs=[_proj_spec(RET_CHUNK, COL_RET_Q, prow), _proj_spec(RET_CHUNK, COL_RET_K, prow),
                  _proj_spec(RET_CHUNK, COL_RET_V, prow), _proj_spec(RET_CHUNK, COL_RET_G, prow),
                  pl.BlockSpec((RET_CHUNK, DH), lambda bi, c: (c, 0)),
                  pl.BlockSpec((RET_CHUNK, DH), lambda bi, c: (c, 0)),
                  pl.BlockSpec((HEADS, DH), const2),
                  pl.BlockSpec((None, 1, HW), lambda bi, c: (layer, 0, 0))],
        out_specs=[pl.BlockSpec((RET_CHUNK, HW), lambda bi, c: (prow(bi, c), 0)),
                   pl.BlockSpec((1, HEADS, DH, DH), lambda bi, c: (bi, 0, 0, 0))],
        out_shape=[jax.ShapeDtypeStruct((n_p, HW), BF16),
                   jax.ShapeDtypeStruct((b, HEADS, DH, DH), F32)],
        scratch_shapes=[pltpu.VMEM((HEADS, DH, DH), F32)],
        compiler_params=_params(("arbitrary", "arbitrary")),
        name="retention_prompt",
    )(proj, proj, proj, proj, cos_p, sin_p, gl, gain)

    rows = SAMPLE_BB * ts
    base = n_p // rows
    srow = lambda i: base + i
    y_s, st_s = _sample_state_call(
        functools.partial(_ret_sample_body, ts=ts), (nb // SAMPLE_BB,),
        [_proj_spec(rows, COL_RET_Q, srow), _proj_spec(rows, COL_RET_K, srow),
         _proj_spec(rows, COL_RET_V, srow), _proj_spec(rows, COL_RET_G, srow),
         pl.BlockSpec((rows, DH), const2), pl.BlockSpec((rows, DH), const2),
         pl.BlockSpec((HEADS, DH), const2),
         pl.BlockSpec((None, 1, HW), lambda i: (layer, 0, 0)),
         pl.BlockSpec((None, SAMPLE_BB, HEADS, DH, DH), lambda i: (layer, i, 0, 0, 0))],
        (proj, proj, proj, proj, cos_s, sin_s, gl, gain, state),
        jax.ShapeDtypeStruct((nb * ts, HW), BF16), pl.BlockSpec((rows, HW), lambda i: (i, 0)),
        layer, nb, prev_s, "retention_sample")
    return (y_p, y_s), st_p, st_s


def _hg_prepare(hq_ref, hf_ref, lbt_ref, rows, chunk):
    shift = chunk.bit_length() - 1
    ri = lax.broadcasted_iota(I32, (rows, rows), 0)
    ci = lax.broadcasted_iota(I32, (rows, rows), 1)
    same = (ri >> shift) == (ci >> shift)
    causal = same & (ci <= ri)
    z = hf_ref[...]
    log_lb = lbt_ref[0:1, :]
    log_1m_lb = lbt_ref[1:2, :]
    one_m_lb = lbt_ref[2:3, :]
    log_sig = jnp.minimum(z, 0.0) - jnp.log1p(jnp.exp(-jnp.abs(z)))
    bterm = log_1m_lb + log_sig
    logf = jnp.maximum(log_lb, bterm) + jnp.log1p(jnp.exp(-jnp.abs(log_lb - bterm)))
    kh = one_m_lb * jax.nn.sigmoid(-z)
    qh = _silu(hq_ref[...]) * (DH ** -0.5)
    cum = jnp.dot(causal.astype(F32), logf, precision=HIGHEST, preferred_element_type=F32)
    tot = jnp.dot(same.astype(F32), logf, precision=HIGHEST, preferred_element_type=F32)
    qi = qh * jnp.exp(cum)
    ki = kh * jnp.exp(-cum)
    ke = kh * jnp.exp(tot - cum)
    return causal, qi, ki, ke, tot, logf


def _rms_norm_gate(o, gain, gate):
    return o * lax.rsqrt(jnp.mean(jnp.square(o), axis=-1, keepdims=True) + LN_EPS) * gain * _silu(gate)


def _hg_prompt_body(hq_ref, hf_ref, hi_ref, hg_ref, lbt_ref, gain_ref, y_ref, st_ref, s_scr):
    c = pl.program_id(1)

    @pl.when(c == 0)
    def _():
        s_scr[...] = jnp.zeros_like(s_scr)

    rows = RET_CHUNK
    n_sub = rows // HG_CHUNK
    shift = HG_CHUNK.bit_length() - 1
    causal, qi, ki, ke, tot, logf = _hg_prepare(hq_ref, hf_ref, lbt_ref, rows, HG_CHUNK)
    ri = lax.broadcasted_iota(I32, (rows, rows), 0)
    ci = lax.broadcasted_iota(I32, (rows, rows), 1)
    pre = jnp.dot(((ci >> shift) < (ri >> shift)).astype(F32), logf, precision=HIGHEST,
                  preferred_element_type=F32)
    sub = lax.broadcasted_iota(I32, (rows, DH), 0) >> shift
    v = hi_ref[...]
    for h in range(HEADS):
        sl = slice(h * DH, (h + 1) * DH)
        q_h, ke_h, v_h, pre_h = qi[:, sl], ke[:, sl], v[:, sl], pre[:, sl]
        att = jnp.where(causal, _bdot_nt(q_h, ki[:, sl]), 0.0)
        st0 = s_scr[h]
        o = _bdot(att, v_h) + _bdot_nt(q_h * jnp.exp(pre_h), st0)
        end_last = pre_h[rows - 1:rows] + tot[rows - 1:rows, sl]
        st = st0 * jnp.exp(end_last)
        for i in range(n_sub):
            rs = slice(i * HG_CHUNK, (i + 1) * HG_CHUNK)
            u_t = _bdot_tn(v_h[rs], ke_h[rs])
            if i + 1 < n_sub:
                end_i = pre_h[(i + 1) * HG_CHUNK:(i + 1) * HG_CHUNK + 1]
                later = q_h * jnp.exp(jnp.where(sub > i, pre_h - end_i, -jnp.inf))
                o = o + _bdot_nt(later, u_t)
                st = st + u_t * jnp.exp(end_last - end_i)
            else:
                st = st + u_t
        s_scr[h] = st
        y_ref[:, sl] = _rms_norm_gate(o, gain_ref[:, sl], hg_ref[:, sl]).astype(BF16)

    @pl.when(c == pl.num_programs(1) - 1)
    def _():
        for h in range(HEADS):
            st_ref[0, h] = s_scr[h].T


def _hg_sample_body(hq_ref, hf_ref, hi_ref, hg_ref, lbt_ref, gain_ref, sin_ref_state,
                    y_ref, st_ref, *, ts):
    rows = SAMPLE_BB * ts
    causal, qi, ki, ke, tot, _ = _hg_prepare(hq_ref, hf_ref, lbt_ref, rows, ts)
    etot = jnp.exp(tot)
    v = hi_ref[...]
    for h in range(HEADS):
        sl = slice(h * DH, (h + 1) * DH)
        att = jnp.where(causal, _bdot_nt(qi[:, sl], ki[:, sl]), 0.0)
        o_intra = _bdot(att, v[:, sl])
        outs = []
        for j in range(SAMPLE_BB):
            rs = slice(j * ts, (j + 1) * ts)
            s = sin_ref_state[j, h]
            outs.append(o_intra[rs] + _bdot(qi[rs, sl], s))
            scale = jnp.broadcast_to(etot[j * ts:j * ts + 1, sl], (DH, DH)).T
            new_state = s * scale + _bdot_tn(ke[rs, sl], v[rs, sl])
            for slot in range(st_ref.shape[0]):
                st_ref[slot, j, h] = new_state
        o = jnp.concatenate(outs, axis=0)
        y_ref[:, sl] = _rms_norm_gate(o, gain_ref[:, sl], hg_ref[:, sl]).astype(BF16)


def _hgrn(proj, state, layer, lbt, gain, b, t, nb, ts, prev_s):
    n_p = b * t
    nc = t // RET_CHUNK
    prow = lambda bi, c: bi * nc + c
    y_p, st_p = pl.pallas_call(
        _hg_prompt_body,
        grid=(b, nc),
        in_specs=[_proj_spec(RET_CHUNK, COL_HG_Q, prow), _proj_spec(RET_CHUNK, COL_HG_F, prow),
                  _proj_spec(RET_CHUNK, COL_HG_I, prow), _proj_spec(RET_CHUNK, COL_HG_G, prow),
                  pl.BlockSpec((None, SUBLANES, HW), lambda bi, c: (layer, 0, 0)),
                  pl.BlockSpec((None, 1, HW), lambda bi, c: (layer, 0, 0))],
        out_specs=[pl.BlockSpec((RET_CHUNK, HW), lambda bi, c: (prow(bi, c), 0)),
                   pl.BlockSpec((1, HEADS, DH, DH), lambda bi, c: (bi, 0, 0, 0))],
        out_shape=[jax.ShapeDtypeStruct((n_p, HW), BF16),
                   jax.ShapeDtypeStruct((b, HEADS, DH, DH), F32)],
        scratch_shapes=[pltpu.VMEM((HEADS, DH, DH), F32)],
        compiler_params=_params(("arbitrary", "arbitrary")),
        name="hgrn_prompt",
    )(proj, proj, proj, proj, lbt, gain)

    rows = SAMPLE_BB * ts
    base = n_p // rows
    srow = lambda i: base + i
    y_s, st_s = _sample_state_call(
        functools.partial(_hg_sample_body, ts=ts), (nb // SAMPLE_BB,),
        [_proj_spec(rows, COL_HG_Q, srow), _proj_spec(rows, COL_HG_F, srow),
         _proj_spec(rows, COL_HG_I, srow), _proj_spec(rows, COL_HG_G, srow),
         pl.BlockSpec((None, SUBLANES, HW), lambda i: (layer, 0, 0)),
         pl.BlockSpec((None, 1, HW), lambda i: (layer, 0, 0)),
         pl.BlockSpec((None, SAMPLE_BB, HEADS, DH, DH), lambda i: (layer, i, 0, 0, 0))],
        (proj, proj, proj, proj, lbt, gain, state),
        jax.ShapeDtypeStruct((nb * ts, HW), BF16), pl.BlockSpec((rows, HW), lambda i: (i, 0)),
        layer, nb, prev_s, "hgrn_sample")
    return (y_p, y_s), st_p, st_s


def _softmax_rows(s):
    e = jnp.exp(s - jnp.max(s, axis=-1, keepdims=True))
    return e / jnp.sum(e, axis=-1, keepdims=True)


def _xa_prompt_body(q_ref, k_ref, v_ref, y_ref):
    for h in range(HEADS):
        sl = slice(h * DH, (h + 1) * DH)
        a = _softmax_rows(_bdot_nt(q_ref[:, sl] * (DH ** -0.5), k_ref[:, sl]))
        y_ref[:, sl] = _bdot(a, v_ref[:, sl]).astype(BF16)


def _xa_sample_body(q_ref, k_ref, v_ref, y_ref, *, ts):
    n_mem = k_ref.shape[1] // HEADS
    pairs = [(j, h) for j in range(SAMPLE_BB) for h in range(HEADS)]
    q = q_ref[...] * (DH ** -0.5)
    scores = [_bdot_nt(q[j * ts:(j + 1) * ts, h * DH:(h + 1) * DH], k_ref[j, pl.ds(h, n_mem, stride=HEADS), :])
              for j, h in pairs]
    a = _softmax_rows(jnp.concatenate(scores, axis=0))
    for n, (j, h) in enumerate(pairs):
        y = _bdot(a[n * ts:(n + 1) * ts], v_ref[j, pl.ds(h, n_mem, stride=HEADS), :])
        y_ref[j * ts:(j + 1) * ts, h * DH:(h + 1) * DH] = y.astype(BF16)


def _cross_attention(proj, kv_p, cache_k, cache_v, layer, b, t, nb, ts):
    n_p = b * t
    n_mem = kv_p.shape[0] // b
    tq = _pick(t, (512, 256, 128))
    nq = t // tq
    y_p = pl.pallas_call(
        _xa_prompt_body,
        grid=(b, nq),
        in_specs=[_proj_spec(tq, COL_XA_Q, lambda bi, qi: bi * nq + qi),
                  pl.BlockSpec((n_mem, HW), lambda bi, qi: (bi, 0)),
                  pl.BlockSpec((n_mem, HW), lambda bi, qi: (bi, 1))],
        out_specs=pl.BlockSpec((tq, HW), lambda bi, qi: (bi * nq + qi, 0)),
        out_shape=jax.ShapeDtypeStruct((n_p, HW), BF16),
        compiler_params=_params(("arbitrary", "arbitrary")),
        name="xattn_prompt",
    )(proj, kv_p, kv_p)

    rows = SAMPLE_BB * ts
    base = n_p // rows
    cache_k = cache_k.reshape(DEPTH, nb, n_mem * HEADS, DH)
    cache_v = cache_v.reshape(DEPTH, nb, n_mem * HEADS, DH)
    kv_spec = pl.BlockSpec((None, SAMPLE_BB, n_mem * HEADS, DH), lambda i: (layer, i, 0, 0))
    y_s = pl.pallas_call(
        functools.partial(_xa_sample_body, ts=ts),
        grid=(nb // SAMPLE_BB,),
        in_specs=[_proj_spec(rows, COL_XA_Q, lambda i: base + i), kv_spec, kv_spec],
        out_specs=pl.BlockSpec((rows, HW), lambda i: (i, 0)),
        out_shape=jax.ShapeDtypeStruct((nb * ts, HW), BF16),
        compiler_params=_params(("arbitrary",)),
        name="xattn_sample",
    )(proj, cache_k, cache_v)
    return (y_p, y_s)


def _layer_norm(tv, g, b):
    mu = jnp.mean(tv, axis=-1, keepdims=True)
    var = jnp.mean(jnp.square(tv - mu), axis=-1, keepdims=True)
    return (tv - mu) * lax.rsqrt(var + LN_EPS) * g + b


def _merge_body(yrp_ref, yrs_ref, yhp_ref, yhs_ref, yxp_ref, yxs_ref, g0a, g0b, g1a, g1b, g2a, g2b, xp_ref, xs_ref,
                wr_ref, wh_ref, wx_ref, wo_ref, lg_ref, lb_ref,
                x1_ref, x1b_ref, x1t_ref, wr_s, wh_s, wx_s, wo_s, *, prompt_tiles):
    @pl.when(pl.program_id(0) == 0)
    def _():
        wr_s[...] = wr_ref[...].astype(BF16)
        wh_s[...] = wh_ref[...].astype(BF16)
        wx_s[...] = wx_ref[...].astype(BF16)
        wo_s[...] = wo_ref[...].astype(BF16)

    is_prompt = pl.program_id(0) < prompt_tiles

    def branch(yp_ref, ys_ref, w_s, ga, gb):
        y = jnp.where(is_prompt, yp_ref[...], ys_ref[...])
        gate = jax.nn.sigmoid(jnp.concatenate([ga[...], gb[...]], axis=-1))
        return gate * jnp.dot(y, w_s[...], preferred_element_type=F32)

    m = (branch(yrp_ref, yrs_ref, wr_s, g0a, g0b) + branch(yhp_ref, yhs_ref, wh_s, g1a, g1b)
         + branch(yxp_ref, yxs_ref, wx_s, g2a, g2b))
    hmix = jnp.dot(m.astype(BF16), wo_s[...], preferred_element_type=F32)
    x = jnp.where(is_prompt, xp_ref[...], xs_ref[...])
    x1 = _layer_norm(DN_ALPHA * x + hmix, lg_ref[...], lb_ref[...])
    x1_ref[...] = x1
    x1b_ref[...] = x1.astype(BF16)
    tm = x1.shape[0]
    bits = lax.bitcast_convert_type(x1.astype(BF16).astype(F32), U32)
    half = D_MODEL // 2
    packed = (bits[:, :half] >> 16) | (bits[:, half:] & jnp.uint32(0xFFFF0000))
    for s in range(PACKED_TILES):
        x1t_ref[pl.ds(s, tm, stride=PACKED_TILES), :] = packed[:, s * LANES:(s + 1) * LANES]


def _merge(ys, proj, x, w_up_ret, w_up_hgrn, w_up_xattn, w_out, ln_g, ln_b, layer):
    n_p = ys[0].shape[0]
    nt = n_p + ys[1].shape[0]
    tm = _pick(nt, (256, 128))
    assert n_p % tm == 0 and ys[1].shape[0] % tm == 0
    p_tiles = n_p // tm
    row = lambda i: (i, 0)
    p_map = lambda i: (jnp.minimum(i, p_tiles - 1), 0)
    s_map = lambda i: (jnp.maximum(i - p_tiles, 0), 0)
    y_specs = [pl.BlockSpec((tm, HW), p_map), pl.BlockSpec((tm, HW), s_map)] * 3
    wspec = lambda k: pl.BlockSpec((None, k, D_MODEL), lambda i: (layer, 0, 0))
    vec = pl.BlockSpec((None, 1, D_MODEL), lambda i: (layer, 0, 0))
    gate_specs = [pl.BlockSpec((tm, HW), lambda i, c=c: (i, COL_GATES + c)) for c in range(6)]
    return pl.pallas_call(
        functools.partial(_merge_body, prompt_tiles=p_tiles),
        grid=(nt // tm,),
        in_specs=y_specs + gate_specs + [pl.BlockSpec((tm, D_MODEL), p_map), pl.BlockSpec((tm, D_MODEL), s_map),
                  wspec(HW), wspec(HW), wspec(HW), wspec(D_MODEL), vec, vec],
        out_specs=[pl.BlockSpec((tm, D_MODEL), row), pl.BlockSpec((tm, D_MODEL), row),
                   pl.BlockSpec((tm * PACKED_TILES, LANES), row)],
        out_shape=[jax.ShapeDtypeStruct((nt, D_MODEL), F32),
                   jax.ShapeDtypeStruct((nt, D_MODEL), BF16),
                   jax.ShapeDtypeStruct((nt * PACKED_TILES, LANES), U32)],
        scratch_shapes=[pltpu.VMEM((HW, D_MODEL), BF16)] * 3 + [pltpu.VMEM((D_MODEL, D_MODEL), BF16)],
        compiler_params=_params(("arbitrary",)),
        name="merge_out_ln1",
    )(*ys, *([proj] * 6), *x, w_up_ret, w_up_hgrn, w_up_xattn, w_out, ln_g, ln_b)


def _router_body(x_ref, wt_ref, b_ref, eidx_ref, wn_ref):
    tm = x_ref.shape[0]
    x = x_ref[...]
    w = wt_ref[...]
    xh = x.astype(BF16)
    xl = (x - xh.astype(F32)).astype(BF16)
    wh = w.astype(BF16)
    wl = (w - wh.astype(F32)).astype(BF16)
    logits = _bdot_nt(wh, xh) + (_bdot_nt(wh, xl) + _bdot_nt(wl, xh))
    s = jax.nn.sigmoid(logits)
    sel = s + b_ref[...]
    neg = -jnp.inf
    groups = [sel[g * GROUP_SIZE:(g + 1) * GROUP_SIZE, :] for g in range(N_GROUPS)]
    ie = lax.broadcasted_iota(I32, (GROUP_SIZE, tm), 0).astype(F32)
    rows = []
    for blk in groups:
        m1 = jnp.max(blk, axis=0, keepdims=True)
        first = jnp.min(jnp.where(blk == m1, ie, float(GROUP_SIZE)), axis=0, keepdims=True)
        rows.append(m1 + jnp.max(jnp.where(ie == first, neg, blk), axis=0, keepdims=True))
    gscore = jnp.concatenate(rows, axis=0)
    ig = lax.broadcasted_iota(I32, gscore.shape, 0).astype(F32)
    gmask = jnp.zeros(gscore.shape, F32)
    for _ in range(TOPK_GROUPS):
        m = jnp.max(gscore, axis=0, keepdims=True)
        gi = jnp.min(jnp.where(gscore == m, ig, float(N_GROUPS)), axis=0, keepdims=True)
        hit = ig == gi
        gmask = jnp.where(hit, 1.0, gmask)
        gscore = jnp.where(hit, neg, gscore)
    masked = jnp.concatenate([jnp.where(gmask[g:g + 1, :] > 0.5, blk, neg)
                              for g, blk in enumerate(groups)], axis=0)
    ix = lax.broadcasted_iota(I32, masked.shape, 0).astype(F32)
    idxs, ws = [], []
    for _ in range(TOP_K):
        m = jnp.max(masked, axis=0, keepdims=True)
        ei = jnp.min(jnp.where(masked == m, ix, float(N_EXPERTS)), axis=0, keepdims=True)
        hit = ix == ei
        idxs.append(ei)
        ws.append(jnp.sum(jnp.where(hit, s, 0.0), axis=0, keepdims=True))
        masked = jnp.where(hit, neg, masked)
    wsum = ws[0]
    for w in ws[1:]:
        wsum = wsum + w
    pad = [jnp.zeros((1, tm), F32)] * (SUBLANES - TOP_K)
    eidx_ref[...] = jnp.concatenate(idxs + pad, axis=0).astype(I32)
    wn_ref[...] = jnp.concatenate([w / wsum * ROUTED_SCALE for w in ws] + pad, axis=0)


def _router(x1, w_router_t, b_router, layer):
    nt = x1.shape[0]
    tm = _pick(nt, (512, 256, 128))
    return pl.pallas_call(
        _router_body,
        grid=(nt // tm,),
        in_specs=[pl.BlockSpec((tm, D_MODEL), lambda i: (i, 0)),
                  pl.BlockSpec((None, N_EXPERTS, D_MODEL), lambda i: (layer, 0, 0)),
                  pl.BlockSpec((None, N_EXPERTS, 1), lambda i: (layer, 0, 0))],
        out_specs=[pl.BlockSpec((SUBLANES, tm), lambda i: (0, i))] * 2,
        out_shape=[jax.ShapeDtypeStruct((SUBLANES, nt), I32),
                   jax.ShapeDtypeStruct((SUBLANES, nt), F32)],
        compiler_params=_params(("arbitrary",)),
        name="moe_router",
    )(x1, w_router_t, b_router)


def _positions_body(eidx_ref, pos_ref, cnt_ref, off_ref, base_scr, off_scr):
    phase = pl.program_id(0)
    i = pl.program_id(1)
    tp = eidx_ref.shape[1]
    ix = lax.broadcasted_iota(I32, (N_EXPERTS, tp), 0)
    eidx = eidx_ref[...]
    member = jnp.zeros((N_EXPERTS, tp), F32)
    for k in range(TOP_K):
        member = member + (ix == eidx[k:k + 1, :]).astype(F32)
    tile_cnt = jnp.sum(member, axis=1, keepdims=True)

    @pl.when((phase == 0) & (i == 0))
    def _():
        base_scr[...] = jnp.zeros_like(base_scr)

    @pl.when((phase == 1) & (i == 0))
    def _():
        cnt = base_scr[...]
        er = lax.broadcasted_iota(I32, (N_EXPERTS, N_EXPERTS), 0)
        ec = lax.broadcasted_iota(I32, (N_EXPERTS, N_EXPERTS), 1)
        off = jnp.dot((ec < er).astype(F32), cnt, precision=HIGHEST, preferred_element_type=F32)
        off_scr[...] = off
        cnt_ref[...] = cnt
        off_ref[...] = off
        base_scr[...] = jnp.zeros_like(base_scr)

    @pl.when(phase == 1)
    def _():
        tr = lax.broadcasted_iota(I32, (tp, tp), 0)
        tc = lax.broadcasted_iota(I32, (tp, tp), 1)
        before = jnp.dot(member.astype(BF16), (tr < tc).astype(BF16), preferred_element_type=F32)
        where_to = before + (off_scr[...] + base_scr[...])[:, 0:1]
        rows = [jnp.sum(jnp.where(ix == eidx[k:k + 1, :], where_to, 0.0), axis=0, keepdims=True)
                for k in range(TOP_K)]
        rows += [jnp.zeros((1, tp), F32)] * (SUBLANES - TOP_K)
        pos_ref[...] = jnp.concatenate(rows, axis=0).astype(I32)

    base_scr[...] = base_scr[...] + tile_cnt


def _positions(eidx):
    nt = eidx.shape[1]
    tp = _pick(nt, (512, 256, 128))
    const = lambda p, i: (0, 0)
    return pl.pallas_call(
        _positions_body,
        grid=(2, nt // tp),
        in_specs=[pl.BlockSpec((SUBLANES, tp), lambda p, i: (0, i))],
        out_specs=[pl.BlockSpec((SUBLANES, tp), lambda p, i: (0, i * p)),
                   pl.BlockSpec((N_EXPERTS, LANES), const), pl.BlockSpec((N_EXPERTS, LANES), const)],
        out_shape=[jax.ShapeDtypeStruct((SUBLANES, nt), I32),
                   jax.ShapeDtypeStruct((N_EXPERTS, LANES), F32),
                   jax.ShapeDtypeStruct((N_EXPERTS, LANES), F32)],
        scratch_shapes=[pltpu.VMEM((N_EXPERTS, LANES), F32), pltpu.VMEM((N_EXPERTS, LANES), F32)],
        compiler_params=_params(("arbitrary", "arbitrary")),
        name="moe_positions",
    )(eidx)


T_TILE, T_EXPERT, T_LO, T_HI, T_FRESH, T_NEWEXP = range(6)


def _table_body(cnt_ref, off_ref, tbl_ref, *, tile_rows):
    te = float(tile_rows)
    n = tbl_ref.shape[1]
    cnt = cnt_ref[...]
    off = off_ref[...]
    first = jnp.floor(off * (1.0 / te))
    last = jnp.floor((off + cnt - 1.0) * (1.0 / te))
    nst = jnp.where(cnt > 0.0, last - first + 1.0, 0.0)
    er = lax.broadcasted_iota(I32, (N_EXPERTS, N_EXPERTS), 0)
    ec = lax.broadcasted_iota(I32, (N_EXPERTS, N_EXPERTS), 1)
    s_end = jnp.dot((ec <= er).astype(F32), nst, precision=HIGHEST, preferred_element_type=F32)
    s_beg = s_end - nst
    total = s_end[N_EXPERTS - 1:N_EXPERTS, 0:1]
    sidx = lax.broadcasted_iota(I32, (1, n), 1).astype(F32)
    s = jnp.minimum(sidx, total - 1.0)
    e_s = jnp.sum((s_end[:, 0:1] <= s).astype(F32), axis=0, keepdims=True)
    hot = lax.broadcasted_iota(I32, (N_EXPERTS, n), 0).astype(F32) == e_s

    def pick(col):
        return jnp.sum(jnp.where(hot, col[:, 0:1], 0.0), axis=0, keepdims=True)

    tile = pick(first) + s - pick(s_beg)
    valid = sidx < total
    o, c = pick(off), pick(cnt)
    lo = jnp.where(valid, jnp.maximum(o, tile * te), 0.0)
    hi = jnp.where(valid, jnp.minimum(o + c, (tile + 1.0) * te), 0.0)
    head = sidx == 0.0
    fresh = jnp.where((tile != pltpu.roll(tile, 1, 1)) | head, 1.0, 0.0)
    newexp = jnp.where((e_s != pltpu.roll(e_s, 1, 1)) | head, 1.0, 0.0)
    pad = [jnp.zeros((1, n), F32)] * (SUBLANES - 6)
    tbl_ref[...] = jnp.concatenate([tile, e_s, lo, hi, fresh, newexp] + pad, axis=0).astype(I32)


def _step_table(cnt, off, n_rows, te):
    n_steps = n_rows // te + N_EXPERTS
    width = -(-n_steps // LANES) * LANES
    tbl = pl.pallas_call(
        functools.partial(_table_body, tile_rows=te),
        out_shape=jax.ShapeDtypeStruct((SUBLANES, width), I32),
        name="moe_step_table",
    )(cnt, off)
    return tbl, n_steps


def _dispatch_body(pos_ref, xt_ref, xs_ref, pos_s, sem_p, sem):
    td = pos_ref.shape[1]
    cp = pltpu.make_async_copy(pos_ref, pos_s, sem_p)
    cp.start()
    cp.wait()

    def issue(g, carry):
        for u in range(ISSUE_UNROLL):
            r = g * ISSUE_UNROLL + u
            for k in range(TOP_K):
                pltpu.make_async_copy(xt_ref.at[r], xs_ref.at[pos_s[k, r]], sem).start(priority=k % 2)
        return carry

    lax.fori_loop(0, td // ISSUE_UNROLL, issue, 0)
    for k in range(TOP_K):
        pltpu.make_async_copy(xt_ref, xs_ref.at[pl.ds(0, td)], sem).wait()


def _dispatch(pos, x1t):
    nt = x1t.shape[0]
    td = _pick(nt, (512, 256, 128))
    return pl.pallas_call(
        _dispatch_body,
        grid=(nt // td,),
        in_specs=[pl.BlockSpec((SUBLANES, td), lambda i: (0, i)),
                  pl.BlockSpec((td,) + x1t.shape[1:], lambda i: (i, 0, 0))],
        out_specs=pl.BlockSpec(memory_space=pl.ANY),
        out_shape=jax.ShapeDtypeStruct((nt * TOP_K,) + x1t.shape[1:], x1t.dtype),
        scratch_shapes=[pltpu.SMEM((SUBLANES, td), I32), pltpu.SemaphoreType.DMA, pltpu.SemaphoreType.DMA],
        compiler_params=_params(("arbitrary",)),
        name="moe_dispatch",
    )(pos, x1t)


def _experts_body(tbl_ref, xs_ref, wg_ref, wu_ref, wd_ref, ye_ref, wg_s, wu_s, wd_s):
    s = pl.program_id(0)
    te = xs_ref.shape[0] // PACKED_TILES
    lo = tbl_ref[T_LO, s]
    hi = tbl_ref[T_HI, s]

    @pl.when(tbl_ref[T_NEWEXP, s] == 1)
    def _():
        wg_s[...] = wg_ref[...].astype(BF16)
        wu_s[...] = wu_ref[...].astype(BF16)
        wd_s[...] = wd_ref[...].astype(BF16)

    @pl.when(tbl_ref[T_FRESH, s] == 1)
    def _():
        ye_ref[...] = jnp.zeros_like(ye_ref)

    @pl.when(hi > lo)
    def _():
        words = [xs_ref[pl.ds(t, te, stride=PACKED_TILES), :] for t in range(PACKED_TILES)]
        low = [lax.bitcast_convert_type(w << 16, F32).astype(BF16) for w in words]
        high = [lax.bitcast_convert_type(w & jnp.uint32(0xFFFF0000), F32).astype(BF16) for w in words]
        x = jnp.concatenate(low + high, axis=-1)
        g = jnp.dot(x, wg_s[...], preferred_element_type=F32)
        u = jnp.dot(x, wu_s[...], preferred_element_type=F32)
        y = jnp.dot((_silu(g) * u).astype(BF16), wd_s[...], preferred_element_type=F32)
        row = tbl_ref[T_TILE, s] * te + lax.broadcasted_iota(I32, (te, LANES), 0)
        mine = (row >= lo) & (row < hi)
        for t in range(ROW_TILES):
            sl = pl.ds(t, te, stride=ROW_TILES)
            ye_ref[sl, :] = jnp.where(mine, y[:, t * LANES:(t + 1) * LANES], ye_ref[sl, :])


def _experts(tbl, n_steps, te, xs, w_gate, w_up, w_down, layer):
    n_rows = xs.shape[0] // PACKED_TILES
    tile_map = lambda s, tbl: (tbl[T_TILE, s], 0)
    w_map = lambda s, tbl: (layer, tbl[T_EXPERT, s], 0, 0)
    w_in_spec = pl.BlockSpec((None, None, D_MODEL, D_EXPERT), w_map)
    w_dn_spec = pl.BlockSpec((None, None, D_EXPERT, D_MODEL), w_map)
    return pl.pallas_call(
        _experts_body,
        grid_spec=pltpu.PrefetchScalarGridSpec(
            num_scalar_prefetch=1,
            grid=(n_steps,),
            in_specs=[pl.BlockSpec((te * PACKED_TILES, LANES), tile_map), w_in_spec, w_in_spec, w_dn_spec],
            out_specs=pl.BlockSpec((te * ROW_TILES, LANES), tile_map),
            scratch_shapes=[pltpu.VMEM((D_MODEL, D_EXPERT), BF16), pltpu.VMEM((D_MODEL, D_EXPERT), BF16),
                            pltpu.VMEM((D_EXPERT, D_MODEL), BF16)]),
        out_shape=jax.ShapeDtypeStruct((n_rows * ROW_TILES, LANES), F32),
        compiler_params=_params(("arbitrary",)),
        name="moe_experts",
    )(tbl, xs, w_gate, w_up, w_down)


def _combine_body(pos_ref, wn_ref, ye_ref, x1_ref, x1b_ref, wsg_ref, wsu_ref, wsd_ref, lg_ref, lb_ref,
                  x2p_ref, x2s_ref, x2b_ref, pos_s, wn_s, buf, acc, wsg_s, wsu_s, wsd_s, sem_p, sem_w, sem,
                  *, prompt_tiles):
    i = pl.program_id(0)
    n = pl.num_programs(0)
    tc = wn_s.shape[1]
    slot = i % 2
    groups = tc // ISSUE_UNROLL

    def load_positions(tile, into):
        cp = pltpu.make_async_copy(pos_ref.at[tile], pos_s.at[into], sem_p)
        cp.start()
        cp.wait()

    def issue_rows(g, into):
        for u in range(ISSUE_UNROLL):
            r = g * ISSUE_UNROLL + u
            for k in range(TOP_K):
                pltpu.make_async_copy(ye_ref.at[pos_s[into, k, r]], buf.at[into, k, r],
                                      sem.at[into]).start(priority=k % 2)

    @pl.when(i == 0)
    def _():
        wsg_s[...] = wsg_ref[...].astype(BF16)
        wsu_s[...] = wsu_ref[...].astype(BF16)
        wsd_s[...] = wsd_ref[...].astype(BF16)
        load_positions(0, 0)

        def first(g, carry):
            issue_rows(g, 0)
            return carry

        lax.fori_loop(0, groups, first, 0)

    has_next = i + 1 < n

    @pl.when(has_next)
    def _():
        load_positions(i + 1, 1 - slot)

        def ahead(g, carry):
            issue_rows(g, 1 - slot)
            return carry

        lax.fori_loop(0, groups, ahead, 0)

    cw = pltpu.make_async_copy(wn_ref.at[i], wn_s, sem_w)
    cw.start()
    xb = x1b_ref[...]
    hs = _silu(jnp.dot(xb, wsg_s[...], preferred_element_type=F32)) * jnp.dot(xb, wsu_s[...], preferred_element_type=F32)
    shared = jnp.dot(hs.astype(BF16), wsd_s[...], preferred_element_type=F32)
    cw.wait()
    for k in range(TOP_K):
        pltpu.make_async_copy(ye_ref.at[pl.ds(0, tc)], buf.at[slot, k], sem.at[slot]).wait()

    def reduce_rows(g, carry):
        for u in range(ISSUE_UNROLL):
            r = g * ISSUE_UNROLL + u
            tot = buf[slot, 0, r] * wn_s[0, r]
            for k in range(1, TOP_K):
                tot = tot + buf[slot, k, r] * wn_s[k, r]
            acc[pl.ds(pl.multiple_of(r * ROW_TILES, ROW_TILES), ROW_TILES), :] = tot
        return carry

    lax.fori_loop(0, groups, reduce_rows, 0)
    routed = jnp.concatenate([acc[pl.ds(t, tc, stride=ROW_TILES), :] for t in range(ROW_TILES)], axis=-1)
    x2 = _layer_norm(DN_ALPHA * x1_ref[...] + (routed + shared), lg_ref[...], lb_ref[...])
    x2b_ref[...] = x2.astype(BF16)

    @pl.when(i < prompt_tiles)
    def _():
        x2p_ref[...] = x2

    @pl.when(i >= prompt_tiles)
    def _():
        x2s_ref[...] = x2


def _combine(pos, wn, ye, x1, x1b, w_s_gate, w_s_up, w_s_down, ln_g, ln_b, layer, n_p):
    nt = x1.shape[0]
    tc = _pick(nt, (256, 128))
    assert n_p % tc == 0
    n_tiles = nt // tc
    p_tiles = n_p // tc
    d_sh = w_s_gate.shape[2]
    pos3 = pos.reshape(SUBLANES, n_tiles, tc).transpose(1, 0, 2)
    wn3 = wn.reshape(SUBLANES, n_tiles, tc).transpose(1, 0, 2)
    row = lambda i: (i, 0)
    whole = pl.BlockSpec((n_tiles, SUBLANES, tc), lambda i: (0, 0, 0))
    vec = pl.BlockSpec((None, 1, D_MODEL), lambda i: (layer, 0, 0))
    return pl.pallas_call(
        functools.partial(_combine_body, prompt_tiles=p_tiles),
        grid=(n_tiles,),
        in_specs=[whole, whole,
                  pl.BlockSpec(memory_space=pl.ANY),
                  pl.BlockSpec((tc, D_MODEL), row), pl.BlockSpec((tc, D_MODEL), row),
                  pl.BlockSpec((None, D_MODEL, d_sh), lambda i: (layer, 0, 0)),
                  pl.BlockSpec((None, D_MODEL, d_sh), lambda i: (layer, 0, 0)),
                  pl.BlockSpec((None, d_sh, D_MODEL), lambda i: (layer, 0, 0)), vec, vec],
        out_specs=[pl.BlockSpec((tc, D_MODEL), lambda i: (jnp.minimum(i, p_tiles - 1), 0)),
                   pl.BlockSpec((tc, D_MODEL), lambda i: (jnp.maximum(i - p_tiles, 0), 0)),
                   pl.BlockSpec((tc, D_MODEL), row)],
        out_shape=[jax.ShapeDtypeStruct((n_p, D_MODEL), F32), jax.ShapeDtypeStruct((nt - n_p, D_MODEL), F32),
                   jax.ShapeDtypeStruct((nt, D_MODEL), BF16)],
        scratch_shapes=[pltpu.SMEM((2, SUBLANES, tc), I32), pltpu.SMEM((SUBLANES, tc), F32),
                        pltpu.VMEM((2, TOP_K, tc, ROW_TILES, LANES), F32),
                        pltpu.VMEM((tc * ROW_TILES, LANES), F32),
                        pltpu.VMEM((D_MODEL, d_sh), BF16), pltpu.VMEM((D_MODEL, d_sh), BF16),
                        pltpu.VMEM((d_sh, D_MODEL), BF16),
                        pltpu.SemaphoreType.DMA, pltpu.SemaphoreType.DMA, pltpu.SemaphoreType.DMA((2,))],
        compiler_params=_params(("arbitrary",)),
        name="moe_combine_ln2",
    )(pos3, wn3, ye, x1, x1b, w_s_gate, w_s_up, w_s_down, ln_g, ln_b)


def _rope_tables(t, pos0):
    inv = 1.0 / (ROPE_BASE ** (jnp.arange(0, DH, 2, dtype=F32) / DH))
    ang = (jnp.arange(t, dtype=F32) + pos0)[:, None] * inv[None, :]
    cos, sin = jnp.cos(ang), jnp.sin(ang)
    return jnp.concatenate([cos, cos], axis=-1), jnp.concatenate([-sin, sin], axis=-1)


def kernel(x_prompt, x_sample, mem_prompt, state_ret, state_hgrn, cache_mem_k, cache_mem_v, w_in, w_up_ret, w_up_hgrn, w_up_xattn, w_out, w_mem_kv, ret_norm_g, hgrn_norm_g, lb_logits, ln1_g, ln1_b, ln2_g, ln2_b, w_router, b_router, w_e_gate, w_e_up, w_e_down, w_s_gate, w_s_up, w_s_down):
    b, t, d = x_prompt.shape
    nb, ts, _ = x_sample.shape
    n_mem = mem_prompt.shape[1]
    assert d == D_MODEL and t % RET_CHUNK == 0 and nb % SAMPLE_BB == 0
    assert ts & (ts - 1) == 0 and HG_CHUNK % ts == 0 and RET_CHUNK % ts == 0
    n_p, n_s = b * t, nb * ts
    nt = n_p + n_s
    assert n_p % (SAMPLE_BB * ts) == 0

    lb_cum = jnp.cumsum(jax.nn.softmax(lb_logits.astype(F32), axis=0), axis=0)
    lbs = lb_cum - lb_cum[0:1]
    lbt = jnp.stack([jnp.log(lbs), jnp.log1p(-lbs), 1.0 - lbs] + [jnp.zeros_like(lbs)] * (SUBLANES - 3), axis=1)
    gl = jnp.broadcast_to(jnp.log1p(-jnp.exp2(-5.0 - jnp.arange(HEADS, dtype=F32)))[:, None], (HEADS, DH))
    cos_p, sin_p = _rope_tables(t, 0)
    cos_s, sin_s = _rope_tables(ts, PAST_LEN)
    cos_s, sin_s = jnp.tile(cos_s, (SAMPLE_BB, 1)), jnp.tile(sin_s, (SAMPLE_BB, 1))
    vec3 = lambda a: a.reshape(DEPTH, 1, -1)
    w_router_t = jnp.swapaxes(w_router, 1, 2)
    b_router3 = b_router.reshape(DEPTH, N_EXPERTS, 1)
    mem2 = mem_prompt.reshape(b * n_mem, d)

    x = (x_prompt.reshape(n_p, d), x_sample.reshape(n_s, d))
    xb = jnp.concatenate([x[0].astype(BF16), x[1].astype(BF16)], axis=0)
    tm_proj = _pick(nt, (1024, 512, 128))
    te = _pick(nt * TOP_K, EXPERT_TILES)
    outs = {k: [] for k in ("ret_p", "hg_p", "mk", "mv")}
    ret_s = hg_s = None
    for l in range(DEPTH):
        proj = _matmul(xb, w_in, l, tm_proj, 1280)
        kv_p = _matmul(mem2, w_mem_kv, l, _pick(b * n_mem, (1024, 512, 256)), 2 * HW)
        yr, ret_p, ret_s = _retention(proj, state_ret, l, cos_p, sin_p, cos_s, sin_s, gl,
                                      vec3(ret_norm_g), b, t, nb, ts, ret_s)
        yh, hg_p, hg_s = _hgrn(proj, state_hgrn, l, lbt, vec3(hgrn_norm_g), b, t, nb, ts, hg_s)
        yx = _cross_attention(proj, kv_p, cache_mem_k, cache_mem_v, l, b, t, nb, ts)
        x1, x1b, x1t = _merge((*yr, *yh, *yx), proj, x, w_up_ret, w_up_hgrn, w_up_xattn, w_out,
                              vec3(ln1_g), vec3(ln1_b), l)
        eidx, wn = _router(x1, w_router_t, b_router3, l)
        pos, cnt, off = _positions(eidx)
        tbl, n_steps = _step_table(cnt, off, nt * TOP_K, te)
        xs = _dispatch(pos, x1t.reshape(nt, PACKED_TILES, LANES))
        ye = _experts(tbl, n_steps, te, xs.reshape(-1, LANES), w_e_gate, w_e_up, w_e_down, l)
        ye = ye.reshape(-1, ROW_TILES, LANES)
        x_p, x_s, xb = _combine(pos, wn, ye, x1, x1b, w_s_gate, w_s_up, w_s_down,
                                vec3(ln2_g), vec3(ln2_b), l, n_p)
        x = (x_p, x_s)
        outs["ret_p"].append(ret_p)
        outs["hg_p"].append(hg_p)
        outs["mk"].append(kv_p[:, :HW].reshape(b, n_mem, HEADS, DH))
        outs["mv"].append(kv_p[:, HW:].reshape(b, n_mem, HEADS, DH))
    return (x[0].reshape(b, t, d), x[1].reshape(nb, ts, d),
            jnp.stack(outs["ret_p"]), jnp.stack(outs["hg_p"]), jnp.stack(outs["mk"]), jnp.stack(outs["mv"]),
            ret_s, hg_s)
```

```python
import functools

import jax
import jax.numpy as jnp
from jax import lax
from jax.experimental import pallas as pl
from jax.experimental.pallas import tpu as pltpu

F32 = jnp.float32
BF16 = jnp.bfloat16
I32 = jnp.int32
HIGHEST = lax.Precision.HIGHEST

D_MODEL = 1024
DEPTH = 2
PAST_LEN = 16384
HEADS = 4
DH = 128
HW = HEADS * DH
RET_CHUNK = 128
HG_CHUNK = 16
ROPE_BASE = 10000.0
N_EXPERTS = 64
N_GROUPS = 8
GROUP_SIZE = N_EXPERTS // N_GROUPS
TOPK_GROUPS = 4
TOP_K = 6
D_EXPERT = 256
ROUTED_SCALE = 2.5
LN_EPS = 1e-5
DN_ALPHA = (2 * DEPTH) ** 0.25
N_IN = 9 * HW + 3 * D_MODEL
COL_RET_Q, COL_RET_K, COL_RET_V, COL_RET_G = 0, 1, 2, 3
COL_HG_Q, COL_HG_F, COL_HG_I, COL_HG_G = 4, 5, 6, 7
COL_XA_Q = 8
COL_GATES = 9
LANES = 128
SUBLANES = 8
ROW_TILES = D_MODEL // LANES
PACKED_TILES = ROW_TILES // 2
U32 = jnp.uint32
SAMPLE_BB = 8
EXPERT_TILES = (512, 256)
ISSUE_UNROLL = 8
VMEM_LIMIT = 56 * 1024 * 1024


def _params(sem):
    return pltpu.CompilerParams(dimension_semantics=sem, vmem_limit_bytes=VMEM_LIMIT)


def _bdot(a, b):
    return jnp.dot(a.astype(BF16), b.astype(BF16), preferred_element_type=F32)


def _bdot_nt(a, b):
    return lax.dot_general(a.astype(BF16), b.astype(BF16), (((1,), (1,)), ((), ())),
                           preferred_element_type=F32)


def _bdot_tn(a, b):
    return lax.dot_general(a.astype(BF16), b.astype(BF16), (((0,), (0,)), ((), ())),
                           preferred_element_type=F32)


def _silu(x):
    return x * jax.nn.sigmoid(x)


def _pick(n, prefs):
    for p in prefs:
        if n % p == 0:
            return p
    raise ValueError(f"no tile for {n}")


def _mm_body(x_ref, w_ref, o_ref, wb_ref):
    @pl.when(pl.program_id(1) == 0)
    def _():
        wb_ref[...] = w_ref[...].astype(BF16)

    o_ref[...] = jnp.dot(x_ref[...].astype(BF16), wb_ref[...],
                         preferred_element_type=F32).astype(o_ref.dtype)


def _matmul(x, w, layer, tm, tn):
    m, k = x.shape
    n = w.shape[2]
    return pl.pallas_call(
        _mm_body,
        grid=(n // tn, m // tm),
        in_specs=[pl.BlockSpec((tm, k), lambda j, i: (i, 0)),
                  pl.BlockSpec((None, k, tn), lambda j, i: (layer, 0, j))],
        out_specs=pl.BlockSpec((tm, tn), lambda j, i: (i, j)),
        out_shape=jax.ShapeDtypeStruct((m, n), F32),
        scratch_shapes=[pltpu.VMEM((k, tn), BF16)],
        compiler_params=_params(("arbitrary", "arbitrary")),
        name="dense_matmul",
    )(x, w)


def _rotary(x, cos, sin_signed):
    return x * cos + pltpu.roll(x, DH // 2, 1) * sin_signed


def _group_norm_gate(o, gain, gate):
    mu = jnp.mean(o, axis=-1, keepdims=True)
    var = jnp.mean(jnp.square(o - mu), axis=-1, keepdims=True)
    return (o - mu) * lax.rsqrt(var + LN_EPS) * gain * _silu(gate)


def _ret_prompt_body(q_ref, k_ref, v_ref, g_ref, cos_ref, sin_ref, gl_ref, gain_ref,
                     y_ref, st_ref, s_scr):
    c = pl.program_id(1)

    @pl.when(c == 0)
    def _():
        s_scr[...] = jnp.zeros_like(s_scr)

    ch = RET_CHUNK
    cos = cos_ref[...]
    sin = sin_ref[...]
    ri = lax.broadcasted_iota(I32, (ch, ch), 0)
    ci = lax.broadcasted_iota(I32, (ch, ch), 1)
    rel = (ri - ci).astype(F32)
    idx = lax.broadcasted_iota(I32, (ch, DH), 0).astype(F32)
    for h in range(HEADS):
        sl = slice(h * DH, (h + 1) * DH)
        gl = gl_ref[h:h + 1, :]
        qr = _rotary(q_ref[:, sl], cos, sin)
        kr = _rotary(k_ref[:, sl], cos, sin) * (DH ** -0.5)
        v = v_ref[:, sl]
        intra = jnp.where(rel >= 0, jnp.exp(gl * rel), 0.0)
        att = _bdot_nt(qr, kr) * intra
        s = s_scr[h]
        o = _bdot(att, v) + _bdot(qr, s) * jnp.exp(gl * (idx + 1.0))
        s_scr[h] = s * jnp.exp(gl * float(ch)) + _bdot_tn(kr * jnp.exp(gl * (ch - 1.0 - idx)), v)
        y_ref[:, sl] = _group_norm_gate(o, gain_ref[:, sl], g_ref[:, sl]).astype(BF16)

    @pl.when(c == pl.num_programs(1) - 1)
    def _():
        st_ref[0] = s_scr[...]


def _ret_sample_body(q_ref, k_ref, v_ref, g_ref, cos_ref, sin_ref, gl_ref, gain_ref, sin_ref_state,
                     y_ref, st_ref, *, ts):
    rows = SAMPLE_BB * ts
    shift = ts.bit_length() - 1
    cos = cos_ref[...]
    sin = sin_ref[...]
    ri = lax.broadcasted_iota(I32, (rows, rows), 0)
    ci = lax.broadcasted_iota(I32, (rows, rows), 1)
    rel = (ri - ci).astype(F32)
    mask = ((ri >> shift) == (ci >> shift)) & (ri >= ci)
    idx = (lax.broadcasted_iota(I32, (rows, DH), 0) & (ts - 1)).astype(F32)
    for h in range(HEADS):
        sl = slice(h * DH, (h + 1) * DH)
        gl = gl_ref[h:h + 1, :]
        qr = _rotary(q_ref[:, sl], cos, sin)
        kr = _rotary(k_ref[:, sl], cos, sin) * (DH ** -0.5)
        v = v_ref[:, sl]
        intra = jnp.where(mask, jnp.exp(gl[:, :rows] * rel), 0.0)
        o_intra = _bdot(_bdot_nt(qr, kr) * intra, v)
        q_dec = jnp.exp(gl * (idx + 1.0))
        kd = kr * jnp.exp(gl * (ts - 1.0 - idx))
        c_dec = jnp.exp(gl * float(ts))
        outs = []
        for j in range(SAMPLE_BB):
            rs = slice(j * ts, (j + 1) * ts)
            s = sin_ref_state[j, h]
            outs.append(o_intra[rs] + _bdot(qr[rs], s) * q_dec[rs])
            new_state = s * c_dec + _bdot_tn(kd[rs], v[rs])
            for slot in range(st_ref.shape[0]):
                st_ref[slot, j, h] = new_state
        o = jnp.concatenate(outs, axis=0)
        y_ref[:, sl] = _group_norm_gate(o, gain_ref[:, sl], g_ref[:, sl]).astype(BF16)


def _proj_spec(rows, col, row_map):
    return pl.BlockSpec((rows, HW), lambda *a: (row_map(*a), col))


def _sample_state_call(body, grid, in_specs, args, y_shape, y_spec, layer, nb, prev, name):
    st_shape = jax.ShapeDtypeStruct((DEPTH, nb, HEADS, DH, DH), F32)
    slots = DEPTH if prev is None else 1
    st_spec = pl.BlockSpec((slots, SAMPLE_BB, HEADS, DH, DH), lambda i: (layer, i, 0, 0, 0))
    aliases = {}
    if prev is not None:
        n_in = len(args)
        inner = body
        body = lambda *refs: inner(*refs[:n_in], *refs[n_in + 1:])
        in_specs = in_specs + [pl.BlockSpec(memory_space=pl.ANY)]
        args = args + (prev,)
        aliases = {n_in: 1}
    return pl.pallas_call(
        body, grid=grid, in_specs=in_specs, out_specs=[y_spec, st_spec], out_shape=[y_shape, st_shape],
        input_output_aliases=aliases, compiler_params=_params(("arbitrary",)), name=name,
    )(*args)


def _retention(proj, state, layer, cos_p, sin_p, cos_s, sin_s, gl, gain, b, t, nb, ts, prev_s):
    n_p = b * t
    nc = t // RET_CHUNK
    prow = lambda bi, c: bi * nc + c
    const2 = lambda *a: (0, 0)
    y_p, st_p = pl.pallas_call(
        _ret_prompt_body,
        grid=(b, nc),
        in_specs=[_proj_spec(RET_CHUNK, COL_RET_Q, prow), _proj_spec(RET_CHUNK, COL_RET_K, prow),
                  _proj_spec(RET_CHUNK, COL_RET_V, prow), _proj_spec(RET_CHUNK, COL_RET_G, prow),
                  pl.BlockSpec((RET_CHUNK, DH), lambda bi, c: (c, 0)),
                  pl.BlockSpec((RET_CHUNK, DH), lambda bi, c: (c, 0)),
                  pl.BlockSpec((HEADS, DH), const2),
                  pl.BlockSpec((None, 1, HW), lambda bi, c: (layer, 0, 0))],
        out_specs=[pl.BlockSpec((RET_CHUNK, HW), lambda bi, c: (prow(bi, c), 0)),
                   pl.BlockSpec((1, HEADS, DH, DH), lambda bi, c: (bi, 0, 0, 0))],
        out_shape=[jax.ShapeDtypeStruct((n_p, HW), BF16),
                   jax.ShapeDtypeStruct((b, HEADS, DH, DH), F32)],
        scratch_shapes=[pltpu.VMEM((HEADS, DH, DH), F32)],
        compiler_params=_params(("arbitrary", "arbitrary")),
        name="retention_prompt",
    )(proj, proj, proj, proj, cos_p, sin_p, gl, gain)

    rows = SAMPLE_BB * ts
    base = n_p // rows
    srow = lambda i: base + i
    y_s, st_s = _sample_state_call(
        functools.partial(_ret_sample_body, ts=ts), (nb // SAMPLE_BB,),
        [_proj_spec(rows, COL_RET_Q, srow), _proj_spec(rows, COL_RET_K, srow),
         _proj_spec(rows, COL_RET_V, srow), _proj_spec(rows, COL_RET_G, srow),
         pl.BlockSpec((rows, DH), const2), pl.BlockSpec((rows, DH), const2),
         pl.BlockSpec((HEADS, DH), const2),
         pl.BlockSpec((None, 1, HW), lambda i: (layer, 0, 0)),
         pl.BlockSpec((None, SAMPLE_BB, HEADS, DH, DH), lambda i: (layer, i, 0, 0, 0))],
        (proj, proj, proj, proj, cos_s, sin_s, gl, gain, state),
        jax.ShapeDtypeStruct((nb * ts, HW), BF16), pl.BlockSpec((rows, HW), lambda i: (i, 0)),
        layer, nb, prev_s, "retention_sample")
    return (y_p, y_s), st_p, st_s


def _hg_prepare(hq_ref, hf_ref, lbt_ref, rows, chunk):
    shift = chunk.bit_length() - 1
    ri = lax.broadcasted_iota(I32, (rows, rows), 0)
    ci = lax.broadcasted_iota(I32, (rows, rows), 1)
    same = (ri >> shift) == (ci >> shift)
    causal = same & (ci <= ri)
    z = hf_ref[...]
    log_lb = lbt_ref[0:1, :]
    log_1m_lb = lbt_ref[1:2, :]
    one_m_lb = lbt_ref[2:3, :]
    log_sig = jnp.minimum(z, 0.0) - jnp.log1p(jnp.exp(-jnp.abs(z)))
    bterm = log_1m_lb + log_sig
    logf = jnp.maximum(log_lb, bterm) + jnp.log1p(jnp.exp(-jnp.abs(log_lb - bterm)))
    kh = one_m_lb * jax.nn.sigmoid(-z)
    qh = _silu(hq_ref[...]) * (DH ** -0.5)
    cum = jnp.dot(causal.astype(F32), logf, precision=HIGHEST, preferred_element_type=F32)
    tot = jnp.dot(same.astype(F32), logf, precision=HIGHEST, preferred_element_type=F32)
    qi = qh * jnp.exp(cum)
    ki = kh * jnp.exp(-cum)
    ke = kh * jnp.exp(tot - cum)
    return causal, qi, ki, ke, tot, logf


def _rms_norm_gate(o, gain, gate):
    return o * lax.rsqrt(jnp.mean(jnp.square(o), axis=-1, keepdims=True) + LN_EPS) * gain * _silu(gate)


def _hg_prompt_body(hq_ref, hf_ref, hi_ref, hg_ref, lbt_ref, gain_ref, y_ref, st_ref, s_scr):
    c = pl.program_id(1)

    @pl.when(c == 0)
    def _():
        s_scr[...] = jnp.zeros_like(s_scr)

    rows = RET_CHUNK
    n_sub = rows // HG_CHUNK
    shift = HG_CHUNK.bit_length() - 1
    causal, qi, ki, ke, tot, logf = _hg_prepare(hq_ref, hf_ref, lbt_ref, rows, HG_CHUNK)
    ri = lax.broadcasted_iota(I32, (rows, rows), 0)
    ci = lax.broadcasted_iota(I32, (rows, rows), 1)
    pre = jnp.dot(((ci >> shift) < (ri >> shift)).astype(F32), logf, precision=HIGHEST,
                  preferred_element_type=F32)
    sub = lax.broadcasted_iota(I32, (rows, DH), 0) >> shift
    v = hi_ref[...]
    for h in range(HEADS):
        sl = slice(h * DH, (h + 1) * DH)
        q_h, ke_h, v_h, pre_h = qi[:, sl], ke[:, sl], v[:, sl], pre[:, sl]
        att = jnp.where(causal, _bdot_nt(q_h, ki[:, sl]), 0.0)
        st0 = s_scr[h]
        o = _bdot(att, v_h) + _bdot_nt(q_h * jnp.exp(pre_h), st0)
        end_last = pre_h[rows - 1:rows] + tot[rows - 1:rows, sl]
        st = st0 * jnp.exp(end_last)
        for i in range(n_sub):
            rs = slice(i * HG_CHUNK, (i + 1) * HG_CHUNK)
            u_t = _bdot_tn(v_h[rs], ke_h[rs])
            if i + 1 < n_sub:
                end_i = pre_h[(i + 1) * HG_CHUNK:(i + 1) * HG_CHUNK + 1]
                later = q_h * jnp.exp(jnp.where(sub > i, pre_h - end_i, -jnp.inf))
                o = o + _bdot_nt(later, u_t)
                st = st + u_t * jnp.exp(end_last - end_i)
            else:
                st = st + u_t
        s_scr[h] = st
        y_ref[:, sl] = _rms_norm_gate(o, gain_ref[:, sl], hg_ref[:, sl]).astype(BF16)

    @pl.when(c == pl.num_programs(1) - 1)
    def _():
        for h in range(HEADS):
            st_ref[0, h] = s_scr[h].T


def _hg_sample_body(hq_ref, hf_ref, hi_ref, hg_ref, lbt_ref, gain_ref, sin_ref_state,
                    y_ref, st_ref, *, ts):
    rows = SAMPLE_BB * ts
    causal, qi, ki, ke, tot, _ = _hg_prepare(hq_ref, hf_ref, lbt_ref, rows, ts)
    etot = jnp.exp(tot)
    v = hi_ref[...]
    for h in range(HEADS):
        sl = slice(h * DH, (h + 1) * DH)
        att = jnp.where(causal, _bdot_nt(qi[:, sl], ki[:, sl]), 0.0)
        o_intra = _bdot(att, v[:, sl])
        outs = []
        for j in range(SAMPLE_BB):
            rs = slice(j * ts, (j + 1) * ts)
            s = sin_ref_state[j, h]
            outs.append(o_intra[rs] + _bdot(qi[rs, sl], s))
            scale = jnp.broadcast_to(etot[j * ts:j * ts + 1, sl], (DH, DH)).T
            new_state = s * scale + _bdot_tn(ke[rs, sl], v[rs, sl])
            for slot in range(st_ref.shape[0]):
                st_ref[slot, j, h] = new_state
        o = jnp.concatenate(outs, axis=0)
        y_ref[:, sl] = _rms_norm_gate(o, gain_ref[:, sl], hg_ref[:, sl]).astype(BF16)


def _hgrn(proj, state, layer, lbt, gain, b, t, nb, ts, prev_s):
    n_p = b * t
    nc = t // RET_CHUNK
    prow = lambda bi, c: bi * nc + c
    y_p, st_p = pl.pallas_call(
        _hg_prompt_body,
        grid=(b, nc),
        in_specs=[_proj_spec(RET_CHUNK, COL_HG_Q, prow), _proj_spec(RET_CHUNK, COL_HG_F, prow),
                  _proj_spec(RET_CHUNK, COL_HG_I, prow), _proj_spec(RET_CHUNK, COL_HG_G, prow),
                  pl.BlockSpec((None, SUBLANES, HW), lambda bi, c: (layer, 0, 0)),
                  pl.BlockSpec((None, 1, HW), lambda bi, c: (layer, 0, 0))],
        out_specs=[pl.BlockSpec((RET_CHUNK, HW), lambda bi, c: (prow(bi, c), 0)),
                   pl.BlockSpec((1, HEADS, DH, DH), lambda bi, c: (bi, 0, 0, 0))],
        out_shape=[jax.ShapeDtypeStruct((n_p, HW), BF16),
                   jax.ShapeDtypeStruct((b, HEADS, DH, DH), F32)],
        scratch_shapes=[pltpu.VMEM((HEADS, DH, DH), F32)],
        compiler_params=_params(("arbitrary", "arbitrary")),
        name="hgrn_prompt",
    )(proj, proj, proj, proj, lbt, gain)

    rows = SAMPLE_BB * ts
    base = n_p // rows
    srow = lambda i: base + i
    y_s, st_s = _sample_state_call(
        functools.partial(_hg_sample_body, ts=ts), (nb // SAMPLE_BB,),
        [_proj_spec(rows, COL_HG_Q, srow), _proj_spec(rows, COL_HG_F, srow),
         _proj_spec(rows, COL_HG_I, srow), _proj_spec(rows, COL_HG_G, srow),
         pl.BlockSpec((None, SUBLANES, HW), lambda i: (layer, 0, 0)),
         pl.BlockSpec((None, 1, HW), lambda i: (layer, 0, 0)),
         pl.BlockSpec((None, SAMPLE_BB, HEADS, DH, DH), lambda i: (layer, i, 0, 0, 0))],
        (proj, proj, proj, proj, lbt, gain, state),
        jax.ShapeDtypeStruct((nb * ts, HW), BF16), pl.BlockSpec((rows, HW), lambda i: (i, 0)),
        layer, nb, prev_s, "hgrn_sample")
    return (y_p, y_s), st_p, st_s


def _softmax_rows(s):
    e = jnp.exp(s - jnp.max(s, axis=-1, keepdims=True))
    return e / jnp.sum(e, axis=-1, keepdims=True)


def _xa_prompt_body(q_ref, k_ref, v_ref, y_ref):
    for h in range(HEADS):
        sl = slice(h * DH, (h + 1) * DH)
        a = _softmax_rows(_bdot_nt(q_ref[:, sl] * (DH ** -0.5), k_ref[:, sl]))
        y_ref[:, sl] = _bdot(a, v_ref[:, sl]).astype(BF16)


def _xa_sample_body(q_ref, k_ref, v_ref, y_ref, *, ts):
    n_mem = k_ref.shape[1] // HEADS
    pairs = [(j, h) for j in range(SAMPLE_BB) for h in range(HEADS)]
    q = q_ref[...] * (DH ** -0.5)
    scores = [_bdot_nt(q[j * ts:(j + 1) * ts, h * DH:(h + 1) * DH], k_ref[j, pl.ds(h, n_mem, stride=HEADS), :])
              for j, h in pairs]
    a = _softmax_rows(jnp.concatenate(scores, axis=0))
    for n, (j, h) in enumerate(pairs):
        y = _bdot(a[n * ts:(n + 1) * ts], v_ref[j, pl.ds(h, n_mem, stride=HEADS), :])
        y_ref[j * ts:(j + 1) * ts, h * DH:(h + 1) * DH] = y.astype(BF16)


def _cross_attention(proj, kv_p, cache_k, cache_v, layer, b, t, nb, ts):
    n_p = b * t
    n_mem = kv_p.shape[0] // b
    tq = _pick(t, (512, 256, 128))
    nq = t // tq
    y_p = pl.pallas_call(
        _xa_prompt_body,
        grid=(b, nq),
        in_specs=[_proj_spec(tq, COL_XA_Q, lambda bi, qi: bi * nq + qi),
                  pl.BlockSpec((n_mem, HW), lambda bi, qi: (bi, 0)),
                  pl.BlockSpec((n_mem, HW), lambda bi, qi: (bi, 1))],
        out_specs=pl.BlockSpec((tq, HW), lambda bi, qi: (bi * nq + qi, 0)),
        out_shape=jax.ShapeDtypeStruct((n_p, HW), BF16),
        compiler_params=_params(("arbitrary", "arbitrary")),
        name="xattn_prompt",
    )(proj, kv_p, kv_p)

    rows = SAMPLE_BB * ts
    base = n_p // rows
    cache_k = cache_k.reshape(DEPTH, nb, n_mem * HEADS, DH)
    cache_v = cache_v.reshape(DEPTH, nb, n_mem * HEADS, DH)
    kv_spec = pl.BlockSpec((None, SAMPLE_BB, n_mem * HEADS, DH), lambda i: (layer, i, 0, 0))
    y_s = pl.pallas_call(
        functools.partial(_xa_sample_body, ts=ts),
        grid=(nb // SAMPLE_BB,),
        in_specs=[_proj_spec(rows, COL_XA_Q, lambda i: base + i), kv_spec, kv_spec],
        out_specs=pl.BlockSpec((rows, HW), lambda i: (i, 0)),
        out_shape=jax.ShapeDtypeStruct((nb * ts, HW), BF16),
        compiler_params=_params(("arbitrary",)),
        name="xattn_sample",
    )(proj, cache_k, cache_v)
    return (y_p, y_s)


def _layer_norm(tv, g, b):
    mu = jnp.mean(tv, axis=-1, keepdims=True)
    var = jnp.mean(jnp.square(tv - mu), axis=-1, keepdims=True)
    return (tv - mu) * lax.rsqrt(var + LN_EPS) * g + b


def _merge_body(yrp_ref, yrs_ref, yhp_ref, yhs_ref, yxp_ref, yxs_ref, g0a, g0b, g1a, g1b, g2a, g2b, xp_ref, xs_ref,
                wr_ref, wh_ref, wx_ref, wo_ref, lg_ref, lb_ref,
                x1_ref, x1b_ref, x1t_ref, wr_s, wh_s, wx_s, wo_s, *, prompt_tiles):
    @pl.when(pl.program_id(0) == 0)
    def _():
        wr_s[...] = wr_ref[...].astype(BF16)
        wh_s[...] = wh_ref[...].astype(BF16)
        wx_s[...] = wx_ref[...].astype(BF16)
        wo_s[...] = wo_ref[...].astype(BF16)

    is_prompt = pl.program_id(0) < prompt_tiles

    def branch(yp_ref, ys_ref, w_s, ga, gb):
        y = jnp.where(is_prompt, yp_ref[...], ys_ref[...])
        gate = jax.nn.sigmoid(jnp.concatenate([ga[...], gb[...]], axis=-1))
        return gate * jnp.dot(y, w_s[...], preferred_element_type=F32)

    m = (branch(yrp_ref, yrs_ref, wr_s, g0a, g0b) + branch(yhp_ref, yhs_ref, wh_s, g1a, g1b)
         + branch(yxp_ref, yxs_ref, wx_s, g2a, g2b))
    hmix = jnp.dot(m.astype(BF16), wo_s[...], preferred_element_type=F32)
    x = jnp.where(is_prompt, xp_ref[...], xs_ref[...])
    x1 = _layer_norm(DN_ALPHA * x + hmix, lg_ref[...], lb_ref[...])
    x1_ref[...] = x1
    x1b_ref[...] = x1.astype(BF16)
    tm = x1.shape[0]
    bits = lax.bitcast_convert_type(x1.astype(BF16).astype(F32), U32)
    half = D_MODEL // 2
    packed = (bits[:, :half] >> 16) | (bits[:, half:] & jnp.uint32(0xFFFF0000))
    for s in range(PACKED_TILES):
        x1t_ref[pl.ds(s, tm, stride=PACKED_TILES), :] = packed[:, s * LANES:(s + 1) * LANES]


def _merge(ys, proj, x, w_up_ret, w_up_hgrn, w_up_xattn, w_out, ln_g, ln_b, layer):
    n_p = ys[0].shape[0]
    nt = n_p + ys[1].shape[0]
    tm = _pick(nt, (256, 128))
    assert n_p % tm == 0 and ys[1].shape[0] % tm == 0
    p_tiles = n_p // tm
    row = lambda i: (i, 0)
    p_map = lambda i: (jnp.minimum(i, p_tiles - 1), 0)
    s_map = lambda i: (jnp.maximum(i - p_tiles, 0), 0)
    y_specs = [pl.BlockSpec((tm, HW), p_map), pl.BlockSpec((tm, HW), s_map)] * 3
    wspec = lambda k: pl.BlockSpec((None, k, D_MODEL), lambda i: (layer, 0, 0))
    vec = pl.BlockSpec((None, 1, D_MODEL), lambda i: (layer, 0, 0))
    gate_specs = [pl.BlockSpec((tm, HW), lambda i, c=c: (i, COL_GATES + c)) for c in range(6)]
    return pl.pallas_call(
        functools.partial(_merge_body, prompt_tiles=p_tiles),
        grid=(nt // tm,),
        in_specs=y_specs + gate_specs + [pl.BlockSpec((tm, D_MODEL), p_map), pl.BlockSpec((tm, D_MODEL), s_map),
                  wspec(HW), wspec(HW), wspec(HW), wspec(D_MODEL), vec, vec],
        out_specs=[pl.BlockSpec((tm, D_MODEL), row), pl.BlockSpec((tm, D_MODEL), row),
                   pl.BlockSpec((tm * PACKED_TILES, LANES), row)],
        out_shape=[jax.ShapeDtypeStruct((nt, D_MODEL), F32),
                   jax.ShapeDtypeStruct((nt, D_MODEL), BF16),
                   jax.ShapeDtypeStruct((nt * PACKED_TILES, LANES), U32)],
        scratch_shapes=[pltpu.VMEM((HW, D_MODEL), BF16)] * 3 + [pltpu.VMEM((D_MODEL, D_MODEL), BF16)],
        compiler_params=_params(("arbitrary",)),
        name="merge_out_ln1",
    )(*ys, *([proj] * 6), *x, w_up_ret, w_up_hgrn, w_up_xattn, w_out, ln_g, ln_b)


def _router_body(x_ref, wt_ref, b_ref, eidx_ref, wn_ref):
    tm = x_ref.shape[0]
    x = x_ref[...]
    w = wt_ref[...]
    xh = x.astype(BF16)
    xl = (x - xh.astype(F32)).astype(BF16)
    wh = w.astype(BF16)
    wl = (w - wh.astype(F32)).astype(BF16)
    logits = _bdot_nt(wh, xh) + (_bdot_nt(wh, xl) + _bdot_nt(wl, xh))
    s = jax.nn.sigmoid(logits)
    sel = s + b_ref[...]
    neg = -jnp.inf
    groups = [sel[g * GROUP_SIZE:(g + 1) * GROUP_SIZE, :] for g in range(N_GROUPS)]
    ie = lax.broadcasted_iota(I32, (GROUP_SIZE, tm), 0).astype(F32)
    rows = []
    for blk in groups:
        m1 = jnp.max(blk, axis=0, keepdims=True)
        first = jnp.min(jnp.where(blk == m1, ie, float(GROUP_SIZE)), axis=0, keepdims=True)
        rows.append(m1 + jnp.max(jnp.where(ie == first, neg, blk), axis=0, keepdims=True))
    gscore = jnp.concatenate(rows, axis=0)
    ig = lax.broadcasted_iota(I32, gscore.shape, 0).astype(F32)
    gmask = jnp.zeros(gscore.shape, F32)
    for _ in range(TOPK_GROUPS):
        m = jnp.max(gscore, axis=0, keepdims=True)
        gi = jnp.min(jnp.where(gscore == m, ig, float(N_GROUPS)), axis=0, keepdims=True)
        hit = ig == gi
        gmask = jnp.where(hit, 1.0, gmask)
        gscore = jnp.where(hit, neg, gscore)
    masked = jnp.concatenate([jnp.where(gmask[g:g + 1, :] > 0.5, blk, neg)
                              for g, blk in enumerate(groups)], axis=0)
    ix = lax.broadcasted_iota(I32, masked.shape, 0).astype(F32)
    idxs, ws = [], []
    for _ in range(TOP_K):
        m = jnp.max(masked, axis=0, keepdims=True)
        ei = jnp.min(jnp.where(masked == m, ix, float(N_EXPERTS)), axis=0, keepdims=True)
        hit = ix == ei
        idxs.append(ei)
        ws.append(jnp.sum(jnp.where(hit, s, 0.0), axis=0, keepdims=True))
        masked = jnp.where(hit, neg, masked)
    wsum = ws[0]
    for w in ws[1:]:
        wsum = wsum + w
    pad = [jnp.zeros((1, tm), F32)] * (SUBLANES - TOP_K)
    eidx_ref[...] = jnp.concatenate(idxs + pad, axis=0).astype(I32)
    wn_ref[...] = jnp.concatenate([w / wsum * ROUTED_SCALE for w in ws] + pad, axis=0)


def _router(x1, w_router_t, b_router, layer):
    nt = x1.shape[0]
    tm = _pick(nt, (512, 256, 128))
    return pl.pallas_call(
        _router_body,
        grid=(nt // tm,),
        in_specs=[pl.BlockSpec((tm, D_MODEL), lambda i: (i, 0)),
                  pl.BlockSpec((None, N_EXPERTS, D_MODEL), lambda i: (layer, 0, 0)),
                  pl.BlockSpec((None, N_EXPERTS, 1), lambda i: (layer, 0, 0))],
        out_specs=[pl.BlockSpec((SUBLANES, tm), lambda i: (0, i))] * 2,
        out_shape=[jax.ShapeDtypeStruct((SUBLANES, nt), I32),
                   jax.ShapeDtypeStruct((SUBLANES, nt), F32)],
        compiler_params=_params(("arbitrary",)),
        name="moe_router",
    )(x1, w_router_t, b_router)


def _positions_body(eidx_ref, pos_ref, lpos_ref, runs_ref, cnt_ref, off_ref, base_scr, off_scr):
    phase = pl.program_id(0)
    i = pl.program_id(1)
    tp = eidx_ref.shape[1]
    ix = lax.broadcasted_iota(I32, (N_EXPERTS, tp), 0)
    eidx = eidx_ref[...]
    member = jnp.zeros((N_EXPERTS, tp), F32)
    for k in range(TOP_K):
        member = member + (ix == eidx[k:k + 1, :]).astype(F32)
    tile_cnt = jnp.sum(member, axis=1, keepdims=True)

    @pl.when((phase == 0) & (i == 0))
    def _():
        base_scr[...] = jnp.zeros_like(base_scr)

    @pl.when((phase == 1) & (i == 0))
    def _():
        cnt = base_scr[...]
        er = lax.broadcasted_iota(I32, (N_EXPERTS, N_EXPERTS), 0)
        ec = lax.broadcasted_iota(I32, (N_EXPERTS, N_EXPERTS), 1)
        off = jnp.dot((ec < er).astype(F32), cnt, precision=HIGHEST, preferred_element_type=F32)
        off_scr[...] = off
        cnt_ref[...] = cnt
        off_ref[...] = off
        base_scr[...] = jnp.zeros_like(base_scr)

    @pl.when(phase == 1)
    def _():
        tr = lax.broadcasted_iota(I32, (tp, tp), 0)
        tc = lax.broadcasted_iota(I32, (tp, tp), 1)
        before = jnp.dot(member.astype(BF16), (tr < tc).astype(BF16), preferred_element_type=F32)
        er = lax.broadcasted_iota(I32, (N_EXPERTS, N_EXPERTS), 0)
        ec = lax.broadcasted_iota(I32, (N_EXPERTS, N_EXPERTS), 1)
        run_len = jnp.broadcast_to(tile_cnt, (N_EXPERTS, LANES))
        run_local = jnp.dot((ec < er).astype(F32), run_len, precision=HIGHEST, preferred_element_type=F32)
        run_global = off_scr[...] + base_scr[...]
        pad = [jnp.zeros((1, tp), F32)] * (SUBLANES - TOP_K)

        def per_pair(run_start):
            where_to = before + run_start[:, 0:1]
            rows = [jnp.sum(jnp.where(ix == eidx[k:k + 1, :], where_to, 0.0), axis=0, keepdims=True)
                    for k in range(TOP_K)]
            return jnp.concatenate(rows + pad, axis=0).astype(I32)

        pos_ref[...] = per_pair(run_global)
        lpos_ref[...] = per_pair(run_local)
        lane = lax.broadcasted_iota(I32, (N_EXPERTS, LANES), 1)
        runs_ref[...] = jnp.where(lane == R_GLOBAL, run_global,
                                  jnp.where(lane == R_COUNT, run_len, run_local)).astype(I32)

    base_scr[...] = base_scr[...] + tile_cnt


def _positions(eidx, tp):
    nt = eidx.shape[1]
    const = lambda p, i: (0, 0)
    return pl.pallas_call(
        _positions_body,
        grid=(2, nt // tp),
        in_specs=[pl.BlockSpec((SUBLANES, tp), lambda p, i: (0, i))],
        out_specs=[pl.BlockSpec((SUBLANES, tp), lambda p, i: (0, i * p)),
                   pl.BlockSpec((SUBLANES, tp), lambda p, i: (0, i * p)),
                   pl.BlockSpec((N_EXPERTS, LANES), lambda p, i: (i * p, 0)),
                   pl.BlockSpec((N_EXPERTS, LANES), const), pl.BlockSpec((N_EXPERTS, LANES), const)],
        out_shape=[jax.ShapeDtypeStruct((SUBLANES, nt), I32),
                   jax.ShapeDtypeStruct((SUBLANES, nt), I32),
                   jax.ShapeDtypeStruct((nt // tp * N_EXPERTS, LANES), I32),
                   jax.ShapeDtypeStruct((N_EXPERTS, LANES), F32),
                   jax.ShapeDtypeStruct((N_EXPERTS, LANES), F32)],
        scratch_shapes=[pltpu.VMEM((N_EXPERTS, LANES), F32), pltpu.VMEM((N_EXPERTS, LANES), F32)],
        compiler_params=_params(("arbitrary", "arbitrary")),
        name="moe_positions",
    )(eidx)


T_TILE, T_EXPERT, T_LO, T_HI, T_FRESH, T_NEWEXP = range(6)
R_GLOBAL, R_COUNT, R_LOCAL = range(3)
RUN_CHUNK_BITS = 6


def _table_body(cnt_ref, off_ref, tbl_ref, *, tile_rows):
    te = float(tile_rows)
    n = tbl_ref.shape[1]
    cnt = cnt_ref[...]
    off = off_ref[...]
    first = jnp.floor(off * (1.0 / te))
    last = jnp.floor((off + cnt - 1.0) * (1.0 / te))
    nst = jnp.where(cnt > 0.0, last - first + 1.0, 0.0)
    er = lax.broadcasted_iota(I32, (N_EXPERTS, N_EXPERTS), 0)
    ec = lax.broadcasted_iota(I32, (N_EXPERTS, N_EXPERTS), 1)
    s_end = jnp.dot((ec <= er).astype(F32), nst, precision=HIGHEST, preferred_element_type=F32)
    s_beg = s_end - nst
    total = s_end[N_EXPERTS - 1:N_EXPERTS, 0:1]
    sidx = lax.broadcasted_iota(I32, (1, n), 1).astype(F32)
    s = jnp.minimum(sidx, total - 1.0)
    e_s = jnp.sum((s_end[:, 0:1] <= s).astype(F32), axis=0, keepdims=True)
    hot = lax.broadcasted_iota(I32, (N_EXPERTS, n), 0).astype(F32) == e_s

    def pick(col):
        return jnp.sum(jnp.where(hot, col[:, 0:1], 0.0), axis=0, keepdims=True)

    tile = pick(first) + s - pick(s_beg)
    valid = sidx < total
    o, c = pick(off), pick(cnt)
    lo = jnp.where(valid, jnp.maximum(o, tile * te), 0.0)
    hi = jnp.where(valid, jnp.minimum(o + c, (tile + 1.0) * te), 0.0)
    head = sidx == 0.0
    fresh = jnp.where((tile != pltpu.roll(tile, 1, 1)) | head, 1.0, 0.0)
    newexp = jnp.where((e_s != pltpu.roll(e_s, 1, 1)) | head, 1.0, 0.0)
    pad = [jnp.zeros((1, n), F32)] * (SUBLANES - 6)
    tbl_ref[...] = jnp.concatenate([tile, e_s, lo, hi, fresh, newexp] + pad, axis=0).astype(I32)


def _step_table(cnt, off, n_rows, te):
    n_steps = n_rows // te + N_EXPERTS
    width = -(-n_steps // LANES) * LANES
    tbl = pl.pallas_call(
        functools.partial(_table_body, tile_rows=te),
        out_shape=jax.ShapeDtypeStruct((SUBLANES, width), I32),
        name="moe_step_table",
    )(cnt, off)
    return tbl, n_steps


def _start_run_copies(count, copy):
    chunk = 1 << RUN_CHUNK_BITS
    chunks = count >> RUN_CHUNK_BITS

    def whole(j, carry):
        copy(j * chunk, chunk).start()
        return carry

    lax.fori_loop(0, chunks, whole, 0)
    done = chunks << RUN_CHUNK_BITS
    for bit in reversed(range(RUN_CHUNK_BITS)):
        size = 1 << bit
        part = count & size

        @pl.when(part != 0)
        def _(done=done, size=size):
            copy(done, size).start()

        done = done + part


def _wait_tile_rows(like_src, dst_rows_ref, sem_ref):
    rows = like_src.shape[0]
    for _ in range(TOP_K):
        pltpu.make_async_copy(like_src, dst_rows_ref.at[pl.ds(0, rows)], sem_ref).wait()


def _dispatch_body(pos_ref, xt_ref, xs_ref, pos_s, sem_p, sem):
    td = pos_ref.shape[1]
    cp = pltpu.make_async_copy(pos_ref, pos_s, sem_p)
    cp.start()
    cp.wait()

    def issue(g, carry):
        for u in range(ISSUE_UNROLL):
            r = g * ISSUE_UNROLL + u
            for k in range(TOP_K):
                pltpu.make_async_copy(xt_ref.at[r], xs_ref.at[pos_s[k, r]], sem).start(priority=k % 2)
        return carry

    lax.fori_loop(0, td // ISSUE_UNROLL, issue, 0)
    _wait_tile_rows(xt_ref, xs_ref, sem)


def _dispatch(pos, x1t):
    nt = x1t.shape[0]
    td = _pick(nt, (512, 256, 128))
    return pl.pallas_call(
        _dispatch_body,
        grid=(nt // td,),
        in_specs=[pl.BlockSpec((SUBLANES, td), lambda i: (0, i)),
                  pl.BlockSpec((td,) + x1t.shape[1:], lambda i: (i, 0, 0))],
        out_specs=pl.BlockSpec(memory_space=pl.ANY),
        out_shape=jax.ShapeDtypeStruct((nt * TOP_K,) + x1t.shape[1:], x1t.dtype),
        scratch_shapes=[pltpu.SMEM((SUBLANES, td), I32), pltpu.SemaphoreType.DMA, pltpu.SemaphoreType.DMA],
        compiler_params=_params(("arbitrary",)),
        name="moe_dispatch",
    )(pos, x1t)


def _experts_body(tbl_ref, xs_ref, wg_ref, wu_ref, wd_ref, ye_ref, wg_s, wu_s, wd_s):
    s = pl.program_id(0)
    te = xs_ref.shape[0] // PACKED_TILES
    lo = tbl_ref[T_LO, s]
    hi = tbl_ref[T_HI, s]

    @pl.when(tbl_ref[T_NEWEXP, s] == 1)
    def _():
        wg_s[...] = wg_ref[...].astype(BF16)
        wu_s[...] = wu_ref[...].astype(BF16)
        wd_s[...] = wd_ref[...].astype(BF16)

    @pl.when(tbl_ref[T_FRESH, s] == 1)
    def _():
        ye_ref[...] = jnp.zeros_like(ye_ref)

    @pl.when(hi > lo)
    def _():
        words = [xs_ref[pl.ds(t, te, stride=PACKED_TILES), :] for t in range(PACKED_TILES)]
        low = [lax.bitcast_convert_type(w << 16, F32).astype(BF16) for w in words]
        high = [lax.bitcast_convert_type(w & jnp.uint32(0xFFFF0000), F32).astype(BF16) for w in words]
        x = jnp.concatenate(low + high, axis=-1)
        g = jnp.dot(x, wg_s[...], preferred_element_type=F32)
        u = jnp.dot(x, wu_s[...], preferred_element_type=F32)
        y = jnp.dot((_silu(g) * u).astype(BF16), wd_s[...], preferred_element_type=F32)
        row = tbl_ref[T_TILE, s] * te + lax.broadcasted_iota(I32, (te, LANES), 0)
        mine = (row >= lo) & (row < hi)
        for t in range(ROW_TILES):
            sl = pl.ds(t, te, stride=ROW_TILES)
            ye_ref[sl, :] = jnp.where(mine, y[:, t * LANES:(t + 1) * LANES], ye_ref[sl, :])


def _experts(tbl, n_steps, te, xs, w_gate, w_up, w_down, layer):
    n_rows = xs.shape[0] // PACKED_TILES
    tile_map = lambda s, tbl: (tbl[T_TILE, s], 0)
    w_map = lambda s, tbl: (layer, tbl[T_EXPERT, s], 0, 0)
    w_in_spec = pl.BlockSpec((None, None, D_MODEL, D_EXPERT), w_map)
    w_dn_spec = pl.BlockSpec((None, None, D_EXPERT, D_MODEL), w_map)
    return pl.pallas_call(
        _experts_body,
        grid_spec=pltpu.PrefetchScalarGridSpec(
            num_scalar_prefetch=1,
            grid=(n_steps,),
            in_specs=[pl.BlockSpec((te * PACKED_TILES, LANES), tile_map), w_in_spec, w_in_spec, w_dn_spec],
            out_specs=pl.BlockSpec((te * ROW_TILES, LANES), tile_map),
            scratch_shapes=[pltpu.VMEM((D_MODEL, D_EXPERT), BF16), pltpu.VMEM((D_MODEL, D_EXPERT), BF16),
                            pltpu.VMEM((D_EXPERT, D_MODEL), BF16)]),
        out_shape=jax.ShapeDtypeStruct((n_rows * ROW_TILES, LANES), F32),
        compiler_params=_params(("arbitrary",)),
        name="moe_experts",
    )(tbl, xs, w_gate, w_up, w_down)


def _combine_body(lpos_ref, wn_ref, runs_ref, ye_ref, x1_ref, x1b_ref, wsg_ref, wsu_ref, wsd_ref, lg_ref, lb_ref,
                  x2p_ref, x2s_ref, x2b_ref, lpos_s, wn_s, runs_s, buf, acc, wsg_s, wsu_s, wsd_s, sem_c, sem,
                  *, prompt_tiles):
    i = pl.program_id(0)
    n = pl.num_programs(0)
    tc = wn_s.shape[1]
    slot = i % 2
    groups = tc // ISSUE_UNROLL
    half = TOP_K * tc

    def fetch(tile, into):
        cr = pltpu.make_async_copy(runs_ref.at[tile], runs_s, sem_c.at[0])
        cr.start()
        cr.wait()

        def recv(e, carry):
            src = runs_s[e, R_GLOBAL]
            dst = into * half + runs_s[e, R_LOCAL]
            _start_run_copies(
                runs_s[e, R_COUNT],
                lambda off, size: pltpu.make_async_copy(ye_ref.at[pl.ds(src + off, size)],
                                                        buf.at[pl.ds(dst + off, size)], sem.at[into]))
            return carry

        lax.fori_loop(0, N_EXPERTS, recv, 0)

    @pl.when(i == 0)
    def _():
        wsg_s[...] = wsg_ref[...].astype(BF16)
        wsu_s[...] = wsu_ref[...].astype(BF16)
        wsd_s[...] = wsd_ref[...].astype(BF16)
        fetch(0, 0)

    @pl.when(i + 1 < n)
    def _():
        fetch(i + 1, 1 - slot)

    cl = pltpu.make_async_copy(lpos_ref.at[i], lpos_s, sem_c.at[0])
    cw = pltpu.make_async_copy(wn_ref.at[i], wn_s, sem_c.at[1])
    cl.start()
    cw.start()
    xb = x1b_ref[...]
    hs = _silu(jnp.dot(xb, wsg_s[...], preferred_element_type=F32)) * jnp.dot(xb, wsu_s[...], preferred_element_type=F32)
    shared = jnp.dot(hs.astype(BF16), wsd_s[...], preferred_element_type=F32)
    cl.wait()
    cw.wait()
    for k in range(TOP_K):
        pltpu.make_async_copy(ye_ref.at[pl.ds(0, tc)], buf.at[pl.ds(0, tc)], sem.at[slot]).wait()
    mine = slot * half

    def reduce_rows(g, carry):
        for u in range(ISSUE_UNROLL):
            r = g * ISSUE_UNROLL + u
            tot = buf[mine + lpos_s[0, r]] * wn_s[0, r]
            for k in range(1, TOP_K):
                tot = tot + buf[mine + lpos_s[k, r]] * wn_s[k, r]
            acc[pl.ds(pl.multiple_of(r * ROW_TILES, ROW_TILES), ROW_TILES), :] = tot
        return carry

    lax.fori_loop(0, groups, reduce_rows, 0)
    routed = jnp.concatenate([acc[pl.ds(t, tc, stride=ROW_TILES), :] for t in range(ROW_TILES)], axis=-1)
    x2 = _layer_norm(DN_ALPHA * x1_ref[...] + (routed + shared), lg_ref[...], lb_ref[...])
    x2b_ref[...] = x2.astype(BF16)

    @pl.when(i < prompt_tiles)
    def _():
        x2p_ref[...] = x2

    @pl.when(i >= prompt_tiles)
    def _():
        x2s_ref[...] = x2


def _combine(lpos, wn, runs, ye, x1, x1b, w_s_gate, w_s_up, w_s_down, ln_g, ln_b, layer, n_p, tc):
    nt = x1.shape[0]
    assert n_p % tc == 0
    n_tiles = nt // tc
    p_tiles = n_p // tc
    d_sh = w_s_gate.shape[2]
    lpos3 = lpos.reshape(SUBLANES, n_tiles, tc).transpose(1, 0, 2)
    wn3 = wn.reshape(SUBLANES, n_tiles, tc).transpose(1, 0, 2)
    runs3 = runs.reshape(n_tiles, N_EXPERTS, LANES)
    row = lambda i: (i, 0)
    whole = pl.BlockSpec((n_tiles, SUBLANES, tc), lambda i: (0, 0, 0))
    vec = pl.BlockSpec((None, 1, D_MODEL), lambda i: (layer, 0, 0))
    return pl.pallas_call(
        functools.partial(_combine_body, prompt_tiles=p_tiles),
        grid=(n_tiles,),
        in_specs=[whole, whole, pl.BlockSpec((n_tiles, N_EXPERTS, LANES), lambda i: (0, 0, 0)),
                  pl.BlockSpec(memory_space=pl.ANY),
                  pl.BlockSpec((tc, D_MODEL), row), pl.BlockSpec((tc, D_MODEL), row),
                  pl.BlockSpec((None, D_MODEL, d_sh), lambda i: (layer, 0, 0)),
                  pl.BlockSpec((None, D_MODEL, d_sh), lambda i: (layer, 0, 0)),
                  pl.BlockSpec((None, d_sh, D_MODEL), lambda i: (layer, 0, 0)), vec, vec],
        out_specs=[pl.BlockSpec((tc, D_MODEL), lambda i: (jnp.minimum(i, p_tiles - 1), 0)),
                   pl.BlockSpec((tc, D_MODEL), lambda i: (jnp.maximum(i - p_tiles, 0), 0)),
                   pl.BlockSpec((tc, D_MODEL), row)],
        out_shape=[jax.ShapeDtypeStruct((n_p, D_MODEL), F32), jax.ShapeDtypeStruct((nt - n_p, D_MODEL), F32),
                   jax.ShapeDtypeStruct((nt, D_MODEL), BF16)],
        scratch_shapes=[pltpu.SMEM((SUBLANES, tc), I32), pltpu.SMEM((SUBLANES, tc), F32),
                        pltpu.SMEM((N_EXPERTS, LANES), I32),
                        pltpu.VMEM((2 * TOP_K * tc, ROW_TILES, LANES), F32),
                        pltpu.VMEM((tc * ROW_TILES, LANES), F32),
                        pltpu.VMEM((D_MODEL, d_sh), BF16), pltpu.VMEM((D_MODEL, d_sh), BF16),
                        pltpu.VMEM((d_sh, D_MODEL), BF16),
                        pltpu.SemaphoreType.DMA((2,)), pltpu.SemaphoreType.DMA((2,))],
        compiler_params=_params(("arbitrary",)),
        name="moe_combine_ln2",
    )(lpos3, wn3, runs3, ye, x1, x1b, w_s_gate, w_s_up, w_s_down, ln_g, ln_b)


def _rope_tables(t, pos0):
    inv = 1.0 / (ROPE_BASE ** (jnp.arange(0, DH, 2, dtype=F32) / DH))
    ang = (jnp.arange(t, dtype=F32) + pos0)[:, None] * inv[None, :]
    cos, sin = jnp.cos(ang), jnp.sin(ang)
    return jnp.concatenate([cos, cos], axis=-1), jnp.concatenate([-sin, sin], axis=-1)


def kernel(x_prompt, x_sample, mem_prompt, state_ret, state_hgrn, cache_mem_k, cache_mem_v, w_in, w_up_ret, w_up_hgrn, w_up_xattn, w_out, w_mem_kv, ret_norm_g, hgrn_norm_g, lb_logits, ln1_g, ln1_b, ln2_g, ln2_b, w_router, b_router, w_e_gate, w_e_up, w_e_down, w_s_gate, w_s_up, w_s_down):
    b, t, d = x_prompt.shape
    nb, ts, _ = x_sample.shape
    n_mem = mem_prompt.shape[1]
    assert d == D_MODEL and t % RET_CHUNK == 0 and nb % SAMPLE_BB == 0
    assert ts & (ts - 1) == 0 and HG_CHUNK % ts == 0 and RET_CHUNK % ts == 0
    n_p, n_s = b * t, nb * ts
    nt = n_p + n_s
    assert n_p % (SAMPLE_BB * ts) == 0

    lb_cum = jnp.cumsum(jax.nn.softmax(lb_logits.astype(F32), axis=0), axis=0)
    lbs = lb_cum - lb_cum[0:1]
    lbt = jnp.stack([jnp.log(lbs), jnp.log1p(-lbs), 1.0 - lbs] + [jnp.zeros_like(lbs)] * (SUBLANES - 3), axis=1)
    gl = jnp.broadcast_to(jnp.log1p(-jnp.exp2(-5.0 - jnp.arange(HEADS, dtype=F32)))[:, None], (HEADS, DH))
    cos_p, sin_p = _rope_tables(t, 0)
    cos_s, sin_s = _rope_tables(ts, PAST_LEN)
    cos_s, sin_s = jnp.tile(cos_s, (SAMPLE_BB, 1)), jnp.tile(sin_s, (SAMPLE_BB, 1))
    vec3 = lambda a: a.reshape(DEPTH, 1, -1)
    w_router_t = jnp.swapaxes(w_router, 1, 2)
    b_router3 = b_router.reshape(DEPTH, N_EXPERTS, 1)
    mem2 = mem_prompt.reshape(b * n_mem, d)

    x = (x_prompt.reshape(n_p, d), x_sample.reshape(n_s, d))
    xb = jnp.concatenate([x[0].astype(BF16), x[1].astype(BF16)], axis=0)
    tm_proj = _pick(nt, (1024, 512, 128))
    te = _pick(nt * TOP_K, EXPERT_TILES)
    tt = _pick(nt, (256, 128))
    outs = {k: [] for k in ("ret_p", "hg_p", "mk", "mv")}
    ret_s = hg_s = None
    for l in range(DEPTH):
        proj = _matmul(xb, w_in, l, tm_proj, 1280)
        kv_p = _matmul(mem2, w_mem_kv, l, _pick(b * n_mem, (1024, 512, 256)), 2 * HW)
        yr, ret_p, ret_s = _retention(proj, state_ret, l, cos_p, sin_p, cos_s, sin_s, gl,
                                      vec3(ret_norm_g), b, t, nb, ts, ret_s)
        yh, hg_p, hg_s = _hgrn(proj, state_hgrn, l, lbt, vec3(hgrn_norm_g), b, t, nb, ts, hg_s)
        yx = _cross_attention(proj, kv_p, cache_mem_k, cache_mem_v, l, b, t, nb, ts)
        x1, x1b, x1t = _merge((*yr, *yh, *yx), proj, x, w_up_ret, w_up_hgrn, w_up_xattn, w_out,
                              vec3(ln1_g), vec3(ln1_b), l)
        eidx, wn = _router(x1, w_router_t, b_router3, l)
        pos, lpos, runs, cnt, off = _positions(eidx, tt)
        tbl, n_steps = _step_table(cnt, off, nt * TOP_K, te)
        xs = _dispatch(pos, x1t.reshape(nt, PACKED_TILES, LANES))
        ye = _experts(tbl, n_steps, te, xs.reshape(-1, LANES), w_e_gate, w_e_up, w_e_down, l)
        ye = ye.reshape(-1, ROW_TILES, LANES)
        x_p, x_s, xb = _combine(lpos, wn, runs, ye, x1, x1b, w_s_gate, w_s_up, w_s_down,
                                vec3(ln2_g), vec3(ln2_b), l, n_p, tt)
        x = (x_p, x_s)
        outs["ret_p"].append(ret_p)
        outs["hg_p"].append(hg_p)
        outs["mk"].append(kv_p[:, :HW].reshape(b, n_mem, HEADS, DH))
        outs["mv"].append(kv_p[:, HW:].reshape(b, n_mem, HEADS, DH))
    return (x[0].reshape(b, t, d), x[1].reshape(nb, ts, d),
            jnp.stack(outs["ret_p"]), jnp.stack(outs["hg_p"]), jnp.stack(outs["mk"]), jnp.stack(outs["mv"]),
            ret_s, hg_s)
```

```python
import functools

import jax
import jax.numpy as jnp
from jax import lax
from jax.experimental import pallas as pl
from jax.experimental.pallas import tpu as pltpu

F32 = jnp.float32
BF16 = jnp.bfloat16
I32 = jnp.int32
HIGHEST = lax.Precision.HIGHEST

D_MODEL = 1024
DEPTH = 2
PAST_LEN = 16384
HEADS = 4
DH = 128
HW = HEADS * DH
RET_CHUNK = 128
HG_CHUNK = 16
ROPE_BASE = 10000.0
N_EXPERTS = 64
N_GROUPS = 8
GROUP_SIZE = N_EXPERTS // N_GROUPS
TOPK_GROUPS = 4
TOP_K = 6
D_EXPERT = 256
ROUTED_SCALE = 2.5
LN_EPS = 1e-5
DN_ALPHA = (2 * DEPTH) ** 0.25
N_IN = 9 * HW + 3 * D_MODEL
COL_RET_Q, COL_RET_K, COL_RET_V, COL_RET_G = 0, 1, 2, 3
COL_HG_Q, COL_HG_F, COL_HG_I, COL_HG_G = 4, 5, 6, 7
COL_XA_Q = 8
COL_GATES = 9
LANES = 128
SUBLANES = 8
ROW_TILES = D_MODEL // LANES
PACKED_TILES = ROW_TILES // 2
U32 = jnp.uint32
SAMPLE_BB = 8
EXPERT_TILES = (512, 256)
ISSUE_UNROLL = 8
VMEM_LIMIT = 56 * 1024 * 1024


def _params(sem):
    return pltpu.CompilerParams(dimension_semantics=sem, vmem_limit_bytes=VMEM_LIMIT)


def _bdot(a, b):
    return jnp.dot(a.astype(BF16), b.astype(BF16), preferred_element_type=F32)


def _bdot_nt(a, b):
    return lax.dot_general(a.astype(BF16), b.astype(BF16), (((1,), (1,)), ((), ())),
                           preferred_element_type=F32)


def _bdot_tn(a, b):
    return lax.dot_general(a.astype(BF16), b.astype(BF16), (((0,), (0,)), ((), ())),
                           preferred_element_type=F32)


def _silu(x):
    return x * jax.nn.sigmoid(x)


def _pick(n, prefs):
    for p in prefs:
        if n % p == 0:
            return p
    raise ValueError(f"no tile for {n}")


def _mm_body(x_ref, w_ref, o_ref, wb_ref):
    @pl.when(pl.program_id(1) == 0)
    def _():
        wb_ref[...] = w_ref[...].astype(BF16)

    o_ref[...] = jnp.dot(x_ref[...].astype(BF16), wb_ref[...],
                         preferred_element_type=F32).astype(o_ref.dtype)


def _matmul(x, w, layer, tm, tn):
    m, k = x.shape
    n = w.shape[2]
    return pl.pallas_call(
        _mm_body,
        grid=(n // tn, m // tm),
        in_specs=[pl.BlockSpec((tm, k), lambda j, i: (i, 0)),
                  pl.BlockSpec((None, k, tn), lambda j, i: (layer, 0, j))],
        out_specs=pl.BlockSpec((tm, tn), lambda j, i: (i, j)),
        out_shape=jax.ShapeDtypeStruct((m, n), F32),
        scratch_shapes=[pltpu.VMEM((k, tn), BF16)],
        compiler_params=_params(("arbitrary", "arbitrary")),
        name="dense_matmul",
    )(x, w)


def _rotary(x, cos, sin_signed):
    return x * cos + pltpu.roll(x, DH // 2, 1) * sin_signed


def _group_norm_gate(o, gain, gate):
    mu = jnp.mean(o, axis=-1, keepdims=True)
    var = jnp.mean(jnp.square(o - mu), axis=-1, keepdims=True)
    return (o - mu) * lax.rsqrt(var + LN_EPS) * gain * _silu(gate)


def _ret_prompt_body(q_ref, k_ref, v_ref, g_ref, cos_ref, sin_ref, gl_ref, gain_ref,
                     y_ref, st_ref, s_scr):
    c = pl.program_id(1)

    @pl.when(c == 0)
    def _():
        s_scr[...] = jnp.zeros_like(s_scr)

    ch = RET_CHUNK
    cos = cos_ref[...]
    sin = sin_ref[...]
    ri = lax.broadcasted_iota(I32, (ch, ch), 0)
    ci = lax.broadcasted_iota(I32, (ch, ch), 1)
    rel = (ri - ci).astype(F32)
    idx = lax.broadcasted_iota(I32, (ch, DH), 0).astype(F32)
    for h in range(HEADS):
        sl = slice(h * DH, (h + 1) * DH)
        gl = gl_ref[h:h + 1, :]
        qr = _rotary(q_ref[:, sl], cos, sin)
        kr = _rotary(k_ref[:, sl], cos, sin) * (DH ** -0.5)
        v = v_ref[:, sl]
        intra = jnp.where(rel >= 0, jnp.exp(gl * rel), 0.0)
        att = _bdot_nt(qr, kr) * intra
        s = s_scr[h]
        o = _bdot(att, v) + _bdot(qr, s) * jnp.exp(gl * (idx + 1.0))
        s_scr[h] = s * jnp.exp(gl * float(ch)) + _bdot_tn(kr * jnp.exp(gl * (ch - 1.0 - idx)), v)
        y_ref[:, sl] = _group_norm_gate(o, gain_ref[:, sl], g_ref[:, sl]).astype(BF16)

    @pl.when(c == pl.num_programs(1) - 1)
    def _():
        st_ref[0] = s_scr[...]


def _ret_sample_body(q_ref, k_ref, v_ref, g_ref, cos_ref, sin_ref, gl_ref, gain_ref, sin_ref_state,
                     y_ref, st_ref, *, ts):
    rows = SAMPLE_BB * ts
    shift = ts.bit_length() - 1
    cos = cos_ref[...]
    sin = sin_ref[...]
    ri = lax.broadcasted_iota(I32, (rows, rows), 0)
    ci = lax.broadcasted_iota(I32, (rows, rows), 1)
    rel = (ri - ci).astype(F32)
    mask = ((ri >> shift) == (ci >> shift)) & (ri >= ci)
    idx = (lax.broadcasted_iota(I32, (rows, DH), 0) & (ts - 1)).astype(F32)
    for h in range(HEADS):
        sl = slice(h * DH, (h + 1) * DH)
        gl = gl_ref[h:h + 1, :]
        qr = _rotary(q_ref[:, sl], cos, sin)
        kr = _rotary(k_ref[:, sl], cos, sin) * (DH ** -0.5)
        v = v_ref[:, sl]
        intra = jnp.where(mask, jnp.exp(gl[:, :rows] * rel), 0.0)
        o_intra = _bdot(_bdot_nt(qr, kr) * intra, v)
        q_dec = jnp.exp(gl * (idx + 1.0))
        kd = kr * jnp.exp(gl * (ts - 1.0 - idx))
        c_dec = jnp.exp(gl * float(ts))
        outs = []
        for j in range(SAMPLE_BB):
            rs = slice(j * ts, (j + 1) * ts)
            s = sin_ref_state[j, h]
            outs.append(o_intra[rs] + _bdot(qr[rs], s) * q_dec[rs])
            new_state = s * c_dec + _bdot_tn(kd[rs], v[rs])
            for slot in range(st_ref.shape[0]):
                st_ref[slot, j, h] = new_state
        o = jnp.concatenate(outs, axis=0)
        y_ref[:, sl] = _group_norm_gate(o, gain_ref[:, sl], g_ref[:, sl]).astype(BF16)


def _proj_spec(rows, col, row_map):
    return pl.BlockSpec((rows, HW), lambda *a: (row_map(*a), col))


def _sample_state_call(body, grid, in_specs, args, y_shape, y_spec, layer, nb, prev, name):
    st_shape = jax.ShapeDtypeStruct((DEPTH, nb, HEADS, DH, DH), F32)
    slots = DEPTH if prev is None else 1
    st_spec = pl.BlockSpec((slots, SAMPLE_BB, HEADS, DH, DH), lambda i: (layer, i, 0, 0, 0))
    aliases = {}
    if prev is not None:
        n_in = len(args)
        inner = body
        body = lambda *refs: inner(*refs[:n_in], *refs[n_in + 1:])
        in_specs = in_specs + [pl.BlockSpec(memory_space=pl.ANY)]
        args = args + (prev,)
        aliases = {n_in: 1}
    return pl.pallas_call(
        body, grid=grid, in_specs=in_specs, out_specs=[y_spec, st_spec], out_shape=[y_shape, st_shape],
        input_output_aliases=aliases, compiler_params=_params(("arbitrary",)), name=name,
    )(*args)


def _retention(proj, state, layer, cos_p, sin_p, cos_s, sin_s, gl, gain, b, t, nb, ts, prev_s):
    n_p = b * t
    nc = t // RET_CHUNK
    prow = lambda bi, c: bi * nc + c
    const2 = lambda *a: (0, 0)
    y_p, st_p = pl.pallas_call(
        _ret_prompt_body,
        grid=(b, nc),
        in_specs=[_proj_spec(RET_CHUNK, COL_RET_Q, prow), _proj_spec(RET_CHUNK, COL_RET_K, prow),
                  _proj_spec(RET_CHUNK, COL_RET_V, prow), _proj_spec(RET_CHUNK, COL_RET_G, prow),
                  pl.BlockSpec((RET_CHUNK, DH), lambda bi, c: (c, 0)),
                  pl.BlockSpec((RET_CHUNK, DH), lambda bi, c: (c, 0)),
                  pl.BlockSpec((HEADS, DH), const2),
                  pl.BlockSpec((None, 1, HW), lambda bi, c: (layer, 0, 0))],
        out_specs=[pl.BlockSpec((RET_CHUNK, HW), lambda bi, c: (prow(bi, c), 0)),
                   pl.BlockSpec((1, HEADS, DH, DH), lambda bi, c: (bi, 0, 0, 0))],
        out_shape=[jax.ShapeDtypeStruct((n_p, HW), BF16),
                   jax.ShapeDtypeStruct((b, HEADS, DH, DH), F32)],
        scratch_shapes=[pltpu.VMEM((HEADS, DH, DH), F32)],
        compiler_params=_params(("arbitrary", "arbitrary")),
        name="retention_prompt",
    )(proj, proj, proj, proj, cos_p, sin_p, gl, gain)

    rows = SAMPLE_BB * ts
    base = n_p // rows
    srow = lambda i: base + i
    y_s, st_s = _sample_state_call(
        functools.partial(_ret_sample_body, ts=ts), (nb // SAMPLE_BB,),
        [_proj_spec(rows, COL_RET_Q, srow), _proj_spec(rows, COL_RET_K, srow),
         _proj_spec(rows, COL_RET_V, srow), _proj_spec(rows, COL_RET_G, srow),
         pl.BlockSpec((rows, DH), const2), pl.BlockSpec((rows, DH), const2),
         pl.BlockSpec((HEADS, DH), const2),
         pl.BlockSpec((None, 1, HW), lambda i: (layer, 0, 0)),
         pl.BlockSpec((None, SAMPLE_BB, HEADS, DH, DH), lambda i: (layer, i, 0, 0, 0))],
        (proj, proj, proj, proj, cos_s, sin_s, gl, gain, state),
        jax.ShapeDtypeStruct((nb * ts, HW), BF16), pl.BlockSpec((rows, HW), lambda i: (i, 0)),
        layer, nb, prev_s, "retention_sample")
    return (y_p, y_s), st_p, st_s


def _hg_prepare(hq_ref, hf_ref, lbt_ref, rows, chunk):
    shift = chunk.bit_length() - 1
    ri = lax.broadcasted_iota(I32, (rows, rows), 0)
    ci = lax.broadcasted_iota(I32, (rows, rows), 1)
    same = (ri >> shift) == (ci >> shift)
    causal = same & (ci <= ri)
    z = hf_ref[...]
    log_lb = lbt_ref[0:1, :]
    log_1m_lb = lbt_ref[1:2, :]
    one_m_lb = lbt_ref[2:3, :]
    log_sig = jnp.minimum(z, 0.0) - jnp.log1p(jnp.exp(-jnp.abs(z)))
    bterm = log_1m_lb + log_sig
    logf = jnp.maximum(log_lb, bterm) + jnp.log1p(jnp.exp(-jnp.abs(log_lb - bterm)))
    kh = one_m_lb * jax.nn.sigmoid(-z)
    qh = _silu(hq_ref[...]) * (DH ** -0.5)
    cum = jnp.dot(causal.astype(F32), logf, precision=HIGHEST, preferred_element_type=F32)
    tot = jnp.dot(same.astype(F32), logf, precision=HIGHEST, preferred_element_type=F32)
    qi = qh * jnp.exp(cum)
    ki = kh * jnp.exp(-cum)
    ke = kh * jnp.exp(tot - cum)
    return causal, qi, ki, ke, tot, logf


def _rms_norm_gate(o, gain, gate):
    return o * lax.rsqrt(jnp.mean(jnp.square(o), axis=-1, keepdims=True) + LN_EPS) * gain * _silu(gate)


def _hg_prompt_body(hq_ref, hf_ref, hi_ref, hg_ref, lbt_ref, gain_ref, y_ref, st_ref, s_scr):
    c = pl.program_id(1)

    @pl.when(c == 0)
    def _():
        s_scr[...] = jnp.zeros_like(s_scr)

    rows = RET_CHUNK
    n_sub = rows // HG_CHUNK
    shift = HG_CHUNK.bit_length() - 1
    causal, qi, ki, ke, tot, logf = _hg_prepare(hq_ref, hf_ref, lbt_ref, rows, HG_CHUNK)
    ri = lax.broadcasted_iota(I32, (rows, rows), 0)
    ci = lax.broadcasted_iota(I32, (rows, rows), 1)
    pre = jnp.dot(((ci >> shift) < (ri >> shift)).astype(F32), logf, precision=HIGHEST,
                  preferred_element_type=F32)
    sub = lax.broadcasted_iota(I32, (rows, DH), 0) >> shift
    v = hi_ref[...]
    for h in range(HEADS):
        sl = slice(h * DH, (h + 1) * DH)
        q_h, ke_h, v_h, pre_h = qi[:, sl], ke[:, sl], v[:, sl], pre[:, sl]
        att = jnp.where(causal, _bdot_nt(q_h, ki[:, sl]), 0.0)
        st0 = s_scr[h]
        o = _bdot(att, v_h) + _bdot_nt(q_h * jnp.exp(pre_h), st0)
        end_last = pre_h[rows - 1:rows] + tot[rows - 1:rows, sl]
        st = st0 * jnp.exp(end_last)
        for i in range(n_sub):
            rs = slice(i * HG_CHUNK, (i + 1) * HG_CHUNK)
            u_t = _bdot_tn(v_h[rs], ke_h[rs])
            if i + 1 < n_sub:
                end_i = pre_h[(i + 1) * HG_CHUNK:(i + 1) * HG_CHUNK + 1]
                later = q_h * jnp.exp(jnp.where(sub > i, pre_h - end_i, -jnp.inf))
                o = o + _bdot_nt(later, u_t)
                st = st + u_t * jnp.exp(end_last - end_i)
            else:
                st = st + u_t
        s_scr[h] = st
        y_ref[:, sl] = _rms_norm_gate(o, gain_ref[:, sl], hg_ref[:, sl]).astype(BF16)

    @pl.when(c == pl.num_programs(1) - 1)
    def _():
        for h in range(HEADS):
            st_ref[0, h] = s_scr[h].T


def _hg_sample_body(hq_ref, hf_ref, hi_ref, hg_ref, lbt_ref, gain_ref, sin_ref_state,
                    y_ref, st_ref, *, ts):
    rows = SAMPLE_BB * ts
    causal, qi, ki, ke, tot, _ = _hg_prepare(hq_ref, hf_ref, lbt_ref, rows, ts)
    etot = jnp.exp(tot)
    v = hi_ref[...]
    for h in range(HEADS):
        sl = slice(h * DH, (h + 1) * DH)
        att = jnp.where(causal, _bdot_nt(qi[:, sl], ki[:, sl]), 0.0)
        o_intra = _bdot(att, v[:, sl])
        outs = []
        for j in range(SAMPLE_BB):
            rs = slice(j * ts, (j + 1) * ts)
            s = sin_ref_state[j, h]
            outs.append(o_intra[rs] + _bdot(qi[rs, sl], s))
            scale = jnp.broadcast_to(etot[j * ts:j * ts + 1, sl], (DH, DH)).T
            new_state = s * scale + _bdot_tn(ke[rs, sl], v[rs, sl])
            for slot in range(st_ref.shape[0]):
                st_ref[slot, j, h] = new_state
        o = jnp.concatenate(outs, axis=0)
        y_ref[:, sl] = _rms_norm_gate(o, gain_ref[:, sl], hg_ref[:, sl]).astype(BF16)


def _hgrn(proj, state, layer, lbt, gain, b, t, nb, ts, prev_s):
    n_p = b * t
    nc = t // RET_CHUNK
    prow = lambda bi, c: bi * nc + c
    y_p, st_p = pl.pallas_call(
        _hg_prompt_body,
        grid=(b, nc),
        in_specs=[_proj_spec(RET_CHUNK, COL_HG_Q, prow), _proj_spec(RET_CHUNK, COL_HG_F, prow),
                  _proj_spec(RET_CHUNK, COL_HG_I, prow), _proj_spec(RET_CHUNK, COL_HG_G, prow),
                  pl.BlockSpec((None, SUBLANES, HW), lambda bi, c: (layer, 0, 0)),
                  pl.BlockSpec((None, 1, HW), lambda bi, c: (layer, 0, 0))],
        out_specs=[pl.BlockSpec((RET_CHUNK, HW), lambda bi, c: (prow(bi, c), 0)),
                   pl.BlockSpec((1, HEADS, DH, DH), lambda bi, c: (bi, 0, 0, 0))],
        out_shape=[jax.ShapeDtypeStruct((n_p, HW), BF16),
                   jax.ShapeDtypeStruct((b, HEADS, DH, DH), F32)],
        scratch_shapes=[pltpu.VMEM((HEADS, DH, DH), F32)],
        compiler_params=_params(("arbitrary", "arbitrary")),
        name="hgrn_prompt",
    )(proj, proj, proj, proj, lbt, gain)

    rows = SAMPLE_BB * ts
    base = n_p // rows
    srow = lambda i: base + i
    y_s, st_s = _sample_state_call(
        functools.partial(_hg_sample_body, ts=ts), (nb // SAMPLE_BB,),
        [_proj_spec(rows, COL_HG_Q, srow), _proj_spec(rows, COL_HG_F, srow),
         _proj_spec(rows, COL_HG_I, srow), _proj_spec(rows, COL_HG_G, srow),
         pl.BlockSpec((None, SUBLANES, HW), lambda i: (layer, 0, 0)),
         pl.BlockSpec((None, 1, HW), lambda i: (layer, 0, 0)),
         pl.BlockSpec((None, SAMPLE_BB, HEADS, DH, DH), lambda i: (layer, i, 0, 0, 0))],
        (proj, proj, proj, proj, lbt, gain, state),
        jax.ShapeDtypeStruct((nb * ts, HW), BF16), pl.BlockSpec((rows, HW), lambda i: (i, 0)),
        layer, nb, prev_s, "hgrn_sample")
    return (y_p, y_s), st_p, st_s


def _softmax_rows(s):
    e = jnp.exp(s - jnp.max(s, axis=-1, keepdims=True))
    return e / jnp.sum(e, axis=-1, keepdims=True)


def _xa_prompt_body(q_ref, k_ref, v_ref, y_ref):
    for h in range(HEADS):
        sl = slice(h * DH, (h + 1) * DH)
        a = _softmax_rows(_bdot_nt(q_ref[:, sl] * (DH ** -0.5), k_ref[:, sl]))
        y_ref[:, sl] = _bdot(a, v_ref[:, sl]).astype(BF16)


def _xa_sample_body(q_ref, k_ref, v_ref, y_ref, *, ts):
    n_mem = k_ref.shape[1] // HEADS
    pairs = [(j, h) for j in range(SAMPLE_BB) for h in range(HEADS)]
    q = q_ref[...] * (DH ** -0.5)
    scores = [_bdot_nt(q[j * ts:(j + 1) * ts, h * DH:(h + 1) * DH], k_ref[j, pl.ds(h, n_mem, stride=HEADS), :])
              for j, h in pairs]
    a = _softmax_rows(jnp.concatenate(scores, axis=0))
    for n, (j, h) in enumerate(pairs):
        y = _bdot(a[n * ts:(n + 1) * ts], v_ref[j, pl.ds(h, n_mem, stride=HEADS), :])
        y_ref[j * ts:(j + 1) * ts, h * DH:(h + 1) * DH] = y.astype(BF16)


def _cross_attention(proj, kv_p, cache_k, cache_v, layer, b, t, nb, ts):
    n_p = b * t
    n_mem = kv_p.shape[0] // b
    tq = _pick(t, (512, 256, 128))
    nq = t // tq
    y_p = pl.pallas_call(
        _xa_prompt_body,
        grid=(b, nq),
        in_specs=[_proj_spec(tq, COL_XA_Q, lambda bi, qi: bi * nq + qi),
                  pl.BlockSpec((n_mem, HW), lambda bi, qi: (bi, 0)),
                  pl.BlockSpec((n_mem, HW), lambda bi, qi: (bi, 1))],
        out_specs=pl.BlockSpec((tq, HW), lambda bi, qi: (bi * nq + qi, 0)),
        out_shape=jax.ShapeDtypeStruct((n_p, HW), BF16),
        compiler_params=_params(("arbitrary", "arbitrary")),
        name="xattn_prompt",
    )(proj, kv_p, kv_p)

    rows = SAMPLE_BB * ts
    base = n_p // rows
    cache_k = cache_k.reshape(DEPTH, nb, n_mem * HEADS, DH)
    cache_v = cache_v.reshape(DEPTH, nb, n_mem * HEADS, DH)
    kv_spec = pl.BlockSpec((None, SAMPLE_BB, n_mem * HEADS, DH), lambda i: (layer, i, 0, 0))
    y_s = pl.pallas_call(
        functools.partial(_xa_sample_body, ts=ts),
        grid=(nb // SAMPLE_BB,),
        in_specs=[_proj_spec(rows, COL_XA_Q, lambda i: base + i), kv_spec, kv_spec],
        out_specs=pl.BlockSpec((rows, HW), lambda i: (i, 0)),
        out_shape=jax.ShapeDtypeStruct((nb * ts, HW), BF16),
        compiler_params=_params(("arbitrary",)),
        name="xattn_sample",
    )(proj, cache_k, cache_v)
    return (y_p, y_s)


def _layer_norm(tv, g, b):
    mu = jnp.mean(tv, axis=-1, keepdims=True)
    var = jnp.mean(jnp.square(tv - mu), axis=-1, keepdims=True)
    return (tv - mu) * lax.rsqrt(var + LN_EPS) * g + b


def _merge_body(yrp_ref, yrs_ref, yhp_ref, yhs_ref, yxp_ref, yxs_ref, g0a, g0b, g1a, g1b, g2a, g2b, xp_ref, xs_ref,
                wr_ref, wh_ref, wx_ref, wo_ref, lg_ref, lb_ref,
                x1_ref, x1b_ref, x1t_ref, wr_s, wh_s, wx_s, wo_s, *, prompt_tiles):
    @pl.when(pl.program_id(0) == 0)
    def _():
        wr_s[...] = wr_ref[...].astype(BF16)
        wh_s[...] = wh_ref[...].astype(BF16)
        wx_s[...] = wx_ref[...].astype(BF16)
        wo_s[...] = wo_ref[...].astype(BF16)

    is_prompt = pl.program_id(0) < prompt_tiles

    def branch(yp_ref, ys_ref, w_s, ga, gb):
        y = jnp.where(is_prompt, yp_ref[...], ys_ref[...])
        gate = jax.nn.sigmoid(jnp.concatenate([ga[...], gb[...]], axis=-1))
        return gate * jnp.dot(y, w_s[...], preferred_element_type=F32)

    m = (branch(yrp_ref, yrs_ref, wr_s, g0a, g0b) + branch(yhp_ref, yhs_ref, wh_s, g1a, g1b)
         + branch(yxp_ref, yxs_ref, wx_s, g2a, g2b))
    hmix = jnp.dot(m.astype(BF16), wo_s[...], preferred_element_type=F32)
    x = jnp.where(is_prompt, xp_ref[...], xs_ref[...])
    x1 = _layer_norm(DN_ALPHA * x + hmix, lg_ref[...], lb_ref[...])
    x1_ref[...] = x1
    x1b_ref[...] = x1.astype(BF16)
    tm = x1.shape[0]
    bits = lax.bitcast_convert_type(x1.astype(BF16).astype(F32), U32)
    half = D_MODEL // 2
    packed = (bits[:, :half] >> 16) | (bits[:, half:] & jnp.uint32(0xFFFF0000))
    for s in range(PACKED_TILES):
        x1t_ref[pl.ds(s, tm, stride=PACKED_TILES), :] = packed[:, s * LANES:(s + 1) * LANES]


def _merge(ys, proj, x, w_up_ret, w_up_hgrn, w_up_xattn, w_out, ln_g, ln_b, layer):
    n_p = ys[0].shape[0]
    nt = n_p + ys[1].shape[0]
    tm = _pick(nt, (256, 128))
    assert n_p % tm == 0 and ys[1].shape[0] % tm == 0
    p_tiles = n_p // tm
    row = lambda i: (i, 0)
    p_map = lambda i: (jnp.minimum(i, p_tiles - 1), 0)
    s_map = lambda i: (jnp.maximum(i - p_tiles, 0), 0)
    y_specs = [pl.BlockSpec((tm, HW), p_map), pl.BlockSpec((tm, HW), s_map)] * 3
    wspec = lambda k: pl.BlockSpec((None, k, D_MODEL), lambda i: (layer, 0, 0))
    vec = pl.BlockSpec((None, 1, D_MODEL), lambda i: (layer, 0, 0))
    gate_specs = [pl.BlockSpec((tm, HW), lambda i, c=c: (i, COL_GATES + c)) for c in range(6)]
    return pl.pallas_call(
        functools.partial(_merge_body, prompt_tiles=p_tiles),
        grid=(nt // tm,),
        in_specs=y_specs + gate_specs + [pl.BlockSpec((tm, D_MODEL), p_map), pl.BlockSpec((tm, D_MODEL), s_map),
                  wspec(HW), wspec(HW), wspec(HW), wspec(D_MODEL), vec, vec],
        out_specs=[pl.BlockSpec((tm, D_MODEL), row), pl.BlockSpec((tm, D_MODEL), row),
                   pl.BlockSpec((tm * PACKED_TILES, LANES), row)],
        out_shape=[jax.ShapeDtypeStruct((nt, D_MODEL), F32),
                   jax.ShapeDtypeStruct((nt, D_MODEL), BF16),
                   jax.ShapeDtypeStruct((nt * PACKED_TILES, LANES), U32)],
        scratch_shapes=[pltpu.VMEM((HW, D_MODEL), BF16)] * 3 + [pltpu.VMEM((D_MODEL, D_MODEL), BF16)],
        compiler_params=_params(("arbitrary",)),
        name="merge_out_ln1",
    )(*ys, *([proj] * 6), *x, w_up_ret, w_up_hgrn, w_up_xattn, w_out, ln_g, ln_b)


def _router_body(x_ref, wt_ref, b_ref, eidx_ref, wn_ref):
    tm = x_ref.shape[0]
    x = x_ref[...]
    w = wt_ref[...]
    xh = x.astype(BF16)
    xl = (x - xh.astype(F32)).astype(BF16)
    wh = w.astype(BF16)
    wl = (w - wh.astype(F32)).astype(BF16)
    logits = _bdot_nt(wh, xh) + (_bdot_nt(wh, xl) + _bdot_nt(wl, xh))
    s = jax.nn.sigmoid(logits)
    sel = s + b_ref[...]
    neg = -jnp.inf
    groups = [sel[g * GROUP_SIZE:(g + 1) * GROUP_SIZE, :] for g in range(N_GROUPS)]
    ie = lax.broadcasted_iota(I32, (GROUP_SIZE, tm), 0).astype(F32)
    rows = []
    for blk in groups:
        m1 = jnp.max(blk, axis=0, keepdims=True)
        first = jnp.min(jnp.where(blk == m1, ie, float(GROUP_SIZE)), axis=0, keepdims=True)
        rows.append(m1 + jnp.max(jnp.where(ie == first, neg, blk), axis=0, keepdims=True))
    gscore = jnp.concatenate(rows, axis=0)
    ig = lax.broadcasted_iota(I32, gscore.shape, 0).astype(F32)
    gmask = jnp.zeros(gscore.shape, F32)
    for _ in range(TOPK_GROUPS):
        m = jnp.max(gscore, axis=0, keepdims=True)
        gi = jnp.min(jnp.where(gscore == m, ig, float(N_GROUPS)), axis=0, keepdims=True)
        hit = ig == gi
        gmask = jnp.where(hit, 1.0, gmask)
        gscore = jnp.where(hit, neg, gscore)
    masked = jnp.concatenate([jnp.where(gmask[g:g + 1, :] > 0.5, blk, neg)
                              for g, blk in enumerate(groups)], axis=0)
    ix = lax.broadcasted_iota(I32, masked.shape, 0).astype(F32)
    idxs, ws = [], []
    for _ in range(TOP_K):
        m = jnp.max(masked, axis=0, keepdims=True)
        ei = jnp.min(jnp.where(masked == m, ix, float(N_EXPERTS)), axis=0, keepdims=True)
        hit = ix == ei
        idxs.append(ei)
        ws.append(jnp.sum(jnp.where(hit, s, 0.0), axis=0, keepdims=True))
        masked = jnp.where(hit, neg, masked)
    wsum = ws[0]
    for w in ws[1:]:
        wsum = wsum + w
    pad = [jnp.zeros((1, tm), F32)] * (SUBLANES - TOP_K)
    eidx_ref[...] = jnp.concatenate(idxs + pad, axis=0).astype(I32)
    wn_ref[...] = jnp.concatenate([w / wsum * ROUTED_SCALE for w in ws] + pad, axis=0)


def _router(x1, w_router_t, b_router, layer):
    nt = x1.shape[0]
    tm = _pick(nt, (512, 256, 128))
    return pl.pallas_call(
        _router_body,
        grid=(nt // tm,),
        in_specs=[pl.BlockSpec((tm, D_MODEL), lambda i: (i, 0)),
                  pl.BlockSpec((None, N_EXPERTS, D_MODEL), lambda i: (layer, 0, 0)),
                  pl.BlockSpec((None, N_EXPERTS, 1), lambda i: (layer, 0, 0))],
        out_specs=[pl.BlockSpec((SUBLANES, tm), lambda i: (0, i))] * 2,
        out_shape=[jax.ShapeDtypeStruct((SUBLANES, nt), I32),
                   jax.ShapeDtypeStruct((SUBLANES, nt), F32)],
        compiler_params=_params(("arbitrary",)),
        name="moe_router",
    )(x1, w_router_t, b_router)


def _positions_body(eidx_ref, pos_ref, cnt_ref, off_ref, base_scr, off_scr):
    phase = pl.program_id(0)
    i = pl.program_id(1)
    tp = eidx_ref.shape[1]
    ix = lax.broadcasted_iota(I32, (N_EXPERTS, tp), 0)
    eidx = eidx_ref[...]
    member = jnp.zeros((N_EXPERTS, tp), F32)
    for k in range(TOP_K):
        member = member + (ix == eidx[k:k + 1, :]).astype(F32)
    tile_cnt = jnp.sum(member, axis=1, keepdims=True)

    @pl.when((phase == 0) & (i == 0))
    def _():
        base_scr[...] = jnp.zeros_like(base_scr)

    @pl.when((phase == 1) & (i == 0))
    def _():
        cnt = base_scr[...]
        er = lax.broadcasted_iota(I32, (N_EXPERTS, N_EXPERTS), 0)
        ec = lax.broadcasted_iota(I32, (N_EXPERTS, N_EXPERTS), 1)
        off = jnp.dot((ec < er).astype(F32), cnt, precision=HIGHEST, preferred_element_type=F32)
        off_scr[...] = off
        cnt_ref[...] = cnt
        off_ref[...] = off
        base_scr[...] = jnp.zeros_like(base_scr)

    @pl.when(phase == 1)
    def _():
        tr = lax.broadcasted_iota(I32, (tp, tp), 0)
        tc = lax.broadcasted_iota(I32, (tp, tp), 1)
        before = jnp.dot(member.astype(BF16), (tr < tc).astype(BF16), preferred_element_type=F32)
        where_to = before + (off_scr[...] + base_scr[...])[:, 0:1]
        rows = [jnp.sum(jnp.where(ix == eidx[k:k + 1, :], where_to, 0.0), axis=0, keepdims=True)
                for k in range(TOP_K)]
        rows += [jnp.zeros((1, tp), F32)] * (SUBLANES - TOP_K)
        pos_ref[...] = jnp.concatenate(rows, axis=0).astype(I32)

    base_scr[...] = base_scr[...] + tile_cnt


def _positions(eidx):
    nt = eidx.shape[1]
    tp = _pick(nt, (512, 256, 128))
    const = lambda p, i: (0, 0)
    return pl.pallas_call(
        _positions_body,
        grid=(2, nt // tp),
        in_specs=[pl.BlockSpec((SUBLANES, tp), lambda p, i: (0, i))],
        out_specs=[pl.BlockSpec((SUBLANES, tp), lambda p, i: (0, i * p)),
                   pl.BlockSpec((N_EXPERTS, LANES), const), pl.BlockSpec((N_EXPERTS, LANES), const)],
        out_shape=[jax.ShapeDtypeStruct((SUBLANES, nt), I32),
                   jax.ShapeDtypeStruct((N_EXPERTS, LANES), F32),
                   jax.ShapeDtypeStruct((N_EXPERTS, LANES), F32)],
        scratch_shapes=[pltpu.VMEM((N_EXPERTS, LANES), F32), pltpu.VMEM((N_EXPERTS, LANES), F32)],
        compiler_params=_params(("arbitrary", "arbitrary")),
        name="moe_positions",
    )(eidx)


T_TILE, T_EXPERT, T_LO, T_HI, T_FRESH, T_NEWEXP = range(6)

def _table_body(cnt_ref, off_ref, tbl_ref, *, tile_rows):
    te = float(tile_rows)
    n = tbl_ref.shape[1]
    cnt = cnt_ref[...]
    off = off_ref[...]
    first = jnp.floor(off * (1.0 / te))
    last = jnp.floor((off + cnt - 1.0) * (1.0 / te))
    nst = jnp.where(cnt > 0.0, last - first + 1.0, 0.0)
    er = lax.broadcasted_iota(I32, (N_EXPERTS, N_EXPERTS), 0)
    ec = lax.broadcasted_iota(I32, (N_EXPERTS, N_EXPERTS), 1)
    s_end = jnp.dot((ec <= er).astype(F32), nst, precision=HIGHEST, preferred_element_type=F32)
    s_beg = s_end - nst
    total = s_end[N_EXPERTS - 1:N_EXPERTS, 0:1]
    sidx = lax.broadcasted_iota(I32, (1, n), 1).astype(F32)
    s = jnp.minimum(sidx, total - 1.0)
    e_s = jnp.sum((s_end[:, 0:1] <= s).astype(F32), axis=0, keepdims=True)
    hot = lax.broadcasted_iota(I32, (N_EXPERTS, n), 0).astype(F32) == e_s

    def pick(col):
        return jnp.sum(jnp.where(hot, col[:, 0:1], 0.0), axis=0, keepdims=True)

    tile = pick(first) + s - pick(s_beg)
    valid = sidx < total
    o, c = pick(off), pick(cnt)
    lo = jnp.where(valid, jnp.maximum(o, tile * te), 0.0)
    hi = jnp.where(valid, jnp.minimum(o + c, (tile + 1.0) * te), 0.0)
    head = sidx == 0.0
    fresh = jnp.where((tile != pltpu.roll(tile, 1, 1)) | head, 1.0, 0.0)
    newexp = jnp.where((e_s != pltpu.roll(e_s, 1, 1)) | head, 1.0, 0.0)
    pad = [jnp.zeros((1, n), F32)] * (SUBLANES - 6)
    tbl_ref[...] = jnp.concatenate([tile, e_s, lo, hi, fresh, newexp] + pad, axis=0).astype(I32)


def _step_table(cnt, off, n_rows, te):
    n_steps = n_rows // te + N_EXPERTS
    width = -(-n_steps // LANES) * LANES
    tbl = pl.pallas_call(
        functools.partial(_table_body, tile_rows=te),
        out_shape=jax.ShapeDtypeStruct((SUBLANES, width), I32),
        name="moe_step_table",
    )(cnt, off)
    return tbl, n_steps


def _wait_tile_rows(like_src, dst_rows_ref, sem_ref):
    rows = like_src.shape[0]
    for _ in range(TOP_K):
        pltpu.make_async_copy(like_src, dst_rows_ref.at[pl.ds(0, rows)], sem_ref).wait()


def _dispatch_body(pos_ref, xt_ref, xs_ref, pos_s, sem_p, sem):
    td = pos_ref.shape[1]
    cp = pltpu.make_async_copy(pos_ref, pos_s, sem_p)
    cp.start()
    cp.wait()

    def issue(g, carry):
        for u in range(ISSUE_UNROLL):
            r = g * ISSUE_UNROLL + u
            for k in range(TOP_K):
                pltpu.make_async_copy(xt_ref.at[r], xs_ref.at[pos_s[k, r]], sem).start(priority=k % 2)
        return carry

    lax.fori_loop(0, td // ISSUE_UNROLL, issue, 0)
    _wait_tile_rows(xt_ref, xs_ref, sem)


def _dispatch(pos, x1t):
    nt = x1t.shape[0]
    td = _pick(nt, (512, 256, 128))
    return pl.pallas_call(
        _dispatch_body,
        grid=(nt // td,),
        in_specs=[pl.BlockSpec((SUBLANES, td), lambda i: (0, i)),
                  pl.BlockSpec((td,) + x1t.shape[1:], lambda i: (i, 0, 0))],
        out_specs=pl.BlockSpec(memory_space=pl.ANY),
        out_shape=jax.ShapeDtypeStruct((nt * TOP_K,) + x1t.shape[1:], x1t.dtype),
        scratch_shapes=[pltpu.SMEM((SUBLANES, td), I32), pltpu.SemaphoreType.DMA, pltpu.SemaphoreType.DMA],
        compiler_params=_params(("arbitrary",)),
        name="moe_dispatch",
    )(pos, x1t)


def _experts_body(tbl_ref, xs_ref, wg_ref, wu_ref, wd_ref, ye_ref, wg_s, wu_s, wd_s):
    s = pl.program_id(0)
    te = xs_ref.shape[0] // PACKED_TILES
    lo = tbl_ref[T_LO, s]
    hi = tbl_ref[T_HI, s]

    @pl.when(tbl_ref[T_NEWEXP, s] == 1)
    def _():
        wg_s[...] = wg_ref[...].astype(BF16)
        wu_s[...] = wu_ref[...].astype(BF16)
        wd_s[...] = wd_ref[...].astype(BF16)

    @pl.when(tbl_ref[T_FRESH, s] == 1)
    def _():
        ye_ref[...] = jnp.zeros_like(ye_ref)

    @pl.when(hi > lo)
    def _():
        words = [xs_ref[pl.ds(t, te, stride=PACKED_TILES), :] for t in range(PACKED_TILES)]
        low = [lax.bitcast_convert_type(w << 16, F32).astype(BF16) for w in words]
        high = [lax.bitcast_convert_type(w & jnp.uint32(0xFFFF0000), F32).astype(BF16) for w in words]
        x = jnp.concatenate(low + high, axis=-1)
        g = jnp.dot(x, wg_s[...], preferred_element_type=F32)
        u = jnp.dot(x, wu_s[...], preferred_element_type=F32)
        y = jnp.dot((_silu(g) * u).astype(BF16), wd_s[...], preferred_element_type=F32)
        row = tbl_ref[T_TILE, s] * te + lax.broadcasted_iota(I32, (te, LANES), 0)
        mine = (row >= lo) & (row < hi)
        for t in range(ROW_TILES):
            sl = pl.ds(t, te, stride=ROW_TILES)
            ye_ref[sl, :] = jnp.where(mine, y[:, t * LANES:(t + 1) * LANES], ye_ref[sl, :])


def _experts(tbl, n_steps, te, xs, w_gate, w_up, w_down, layer):
    n_rows = xs.shape[0] // PACKED_TILES
    tile_map = lambda s, tbl: (tbl[T_TILE, s], 0)
    w_map = lambda s, tbl: (layer, tbl[T_EXPERT, s], 0, 0)
    w_in_spec = pl.BlockSpec((None, None, D_MODEL, D_EXPERT), w_map)
    w_dn_spec = pl.BlockSpec((None, None, D_EXPERT, D_MODEL), w_map)
    return pl.pallas_call(
        _experts_body,
        grid_spec=pltpu.PrefetchScalarGridSpec(
            num_scalar_prefetch=1,
            grid=(n_steps,),
            in_specs=[pl.BlockSpec((te * PACKED_TILES, LANES), tile_map), w_in_spec, w_in_spec, w_dn_spec],
            out_specs=pl.BlockSpec((te * ROW_TILES, LANES), tile_map),
            scratch_shapes=[pltpu.VMEM((D_MODEL, D_EXPERT), BF16), pltpu.VMEM((D_MODEL, D_EXPERT), BF16),
                            pltpu.VMEM((D_EXPERT, D_MODEL), BF16)]),
        out_shape=jax.ShapeDtypeStruct((n_rows * ROW_TILES, LANES), F32),
        compiler_params=_params(("arbitrary",)),
        name="moe_experts",
    )(tbl, xs, w_gate, w_up, w_down)


def _combine_body(pos_ref, wn_ref, ye_ref, x1_ref, x1b_ref, wsg_ref, wsu_ref, wsd_ref, lg_ref, lb_ref,
                  x2p_ref, x2s_ref, x2b_ref, pos_s, buf, wsg_s, wsu_s, wsd_s, sem_p, sem, *, prompt_tiles):
    i = pl.program_id(0)
    n = pl.num_programs(0)
    tc = wn_ref.shape[1]
    slot = i % 2
    tile_rows = tc * ROW_TILES

    def request(tile, into):
        cp = pltpu.make_async_copy(pos_ref.at[tile], pos_s, sem_p)
        cp.start()
        cp.wait()

        def issue(g, carry):
            for u in range(ISSUE_UNROLL):
                r = g * ISSUE_UNROLL + u
                for k in range(TOP_K):
                    at = pl.multiple_of(((into * TOP_K + k) * tc + r) * ROW_TILES, ROW_TILES)
                    pltpu.make_async_copy(ye_ref.at[pos_s[k, r]], buf.at[pl.ds(at, ROW_TILES)],
                                          sem.at[into]).start(priority=k % 2)
            return carry

        lax.fori_loop(0, tc // ISSUE_UNROLL, issue, 0)

    @pl.when(i == 0)
    def _():
        wsg_s[...] = wsg_ref[...].astype(BF16)
        wsu_s[...] = wsu_ref[...].astype(BF16)
        wsd_s[...] = wsd_ref[...].astype(BF16)
        request(0, 0)

    @pl.when(i + 1 < n)
    def _():
        request(i + 1, 1 - slot)

    xb = x1b_ref[...]
    hs = _silu(jnp.dot(xb, wsg_s[...], preferred_element_type=F32)) * jnp.dot(xb, wsu_s[...], preferred_element_type=F32)
    shared = jnp.dot(hs.astype(BF16), wsd_s[...], preferred_element_type=F32)
    for k in range(TOP_K):
        pltpu.make_async_copy(buf.at[pl.ds(0, tile_rows)], buf.at[pl.ds(tile_rows, tile_rows)], sem.at[slot]).wait()

    w = wn_ref[...]
    acc = [None] * ROW_TILES
    for k in range(TOP_K):
        wcol = jnp.concatenate([jnp.broadcast_to(w[k:k + 1, c * LANES:(c + 1) * LANES], (LANES, LANES)).T
                                for c in range(tc // LANES)], axis=0)
        base = (slot * TOP_K + k) * tile_rows
        for t in range(ROW_TILES):
            term = wcol * buf[pl.ds(base + t, tc, stride=ROW_TILES), :]
            acc[t] = term if acc[t] is None else acc[t] + term
    routed = jnp.concatenate(acc, axis=-1)
    x2 = _layer_norm(DN_ALPHA * x1_ref[...] + (routed + shared), lg_ref[...], lb_ref[...])
    x2b_ref[...] = x2.astype(BF16)

    @pl.when(i < prompt_tiles)
    def _():
        x2p_ref[...] = x2

    @pl.when(i >= prompt_tiles)
    def _():
        x2s_ref[...] = x2


def _combine(pos, wn, ye, x1, x1b, w_s_gate, w_s_up, w_s_down, ln_g, ln_b, layer, n_p):
    nt = x1.shape[0]
    tc = _pick(nt, (256, 128))
    assert n_p % tc == 0
    n_tiles = nt // tc
    p_tiles = n_p // tc
    d_sh = w_s_gate.shape[2]
    pos3 = pos.reshape(SUBLANES, n_tiles, tc).transpose(1, 0, 2)
    row = lambda i: (i, 0)
    vec = pl.BlockSpec((None, 1, D_MODEL), lambda i: (layer, 0, 0))
    return pl.pallas_call(
        functools.partial(_combine_body, prompt_tiles=p_tiles),
        grid=(n_tiles,),
        in_specs=[pl.BlockSpec((n_tiles, SUBLANES, tc), lambda i: (0, 0, 0)),
                  pl.BlockSpec((SUBLANES, tc), lambda i: (0, i)),
                  pl.BlockSpec(memory_space=pl.ANY),
                  pl.BlockSpec((tc, D_MODEL), row), pl.BlockSpec((tc, D_MODEL), row),
                  pl.BlockSpec((None, D_MODEL, d_sh), lambda i: (layer, 0, 0)),
                  pl.BlockSpec((None, D_MODEL, d_sh), lambda i: (layer, 0, 0)),
                  pl.BlockSpec((None, d_sh, D_MODEL), lambda i: (layer, 0, 0)), vec, vec],
        out_specs=[pl.BlockSpec((tc, D_MODEL), lambda i: (jnp.minimum(i, p_tiles - 1), 0)),
                   pl.BlockSpec((tc, D_MODEL), lambda i: (jnp.maximum(i - p_tiles, 0), 0)),
                   pl.BlockSpec((tc, D_MODEL), row)],
        out_shape=[jax.ShapeDtypeStruct((n_p, D_MODEL), F32), jax.ShapeDtypeStruct((nt - n_p, D_MODEL), F32),
                   jax.ShapeDtypeStruct((nt, D_MODEL), BF16)],
        scratch_shapes=[pltpu.SMEM((SUBLANES, tc), I32),
                        pltpu.VMEM((2 * TOP_K * tc * ROW_TILES, LANES), F32),
                        pltpu.VMEM((D_MODEL, d_sh), BF16), pltpu.VMEM((D_MODEL, d_sh), BF16),
                        pltpu.VMEM((d_sh, D_MODEL), BF16),
                        pltpu.SemaphoreType.DMA, pltpu.SemaphoreType.DMA((2,))],
        compiler_params=_params(("arbitrary",)),
        name="moe_combine_ln2",
    )(pos3, wn, ye, x1, x1b, w_s_gate, w_s_up, w_s_down, ln_g, ln_b)


def _rope_tables(t, pos0):
    inv = 1.0 / (ROPE_BASE ** (jnp.arange(0, DH, 2, dtype=F32) / DH))
    ang = (jnp.arange(t, dtype=F32) + pos0)[:, None] * inv[None, :]
    cos, sin = jnp.cos(ang), jnp.sin(ang)
    return jnp.concatenate([cos, cos], axis=-1), jnp.concatenate([-sin, sin], axis=-1)


def kernel(x_prompt, x_sample, mem_prompt, state_ret, state_hgrn, cache_mem_k, cache_mem_v, w_in, w_up_ret, w_up_hgrn, w_up_xattn, w_out, w_mem_kv, ret_norm_g, hgrn_norm_g, lb_logits, ln1_g, ln1_b, ln2_g, ln2_b, w_router, b_router, w_e_gate, w_e_up, w_e_down, w_s_gate, w_s_up, w_s_down):
    b, t, d = x_prompt.shape
    nb, ts, _ = x_sample.shape
    n_mem = mem_prompt.shape[1]
    assert d == D_MODEL and t % RET_CHUNK == 0 and nb % SAMPLE_BB == 0
    assert ts & (ts - 1) == 0 and HG_CHUNK % ts == 0 and RET_CHUNK % ts == 0
    n_p, n_s = b * t, nb * ts
    nt = n_p + n_s
    assert n_p % (SAMPLE_BB * ts) == 0

    lb_cum = jnp.cumsum(jax.nn.softmax(lb_logits.astype(F32), axis=0), axis=0)
    lbs = lb_cum - lb_cum[0:1]
    lbt = jnp.stack([jnp.log(lbs), jnp.log1p(-lbs), 1.0 - lbs] + [jnp.zeros_like(lbs)] * (SUBLANES - 3), axis=1)
    gl = jnp.broadcast_to(jnp.log1p(-jnp.exp2(-5.0 - jnp.arange(HEADS, dtype=F32)))[:, None], (HEADS, DH))
    cos_p, sin_p = _rope_tables(t, 0)
    cos_s, sin_s = _rope_tables(ts, PAST_LEN)
    cos_s, sin_s = jnp.tile(cos_s, (SAMPLE_BB, 1)), jnp.tile(sin_s, (SAMPLE_BB, 1))
    vec3 = lambda a: a.reshape(DEPTH, 1, -1)
    w_router_t = jnp.swapaxes(w_router, 1, 2)
    b_router3 = b_router.reshape(DEPTH, N_EXPERTS, 1)
    mem2 = mem_prompt.reshape(b * n_mem, d)

    x = (x_prompt.reshape(n_p, d), x_sample.reshape(n_s, d))
    xb = jnp.concatenate([x[0].astype(BF16), x[1].astype(BF16)], axis=0)
    tm_proj = _pick(nt, (1024, 512, 128))
    te = _pick(nt * TOP_K, EXPERT_TILES)
    outs = {k: [] for k in ("ret_p", "hg_p", "mk", "mv")}
    ret_s = hg_s = None
    for l in range(DEPTH):
        proj = _matmul(xb, w_in, l, tm_proj, 1280)
        kv_p = _matmul(mem2, w_mem_kv, l, _pick(b * n_mem, (1024, 512, 256)), 2 * HW)
        yr, ret_p, ret_s = _retention(proj, state_ret, l, cos_p, sin_p, cos_s, sin_s, gl,
                                      vec3(ret_norm_g), b, t, nb, ts, ret_s)
        yh, hg_p, hg_s = _hgrn(proj, state_hgrn, l, lbt, vec3(hgrn_norm_g), b, t, nb, ts, hg_s)
        yx = _cross_attention(proj, kv_p, cache_mem_k, cache_mem_v, l, b, t, nb, ts)
        x1, x1b, x1t = _merge((*yr, *yh, *yx), proj, x, w_up_ret, w_up_hgrn, w_up_xattn, w_out,
                              vec3(ln1_g), vec3(ln1_b), l)
        eidx, wn = _router(x1, w_router_t, b_router3, l)
        pos, cnt, off = _positions(eidx)
        tbl, n_steps = _step_table(cnt, off, nt * TOP_K, te)
        xs = _dispatch(pos, x1t.reshape(nt, PACKED_TILES, LANES))
        ye = _experts(tbl, n_steps, te, xs.reshape(-1, LANES), w_e_gate, w_e_up, w_e_down, l)
        ye = ye.reshape(-1, ROW_TILES, LANES)
        x_p, x_s, xb = _combine(pos, wn, ye, x1, x1b, w_s_gate, w_s_up, w_s_down,
                                vec3(ln2_g), vec3(ln2_b), l, n_p)
        x = (x_p, x_s)
        outs["ret_p"].append(ret_p)
        outs["hg_p"].append(hg_p)
        outs["mk"].append(kv_p[:, :HW].reshape(b, n_mem, HEADS, DH))
        outs["mv"].append(kv_p[:, HW:].reshape(b, n_mem, HEADS, DH))
    return (x[0].reshape(b, t, d), x[1].reshape(nb, ts, d),
            jnp.stack(outs["ret_p"]), jnp.stack(outs["hg_p"]), jnp.stack(outs["mk"]), jnp.stack(outs["mv"]),
            ret_s, hg_s)
```

```python
import functools

import jax
import jax.numpy as jnp
from jax import lax
from jax.experimental import pallas as pl
from jax.experimental.pallas import tpu as pltpu

F32 = jnp.float32
BF16 = jnp.bfloat16
I32 = jnp.int32
HIGHEST = lax.Precision.HIGHEST

D_MODEL = 1024
DEPTH = 2
PAST_LEN = 16384
HEADS = 4
DH = 128
HW = HEADS * DH
RET_CHUNK = 128
HG_CHUNK = 16
ROPE_BASE = 10000.0
N_EXPERTS = 64
N_GROUPS = 8
GROUP_SIZE = N_EXPERTS // N_GROUPS
TOPK_GROUPS = 4
TOP_K = 6
D_EXPERT = 256
ROUTED_SCALE = 2.5
LN_EPS = 1e-5
DN_ALPHA = (2 * DEPTH) ** 0.25
N_IN = 9 * HW + 3 * D_MODEL
COL_RET_Q, COL_RET_K, COL_RET_V, COL_RET_G = 0, 1, 2, 3
COL_HG_Q, COL_HG_F, COL_HG_I, COL_HG_G = 4, 5, 6, 7
COL_XA_Q = 8
COL_GATES = 9
LANES = 128
SUBLANES = 8
ROW_TILES = D_MODEL // LANES
PACKED_TILES = ROW_TILES // 2
U32 = jnp.uint32
SAMPLE_BB = 8
EXPERT_TILES = (512, 256)
ISSUE_UNROLL = 8
VMEM_LIMIT = 56 * 1024 * 1024


def _params(sem):
    return pltpu.CompilerParams(dimension_semantics=sem, vmem_limit_bytes=VMEM_LIMIT)


def _bdot(a, b):
    return jnp.dot(a.astype(BF16), b.astype(BF16), preferred_element_type=F32)


def _bdot_nt(a, b):
    return lax.dot_general(a.astype(BF16), b.astype(BF16), (((1,), (1,)), ((), ())),
                           preferred_element_type=F32)


def _bdot_tn(a, b):
    return lax.dot_general(a.astype(BF16), b.astype(BF16), (((0,), (0,)), ((), ())),
                           preferred_element_type=F32)


def _silu(x):
    return x * jax.nn.sigmoid(x)


def _pick(n, prefs):
    for p in prefs:
        if n % p == 0:
            return p
    raise ValueError(f"no tile for {n}")


def _mm_body(x_ref, w_ref, o_ref, wb_ref):
    @pl.when(pl.program_id(1) == 0)
    def _():
        wb_ref[...] = w_ref[...].astype(BF16)

    o_ref[...] = jnp.dot(x_ref[...].astype(BF16), wb_ref[...],
                         preferred_element_type=F32).astype(o_ref.dtype)


def _matmul(x, w, layer, tm, tn, out_dtype=F32, first_col_block=0, n=None):
    m, k = x.shape
    n = w.shape[2] if n is None else n
    return pl.pallas_call(
        _mm_body,
        grid=(n // tn, m // tm),
        in_specs=[pl.BlockSpec((tm, k), lambda j, i: (i, 0)),
                  pl.BlockSpec((None, k, tn), lambda j, i: (layer, 0, first_col_block + j))],
        out_specs=pl.BlockSpec((tm, tn), lambda j, i: (i, j)),
        out_shape=jax.ShapeDtypeStruct((m, n), out_dtype),
        scratch_shapes=[pltpu.VMEM((k, tn), BF16)],
        compiler_params=_params(("arbitrary", "arbitrary")),
        name="dense_matmul",
    )(x, w)


def _rotary(x, cos, sin_signed):
    return x * cos + pltpu.roll(x, DH // 2, 1) * sin_signed


def _group_norm_gate(o, gain, gate):
    mu = jnp.mean(o, axis=-1, keepdims=True)
    var = jnp.mean(jnp.square(o - mu), axis=-1, keepdims=True)
    return (o - mu) * lax.rsqrt(var + LN_EPS) * gain * _silu(gate)


def _ret_prompt_body(q_ref, k_ref, v_ref, g_ref, cos_ref, sin_ref, gl_ref, gain_ref,
                     y_ref, st_ref, s_scr):
    c = pl.program_id(1)

    @pl.when(c == 0)
    def _():
        s_scr[...] = jnp.zeros_like(s_scr)

    ch = RET_CHUNK
    cos = cos_ref[...]
    sin = sin_ref[...]
    ri = lax.broadcasted_iota(I32, (ch, ch), 0)
    ci = lax.broadcasted_iota(I32, (ch, ch), 1)
    rel = (ri - ci).astype(F32)
    idx = lax.broadcasted_iota(I32, (ch, DH), 0).astype(F32)
    for h in range(HEADS):
        sl = slice(h * DH, (h + 1) * DH)
        gl = gl_ref[h:h + 1, :]
        qr = _rotary(q_ref[:, sl].astype(F32), cos, sin)
        kr = _rotary(k_ref[:, sl].astype(F32), cos, sin) * (DH ** -0.5)
        v = v_ref[:, sl]
        intra = jnp.where(rel >= 0, jnp.exp(gl * rel), 0.0)
        att = _bdot_nt(qr, kr) * intra
        s = s_scr[h]
        o = _bdot(att, v) + _bdot(qr, s) * jnp.exp(gl * (idx + 1.0))
        s_scr[h] = s * jnp.exp(gl * float(ch)) + _bdot_tn(kr * jnp.exp(gl * (ch - 1.0 - idx)), v)
        y_ref[:, sl] = _group_norm_gate(o, gain_ref[:, sl], g_ref[:, sl].astype(F32)).astype(BF16)

    @pl.when(c == pl.num_programs(1) - 1)
    def _():
        st_ref[0] = s_scr[...]


def _ret_sample_body(q_ref, k_ref, v_ref, g_ref, cos_ref, sin_ref, gl_ref, gain_ref, sin_ref_state,
                     y_ref, st_ref, *, ts):
    rows = SAMPLE_BB * ts
    shift = ts.bit_length() - 1
    cos = cos_ref[...]
    sin = sin_ref[...]
    ri = lax.broadcasted_iota(I32, (rows, rows), 0)
    ci = lax.broadcasted_iota(I32, (rows, rows), 1)
    rel = (ri - ci).astype(F32)
    mask = ((ri >> shift) == (ci >> shift)) & (ri >= ci)
    idx = (lax.broadcasted_iota(I32, (rows, DH), 0) & (ts - 1)).astype(F32)
    for h in range(HEADS):
        sl = slice(h * DH, (h + 1) * DH)
        gl = gl_ref[h:h + 1, :]
        qr = _rotary(q_ref[:, sl].astype(F32), cos, sin)
        kr = _rotary(k_ref[:, sl].astype(F32), cos, sin) * (DH ** -0.5)
        v = v_ref[:, sl].astype(F32)
        intra = jnp.where(mask, jnp.exp(gl[:, :rows] * rel), 0.0)
        o_intra = _bdot(_bdot_nt(qr, kr) * intra, v)
        q_dec = jnp.exp(gl * (idx + 1.0))
        kd = kr * jnp.exp(gl * (ts - 1.0 - idx))
        c_dec = jnp.exp(gl * float(ts))
        outs = []
        for j in range(SAMPLE_BB):
            rs = slice(j * ts, (j + 1) * ts)
            s = sin_ref_state[j, h]
            outs.append(o_intra[rs] + _bdot(qr[rs], s) * q_dec[rs])
            new_state = s * c_dec + _bdot_tn(kd[rs], v[rs])
            for slot in range(st_ref.shape[0]):
                st_ref[slot, j, h] = new_state
        o = jnp.concatenate(outs, axis=0)
        y_ref[:, sl] = _group_norm_gate(o, gain_ref[:, sl], g_ref[:, sl].astype(F32)).astype(BF16)


def _proj_spec(rows, col, row_map):
    return pl.BlockSpec((rows, HW), lambda *a: (row_map(*a), col))


def _sample_state_call(body, grid, in_specs, args, y_shape, y_spec, layer, nb, prev, name):
    st_shape = jax.ShapeDtypeStruct((DEPTH, nb, HEADS, DH, DH), F32)
    slots = DEPTH if prev is None else 1
    st_spec = pl.BlockSpec((slots, SAMPLE_BB, HEADS, DH, DH), lambda i: (layer, i, 0, 0, 0))
    aliases = {}
    if prev is not None:
        n_in = len(args)
        inner = body
        body = lambda *refs: inner(*refs[:n_in], *refs[n_in + 1:])
        in_specs = in_specs + [pl.BlockSpec(memory_space=pl.ANY)]
        args = args + (prev,)
        aliases = {n_in: 1}
    return pl.pallas_call(
        body, grid=grid, in_specs=in_specs, out_specs=[y_spec, st_spec], out_shape=[y_shape, st_shape],
        input_output_aliases=aliases, compiler_params=_params(("arbitrary",)), name=name,
    )(*args)


def _retention(proj, state, layer, cos_p, sin_p, cos_s, sin_s, gl, gain, b, t, nb, ts, prev_s):
    n_p = b * t
    nc = t // RET_CHUNK
    prow = lambda bi, c: bi * nc + c
    const2 = lambda *a: (0, 0)
    y_p, st_p = pl.pallas_call(
        _ret_prompt_body,
        grid=(b, nc),
        in_specs=[_proj_spec(RET_CHUNK, COL_RET_Q, prow), _proj_spec(RET_CHUNK, COL_RET_K, prow),
                  _proj_spec(RET_CHUNK, COL_RET_V, prow), _proj_spec(RET_CHUNK, COL_RET_G, prow),
                  pl.BlockSpec((RET_CHUNK, DH), lambda bi, c: (c, 0)),
                  pl.BlockSpec((RET_CHUNK, DH), lambda bi, c: (c, 0)),
                  pl.BlockSpec((HEADS, DH), const2),
                  pl.BlockSpec((None, 1, HW), lambda bi, c: (layer, 0, 0))],
        out_specs=[pl.BlockSpec((RET_CHUNK, HW), lambda bi, c: (prow(bi, c), 0)),
                   pl.BlockSpec((1, HEADS, DH, DH), lambda bi, c: (bi, 0, 0, 0))],
        out_shape=[jax.ShapeDtypeStruct((n_p, HW), BF16),
                   jax.ShapeDtypeStruct((b, HEADS, DH, DH), F32)],
        scratch_shapes=[pltpu.VMEM((HEADS, DH, DH), F32)],
        compiler_params=_params(("arbitrary", "arbitrary")),
        name="retention_prompt",
    )(proj, proj, proj, proj, cos_p, sin_p, gl, gain)

    rows = SAMPLE_BB * ts
    base = n_p // rows
    srow = lambda i: base + i
    y_s, st_s = _sample_state_call(
        functools.partial(_ret_sample_body, ts=ts), (nb // SAMPLE_BB,),
        [_proj_spec(rows, COL_RET_Q, srow), _proj_spec(rows, COL_RET_K, srow),
         _proj_spec(rows, COL_RET_V, srow), _proj_spec(rows, COL_RET_G, srow),
         pl.BlockSpec((rows, DH), const2), pl.BlockSpec((rows, DH), const2),
         pl.BlockSpec((HEADS, DH), const2),
         pl.BlockSpec((None, 1, HW), lambda i: (layer, 0, 0)),
         pl.BlockSpec((None, SAMPLE_BB, HEADS, DH, DH), lambda i: (layer, i, 0, 0, 0))],
        (proj, proj, proj, proj, cos_s, sin_s, gl, gain, state),
        jax.ShapeDtypeStruct((nb * ts, HW), BF16), pl.BlockSpec((rows, HW), lambda i: (i, 0)),
        layer, nb, prev_s, "retention_sample")
    return (y_p, y_s), st_p, st_s


def _hg_prepare(hq_ref, hf_ref, lbt_ref, rows, chunk):
    shift = chunk.bit_length() - 1
    ri = lax.broadcasted_iota(I32, (rows, rows), 0)
    ci = lax.broadcasted_iota(I32, (rows, rows), 1)
    same = (ri >> shift) == (ci >> shift)
    causal = same & (ci <= ri)
    z = hf_ref[...]
    log_lb = lbt_ref[0:1, :]
    log_1m_lb = lbt_ref[1:2, :]
    one_m_lb = lbt_ref[2:3, :]
    log_sig = jnp.minimum(z, 0.0) - jnp.log1p(jnp.exp(-jnp.abs(z)))
    bterm = log_1m_lb + log_sig
    logf = jnp.maximum(log_lb, bterm) + jnp.log1p(jnp.exp(-jnp.abs(log_lb - bterm)))
    kh = one_m_lb * jax.nn.sigmoid(-z)
    qh = _silu(hq_ref[...].astype(F32)) * (DH ** -0.5)
    cum = jnp.dot(causal.astype(F32), logf, precision=HIGHEST, preferred_element_type=F32)
    tot = jnp.dot(same.astype(F32), logf, precision=HIGHEST, preferred_element_type=F32)
    qi = qh * jnp.exp(cum)
    ki = kh * jnp.exp(-cum)
    ke = kh * jnp.exp(tot - cum)
    return causal, qi, ki, ke, tot, logf


def _rms_norm_gate(o, gain, gate):
    return o * lax.rsqrt(jnp.mean(jnp.square(o), axis=-1, keepdims=True) + LN_EPS) * gain * _silu(gate)


def _hg_prompt_body(hq_ref, hf_ref, hi_ref, hg_ref, lbt_ref, gain_ref, y_ref, st_ref, s_scr):
    c = pl.program_id(1)

    @pl.when(c == 0)
    def _():
        s_scr[...] = jnp.zeros_like(s_scr)

    rows = RET_CHUNK
    n_sub = rows // HG_CHUNK
    shift = HG_CHUNK.bit_length() - 1
    causal, qi, ki, ke, tot, logf = _hg_prepare(hq_ref, hf_ref, lbt_ref, rows, HG_CHUNK)
    ri = lax.broadcasted_iota(I32, (rows, rows), 0)
    ci = lax.broadcasted_iota(I32, (rows, rows), 1)
    pre = jnp.dot(((ci >> shift) < (ri >> shift)).astype(F32), logf, precision=HIGHEST,
                  preferred_element_type=F32)
    sub = lax.broadcasted_iota(I32, (rows, DH), 0) >> shift
    v = hi_ref[...].astype(F32)
    for h in range(HEADS):
        sl = slice(h * DH, (h + 1) * DH)
        q_h, ke_h, v_h, pre_h = qi[:, sl], ke[:, sl], v[:, sl], pre[:, sl]
        att = jnp.where(causal, _bdot_nt(q_h, ki[:, sl]), 0.0)
        st0 = s_scr[h]
        o = _bdot(att, v_h) + _bdot_nt(q_h * jnp.exp(pre_h), st0)
        end_last = pre_h[rows - 1:rows] + tot[rows - 1:rows, sl]
        st = st0 * jnp.exp(end_last)
        for i in range(n_sub):
            rs = slice(i * HG_CHUNK, (i + 1) * HG_CHUNK)
            u_t = _bdot_tn(v_h[rs], ke_h[rs])
            if i + 1 < n_sub:
                end_i = pre_h[(i + 1) * HG_CHUNK:(i + 1) * HG_CHUNK + 1]
                later = q_h * jnp.exp(jnp.where(sub > i, pre_h - end_i, -jnp.inf))
                o = o + _bdot_nt(later, u_t)
                st = st + u_t * jnp.exp(end_last - end_i)
            else:
                st = st + u_t
        s_scr[h] = st
        y_ref[:, sl] = _rms_norm_gate(o, gain_ref[:, sl], hg_ref[:, sl].astype(F32)).astype(BF16)

    @pl.when(c == pl.num_programs(1) - 1)
    def _():
        for h in range(HEADS):
            st_ref[0, h] = s_scr[h].T


def _hg_sample_body(hq_ref, hf_ref, hi_ref, hg_ref, lbt_ref, gain_ref, sin_ref_state,
                    y_ref, st_ref, *, ts):
    rows = SAMPLE_BB * ts
    causal, qi, ki, ke, tot, _ = _hg_prepare(hq_ref, hf_ref, lbt_ref, rows, ts)
    etot = jnp.exp(tot)
    v = hi_ref[...].astype(F32)
    for h in range(HEADS):
        sl = slice(h * DH, (h + 1) * DH)
        att = jnp.where(causal, _bdot_nt(qi[:, sl], ki[:, sl]), 0.0)
        o_intra = _bdot(att, v[:, sl])
        outs = []
        for j in range(SAMPLE_BB):
            rs = slice(j * ts, (j + 1) * ts)
            s = sin_ref_state[j, h]
            outs.append(o_intra[rs] + _bdot(qi[rs, sl], s))
            scale = jnp.broadcast_to(etot[j * ts:j * ts + 1, sl], (DH, DH)).T
            new_state = s * scale + _bdot_tn(ke[rs, sl], v[rs, sl])
            for slot in range(st_ref.shape[0]):
                st_ref[slot, j, h] = new_state
        o = jnp.concatenate(outs, axis=0)
        y_ref[:, sl] = _rms_norm_gate(o, gain_ref[:, sl], hg_ref[:, sl].astype(F32)).astype(BF16)


def _hgrn(proj, forget, state, layer, lbt, gain, b, t, nb, ts, prev_s):
    n_p = b * t
    nc = t // RET_CHUNK
    prow = lambda bi, c: bi * nc + c
    y_p, st_p = pl.pallas_call(
        _hg_prompt_body,
        grid=(b, nc),
        in_specs=[_proj_spec(RET_CHUNK, COL_HG_Q, prow), _proj_spec(RET_CHUNK, 0, prow),
                  _proj_spec(RET_CHUNK, COL_HG_I, prow), _proj_spec(RET_CHUNK, COL_HG_G, prow),
                  pl.BlockSpec((None, SUBLANES, HW), lambda bi, c: (layer, 0, 0)),
                  pl.BlockSpec((None, 1, HW), lambda bi, c: (layer, 0, 0))],
        out_specs=[pl.BlockSpec((RET_CHUNK, HW), lambda bi, c: (prow(bi, c), 0)),
                   pl.BlockSpec((1, HEADS, DH, DH), lambda bi, c: (bi, 0, 0, 0))],
        out_shape=[jax.ShapeDtypeStruct((n_p, HW), BF16),
                   jax.ShapeDtypeStruct((b, HEADS, DH, DH), F32)],
        scratch_shapes=[pltpu.VMEM((HEADS, DH, DH), F32)],
        compiler_params=_params(("arbitrary", "arbitrary")),
        name="hgrn_prompt",
    )(proj, forget, proj, proj, lbt, gain)

    rows = SAMPLE_BB * ts
    base = n_p // rows
    srow = lambda i: base + i
    y_s, st_s = _sample_state_call(
        functools.partial(_hg_sample_body, ts=ts), (nb // SAMPLE_BB,),
        [_proj_spec(rows, COL_HG_Q, srow), _proj_spec(rows, 0, srow),
         _proj_spec(rows, COL_HG_I, srow), _proj_spec(rows, COL_HG_G, srow),
         pl.BlockSpec((None, SUBLANES, HW), lambda i: (layer, 0, 0)),
         pl.BlockSpec((None, 1, HW), lambda i: (layer, 0, 0)),
         pl.BlockSpec((None, SAMPLE_BB, HEADS, DH, DH), lambda i: (layer, i, 0, 0, 0))],
        (proj, forget, proj, proj, lbt, gain, state),
        jax.ShapeDtypeStruct((nb * ts, HW), BF16), pl.BlockSpec((rows, HW), lambda i: (i, 0)),
        layer, nb, prev_s, "hgrn_sample")
    return (y_p, y_s), st_p, st_s


def _softmax_rows(s):
    e = jnp.exp(s - jnp.max(s, axis=-1, keepdims=True))
    return e / jnp.sum(e, axis=-1, keepdims=True)


def _xa_prompt_body(q_ref, k_ref, v_ref, y_ref):
    for h in range(HEADS):
        sl = slice(h * DH, (h + 1) * DH)
        a = _softmax_rows(_bdot_nt(q_ref[:, sl].astype(F32) * (DH ** -0.5), k_ref[:, sl]))
        y_ref[:, sl] = _bdot(a, v_ref[:, sl]).astype(BF16)


def _xa_sample_body(q_ref, k_ref, v_ref, y_ref, *, ts):
    n_mem = k_ref.shape[1] // HEADS
    pairs = [(j, h) for j in range(SAMPLE_BB) for h in range(HEADS)]
    q = q_ref[...].astype(F32) * (DH ** -0.5)
    scores = [_bdot_nt(q[j * ts:(j + 1) * ts, h * DH:(h + 1) * DH], k_ref[j, pl.ds(h, n_mem, stride=HEADS), :])
              for j, h in pairs]
    a = _softmax_rows(jnp.concatenate(scores, axis=0))
    for n, (j, h) in enumerate(pairs):
        y = _bdot(a[n * ts:(n + 1) * ts], v_ref[j, pl.ds(h, n_mem, stride=HEADS), :])
        y_ref[j * ts:(j + 1) * ts, h * DH:(h + 1) * DH] = y.astype(BF16)


def _cross_attention(proj, kv_p, cache_k, cache_v, layer, b, t, nb, ts):
    n_p = b * t
    n_mem = kv_p.shape[0] // b
    tq = _pick(t, (512, 256, 128))
    nq = t // tq
    y_p = pl.pallas_call(
        _xa_prompt_body,
        grid=(b, nq),
        in_specs=[_proj_spec(tq, COL_XA_Q, lambda bi, qi: bi * nq + qi),
                  pl.BlockSpec((n_mem, HW), lambda bi, qi: (bi, 0)),
                  pl.BlockSpec((n_mem, HW), lambda bi, qi: (bi, 1))],
        out_specs=pl.BlockSpec((tq, HW), lambda bi, qi: (bi * nq + qi, 0)),
        out_shape=jax.ShapeDtypeStruct((n_p, HW), BF16),
        compiler_params=_params(("arbitrary", "arbitrary")),
        name="xattn_prompt",
    )(proj, kv_p, kv_p)

    rows = SAMPLE_BB * ts
    base = n_p // rows
    cache_k = cache_k.reshape(DEPTH, nb, n_mem * HEADS, DH)
    cache_v = cache_v.reshape(DEPTH, nb, n_mem * HEADS, DH)
    kv_spec = pl.BlockSpec((None, SAMPLE_BB, n_mem * HEADS, DH), lambda i: (layer, i, 0, 0))
    y_s = pl.pallas_call(
        functools.partial(_xa_sample_body, ts=ts),
        grid=(nb // SAMPLE_BB,),
        in_specs=[_proj_spec(rows, COL_XA_Q, lambda i: base + i), kv_spec, kv_spec],
        out_specs=pl.BlockSpec((rows, HW), lambda i: (i, 0)),
        out_shape=jax.ShapeDtypeStruct((nb * ts, HW), BF16),
        compiler_params=_params(("arbitrary",)),
        name="xattn_sample",
    )(proj, cache_k, cache_v)
    return (y_p, y_s)


def _layer_norm(tv, g, b):
    mu = jnp.mean(tv, axis=-1, keepdims=True)
    var = jnp.mean(jnp.square(tv - mu), axis=-1, keepdims=True)
    return (tv - mu) * lax.rsqrt(var + LN_EPS) * g + b


def _merge_body(yrp_ref, yrs_ref, yhp_ref, yhs_ref, yxp_ref, yxs_ref, g0a, g0b, g1a, g1b, g2a, g2b, xp_ref, xs_ref,
                wr_ref, wh_ref, wx_ref, wo_ref, lg_ref, lb_ref,
                x1_ref, x1b_ref, x1t_ref, wr_s, wh_s, wx_s, wo_s, *, prompt_tiles):
    @pl.when(pl.program_id(0) == 0)
    def _():
        wr_s[...] = wr_ref[...].astype(BF16)
        wh_s[...] = wh_ref[...].astype(BF16)
        wx_s[...] = wx_ref[...].astype(BF16)
        wo_s[...] = wo_ref[...].astype(BF16)

    is_prompt = pl.program_id(0) < prompt_tiles

    def branch(yp_ref, ys_ref, w_s, ga, gb):
        y = jnp.where(is_prompt, yp_ref[...], ys_ref[...])
        gate = jax.nn.sigmoid(jnp.concatenate([ga[...], gb[...]], axis=-1).astype(F32))
        return gate * jnp.dot(y, w_s[...], preferred_element_type=F32)

    m = (branch(yrp_ref, yrs_ref, wr_s, g0a, g0b) + branch(yhp_ref, yhs_ref, wh_s, g1a, g1b)
         + branch(yxp_ref, yxs_ref, wx_s, g2a, g2b))
    hmix = jnp.dot(m.astype(BF16), wo_s[...], preferred_element_type=F32)
    x = jnp.where(is_prompt, xp_ref[...], xs_ref[...])
    x1 = _layer_norm(DN_ALPHA * x + hmix, lg_ref[...], lb_ref[...])
    x1_ref[...] = x1
    x1b_ref[...] = x1.astype(BF16)
    tm = x1.shape[0]
    bits = lax.bitcast_convert_type(x1.astype(BF16).astype(F32), U32)
    half = D_MODEL // 2
    packed = (bits[:, :half] >> 16) | (bits[:, half:] & jnp.uint32(0xFFFF0000))
    for s in range(PACKED_TILES):
        x1t_ref[pl.ds(s, tm, stride=PACKED_TILES), :] = packed[:, s * LANES:(s + 1) * LANES]


def _merge(ys, proj, x, w_up_ret, w_up_hgrn, w_up_xattn, w_out, ln_g, ln_b, layer):
    n_p = ys[0].shape[0]
    nt = n_p + ys[1].shape[0]
    tm = _pick(nt, (256, 128))
    assert n_p % tm == 0 and ys[1].shape[0] % tm == 0
    p_tiles = n_p // tm
    row = lambda i: (i, 0)
    p_map = lambda i: (jnp.minimum(i, p_tiles - 1), 0)
    s_map = lambda i: (jnp.maximum(i - p_tiles, 0), 0)
    y_specs = [pl.BlockSpec((tm, HW), p_map), pl.BlockSpec((tm, HW), s_map)] * 3
    wspec = lambda k: pl.BlockSpec((None, k, D_MODEL), lambda i: (layer, 0, 0))
    vec = pl.BlockSpec((None, 1, D_MODEL), lambda i: (layer, 0, 0))
    gate_specs = [pl.BlockSpec((tm, HW), lambda i, c=c: (i, COL_GATES + c)) for c in range(6)]
    return pl.pallas_call(
        functools.partial(_merge_body, prompt_tiles=p_tiles),
        grid=(nt // tm,),
        in_specs=y_specs + gate_specs + [pl.BlockSpec((tm, D_MODEL), p_map), pl.BlockSpec((tm, D_MODEL), s_map),
                  wspec(HW), wspec(HW), wspec(HW), wspec(D_MODEL), vec, vec],
        out_specs=[pl.BlockSpec((tm, D_MODEL), row), pl.BlockSpec((tm, D_MODEL), row),
                   pl.BlockSpec((tm * PACKED_TILES, LANES), row)],
        out_shape=[jax.ShapeDtypeStruct((nt, D_MODEL), F32),
                   jax.ShapeDtypeStruct((nt, D_MODEL), BF16),
                   jax.ShapeDtypeStruct((nt * PACKED_TILES, LANES), U32)],
        scratch_shapes=[pltpu.VMEM((HW, D_MODEL), BF16)] * 3 + [pltpu.VMEM((D_MODEL, D_MODEL), BF16)],
        compiler_params=_params(("arbitrary",)),
        name="merge_out_ln1",
    )(*ys, *([proj] * 6), *x, w_up_ret, w_up_hgrn, w_up_xattn, w_out, ln_g, ln_b)


def _router_body(x_ref, wt_ref, b_ref, eidx_ref, wn_ref):
    tm = x_ref.shape[0]
    x = x_ref[...]
    w = wt_ref[...]
    xh = x.astype(BF16)
    xl = (x - xh.astype(F32)).astype(BF16)
    wh = w.astype(BF16)
    wl = (w - wh.astype(F32)).astype(BF16)
    logits = _bdot_nt(wh, xh) + (_bdot_nt(wh, xl) + _bdot_nt(wl, xh))
    s = jax.nn.sigmoid(logits)
    sel = s + b_ref[...]
    neg = -jnp.inf
    groups = [sel[g * GROUP_SIZE:(g + 1) * GROUP_SIZE, :] for g in range(N_GROUPS)]
    ie = lax.broadcasted_iota(I32, (GROUP_SIZE, tm), 0).astype(F32)
    rows = []
    for blk in groups:
        m1 = jnp.max(blk, axis=0, keepdims=True)
        first = jnp.min(jnp.where(blk == m1, ie, float(GROUP_SIZE)), axis=0, keepdims=True)
        rows.append(m1 + jnp.max(jnp.where(ie == first, neg, blk), axis=0, keepdims=True))
    gscore = jnp.concatenate(rows, axis=0)
    ig = lax.broadcasted_iota(I32, gscore.shape, 0).astype(F32)
    gmask = jnp.zeros(gscore.shape, F32)
    for _ in range(TOPK_GROUPS):
        m = jnp.max(gscore, axis=0, keepdims=True)
        gi = jnp.min(jnp.where(gscore == m, ig, float(N_GROUPS)), axis=0, keepdims=True)
        hit = ig == gi
        gmask = jnp.where(hit, 1.0, gmask)
        gscore = jnp.where(hit, neg, gscore)
    masked = jnp.concatenate([jnp.where(gmask[g:g + 1, :] > 0.5, blk, neg)
                              for g, blk in enumerate(groups)], axis=0)
    ix = lax.broadcasted_iota(I32, masked.shape, 0).astype(F32)
    idxs, ws = [], []
    for _ in range(TOP_K):
        m = jnp.max(masked, axis=0, keepdims=True)
        ei = jnp.min(jnp.where(masked == m, ix, float(N_EXPERTS)), axis=0, keepdims=True)
        hit = ix == ei
        idxs.append(ei)
        ws.append(jnp.sum(jnp.where(hit, s, 0.0), axis=0, keepdims=True))
        masked = jnp.where(hit, neg, masked)
    wsum = ws[0]
    for w in ws[1:]:
        wsum = wsum + w
    pad = [jnp.zeros((1, tm), F32)] * (SUBLANES - TOP_K)
    eidx_ref[...] = jnp.concatenate(idxs + pad, axis=0).astype(I32)
    wn_ref[...] = jnp.concatenate([w / wsum * ROUTED_SCALE for w in ws] + pad, axis=0)


def _router(x1, w_router_t, b_router, layer):
    nt = x1.shape[0]
    tm = _pick(nt, (512, 256, 128))
    return pl.pallas_call(
        _router_body,
        grid=(nt // tm,),
        in_specs=[pl.BlockSpec((tm, D_MODEL), lambda i: (i, 0)),
                  pl.BlockSpec((None, N_EXPERTS, D_MODEL), lambda i: (layer, 0, 0)),
                  pl.BlockSpec((None, N_EXPERTS, 1), lambda i: (layer, 0, 0))],
        out_specs=[pl.BlockSpec((SUBLANES, tm), lambda i: (0, i))] * 2,
        out_shape=[jax.ShapeDtypeStruct((SUBLANES, nt), I32),
                   jax.ShapeDtypeStruct((SUBLANES, nt), F32)],
        compiler_params=_params(("arbitrary",)),
        name="moe_router",
    )(x1, w_router_t, b_router)


def _positions_body(eidx_ref, pos_ref, cnt_ref, off_ref, base_scr, off_scr):
    phase = pl.program_id(0)
    i = pl.program_id(1)
    tp = eidx_ref.shape[1]
    ix = lax.broadcasted_iota(I32, (N_EXPERTS, tp), 0)
    eidx = eidx_ref[...]
    member = jnp.zeros((N_EXPERTS, tp), F32)
    for k in range(TOP_K):
        member = member + (ix == eidx[k:k + 1, :]).astype(F32)
    tile_cnt = jnp.sum(member, axis=1, keepdims=True)

    @pl.when((phase == 0) & (i == 0))
    def _():
        base_scr[...] = jnp.zeros_like(base_scr)

    @pl.when((phase == 1) & (i == 0))
    def _():
        cnt = base_scr[...]
        er = lax.broadcasted_iota(I32, (N_EXPERTS, N_EXPERTS), 0)
        ec = lax.broadcasted_iota(I32, (N_EXPERTS, N_EXPERTS), 1)
        off = jnp.dot((ec < er).astype(F32), cnt, precision=HIGHEST, preferred_element_type=F32)
        off_scr[...] = off
        cnt_ref[...] = cnt
        off_ref[...] = off
        base_scr[...] = jnp.zeros_like(base_scr)

    @pl.when(phase == 1)
    def _():
        tr = lax.broadcasted_iota(I32, (tp, tp), 0)
        tc = lax.broadcasted_iota(I32, (tp, tp), 1)
        before = jnp.dot(member.astype(BF16), (tr < tc).astype(BF16), preferred_element_type=F32)
        where_to = before + (off_scr[...] + base_scr[...])[:, 0:1]
        rows = [jnp.sum(jnp.where(ix == eidx[k:k + 1, :], where_to, 0.0), axis=0, keepdims=True)
                for k in range(TOP_K)]
        rows += [jnp.zeros((1, tp), F32)] * (SUBLANES - TOP_K)
        pos_ref[...] = jnp.concatenate(rows, axis=0).astype(I32)

    base_scr[...] = base_scr[...] + tile_cnt


def _positions(eidx):
    nt = eidx.shape[1]
    tp = _pick(nt, (512, 256, 128))
    const = lambda p, i: (0, 0)
    return pl.pallas_call(
        _positions_body,
        grid=(2, nt // tp),
        in_specs=[pl.BlockSpec((SUBLANES, tp), lambda p, i: (0, i))],
        out_specs=[pl.BlockSpec((SUBLANES, tp), lambda p, i: (0, i * p)),
                   pl.BlockSpec((N_EXPERTS, LANES), const), pl.BlockSpec((N_EXPERTS, LANES), const)],
        out_shape=[jax.ShapeDtypeStruct((SUBLANES, nt), I32),
                   jax.ShapeDtypeStruct((N_EXPERTS, LANES), F32),
                   jax.ShapeDtypeStruct((N_EXPERTS, LANES), F32)],
        scratch_shapes=[pltpu.VMEM((N_EXPERTS, LANES), F32), pltpu.VMEM((N_EXPERTS, LANES), F32)],
        compiler_params=_params(("arbitrary", "arbitrary")),
        name="moe_positions",
    )(eidx)


T_TILE, T_EXPERT, T_LO, T_HI, T_FRESH, T_NEWEXP = range(6)

def _table_body(cnt_ref, off_ref, tbl_ref, *, tile_rows):
    te = float(tile_rows)
    n = tbl_ref.shape[1]
    cnt = cnt_ref[...]
    off = off_ref[...]
    first = jnp.floor(off * (1.0 / te))
    last = jnp.floor((off + cnt - 1.0) * (1.0 / te))
    nst = jnp.where(cnt > 0.0, last - first + 1.0, 0.0)
    er = lax.broadcasted_iota(I32, (N_EXPERTS, N_EXPERTS), 0)
    ec = lax.broadcasted_iota(I32, (N_EXPERTS, N_EXPERTS), 1)
    s_end = jnp.dot((ec <= er).astype(F32), nst, precision=HIGHEST, preferred_element_type=F32)
    s_beg = s_end - nst
    total = s_end[N_EXPERTS - 1:N_EXPERTS, 0:1]
    sidx = lax.broadcasted_iota(I32, (1, n), 1).astype(F32)
    s = jnp.minimum(sidx, total - 1.0)
    e_s = jnp.sum((s_end[:, 0:1] <= s).astype(F32), axis=0, keepdims=True)
    hot = lax.broadcasted_iota(I32, (N_EXPERTS, n), 0).astype(F32) == e_s

    def pick(col):
        return jnp.sum(jnp.where(hot, col[:, 0:1], 0.0), axis=0, keepdims=True)

    tile = pick(first) + s - pick(s_beg)
    valid = sidx < total
    o, c = pick(off), pick(cnt)
    lo = jnp.where(valid, jnp.maximum(o, tile * te), 0.0)
    hi = jnp.where(valid, jnp.minimum(o + c, (tile + 1.0) * te), 0.0)
    head = sidx == 0.0
    fresh = jnp.where((tile != pltpu.roll(tile, 1, 1)) | head, 1.0, 0.0)
    newexp = jnp.where((e_s != pltpu.roll(e_s, 1, 1)) | head, 1.0, 0.0)
    pad = [jnp.zeros((1, n), F32)] * (SUBLANES - 6)
    tbl_ref[...] = jnp.concatenate([tile, e_s, lo, hi, fresh, newexp] + pad, axis=0).astype(I32)


def _step_table(cnt, off, n_rows, te):
    n_steps = n_rows // te + N_EXPERTS
    width = -(-n_steps // LANES) * LANES
    tbl = pl.pallas_call(
        functools.partial(_table_body, tile_rows=te),
        out_shape=jax.ShapeDtypeStruct((SUBLANES, width), I32),
        name="moe_step_table",
    )(cnt, off)
    return tbl, n_steps


def _wait_tile_rows(like_src, dst_rows_ref, sem_ref):
    rows = like_src.shape[0]
    for _ in range(TOP_K):
        pltpu.make_async_copy(like_src, dst_rows_ref.at[pl.ds(0, rows)], sem_ref).wait()


def _dispatch_body(pos_ref, xt_ref, xs_ref, pos_s, sem_p, sem):
    td = pos_ref.shape[1]
    cp = pltpu.make_async_copy(pos_ref, pos_s, sem_p)
    cp.start()
    cp.wait()

    def issue(g, carry):
        for u in range(ISSUE_UNROLL):
            r = g * ISSUE_UNROLL + u
            for k in range(TOP_K):
                pltpu.make_async_copy(xt_ref.at[r], xs_ref.at[pos_s[k, r]], sem).start(priority=k % 2)
        return carry

    lax.fori_loop(0, td // ISSUE_UNROLL, issue, 0)
    _wait_tile_rows(xt_ref, xs_ref, sem)


def _dispatch(pos, x1t):
    nt = x1t.shape[0]
    td = _pick(nt, (512, 256, 128))
    return pl.pallas_call(
        _dispatch_body,
        grid=(nt // td,),
        in_specs=[pl.BlockSpec((SUBLANES, td), lambda i: (0, i)),
                  pl.BlockSpec((td,) + x1t.shape[1:], lambda i: (i, 0, 0))],
        out_specs=pl.BlockSpec(memory_space=pl.ANY),
        out_shape=jax.ShapeDtypeStruct((nt * TOP_K,) + x1t.shape[1:], x1t.dtype),
        scratch_shapes=[pltpu.SMEM((SUBLANES, td), I32), pltpu.SemaphoreType.DMA, pltpu.SemaphoreType.DMA],
        compiler_params=_params(("arbitrary",)),
        name="moe_dispatch",
    )(pos, x1t)


def _experts_body(tbl_ref, xs_ref, wg_ref, wu_ref, wd_ref, ye_ref, wg_s, wu_s, wd_s):
    s = pl.program_id(0)
    te = xs_ref.shape[0] // PACKED_TILES
    lo = tbl_ref[T_LO, s]
    hi = tbl_ref[T_HI, s]

    @pl.when(tbl_ref[T_NEWEXP, s] == 1)
    def _():
        wg_s[...] = wg_ref[...].astype(BF16)
        wu_s[...] = wu_ref[...].astype(BF16)
        wd_s[...] = wd_ref[...].astype(BF16)

    first_row = tbl_ref[T_TILE, s] * te
    whole_tile = (lo <= first_row) & (hi >= first_row + te)
    fresh = tbl_ref[T_FRESH, s] == 1

    @pl.when(fresh & jnp.logical_not(whole_tile))
    def _():
        ye_ref[...] = jnp.zeros_like(ye_ref)

    @pl.when(hi > lo)
    def _():
        words = [xs_ref[pl.ds(t, te, stride=PACKED_TILES), :] for t in range(PACKED_TILES)]
        low = [lax.bitcast_convert_type(w << 16, F32).astype(BF16) for w in words]
        high = [lax.bitcast_convert_type(w & jnp.uint32(0xFFFF0000), F32).astype(BF16) for w in words]
        x = jnp.concatenate(low + high, axis=-1)
        g = jnp.dot(x, wg_s[...], preferred_element_type=F32)
        u = jnp.dot(x, wu_s[...], preferred_element_type=F32)
        y = jnp.dot((_silu(g) * u).astype(BF16), wd_s[...], preferred_element_type=F32)
        @pl.when(whole_tile)
        def _():
            for t in range(ROW_TILES):
                ye_ref[pl.ds(t, te, stride=ROW_TILES), :] = y[:, t * LANES:(t + 1) * LANES]

        @pl.when(jnp.logical_not(whole_tile))
        def _():
            row = first_row + lax.broadcasted_iota(I32, (te, LANES), 0)
            mine = (row >= lo) & (row < hi)
            for t in range(ROW_TILES):
                sl = pl.ds(t, te, stride=ROW_TILES)
                ye_ref[sl, :] = jnp.where(mine, y[:, t * LANES:(t + 1) * LANES], ye_ref[sl, :])


def _experts(tbl, n_steps, te, xs, w_gate, w_up, w_down, layer):
    n_rows = xs.shape[0] // PACKED_TILES
    tile_map = lambda s, tbl: (tbl[T_TILE, s], 0)
    w_map = lambda s, tbl: (layer, tbl[T_EXPERT, s], 0, 0)
    w_in_spec = pl.BlockSpec((None, None, D_MODEL, D_EXPERT), w_map)
    w_dn_spec = pl.BlockSpec((None, None, D_EXPERT, D_MODEL), w_map)
    return pl.pallas_call(
        _experts_body,
        grid_spec=pltpu.PrefetchScalarGridSpec(
            num_scalar_prefetch=1,
            grid=(n_steps,),
            in_specs=[pl.BlockSpec((te * PACKED_TILES, LANES), tile_map), w_in_spec, w_in_spec, w_dn_spec],
            out_specs=pl.BlockSpec((te * ROW_TILES, LANES), tile_map),
            scratch_shapes=[pltpu.VMEM((D_MODEL, D_EXPERT), BF16), pltpu.VMEM((D_MODEL, D_EXPERT), BF16),
                            pltpu.VMEM((D_EXPERT, D_MODEL), BF16)]),
        out_shape=jax.ShapeDtypeStruct((n_rows * ROW_TILES, LANES), F32),
        compiler_params=_params(("arbitrary",)),
        name="moe_experts",
    )(tbl, xs, w_gate, w_up, w_down)


def _combine_body(pos_ref, wn_ref, ye_ref, x1_ref, x1b_ref, wsg_ref, wsu_ref, wsd_ref, lg_ref, lb_ref,
                  x2p_ref, x2s_ref, x2b_ref, pos_s, buf, wsg_s, wsu_s, wsd_s, sem_p, sem, *, prompt_tiles):
    i = pl.program_id(0)
    n = pl.num_programs(0)
    tc = wn_ref.shape[1]
    slot = i % 2
    tile_rows = tc * ROW_TILES

    def request(tile, into):
        cp = pltpu.make_async_copy(pos_ref.at[tile], pos_s, sem_p)
        cp.start()
        cp.wait()

        def issue(g, carry):
            for u in range(ISSUE_UNROLL):
                r = g * ISSUE_UNROLL + u
                for k in range(TOP_K):
                    at = pl.multiple_of(((into * TOP_K + k) * tc + r) * ROW_TILES, ROW_TILES)
                    pltpu.make_async_copy(ye_ref.at[pos_s[k, r]], buf.at[pl.ds(at, ROW_TILES)],
                                          sem.at[into]).start(priority=k % 2)
            return carry

        lax.fori_loop(0, tc // ISSUE_UNROLL, issue, 0)

    @pl.when(i == 0)
    def _():
        wsg_s[...] = wsg_ref[...].astype(BF16)
        wsu_s[...] = wsu_ref[...].astype(BF16)
        wsd_s[...] = wsd_ref[...].astype(BF16)
        request(0, 0)

    @pl.when(i + 1 < n)
    def _():
        request(i + 1, 1 - slot)

    xb = x1b_ref[...]
    hs = _silu(jnp.dot(xb, wsg_s[...], preferred_element_type=F32)) * jnp.dot(xb, wsu_s[...], preferred_element_type=F32)
    shared = jnp.dot(hs.astype(BF16), wsd_s[...], preferred_element_type=F32)
    for k in range(TOP_K):
        pltpu.make_async_copy(buf.at[pl.ds(0, tile_rows)], buf.at[pl.ds(tile_rows, tile_rows)], sem.at[slot]).wait()

    w = wn_ref[...]
    acc = [None] * ROW_TILES
    for k in range(TOP_K):
        wcol = jnp.concatenate([jnp.broadcast_to(w[k:k + 1, c * LANES:(c + 1) * LANES], (LANES, LANES)).T
                                for c in range(tc // LANES)], axis=0)
        base = (slot * TOP_K + k) * tile_rows
        for t in range(ROW_TILES):
            term = wcol * buf[pl.ds(base + t, tc, stride=ROW_TILES), :]
            acc[t] = term if acc[t] is None else acc[t] + term
    routed = jnp.concatenate(acc, axis=-1)
    x2 = _layer_norm(DN_ALPHA * x1_ref[...] + (routed + shared), lg_ref[...], lb_ref[...])
    x2b_ref[...] = x2.astype(BF16)

    @pl.when(i < prompt_tiles)
    def _():
        x2p_ref[...] = x2

    @pl.when(i >= prompt_tiles)
    def _():
        x2s_ref[...] = x2


def _combine(pos, wn, ye, x1, x1b, w_s_gate, w_s_up, w_s_down, ln_g, ln_b, layer, n_p):
    nt = x1.shape[0]
    tc = _pick(nt, (256, 128))
    assert n_p % tc == 0
    n_tiles = nt // tc
    p_tiles = n_p // tc
    d_sh = w_s_gate.shape[2]
    pos3 = pos.reshape(SUBLANES, n_tiles, tc).transpose(1, 0, 2)
    row = lambda i: (i, 0)
    vec = pl.BlockSpec((None, 1, D_MODEL), lambda i: (layer, 0, 0))
    return pl.pallas_call(
        functools.partial(_combine_body, prompt_tiles=p_tiles),
        grid=(n_tiles,),
        in_specs=[pl.BlockSpec((n_tiles, SUBLANES, tc), lambda i: (0, 0, 0)),
                  pl.BlockSpec((SUBLANES, tc), lambda i: (0, i)),
                  pl.BlockSpec(memory_space=pl.ANY),
                  pl.BlockSpec((tc, D_MODEL), row), pl.BlockSpec((tc, D_MODEL), row),
                  pl.BlockSpec((None, D_MODEL, d_sh), lambda i: (layer, 0, 0)),
                  pl.BlockSpec((None, D_MODEL, d_sh), lambda i: (layer, 0, 0)),
                  pl.BlockSpec((None, d_sh, D_MODEL), lambda i: (layer, 0, 0)), vec, vec],
        out_specs=[pl.BlockSpec((tc, D_MODEL), lambda i: (jnp.minimum(i, p_tiles - 1), 0)),
                   pl.BlockSpec((tc, D_MODEL), lambda i: (jnp.maximum(i - p_tiles, 0), 0)),
                   pl.BlockSpec((tc, D_MODEL), row)],
        out_shape=[jax.ShapeDtypeStruct((n_p, D_MODEL), F32), jax.ShapeDtypeStruct((nt - n_p, D_MODEL), F32),
                   jax.ShapeDtypeStruct((nt, D_MODEL), BF16)],
        scratch_shapes=[pltpu.SMEM((SUBLANES, tc), I32),
                        pltpu.VMEM((2 * TOP_K * tc * ROW_TILES, LANES), F32),
                        pltpu.VMEM((D_MODEL, d_sh), BF16), pltpu.VMEM((D_MODEL, d_sh), BF16),
                        pltpu.VMEM((d_sh, D_MODEL), BF16),
                        pltpu.SemaphoreType.DMA, pltpu.SemaphoreType.DMA((2,))],
        compiler_params=_params(("arbitrary",)),
        name="moe_combine_ln2",
    )(pos3, wn, ye, x1, x1b, w_s_gate, w_s_up, w_s_down, ln_g, ln_b)


def _rope_tables(t, pos0):
    inv = 1.0 / (ROPE_BASE ** (jnp.arange(0, DH, 2, dtype=F32) / DH))
    ang = (jnp.arange(t, dtype=F32) + pos0)[:, None] * inv[None, :]
    cos, sin = jnp.cos(ang), jnp.sin(ang)
    return jnp.concatenate([cos, cos], axis=-1), jnp.concatenate([-sin, sin], axis=-1)


def kernel(x_prompt, x_sample, mem_prompt, state_ret, state_hgrn, cache_mem_k, cache_mem_v, w_in, w_up_ret, w_up_hgrn, w_up_xattn, w_out, w_mem_kv, ret_norm_g, hgrn_norm_g, lb_logits, ln1_g, ln1_b, ln2_g, ln2_b, w_router, b_router, w_e_gate, w_e_up, w_e_down, w_s_gate, w_s_up, w_s_down):
    b, t, d = x_prompt.shape
    nb, ts, _ = x_sample.shape
    n_mem = mem_prompt.shape[1]
    assert d == D_MODEL and t % RET_CHUNK == 0 and nb % SAMPLE_BB == 0
    assert ts & (ts - 1) == 0 and HG_CHUNK % ts == 0 and RET_CHUNK % ts == 0
    n_p, n_s = b * t, nb * ts
    nt = n_p + n_s
    assert n_p % (SAMPLE_BB * ts) == 0

    lb_cum = jnp.cumsum(jax.nn.softmax(lb_logits.astype(F32), axis=0), axis=0)
    lbs = lb_cum - lb_cum[0:1]
    lbt = jnp.stack([jnp.log(lbs), jnp.log1p(-lbs), 1.0 - lbs] + [jnp.zeros_like(lbs)] * (SUBLANES - 3), axis=1)
    gl = jnp.broadcast_to(jnp.log1p(-jnp.exp2(-5.0 - jnp.arange(HEADS, dtype=F32)))[:, None], (HEADS, DH))
    cos_p, sin_p = _rope_tables(t, 0)
    cos_s, sin_s = _rope_tables(ts, PAST_LEN)
    cos_s, sin_s = jnp.tile(cos_s, (SAMPLE_BB, 1)), jnp.tile(sin_s, (SAMPLE_BB, 1))
    vec3 = lambda a: a.reshape(DEPTH, 1, -1)
    w_router_t = jnp.swapaxes(w_router, 1, 2)
    b_router3 = b_router.reshape(DEPTH, N_EXPERTS, 1)
    mem2 = mem_prompt.reshape(b * n_mem, d)

    x = (x_prompt.reshape(n_p, d), x_sample.reshape(n_s, d))
    xb = jnp.concatenate([x[0].astype(BF16), x[1].astype(BF16)], axis=0)
    tm_proj = _pick(nt, (1024, 512, 128))
    te = _pick(nt * TOP_K, EXPERT_TILES)
    outs = {k: [] for k in ("ret_p", "hg_p", "mk", "mv")}
    ret_s = hg_s = None
    for l in range(DEPTH):
        proj = _matmul(xb, w_in, l, tm_proj, 1280, out_dtype=BF16)
        forget = _matmul(xb, w_in, l, tm_proj, HW, first_col_block=COL_HG_F, n=HW)
        kv_p = _matmul(mem2, w_mem_kv, l, _pick(b * n_mem, (1024, 512, 256)), 2 * HW)
        yr, ret_p, ret_s = _retention(proj, state_ret, l, cos_p, sin_p, cos_s, sin_s, gl,
                                      vec3(ret_norm_g), b, t, nb, ts, ret_s)
        yh, hg_p, hg_s = _hgrn(proj, forget, state_hgrn, l, lbt, vec3(hgrn_norm_g), b, t, nb, ts, hg_s)
        yx = _cross_attention(proj, kv_p, cache_mem_k, cache_mem_v, l, b, t, nb, ts)
        x1, x1b, x1t = _merge((*yr, *yh, *yx), proj, x, w_up_ret, w_up_hgrn, w_up_xattn, w_out,
                              vec3(ln1_g), vec3(ln1_b), l)
        eidx, wn = _router(x1, w_router_t, b_router3, l)
        pos, cnt, off = _positions(eidx)
        tbl, n_steps = _step_table(cnt, off, nt * TOP_K, te)
        xs = _dispatch(pos, x1t.reshape(nt, PACKED_TILES, LANES))
        ye = _experts(tbl, n_steps, te, xs.reshape(-1, LANES), w_e_gate, w_e_up, w_e_down, l)
        ye = ye.reshape(-1, ROW_TILES, LANES)
        x_p, x_s, xb = _combine(pos, wn, ye, x1, x1b, w_s_gate, w_s_up, w_s_down,
                                vec3(ln2_g), vec3(ln2_b), l, n_p)
        x = (x_p, x_s)
        outs["ret_p"].append(ret_p)
        outs["hg_p"].append(hg_p)
        outs["mk"].append(kv_p[:, :HW].reshape(b, n_mem, HEADS, DH))
        outs["mv"].append(kv_p[:, HW:].reshape(b, n_mem, HEADS, DH))
    return (x[0].reshape(b, t, d), x[1].reshape(nb, ts, d),
            jnp.stack(outs["ret_p"]), jnp.stack(outs["hg_p"]), jnp.stack(outs["mk"]), jnp.stack(outs["mv"]),
            ret_s, hg_s)
```

```python
import functools

import jax
import jax.numpy as jnp
from jax import lax
from jax.experimental import pallas as pl
from jax.experimental.pallas import tpu as pltpu

F32 = jnp.float32
BF16 = jnp.bfloat16
I32 = jnp.int32
HIGHEST = lax.Precision.HIGHEST

D_MODEL = 1024
DEPTH = 2
PAST_LEN = 16384
HEADS = 4
DH = 128
HW = HEADS * DH
RET_CHUNK = 128
HG_CHUNK = 16
ROPE_BASE = 10000.0
N_EXPERTS = 64
N_GROUPS = 8
GROUP_SIZE = N_EXPERTS // N_GROUPS
TOPK_GROUPS = 4
TOP_K = 6
D_EXPERT = 256
ROUTED_SCALE = 2.5
LN_EPS = 1e-5
DN_ALPHA = (2 * DEPTH) ** 0.25
N_IN = 9 * HW + 3 * D_MODEL
COL_RET_Q, COL_RET_K, COL_RET_V, COL_RET_G = 0, 1, 2, 3
COL_HG_Q, COL_HG_F, COL_HG_I, COL_HG_G = 4, 5, 6, 7
COL_XA_Q = 8
COL_GATES = 9
LANES = 128
SUBLANES = 8
ROW_TILES = D_MODEL // LANES
PACKED_TILES = ROW_TILES // 2
U32 = jnp.uint32
SAMPLE_BB = 8
EXPERT_TILES = (512, 256)
ISSUE_UNROLL = 8
PROJ_TILE_N = 1536
VMEM_LIMIT = 56 * 1024 * 1024


def _params(sem):
    return pltpu.CompilerParams(dimension_semantics=sem, vmem_limit_bytes=VMEM_LIMIT)


def _bdot(a, b):
    return jnp.dot(a.astype(BF16), b.astype(BF16), preferred_element_type=F32)


def _bdot_nt(a, b):
    return lax.dot_general(a.astype(BF16), b.astype(BF16), (((1,), (1,)), ((), ())),
                           preferred_element_type=F32)


def _bdot_tn(a, b):
    return lax.dot_general(a.astype(BF16), b.astype(BF16), (((0,), (0,)), ((), ())),
                           preferred_element_type=F32)


def _silu(x):
    return x * jax.nn.sigmoid(x)


def _pick(n, prefs):
    for p in prefs:
        if n % p == 0:
            return p
    raise ValueError(f"no tile for {n}")


def _mm_body(x_ref, w_ref, o_ref, wb_ref, *, gate):
    @pl.when(pl.program_id(1) == 0)
    def _():
        wb_ref[...] = w_ref[...].astype(BF16)

    acc = jnp.dot(x_ref[...].astype(BF16), wb_ref[...], preferred_element_type=F32)
    o_ref[...] = (jax.nn.sigmoid(acc) if gate else acc).astype(o_ref.dtype)


def _matmul(x, w, layer, tm, tn, first_tile=0, n=None, gate=False):
    m, k = x.shape
    n = w.shape[2] if n is None else n
    return pl.pallas_call(
        functools.partial(_mm_body, gate=gate),
        grid=(n // tn, m // tm),
        in_specs=[pl.BlockSpec((tm, k), lambda j, i: (i, 0)),
                  pl.BlockSpec((None, k, tn), lambda j, i: (layer, 0, first_tile + j))],
        out_specs=pl.BlockSpec((tm, tn), lambda j, i: (i, j)),
        out_shape=jax.ShapeDtypeStruct((m, n), BF16 if gate else F32),
        scratch_shapes=[pltpu.VMEM((k, tn), BF16)],
        compiler_params=_params(("arbitrary", "arbitrary")),
        name="dense_matmul",
    )(x, w)


def _rotary(x, cos, sin_signed):
    return x * cos + pltpu.roll(x, DH // 2, 1) * sin_signed


def _group_norm_gate(o, gain, gate):
    mu = jnp.mean(o, axis=-1, keepdims=True)
    var = jnp.mean(jnp.square(o - mu), axis=-1, keepdims=True)
    return (o - mu) * lax.rsqrt(var + LN_EPS) * gain * _silu(gate)


def _ret_prompt_body(q_ref, k_ref, v_ref, g_ref, cos_ref, sin_ref, gl_ref, gain_ref,
                     y_ref, st_ref, s_scr, intra_scr, qdec_scr, kdec_scr):
    c = pl.program_id(1)
    ch = RET_CHUNK

    @pl.when((pl.program_id(0) == 0) & (c == 0))
    def _():
        ri = lax.broadcasted_iota(I32, (ch, ch), 0)
        ci = lax.broadcasted_iota(I32, (ch, ch), 1)
        rel = (ri - ci).astype(F32)
        idx = lax.broadcasted_iota(I32, (ch, DH), 0).astype(F32)
        for h in range(HEADS):
            gl = gl_ref[h:h + 1, :]
            intra_scr[h] = jnp.where(rel >= 0, jnp.exp(gl * rel), 0.0)
            qdec_scr[h] = jnp.exp(gl * (idx + 1.0))
            kdec_scr[h] = jnp.exp(gl * (ch - 1.0 - idx))

    @pl.when(c == 0)
    def _():
        s_scr[...] = jnp.zeros_like(s_scr)

    cos = cos_ref[...]
    sin = sin_ref[...]
    for h in range(HEADS):
        sl = slice(h * DH, (h + 1) * DH)
        gl = gl_ref[h:h + 1, :]
        qr = _rotary(q_ref[:, sl], cos, sin)
        kr = _rotary(k_ref[:, sl], cos, sin) * (DH ** -0.5)
        v = v_ref[:, sl]
        att = _bdot_nt(qr, kr) * intra_scr[h]
        s = s_scr[h]
        o = _bdot(att, v) + _bdot(qr, s) * qdec_scr[h]
        s_scr[h] = s * jnp.exp(gl * float(ch)) + _bdot_tn(kr * kdec_scr[h], v)
        y_ref[:, sl] = _group_norm_gate(o, gain_ref[:, sl], g_ref[:, sl]).astype(BF16)

    @pl.when(c == pl.num_programs(1) - 1)
    def _():
        st_ref[0] = s_scr[...]


def _ret_sample_body(q_ref, k_ref, v_ref, g_ref, cos_ref, sin_ref, gl_ref, gain_ref, sin_ref_state,
                     y_ref, st_ref, *, ts):
    rows = SAMPLE_BB * ts
    shift = ts.bit_length() - 1
    cos = cos_ref[...]
    sin = sin_ref[...]
    ri = lax.broadcasted_iota(I32, (rows, rows), 0)
    ci = lax.broadcasted_iota(I32, (rows, rows), 1)
    rel = (ri - ci).astype(F32)
    mask = ((ri >> shift) == (ci >> shift)) & (ri >= ci)
    idx = (lax.broadcasted_iota(I32, (rows, DH), 0) & (ts - 1)).astype(F32)
    for h in range(HEADS):
        sl = slice(h * DH, (h + 1) * DH)
        gl = gl_ref[h:h + 1, :]
        qr = _rotary(q_ref[:, sl], cos, sin)
        kr = _rotary(k_ref[:, sl], cos, sin) * (DH ** -0.5)
        v = v_ref[:, sl]
        intra = jnp.where(mask, jnp.exp(gl[:, :rows] * rel), 0.0)
        o_intra = _bdot(_bdot_nt(qr, kr) * intra, v)
        q_dec = jnp.exp(gl * (idx + 1.0))
        kd = kr * jnp.exp(gl * (ts - 1.0 - idx))
        c_dec = jnp.exp(gl * float(ts))
        outs = []
        for j in range(SAMPLE_BB):
            rs = slice(j * ts, (j + 1) * ts)
            s = sin_ref_state[j, h]
            outs.append(o_intra[rs] + _bdot(qr[rs], s) * q_dec[rs])
            new_state = s * c_dec + _bdot_tn(kd[rs], v[rs])
            for slot in range(st_ref.shape[0]):
                st_ref[slot, j, h] = new_state
        o = jnp.concatenate(outs, axis=0)
        y_ref[:, sl] = _group_norm_gate(o, gain_ref[:, sl], g_ref[:, sl]).astype(BF16)


def _proj_spec(rows, col, row_map):
    return pl.BlockSpec((rows, HW), lambda *a: (row_map(*a), col))


def _sample_state_call(body, grid, in_specs, args, y_shape, y_spec, layer, nb, prev, name):
    st_shape = jax.ShapeDtypeStruct((DEPTH, nb, HEADS, DH, DH), F32)
    slots = DEPTH if prev is None else 1
    st_spec = pl.BlockSpec((slots, SAMPLE_BB, HEADS, DH, DH), lambda i: (layer, i, 0, 0, 0))
    aliases = {}
    if prev is not None:
        n_in = len(args)
        inner = body
        body = lambda *refs: inner(*refs[:n_in], *refs[n_in + 1:])
        in_specs = in_specs + [pl.BlockSpec(memory_space=pl.ANY)]
        args = args + (prev,)
        aliases = {n_in: 1}
    return pl.pallas_call(
        body, grid=grid, in_specs=in_specs, out_specs=[y_spec, st_spec], out_shape=[y_shape, st_shape],
        input_output_aliases=aliases, compiler_params=_params(("arbitrary",)), name=name,
    )(*args)


def _retention(proj, state, layer, cos_p, sin_p, cos_s, sin_s, gl, gain, b, t, nb, ts, prev_s):
    n_p = b * t
    nc = t // RET_CHUNK
    prow = lambda bi, c: bi * nc + c
    const2 = lambda *a: (0, 0)
    y_p, st_p = pl.pallas_call(
        _ret_prompt_body,
        grid=(b, nc),
        in_specs=[_proj_spec(RET_CHUNK, COL_RET_Q, prow), _proj_spec(RET_CHUNK, COL_RET_K, prow),
                  _proj_spec(RET_CHUNK, COL_RET_V, prow), _proj_spec(RET_CHUNK, COL_RET_G, prow),
                  pl.BlockSpec((RET_CHUNK, DH), lambda bi, c: (c, 0)),
                  pl.BlockSpec((RET_CHUNK, DH), lambda bi, c: (c, 0)),
                  pl.BlockSpec((HEADS, DH), const2),
                  pl.BlockSpec((None, 1, HW), lambda bi, c: (layer, 0, 0))],
        out_specs=[pl.BlockSpec((RET_CHUNK, HW), lambda bi, c: (prow(bi, c), 0)),
                   pl.BlockSpec((1, HEADS, DH, DH), lambda bi, c: (bi, 0, 0, 0))],
        out_shape=[jax.ShapeDtypeStruct((n_p, HW), BF16),
                   jax.ShapeDtypeStruct((b, HEADS, DH, DH), F32)],
        scratch_shapes=[pltpu.VMEM((HEADS, DH, DH), F32), pltpu.VMEM((HEADS, RET_CHUNK, RET_CHUNK), F32),
                        pltpu.VMEM((HEADS, RET_CHUNK, DH), F32), pltpu.VMEM((HEADS, RET_CHUNK, DH), F32)],
        compiler_params=_params(("arbitrary", "arbitrary")),
        name="retention_prompt",
    )(proj, proj, proj, proj, cos_p, sin_p, gl, gain)

    rows = SAMPLE_BB * ts
    base = n_p // rows
    srow = lambda i: base + i
    y_s, st_s = _sample_state_call(
        functools.partial(_ret_sample_body, ts=ts), (nb // SAMPLE_BB,),
        [_proj_spec(rows, COL_RET_Q, srow), _proj_spec(rows, COL_RET_K, srow),
         _proj_spec(rows, COL_RET_V, srow), _proj_spec(rows, COL_RET_G, srow),
         pl.BlockSpec((rows, DH), const2), pl.BlockSpec((rows, DH), const2),
         pl.BlockSpec((HEADS, DH), const2),
         pl.BlockSpec((None, 1, HW), lambda i: (layer, 0, 0)),
         pl.BlockSpec((None, SAMPLE_BB, HEADS, DH, DH), lambda i: (layer, i, 0, 0, 0))],
        (proj, proj, proj, proj, cos_s, sin_s, gl, gain, state),
        jax.ShapeDtypeStruct((nb * ts, HW), BF16), pl.BlockSpec((rows, HW), lambda i: (i, 0)),
        layer, nb, prev_s, "retention_sample")
    return (y_p, y_s), st_p, st_s


def _hg_prepare(hq_ref, hf_ref, lbt_ref, rows, chunk):
    shift = chunk.bit_length() - 1
    ri = lax.broadcasted_iota(I32, (rows, rows), 0)
    ci = lax.broadcasted_iota(I32, (rows, rows), 1)
    same = (ri >> shift) == (ci >> shift)
    causal = same & (ci <= ri)
    z = hf_ref[...]
    log_lb = lbt_ref[0:1, :]
    log_1m_lb = lbt_ref[1:2, :]
    one_m_lb = lbt_ref[2:3, :]
    log_sig = jnp.minimum(z, 0.0) - jnp.log1p(jnp.exp(-jnp.abs(z)))
    bterm = log_1m_lb + log_sig
    logf = jnp.maximum(log_lb, bterm) + jnp.log1p(jnp.exp(-jnp.abs(log_lb - bterm)))
    kh = one_m_lb * jax.nn.sigmoid(-z)
    qh = _silu(hq_ref[...]) * (DH ** -0.5)
    cum = jnp.dot(causal.astype(F32), logf, precision=HIGHEST, preferred_element_type=F32)
    tot = jnp.dot(same.astype(F32), logf, precision=HIGHEST, preferred_element_type=F32)
    qi = qh * jnp.exp(cum)
    ki = kh * jnp.exp(-cum)
    ke = kh * jnp.exp(tot - cum)
    return causal, qi, ki, ke, tot, logf


def _rms_norm_gate(o, gain, gate):
    return o * lax.rsqrt(jnp.mean(jnp.square(o), axis=-1, keepdims=True) + LN_EPS) * gain * _silu(gate)


def _hg_prompt_body(hq_ref, hf_ref, hi_ref, hg_ref, lbt_ref, gain_ref, y_ref, st_ref, s_scr):
    c = pl.program_id(1)

    @pl.when(c == 0)
    def _():
        s_scr[...] = jnp.zeros_like(s_scr)

    rows = RET_CHUNK
    n_sub = rows // HG_CHUNK
    shift = HG_CHUNK.bit_length() - 1
    causal, qi, ki, ke, tot, logf = _hg_prepare(hq_ref, hf_ref, lbt_ref, rows, HG_CHUNK)
    ri = lax.broadcasted_iota(I32, (rows, rows), 0)
    ci = lax.broadcasted_iota(I32, (rows, rows), 1)
    pre = jnp.dot(((ci >> shift) < (ri >> shift)).astype(F32), logf, precision=HIGHEST,
                  preferred_element_type=F32)
    sub = lax.broadcasted_iota(I32, (rows, DH), 0) >> shift
    v = hi_ref[...]
    for h in range(HEADS):
        sl = slice(h * DH, (h + 1) * DH)
        q_h, ke_h, v_h, pre_h = qi[:, sl], ke[:, sl], v[:, sl], pre[:, sl]
        att = jnp.where(causal, _bdot_nt(q_h, ki[:, sl]), 0.0)
        st0 = s_scr[h]
        o = _bdot(att, v_h) + _bdot_nt(q_h * jnp.exp(pre_h), st0)
        end_last = pre_h[rows - 1:rows] + tot[rows - 1:rows, sl]
        st = st0 * jnp.exp(end_last)
        for i in range(n_sub):
            rs = slice(i * HG_CHUNK, (i + 1) * HG_CHUNK)
            u_t = _bdot_tn(v_h[rs], ke_h[rs])
            if i + 1 < n_sub:
                end_i = pre_h[(i + 1) * HG_CHUNK:(i + 1) * HG_CHUNK + 1]
                later = q_h * jnp.exp(jnp.where(sub > i, pre_h - end_i, -jnp.inf))
                o = o + _bdot_nt(later, u_t)
                st = st + u_t * jnp.exp(end_last - end_i)
            else:
                st = st + u_t
        s_scr[h] = st
        y_ref[:, sl] = _rms_norm_gate(o, gain_ref[:, sl], hg_ref[:, sl]).astype(BF16)

    @pl.when(c == pl.num_programs(1) - 1)
    def _():
        for h in range(HEADS):
            st_ref[0, h] = s_scr[h].T


def _hg_sample_body(hq_ref, hf_ref, hi_ref, hg_ref, lbt_ref, gain_ref, sin_ref_state,
                    y_ref, st_ref, *, ts):
    rows = SAMPLE_BB * ts
    causal, qi, ki, ke, tot, _ = _hg_prepare(hq_ref, hf_ref, lbt_ref, rows, ts)
    etot = jnp.exp(tot)
    v = hi_ref[...]
    for h in range(HEADS):
        sl = slice(h * DH, (h + 1) * DH)
        att = jnp.where(causal, _bdot_nt(qi[:, sl], ki[:, sl]), 0.0)
        o_intra = _bdot(att, v[:, sl])
        outs = []
        for j in range(SAMPLE_BB):
            rs = slice(j * ts, (j + 1) * ts)
            s = sin_ref_state[j, h]
            outs.append(o_intra[rs] + _bdot(qi[rs, sl], s))
            scale = jnp.broadcast_to(etot[j * ts:j * ts + 1, sl], (DH, DH)).T
            new_state = s * scale + _bdot_tn(ke[rs, sl], v[rs, sl])
            for slot in range(st_ref.shape[0]):
                st_ref[slot, j, h] = new_state
        o = jnp.concatenate(outs, axis=0)
        y_ref[:, sl] = _rms_norm_gate(o, gain_ref[:, sl], hg_ref[:, sl]).astype(BF16)


def _hgrn(proj, state, layer, lbt, gain, b, t, nb, ts, prev_s):
    n_p = b * t
    nc = t // RET_CHUNK
    prow = lambda bi, c: bi * nc + c
    y_p, st_p = pl.pallas_call(
        _hg_prompt_body,
        grid=(b, nc),
        in_specs=[_proj_spec(RET_CHUNK, COL_HG_Q, prow), _proj_spec(RET_CHUNK, COL_HG_F, prow),
                  _proj_spec(RET_CHUNK, COL_HG_I, prow), _proj_spec(RET_CHUNK, COL_HG_G, prow),
                  pl.BlockSpec((None, SUBLANES, HW), lambda bi, c: (layer, 0, 0)),
                  pl.BlockSpec((None, 1, HW), lambda bi, c: (layer, 0, 0))],
        out_specs=[pl.BlockSpec((RET_CHUNK, HW), lambda bi, c: (prow(bi, c), 0)),
                   pl.BlockSpec((1, HEADS, DH, DH), lambda bi, c: (bi, 0, 0, 0))],
        out_shape=[jax.ShapeDtypeStruct((n_p, HW), BF16),
                   jax.ShapeDtypeStruct((b, HEADS, DH, DH), F32)],
        scratch_shapes=[pltpu.VMEM((HEADS, DH, DH), F32)],
        compiler_params=_params(("arbitrary", "arbitrary")),
        name="hgrn_prompt",
    )(proj, proj, proj, proj, lbt, gain)

    rows = SAMPLE_BB * ts
    base = n_p // rows
    srow = lambda i: base + i
    y_s, st_s = _sample_state_call(
        functools.partial(_hg_sample_body, ts=ts), (nb // SAMPLE_BB,),
        [_proj_spec(rows, COL_HG_Q, srow), _proj_spec(rows, COL_HG_F, srow),
         _proj_spec(rows, COL_HG_I, srow), _proj_spec(rows, COL_HG_G, srow),
         pl.BlockSpec((None, SUBLANES, HW), lambda i: (layer, 0, 0)),
         pl.BlockSpec((None, 1, HW), lambda i: (layer, 0, 0)),
         pl.BlockSpec((None, SAMPLE_BB, HEADS, DH, DH), lambda i: (layer, i, 0, 0, 0))],
        (proj, proj, proj, proj, lbt, gain, state),
        jax.ShapeDtypeStruct((nb * ts, HW), BF16), pl.BlockSpec((rows, HW), lambda i: (i, 0)),
        layer, nb, prev_s, "hgrn_sample")
    return (y_p, y_s), st_p, st_s


def _softmax_rows(s):
    e = jnp.exp(s - jnp.max(s, axis=-1, keepdims=True))
    return e / jnp.sum(e, axis=-1, keepdims=True)


def _xa_prompt_body(q_ref, k_ref, v_ref, y_ref):
    for h in range(HEADS):
        sl = slice(h * DH, (h + 1) * DH)
        a = _softmax_rows(_bdot_nt(q_ref[:, sl] * (DH ** -0.5), k_ref[:, sl]))
        y_ref[:, sl] = _bdot(a, v_ref[:, sl]).astype(BF16)


def _xa_sample_body(q_ref, k_ref, v_ref, y_ref, *, ts):
    n_mem = k_ref.shape[1] // HEADS
    pairs = [(j, h) for j in range(SAMPLE_BB) for h in range(HEADS)]
    q = q_ref[...] * (DH ** -0.5)
    scores = [_bdot_nt(q[j * ts:(j + 1) * ts, h * DH:(h + 1) * DH], k_ref[j, pl.ds(h, n_mem, stride=HEADS), :])
              for j, h in pairs]
    a = _softmax_rows(jnp.concatenate(scores, axis=0))
    for n, (j, h) in enumerate(pairs):
        y = _bdot(a[n * ts:(n + 1) * ts], v_ref[j, pl.ds(h, n_mem, stride=HEADS), :])
        y_ref[j * ts:(j + 1) * ts, h * DH:(h + 1) * DH] = y.astype(BF16)


def _cross_attention(proj, kv_p, cache_k, cache_v, layer, b, t, nb, ts):
    n_p = b * t
    n_mem = kv_p.shape[0] // b
    tq = _pick(t, (512, 256, 128))
    nq = t // tq
    y_p = pl.pallas_call(
        _xa_prompt_body,
        grid=(b, nq),
        in_specs=[_proj_spec(tq, COL_XA_Q, lambda bi, qi: bi * nq + qi),
                  pl.BlockSpec((n_mem, HW), lambda bi, qi: (bi, 0)),
                  pl.BlockSpec((n_mem, HW), lambda bi, qi: (bi, 1))],
        out_specs=pl.BlockSpec((tq, HW), lambda bi, qi: (bi * nq + qi, 0)),
        out_shape=jax.ShapeDtypeStruct((n_p, HW), BF16),
        compiler_params=_params(("arbitrary", "arbitrary")),
        name="xattn_prompt",
    )(proj, kv_p, kv_p)

    rows = SAMPLE_BB * ts
    base = n_p // rows
    cache_k = cache_k.reshape(DEPTH, nb, n_mem * HEADS, DH)
    cache_v = cache_v.reshape(DEPTH, nb, n_mem * HEADS, DH)
    kv_spec = pl.BlockSpec((None, SAMPLE_BB, n_mem * HEADS, DH), lambda i: (layer, i, 0, 0))
    y_s = pl.pallas_call(
        functools.partial(_xa_sample_body, ts=ts),
        grid=(nb // SAMPLE_BB,),
        in_specs=[_proj_spec(rows, COL_XA_Q, lambda i: base + i), kv_spec, kv_spec],
        out_specs=pl.BlockSpec((rows, HW), lambda i: (i, 0)),
        out_shape=jax.ShapeDtypeStruct((nb * ts, HW), BF16),
        compiler_params=_params(("arbitrary",)),
        name="xattn_sample",
    )(proj, cache_k, cache_v)
    return (y_p, y_s)


def _layer_norm(tv, g, b):
    mu = jnp.mean(tv, axis=-1, keepdims=True)
    var = jnp.mean(jnp.square(tv - mu), axis=-1, keepdims=True)
    return (tv - mu) * lax.rsqrt(var + LN_EPS) * g + b


def _merge_body(yrp_ref, yrs_ref, yhp_ref, yhs_ref, yxp_ref, yxs_ref, g0_ref, g1_ref, g2_ref, xp_ref, xs_ref,
                wr_ref, wh_ref, wx_ref, wo_ref, lg_ref, lb_ref,
                x1_ref, x1b_ref, x1t_ref, wr_s, wh_s, wx_s, wo_s, *, prompt_tiles):
    @pl.when(pl.program_id(0) == 0)
    def _():
        wr_s[...] = wr_ref[...].astype(BF16)
        wh_s[...] = wh_ref[...].astype(BF16)
        wx_s[...] = wx_ref[...].astype(BF16)
        wo_s[...] = wo_ref[...].astype(BF16)

    is_prompt = pl.program_id(0) < prompt_tiles

    def branch(yp_ref, ys_ref, w_s, gate_ref):
        y = jnp.where(is_prompt, yp_ref[...], ys_ref[...])
        return gate_ref[...].astype(F32) * jnp.dot(y, w_s[...], preferred_element_type=F32)

    m = (branch(yrp_ref, yrs_ref, wr_s, g0_ref) + branch(yhp_ref, yhs_ref, wh_s, g1_ref)
         + branch(yxp_ref, yxs_ref, wx_s, g2_ref))
    hmix = jnp.dot(m.astype(BF16), wo_s[...], preferred_element_type=F32)
    x = jnp.where(is_prompt, xp_ref[...], xs_ref[...])
    x1 = _layer_norm(DN_ALPHA * x + hmix, lg_ref[...], lb_ref[...])
    x1_ref[...] = x1
    x1b_ref[...] = x1.astype(BF16)
    tm = x1.shape[0]
    bits = lax.bitcast_convert_type(x1.astype(BF16).astype(F32), U32)
    half = D_MODEL // 2
    packed = (bits[:, :half] >> 16) | (bits[:, half:] & jnp.uint32(0xFFFF0000))
    for s in range(PACKED_TILES):
        x1t_ref[pl.ds(s, tm, stride=PACKED_TILES), :] = packed[:, s * LANES:(s + 1) * LANES]


def _merge(ys, gates, x, w_up_ret, w_up_hgrn, w_up_xattn, w_out, ln_g, ln_b, layer):
    n_p = ys[0].shape[0]
    nt = n_p + ys[1].shape[0]
    tm = _pick(nt, (256, 128))
    assert n_p % tm == 0 and ys[1].shape[0] % tm == 0
    p_tiles = n_p // tm
    row = lambda i: (i, 0)
    p_map = lambda i: (jnp.minimum(i, p_tiles - 1), 0)
    s_map = lambda i: (jnp.maximum(i - p_tiles, 0), 0)
    y_specs = [pl.BlockSpec((tm, HW), p_map), pl.BlockSpec((tm, HW), s_map)] * 3
    wspec = lambda k: pl.BlockSpec((None, k, D_MODEL), lambda i: (layer, 0, 0))
    vec = pl.BlockSpec((None, 1, D_MODEL), lambda i: (layer, 0, 0))
    gate_specs = [pl.BlockSpec((tm, D_MODEL), lambda i, c=c: (i, c)) for c in range(3)]
    return pl.pallas_call(
        functools.partial(_merge_body, prompt_tiles=p_tiles),
        grid=(nt // tm,),
        in_specs=y_specs + gate_specs + [pl.BlockSpec((tm, D_MODEL), p_map), pl.BlockSpec((tm, D_MODEL), s_map),
                  wspec(HW), wspec(HW), wspec(HW), wspec(D_MODEL), vec, vec],
        out_specs=[pl.BlockSpec((tm, D_MODEL), row), pl.BlockSpec((tm, D_MODEL), row),
                   pl.BlockSpec((tm * PACKED_TILES, LANES), row)],
        out_shape=[jax.ShapeDtypeStruct((nt, D_MODEL), F32),
                   jax.ShapeDtypeStruct((nt, D_MODEL), BF16),
                   jax.ShapeDtypeStruct((nt * PACKED_TILES, LANES), U32)],
        scratch_shapes=[pltpu.VMEM((HW, D_MODEL), BF16)] * 3 + [pltpu.VMEM((D_MODEL, D_MODEL), BF16)],
        compiler_params=_params(("arbitrary",)),
        name="merge_out_ln1",
    )(*ys, *([gates] * 3), *x, w_up_ret, w_up_hgrn, w_up_xattn, w_out, ln_g, ln_b)


def _router_body(x_ref, wt_ref, b_ref, eidx_ref, wn_ref):
    tm = x_ref.shape[0]
    x = x_ref[...]
    w = wt_ref[...]
    xh = x.astype(BF16)
    xl = (x - xh.astype(F32)).astype(BF16)
    wh = w.astype(BF16)
    wl = (w - wh.astype(F32)).astype(BF16)
    logits = _bdot_nt(wh, xh) + (_bdot_nt(wh, xl) + _bdot_nt(wl, xh))
    s = jax.nn.sigmoid(logits)
    sel = s + b_ref[...]
    neg = -jnp.inf
    groups = [sel[g * GROUP_SIZE:(g + 1) * GROUP_SIZE, :] for g in range(N_GROUPS)]
    ie = lax.broadcasted_iota(I32, (GROUP_SIZE, tm), 0).astype(F32)
    rows = []
    for blk in groups:
        m1 = jnp.max(blk, axis=0, keepdims=True)
        first = jnp.min(jnp.where(blk == m1, ie, float(GROUP_SIZE)), axis=0, keepdims=True)
        rows.append(m1 + jnp.max(jnp.where(ie == first, neg, blk), axis=0, keepdims=True))
    gscore = jnp.concatenate(rows, axis=0)
    ig = lax.broadcasted_iota(I32, gscore.shape, 0).astype(F32)
    gmask = jnp.zeros(gscore.shape, F32)
    for _ in range(TOPK_GROUPS):
        m = jnp.max(gscore, axis=0, keepdims=True)
        gi = jnp.min(jnp.where(gscore == m, ig, float(N_GROUPS)), axis=0, keepdims=True)
        hit = ig == gi
        gmask = jnp.where(hit, 1.0, gmask)
        gscore = jnp.where(hit, neg, gscore)
    masked = jnp.concatenate([jnp.where(gmask[g:g + 1, :] > 0.5, blk, neg)
                              for g, blk in enumerate(groups)], axis=0)
    ix = lax.broadcasted_iota(I32, masked.shape, 0).astype(F32)
    idxs, ws = [], []
    for _ in range(TOP_K):
        m = jnp.max(masked, axis=0, keepdims=True)
        ei = jnp.min(jnp.where(masked == m, ix, float(N_EXPERTS)), axis=0, keepdims=True)
        hit = ix == ei
        idxs.append(ei)
        ws.append(jnp.sum(jnp.where(hit, s, 0.0), axis=0, keepdims=True))
        masked = jnp.where(hit, neg, masked)
    wsum = ws[0]
    for w in ws[1:]:
        wsum = wsum + w
    pad = [jnp.zeros((1, tm), F32)] * (SUBLANES - TOP_K)
    eidx_ref[...] = jnp.concatenate(idxs + pad, axis=0).astype(I32)
    wn_ref[...] = jnp.concatenate([w / wsum * ROUTED_SCALE for w in ws] + pad, axis=0)


def _router(x1, w_router_t, b_router, layer):
    nt = x1.shape[0]
    tm = _pick(nt, (512, 256, 128))
    return pl.pallas_call(
        _router_body,
        grid=(nt // tm,),
        in_specs=[pl.BlockSpec((tm, D_MODEL), lambda i: (i, 0)),
                  pl.BlockSpec((None, N_EXPERTS, D_MODEL), lambda i: (layer, 0, 0)),
                  pl.BlockSpec((None, N_EXPERTS, 1), lambda i: (layer, 0, 0))],
        out_specs=[pl.BlockSpec((SUBLANES, tm), lambda i: (0, i))] * 2,
        out_shape=[jax.ShapeDtypeStruct((SUBLANES, nt), I32),
                   jax.ShapeDtypeStruct((SUBLANES, nt), F32)],
        compiler_params=_params(("arbitrary",)),
        name="moe_router",
    )(x1, w_router_t, b_router)


def _positions_body(eidx_ref, pos_ref, cnt_ref, off_ref, base_scr, off_scr):
    phase = pl.program_id(0)
    i = pl.program_id(1)
    tp = eidx_ref.shape[1]
    ix = lax.broadcasted_iota(I32, (N_EXPERTS, tp), 0)
    eidx = eidx_ref[...]
    member = jnp.zeros((N_EXPERTS, tp), F32)
    for k in range(TOP_K):
        member = member + (ix == eidx[k:k + 1, :]).astype(F32)
    tile_cnt = jnp.sum(member, axis=1, keepdims=True)

    @pl.when((phase == 0) & (i == 0))
    def _():
        base_scr[...] = jnp.zeros_like(base_scr)

    @pl.when((phase == 1) & (i == 0))
    def _():
        cnt = base_scr[...]
        er = lax.broadcasted_iota(I32, (N_EXPERTS, N_EXPERTS), 0)
        ec = lax.broadcasted_iota(I32, (N_EXPERTS, N_EXPERTS), 1)
        off = jnp.dot((ec < er).astype(F32), cnt, precision=HIGHEST, preferred_element_type=F32)
        off_scr[...] = off
        cnt_ref[...] = cnt
        off_ref[...] = off
        base_scr[...] = jnp.zeros_like(base_scr)

    @pl.when(phase == 1)
    def _():
        tr = lax.broadcasted_iota(I32, (tp, tp), 0)
        tc = lax.broadcasted_iota(I32, (tp, tp), 1)
        before = jnp.dot(member.astype(BF16), (tr < tc).astype(BF16), preferred_element_type=F32)
        where_to = before + (off_scr[...] + base_scr[...])[:, 0:1]
        rows = [jnp.sum(jnp.where(ix == eidx[k:k + 1, :], where_to, 0.0), axis=0, keepdims=True)
                for k in range(TOP_K)]
        rows += [jnp.zeros((1, tp), F32)] * (SUBLANES - TOP_K)
        pos_ref[...] = jnp.concatenate(rows, axis=0).astype(I32)

    base_scr[...] = base_scr[...] + tile_cnt


def _positions(eidx):
    nt = eidx.shape[1]
    tp = _pick(nt, (512, 256, 128))
    const = lambda p, i: (0, 0)
    return pl.pallas_call(
        _positions_body,
        grid=(2, nt // tp),
        in_specs=[pl.BlockSpec((SUBLANES, tp), lambda p, i: (0, i))],
        out_specs=[pl.BlockSpec((SUBLANES, tp), lambda p, i: (0, i * p)),
                   pl.BlockSpec((N_EXPERTS, LANES), const), pl.BlockSpec((N_EXPERTS, LANES), const)],
        out_shape=[jax.ShapeDtypeStruct((SUBLANES, nt), I32),
                   jax.ShapeDtypeStruct((N_EXPERTS, LANES), F32),
                   jax.ShapeDtypeStruct((N_EXPERTS, LANES), F32)],
        scratch_shapes=[pltpu.VMEM((N_EXPERTS, LANES), F32), pltpu.VMEM((N_EXPERTS, LANES), F32)],
        compiler_params=_params(("arbitrary", "arbitrary")),
        name="moe_positions",
    )(eidx)


T_TILE, T_EXPERT, T_LO, T_HI, T_FRESH, T_NEWEXP = range(6)

def _table_body(cnt_ref, off_ref, tbl_ref, *, tile_rows):
    te = float(tile_rows)
    n = tbl_ref.shape[1]
    cnt = cnt_ref[...]
    off = off_ref[...]
    first = jnp.floor(off * (1.0 / te))
    last = jnp.floor((off + cnt - 1.0) * (1.0 / te))
    nst = jnp.where(cnt > 0.0, last - first + 1.0, 0.0)
    er = lax.broadcasted_iota(I32, (N_EXPERTS, N_EXPERTS), 0)
    ec = lax.broadcasted_iota(I32, (N_EXPERTS, N_EXPERTS), 1)
    s_end = jnp.dot((ec <= er).astype(F32), nst, precision=HIGHEST, preferred_element_type=F32)
    s_beg = s_end - nst
    total = s_end[N_EXPERTS - 1:N_EXPERTS, 0:1]
    sidx = lax.broadcasted_iota(I32, (1, n), 1).astype(F32)
    s = jnp.minimum(sidx, total - 1.0)
    e_s = jnp.sum((s_end[:, 0:1] <= s).astype(F32), axis=0, keepdims=True)
    hot = lax.broadcasted_iota(I32, (N_EXPERTS, n), 0).astype(F32) == e_s

    def pick(col):
        return jnp.sum(jnp.where(hot, col[:, 0:1], 0.0), axis=0, keepdims=True)

    tile = pick(first) + s - pick(s_beg)
    valid = sidx < total
    o, c = pick(off), pick(cnt)
    lo = jnp.where(valid, jnp.maximum(o, tile * te), 0.0)
    hi = jnp.where(valid, jnp.minimum(o + c, (tile + 1.0) * te), 0.0)
    head = sidx == 0.0
    fresh = jnp.where((tile != pltpu.roll(tile, 1, 1)) | head, 1.0, 0.0)
    newexp = jnp.where((e_s != pltpu.roll(e_s, 1, 1)) | head, 1.0, 0.0)
    pad = [jnp.zeros((1, n), F32)] * (SUBLANES - 6)
    tbl_ref[...] = jnp.concatenate([tile, e_s, lo, hi, fresh, newexp] + pad, axis=0).astype(I32)


def _step_table(cnt, off, n_rows, te):
    n_steps = n_rows // te + N_EXPERTS
    width = -(-n_steps // LANES) * LANES
    tbl = pl.pallas_call(
        functools.partial(_table_body, tile_rows=te),
        out_shape=jax.ShapeDtypeStruct((SUBLANES, width), I32),
        name="moe_step_table",
    )(cnt, off)
    return tbl, n_steps


def _wait_tile_rows(like_src, dst_rows_ref, sem_ref):
    rows = like_src.shape[0]
    for _ in range(TOP_K):
        pltpu.make_async_copy(like_src, dst_rows_ref.at[pl.ds(0, rows)], sem_ref).wait()


def _dispatch_body(pos_ref, xt_ref, xs_ref, pos_s, sem_p, sem):
    td = pos_ref.shape[1]
    cp = pltpu.make_async_copy(pos_ref, pos_s, sem_p)
    cp.start()
    cp.wait()

    def issue(g, carry):
        for u in range(ISSUE_UNROLL):
            r = g * ISSUE_UNROLL + u
            for k in range(TOP_K):
                pltpu.make_async_copy(xt_ref.at[r], xs_ref.at[pos_s[k, r]], sem).start(priority=k % 2)
        return carry

    lax.fori_loop(0, td // ISSUE_UNROLL, issue, 0)
    _wait_tile_rows(xt_ref, xs_ref, sem)


def _dispatch(pos, x1t):
    nt = x1t.shape[0]
    td = _pick(nt, (512, 256, 128))
    return pl.pallas_call(
        _dispatch_body,
        grid=(nt // td,),
        in_specs=[pl.BlockSpec((SUBLANES, td), lambda i: (0, i)),
                  pl.BlockSpec((td,) + x1t.shape[1:], lambda i: (i, 0, 0))],
        out_specs=pl.BlockSpec(memory_space=pl.ANY),
        out_shape=jax.ShapeDtypeStruct((nt * TOP_K,) + x1t.shape[1:], x1t.dtype),
        scratch_shapes=[pltpu.SMEM((SUBLANES, td), I32), pltpu.SemaphoreType.DMA, pltpu.SemaphoreType.DMA],
        compiler_params=_params(("arbitrary",)),
        name="moe_dispatch",
    )(pos, x1t)


def _experts_body(tbl_ref, xs_ref, wg_ref, wu_ref, wd_ref, ye_ref, wg_s, wu_s, wd_s):
    s = pl.program_id(0)
    te = xs_ref.shape[0] // PACKED_TILES
    lo = tbl_ref[T_LO, s]
    hi = tbl_ref[T_HI, s]

    @pl.when(tbl_ref[T_NEWEXP, s] == 1)
    def _():
        wg_s[...] = wg_ref[...].astype(BF16)
        wu_s[...] = wu_ref[...].astype(BF16)
        wd_s[...] = wd_ref[...].astype(BF16)

    @pl.when(tbl_ref[T_FRESH, s] == 1)
    def _():
        ye_ref[...] = jnp.zeros_like(ye_ref)

    @pl.when(hi > lo)
    def _():
        words = [xs_ref[pl.ds(t, te, stride=PACKED_TILES), :] for t in range(PACKED_TILES)]
        low = [lax.bitcast_convert_type(w << 16, F32).astype(BF16) for w in words]
        high = [lax.bitcast_convert_type(w & jnp.uint32(0xFFFF0000), F32).astype(BF16) for w in words]
        x = jnp.concatenate(low + high, axis=-1)
        g = jnp.dot(x, wg_s[...], preferred_element_type=F32)
        u = jnp.dot(x, wu_s[...], preferred_element_type=F32)
        y = jnp.dot((_silu(g) * u).astype(BF16), wd_s[...], preferred_element_type=F32)
        row = tbl_ref[T_TILE, s] * te + lax.broadcasted_iota(I32, (te, LANES), 0)
        mine = (row >= lo) & (row < hi)
        for t in range(ROW_TILES):
            sl = pl.ds(t, te, stride=ROW_TILES)
            ye_ref[sl, :] = jnp.where(mine, y[:, t * LANES:(t + 1) * LANES], ye_ref[sl, :])


def _experts(tbl, n_steps, te, xs, w_gate, w_up, w_down, layer):
    n_rows = xs.shape[0] // PACKED_TILES
    tile_map = lambda s, tbl: (tbl[T_TILE, s], 0)
    w_map = lambda s, tbl: (layer, tbl[T_EXPERT, s], 0, 0)
    w_in_spec = pl.BlockSpec((None, None, D_MODEL, D_EXPERT), w_map)
    w_dn_spec = pl.BlockSpec((None, None, D_EXPERT, D_MODEL), w_map)
    return pl.pallas_call(
        _experts_body,
        grid_spec=pltpu.PrefetchScalarGridSpec(
            num_scalar_prefetch=1,
            grid=(n_steps,),
            in_specs=[pl.BlockSpec((te * PACKED_TILES, LANES), tile_map), w_in_spec, w_in_spec, w_dn_spec],
            out_specs=pl.BlockSpec((te * ROW_TILES, LANES), tile_map),
            scratch_shapes=[pltpu.VMEM((D_MODEL, D_EXPERT), BF16), pltpu.VMEM((D_MODEL, D_EXPERT), BF16),
                            pltpu.VMEM((D_EXPERT, D_MODEL), BF16)]),
        out_shape=jax.ShapeDtypeStruct((n_rows * ROW_TILES, LANES), F32),
        compiler_params=_params(("arbitrary",)),
        name="moe_experts",
    )(tbl, xs, w_gate, w_up, w_down)


def _combine_body(pos_ref, wn_ref, ye_ref, x1_ref, x1b_ref, wsg_ref, wsu_ref, wsd_ref, lg_ref, lb_ref,
                  x2p_ref, x2s_ref, x2b_ref, pos_s, buf, wsg_s, wsu_s, wsd_s, sem_p, sem, *, prompt_tiles):
    i = pl.program_id(0)
    n = pl.num_programs(0)
    tc = wn_ref.shape[1]
    slot = i % 2
    tile_rows = tc * ROW_TILES

    def request(tile, into):
        cp = pltpu.make_async_copy(pos_ref.at[tile], pos_s, sem_p)
        cp.start()
        cp.wait()

        def issue(g, carry):
            for u in range(ISSUE_UNROLL):
                r = g * ISSUE_UNROLL + u
                for k in range(TOP_K):
                    at = pl.multiple_of(((into * TOP_K + k) * tc + r) * ROW_TILES, ROW_TILES)
                    pltpu.make_async_copy(ye_ref.at[pos_s[k, r]], buf.at[pl.ds(at, ROW_TILES)],
                                          sem.at[into]).start(priority=k % 2)
            return carry

        lax.fori_loop(0, tc // ISSUE_UNROLL, issue, 0)

    @pl.when(i == 0)
    def _():
        wsg_s[...] = wsg_ref[...].astype(BF16)
        wsu_s[...] = wsu_ref[...].astype(BF16)
        wsd_s[...] = wsd_ref[...].astype(BF16)
        request(0, 0)

    @pl.when(i + 1 < n)
    def _():
        request(i + 1, 1 - slot)

    xb = x1b_ref[...]
    hs = _silu(jnp.dot(xb, wsg_s[...], preferred_element_type=F32)) * jnp.dot(xb, wsu_s[...], preferred_element_type=F32)
    shared = jnp.dot(hs.astype(BF16), wsd_s[...], preferred_element_type=F32)
    for k in range(TOP_K):
        pltpu.make_async_copy(buf.at[pl.ds(0, tile_rows)], buf.at[pl.ds(tile_rows, tile_rows)], sem.at[slot]).wait()

    w = wn_ref[...]
    acc = [None] * ROW_TILES
    for k in range(TOP_K):
        wcol = jnp.concatenate([jnp.broadcast_to(w[k:k + 1, c * LANES:(c + 1) * LANES], (LANES, LANES)).T
                                for c in range(tc // LANES)], axis=0)
        base = (slot * TOP_K + k) * tile_rows
        for t in range(ROW_TILES):
            term = wcol * buf[pl.ds(base + t, tc, stride=ROW_TILES), :]
            acc[t] = term if acc[t] is None else acc[t] + term
    routed = jnp.concatenate(acc, axis=-1)
    x2 = _layer_norm(DN_ALPHA * x1_ref[...] + (routed + shared), lg_ref[...], lb_ref[...])
    x2b_ref[...] = x2.astype(BF16)

    @pl.when(i < prompt_tiles)
    def _():
        x2p_ref[...] = x2

    @pl.when(i >= prompt_tiles)
    def _():
        x2s_ref[...] = x2


def _combine(pos, wn, ye, x1, x1b, w_s_gate, w_s_up, w_s_down, ln_g, ln_b, layer, n_p):
    nt = x1.shape[0]
    tc = _pick(nt, (256, 128))
    assert n_p % tc == 0
    n_tiles = nt // tc
    p_tiles = n_p // tc
    d_sh = w_s_gate.shape[2]
    pos3 = pos.reshape(SUBLANES, n_tiles, tc).transpose(1, 0, 2)
    row = lambda i: (i, 0)
    vec = pl.BlockSpec((None, 1, D_MODEL), lambda i: (layer, 0, 0))
    return pl.pallas_call(
        functools.partial(_combine_body, prompt_tiles=p_tiles),
        grid=(n_tiles,),
        in_specs=[pl.BlockSpec((n_tiles, SUBLANES, tc), lambda i: (0, 0, 0)),
                  pl.BlockSpec((SUBLANES, tc), lambda i: (0, i)),
                  pl.BlockSpec(memory_space=pl.ANY),
                  pl.BlockSpec((tc, D_MODEL), row), pl.BlockSpec((tc, D_MODEL), row),
                  pl.BlockSpec((None, D_MODEL, d_sh), lambda i: (layer, 0, 0)),
                  pl.BlockSpec((None, D_MODEL, d_sh), lambda i: (layer, 0, 0)),
                  pl.BlockSpec((None, d_sh, D_MODEL), lambda i: (layer, 0, 0)), vec, vec],
        out_specs=[pl.BlockSpec((tc, D_MODEL), lambda i: (jnp.minimum(i, p_tiles - 1), 0)),
                   pl.BlockSpec((tc, D_MODEL), lambda i: (jnp.maximum(i - p_tiles, 0), 0)),
                   pl.BlockSpec((tc, D_MODEL), row)],
        out_shape=[jax.ShapeDtypeStruct((n_p, D_MODEL), F32), jax.ShapeDtypeStruct((nt - n_p, D_MODEL), F32),
                   jax.ShapeDtypeStruct((nt, D_MODEL), BF16)],
        scratch_shapes=[pltpu.SMEM((SUBLANES, tc), I32),
                        pltpu.VMEM((2 * TOP_K * tc * ROW_TILES, LANES), F32),
                        pltpu.VMEM((D_MODEL, d_sh), BF16), pltpu.VMEM((D_MODEL, d_sh), BF16),
                        pltpu.VMEM((d_sh, D_MODEL), BF16),
                        pltpu.SemaphoreType.DMA, pltpu.SemaphoreType.DMA((2,))],
        compiler_params=_params(("arbitrary",)),
        name="moe_combine_ln2",
    )(pos3, wn, ye, x1, x1b, w_s_gate, w_s_up, w_s_down, ln_g, ln_b)


def _rope_tables(t, pos0):
    inv = 1.0 / (ROPE_BASE ** (jnp.arange(0, DH, 2, dtype=F32) / DH))
    ang = (jnp.arange(t, dtype=F32) + pos0)[:, None] * inv[None, :]
    cos, sin = jnp.cos(ang), jnp.sin(ang)
    return jnp.concatenate([cos, cos], axis=-1), jnp.concatenate([-sin, sin], axis=-1)


def kernel(x_prompt, x_sample, mem_prompt, state_ret, state_hgrn, cache_mem_k, cache_mem_v, w_in, w_up_ret, w_up_hgrn, w_up_xattn, w_out, w_mem_kv, ret_norm_g, hgrn_norm_g, lb_logits, ln1_g, ln1_b, ln2_g, ln2_b, w_router, b_router, w_e_gate, w_e_up, w_e_down, w_s_gate, w_s_up, w_s_down):
    b, t, d = x_prompt.shape
    nb, ts, _ = x_sample.shape
    n_mem = mem_prompt.shape[1]
    assert d == D_MODEL and t % RET_CHUNK == 0 and nb % SAMPLE_BB == 0
    assert ts & (ts - 1) == 0 and HG_CHUNK % ts == 0 and RET_CHUNK % ts == 0
    n_p, n_s = b * t, nb * ts
    nt = n_p + n_s
    assert n_p % (SAMPLE_BB * ts) == 0

    lb_cum = jnp.cumsum(jax.nn.softmax(lb_logits.astype(F32), axis=0), axis=0)
    lbs = lb_cum - lb_cum[0:1]
    lbt = jnp.stack([jnp.log(lbs), jnp.log1p(-lbs), 1.0 - lbs] + [jnp.zeros_like(lbs)] * (SUBLANES - 3), axis=1)
    gl = jnp.broadcast_to(jnp.log1p(-jnp.exp2(-5.0 - jnp.arange(HEADS, dtype=F32)))[:, None], (HEADS, DH))
    cos_p, sin_p = _rope_tables(t, 0)
    cos_s, sin_s = _rope_tables(ts, PAST_LEN)
    cos_s, sin_s = jnp.tile(cos_s, (SAMPLE_BB, 1)), jnp.tile(sin_s, (SAMPLE_BB, 1))
    vec3 = lambda a: a.reshape(DEPTH, 1, -1)
    w_router_t = jnp.swapaxes(w_router, 1, 2)
    b_router3 = b_router.reshape(DEPTH, N_EXPERTS, 1)
    mem2 = mem_prompt.reshape(b * n_mem, d)

    x = (x_prompt.reshape(n_p, d), x_sample.reshape(n_s, d))
    xb = jnp.concatenate([x[0].astype(BF16), x[1].astype(BF16)], axis=0)
    tm_proj = _pick(nt, (1024, 512, 128))
    te = _pick(nt * TOP_K, EXPERT_TILES)
    outs = {k: [] for k in ("ret_p", "hg_p", "mk", "mv")}
    ret_s = hg_s = None
    for l in range(DEPTH):
        mix_tiles = COL_GATES * HW // PROJ_TILE_N
        proj = _matmul(xb, w_in, l, tm_proj, PROJ_TILE_N, n=COL_GATES * HW)
        gates = _matmul(xb, w_in, l, tm_proj, PROJ_TILE_N, first_tile=mix_tiles, n=3 * D_MODEL, gate=True)
        kv_p = _matmul(mem2, w_mem_kv, l, _pick(b * n_mem, (1024, 512, 256)), 2 * HW)
        yr, ret_p, ret_s = _retention(proj, state_ret, l, cos_p, sin_p, cos_s, sin_s, gl,
                                      vec3(ret_norm_g), b, t, nb, ts, ret_s)
        yh, hg_p, hg_s = _hgrn(proj, state_hgrn, l, lbt, vec3(hgrn_norm_g), b, t, nb, ts, hg_s)
        yx = _cross_attention(proj, kv_p, cache_mem_k, cache_mem_v, l, b, t, nb, ts)
        x1, x1b, x1t = _merge((*yr, *yh, *yx), gates, x, w_up_ret, w_up_hgrn, w_up_xattn, w_out,
                              vec3(ln1_g), vec3(ln1_b), l)
        eidx, wn = _router(x1, w_router_t, b_router3, l)
        pos, cnt, off = _positions(eidx)
        tbl, n_steps = _step_table(cnt, off, nt * TOP_K, te)
        xs = _dispatch(pos, x1t.reshape(nt, PACKED_TILES, LANES))
        ye = _experts(tbl, n_steps, te, xs.reshape(-1, LANES), w_e_gate, w_e_up, w_e_down, l)
        ye = ye.reshape(-1, ROW_TILES, LANES)
        x_p, x_s, xb = _combine(pos, wn, ye, x1, x1b, w_s_gate, w_s_up, w_s_down,
                                vec3(ln2_g), vec3(ln2_b), l, n_p)
        x = (x_p, x_s)
        outs["ret_p"].append(ret_p)
        outs["hg_p"].append(hg_p)
        outs["mk"].append(kv_p[:, :HW].reshape(b, n_mem, HEADS, DH))
        outs["mv"].append(kv_p[:, HW:].reshape(b, n_mem, HEADS, DH))
    return (x[0].reshape(b, t, d), x[1].reshape(nb, ts, d),
            jnp.stack(outs["ret_p"]), jnp.stack(outs["hg_p"]), jnp.stack(outs["mk"]), jnp.stack(outs["mv"]),
            ret_s, hg_s)
```

```python
import functools

import jax
import jax.numpy as jnp
from jax import lax
from jax.experimental import pallas as pl
from jax.experimental.pallas import tpu as pltpu

F32 = jnp.float32
BF16 = jnp.bfloat16
I32 = jnp.int32
HIGHEST = lax.Precision.HIGHEST

D_MODEL = 1024
DEPTH = 2
PAST_LEN = 16384
HEADS = 4
DH = 128
HW = HEADS * DH
RET_CHUNK = 128
HG_CHUNK = 16
ROPE_BASE = 10000.0
N_EXPERTS = 64
N_GROUPS = 8
GROUP_SIZE = N_EXPERTS // N_GROUPS
TOPK_GROUPS = 4
TOP_K = 6
D_EXPERT = 256
ROUTED_SCALE = 2.5
LN_EPS = 1e-5
DN_ALPHA = (2 * DEPTH) ** 0.25
N_IN = 9 * HW + 3 * D_MODEL
COL_RET_Q, COL_RET_K, COL_RET_V, COL_RET_G = 0, 1, 2, 3
COL_HG_Q, COL_HG_F, COL_HG_I, COL_HG_G = 4, 5, 6, 7
COL_XA_Q = 8
COL_GATES = 9
LANES = 128
SUBLANES = 8
ROW_TILES = D_MODEL // LANES
PACKED_TILES = ROW_TILES // 2
U32 = jnp.uint32
SAMPLE_BB = 8
EXPERT_TILES = (512, 256)
ISSUE_UNROLL = 8
PROJ_TILE_N = 1536
VMEM_LIMIT = 56 * 1024 * 1024


def _params(sem):
    return pltpu.CompilerParams(dimension_semantics=sem, vmem_limit_bytes=VMEM_LIMIT)


def _bdot(a, b):
    return jnp.dot(a.astype(BF16), b.astype(BF16), preferred_element_type=F32)


def _bdot_nt(a, b):
    return lax.dot_general(a.astype(BF16), b.astype(BF16), (((1,), (1,)), ((), ())),
                           preferred_element_type=F32)


def _bdot_tn(a, b):
    return lax.dot_general(a.astype(BF16), b.astype(BF16), (((0,), (0,)), ((), ())),
                           preferred_element_type=F32)


def _silu(x):
    return x * jax.nn.sigmoid(x)


def _pick(n, prefs):
    for p in prefs:
        if n % p == 0:
            return p
    raise ValueError(f"no tile for {n}")


def _mm_body(x_ref, w_ref, o_ref, wb_ref, *, gate):
    @pl.when(pl.program_id(1) == 0)
    def _():
        wb_ref[...] = w_ref[...].astype(BF16)

    acc = jnp.dot(x_ref[...].astype(BF16), wb_ref[...], preferred_element_type=F32)
    o_ref[...] = (jax.nn.sigmoid(acc) if gate else acc).astype(o_ref.dtype)


def _matmul(x, w, layer, tm, tn, first_tile=0, n=None, gate=False):
    m, k = x.shape
    n = w.shape[2] if n is None else n
    return pl.pallas_call(
        functools.partial(_mm_body, gate=gate),
        grid=(n // tn, m // tm),
        in_specs=[pl.BlockSpec((tm, k), lambda j, i: (i, 0)),
                  pl.BlockSpec((None, k, tn), lambda j, i: (layer, 0, first_tile + j))],
        out_specs=pl.BlockSpec((tm, tn), lambda j, i: (i, j)),
        out_shape=jax.ShapeDtypeStruct((m, n), BF16 if gate else F32),
        scratch_shapes=[pltpu.VMEM((k, tn), BF16)],
        compiler_params=_params(("arbitrary", "arbitrary")),
        name="dense_matmul",
    )(x, w)


def _rotary(x, cos, sin_signed):
    return x * cos + pltpu.roll(x, DH // 2, 1) * sin_signed


def _group_norm_gate(o, gain, gate):
    mu = jnp.mean(o, axis=-1, keepdims=True)
    var = jnp.mean(jnp.square(o - mu), axis=-1, keepdims=True)
    return (o - mu) * lax.rsqrt(var + LN_EPS) * gain * _silu(gate)


def _ret_prompt_body(q_ref, k_ref, v_ref, g_ref, cos_ref, sin_ref, gl_ref, gain_ref,
                     y_ref, st_ref, s_scr, intra_scr, qdec_scr, kdec_scr):
    c = pl.program_id(1)
    ch = RET_CHUNK

    @pl.when((pl.program_id(0) == 0) & (c == 0))
    def _():
        ri = lax.broadcasted_iota(I32, (ch, ch), 0)
        ci = lax.broadcasted_iota(I32, (ch, ch), 1)
        rel = (ri - ci).astype(F32)
        idx = lax.broadcasted_iota(I32, (ch, DH), 0).astype(F32)
        for h in range(HEADS):
            gl = gl_ref[h:h + 1, :]
            intra_scr[h] = jnp.where(rel >= 0, jnp.exp(gl * rel), 0.0)
            qdec_scr[h] = jnp.exp(gl * (idx + 1.0))
            kdec_scr[h] = jnp.exp(gl * (ch - 1.0 - idx))

    @pl.when(c == 0)
    def _():
        s_scr[...] = jnp.zeros_like(s_scr)

    cos = cos_ref[...]
    sin = sin_ref[...]
    for h in range(HEADS):
        sl = slice(h * DH, (h + 1) * DH)
        gl = gl_ref[h:h + 1, :]
        qr = _rotary(q_ref[:, sl], cos, sin)
        kr = _rotary(k_ref[:, sl], cos, sin) * (DH ** -0.5)
        v = v_ref[:, sl]
        att = _bdot_nt(qr, kr) * intra_scr[h]
        s = s_scr[h]
        o = _bdot(att, v) + _bdot(qr, s) * qdec_scr[h]
        s_scr[h] = s * jnp.exp(gl * float(ch)) + _bdot_tn(kr * kdec_scr[h], v)
        y_ref[:, sl] = _group_norm_gate(o, gain_ref[:, sl], g_ref[:, sl]).astype(BF16)

    @pl.when(c == pl.num_programs(1) - 1)
    def _():
        st_ref[0] = s_scr[...]


def _ret_sample_body(q_ref, k_ref, v_ref, g_ref, cos_ref, sin_ref, gl_ref, gain_ref, sin_ref_state,
                     y_ref, st_ref, *, ts):
    rows = SAMPLE_BB * ts
    shift = ts.bit_length() - 1
    cos = cos_ref[...]
    sin = sin_ref[...]
    ri = lax.broadcasted_iota(I32, (rows, rows), 0)
    ci = lax.broadcasted_iota(I32, (rows, rows), 1)
    rel = (ri - ci).astype(F32)
    mask = ((ri >> shift) == (ci >> shift)) & (ri >= ci)
    idx = (lax.broadcasted_iota(I32, (rows, DH), 0) & (ts - 1)).astype(F32)
    for h in range(HEADS):
        sl = slice(h * DH, (h + 1) * DH)
        gl = gl_ref[h:h + 1, :]
        qr = _rotary(q_ref[:, sl], cos, sin)
        kr = _rotary(k_ref[:, sl], cos, sin) * (DH ** -0.5)
        v = v_ref[:, sl]
        intra = jnp.where(mask, jnp.exp(gl[:, :rows] * rel), 0.0)
        o_intra = _bdot(_bdot_nt(qr, kr) * intra, v)
        q_dec = jnp.exp(gl * (idx + 1.0))
        kd = kr * jnp.exp(gl * (ts - 1.0 - idx))
        c_dec = jnp.exp(gl * float(ts))
        outs = []
        for j in range(SAMPLE_BB):
            rs = slice(j * ts, (j + 1) * ts)
            s = sin_ref_state[j, h]
            outs.append(o_intra[rs] + _bdot(qr[rs], s) * q_dec[rs])
            new_state = s * c_dec + _bdot_tn(kd[rs], v[rs])
            for slot in range(st_ref.shape[0]):
                st_ref[slot, j, h] = new_state
        o = jnp.concatenate(outs, axis=0)
        y_ref[:, sl] = _group_norm_gate(o, gain_ref[:, sl], g_ref[:, sl]).astype(BF16)


def _proj_spec(rows, col, row_map):
    return pl.BlockSpec((rows, HW), lambda *a: (row_map(*a), col))


def _sample_state_call(body, grid, in_specs, args, y_shape, y_spec, layer, nb, prev, name):
    st_shape = jax.ShapeDtypeStruct((DEPTH, nb, HEADS, DH, DH), F32)
    slots = DEPTH if prev is None else 1
    st_spec = pl.BlockSpec((slots, SAMPLE_BB, HEADS, DH, DH), lambda i: (layer, i, 0, 0, 0))
    aliases = {}
    if prev is not None:
        n_in = len(args)
        inner = body
        body = lambda *refs: inner(*refs[:n_in], *refs[n_in + 1:])
        in_specs = in_specs + [pl.BlockSpec(memory_space=pl.ANY)]
        args = args + (prev,)
        aliases = {n_in: 1}
    return pl.pallas_call(
        body, grid=grid, in_specs=in_specs, out_specs=[y_spec, st_spec], out_shape=[y_shape, st_shape],
        input_output_aliases=aliases, compiler_params=_params(("arbitrary",)), name=name,
    )(*args)


def _retention(proj, state, layer, cos_p, sin_p, cos_s, sin_s, gl, gain, b, t, nb, ts, prev_s):
    n_p = b * t
    nc = t // RET_CHUNK
    prow = lambda bi, c: bi * nc + c
    const2 = lambda *a: (0, 0)
    y_p, st_p = pl.pallas_call(
        _ret_prompt_body,
        grid=(b, nc),
        in_specs=[_proj_spec(RET_CHUNK, COL_RET_Q, prow), _proj_spec(RET_CHUNK, COL_RET_K, prow),
                  _proj_spec(RET_CHUNK, COL_RET_V, prow), _proj_spec(RET_CHUNK, COL_RET_G, prow),
                  pl.BlockSpec((RET_CHUNK, DH), lambda bi, c: (c, 0)),
                  pl.BlockSpec((RET_CHUNK, DH), lambda bi, c: (c, 0)),
                  pl.BlockSpec((HEADS, DH), const2),
                  pl.BlockSpec((None, 1, HW), lambda bi, c: (layer, 0, 0))],
        out_specs=[pl.BlockSpec((RET_CHUNK, HW), lambda bi, c: (prow(bi, c), 0)),
                   pl.BlockSpec((1, HEADS, DH, DH), lambda bi, c: (bi, 0, 0, 0))],
        out_shape=[jax.ShapeDtypeStruct((n_p, HW), BF16),
                   jax.ShapeDtypeStruct((b, HEADS, DH, DH), F32)],
        scratch_shapes=[pltpu.VMEM((HEADS, DH, DH), F32), pltpu.VMEM((HEADS, RET_CHUNK, RET_CHUNK), F32),
                        pltpu.VMEM((HEADS, RET_CHUNK, DH), F32), pltpu.VMEM((HEADS, RET_CHUNK, DH), F32)],
        compiler_params=_params(("arbitrary", "arbitrary")),
        name="retention_prompt",
    )(proj, proj, proj, proj, cos_p, sin_p, gl, gain)

    rows = SAMPLE_BB * ts
    base = n_p // rows
    srow = lambda i: base + i
    y_s, st_s = _sample_state_call(
        functools.partial(_ret_sample_body, ts=ts), (nb // SAMPLE_BB,),
        [_proj_spec(rows, COL_RET_Q, srow), _proj_spec(rows, COL_RET_K, srow),
         _proj_spec(rows, COL_RET_V, srow), _proj_spec(rows, COL_RET_G, srow),
         pl.BlockSpec((rows, DH), const2), pl.BlockSpec((rows, DH), const2),
         pl.BlockSpec((HEADS, DH), const2),
         pl.BlockSpec((None, 1, HW), lambda i: (layer, 0, 0)),
         pl.BlockSpec((None, SAMPLE_BB, HEADS, DH, DH), lambda i: (layer, i, 0, 0, 0))],
        (proj, proj, proj, proj, cos_s, sin_s, gl, gain, state),
        jax.ShapeDtypeStruct((nb * ts, HW), BF16), pl.BlockSpec((rows, HW), lambda i: (i, 0)),
        layer, nb, prev_s, "retention_sample")
    return (y_p, y_s), st_p, st_s


def _mask_sums(masks, x):
    hi = x.astype(BF16)
    rest = x - hi.astype(F32)
    mid = rest.astype(BF16)
    lo = (rest - mid.astype(F32)).astype(BF16)
    m = jnp.concatenate([mk.astype(BF16) for mk in masks], axis=0)
    dot = functools.partial(jnp.dot, preferred_element_type=F32)
    out = dot(m, hi) + (dot(m, mid) + dot(m, lo))
    rows = masks[0].shape[0]
    return [out[i * rows:(i + 1) * rows] for i in range(len(masks))]


def _hg_prepare(hq_ref, hf_ref, lbt_ref, rows, chunk, with_prefix=False):
    shift = chunk.bit_length() - 1
    ri = lax.broadcasted_iota(I32, (rows, rows), 0)
    ci = lax.broadcasted_iota(I32, (rows, rows), 1)
    same = (ri >> shift) == (ci >> shift)
    causal = same & (ci <= ri)
    z = hf_ref[...]
    log_lb = lbt_ref[0:1, :]
    log_1m_lb = lbt_ref[1:2, :]
    one_m_lb = lbt_ref[2:3, :]
    log_sig = jnp.minimum(z, 0.0) - jnp.log(1.0 + jnp.exp(-jnp.abs(z)))
    bterm = log_1m_lb + log_sig
    logf = jnp.maximum(log_lb, bterm) + jnp.log(1.0 + jnp.exp(-jnp.abs(log_lb - bterm)))
    kh = one_m_lb * jax.nn.sigmoid(-z)
    qh = _silu(hq_ref[...]) * (DH ** -0.5)
    masks = [causal, same] + ([(ci >> shift) < (ri >> shift)] if with_prefix else [])
    cum, tot, *pre = _mask_sums(masks, logf)
    qi = qh * jnp.exp(cum)
    ki = kh * jnp.exp(-cum)
    ke = kh * jnp.exp(tot - cum)
    return causal, qi, ki, ke, tot, (pre[0] if with_prefix else None)


def _rms_norm_gate(o, gain, gate):
    return o * lax.rsqrt(jnp.mean(jnp.square(o), axis=-1, keepdims=True) + LN_EPS) * gain * _silu(gate)


def _hg_prompt_body(hq_ref, hf_ref, hi_ref, hg_ref, lbt_ref, gain_ref, y_ref, st_ref, s_scr):
    c = pl.program_id(1)

    @pl.when(c == 0)
    def _():
        s_scr[...] = jnp.zeros_like(s_scr)

    rows = RET_CHUNK
    n_sub = rows // HG_CHUNK
    shift = HG_CHUNK.bit_length() - 1
    causal, qi, ki, ke, tot, pre = _hg_prepare(hq_ref, hf_ref, lbt_ref, rows, HG_CHUNK, with_prefix=True)
    sub = lax.broadcasted_iota(I32, (rows, DH), 0) >> shift
    v = hi_ref[...]
    for h in range(HEADS):
        sl = slice(h * DH, (h + 1) * DH)
        q_h, ke_h, v_h, pre_h = qi[:, sl], ke[:, sl], v[:, sl], pre[:, sl]
        att = jnp.where(causal, _bdot_nt(q_h, ki[:, sl]), 0.0)
        st0 = s_scr[h]
        o = _bdot(att, v_h) + _bdot_nt(q_h * jnp.exp(pre_h), st0)
        end_last = pre_h[rows - 1:rows] + tot[rows - 1:rows, sl]
        st = st0 * jnp.exp(end_last)
        for i in range(n_sub):
            rs = slice(i * HG_CHUNK, (i + 1) * HG_CHUNK)
            u_t = _bdot_tn(v_h[rs], ke_h[rs])
            if i + 1 < n_sub:
                end_i = pre_h[(i + 1) * HG_CHUNK:(i + 1) * HG_CHUNK + 1]
                later = q_h * jnp.exp(jnp.where(sub > i, pre_h - end_i, -jnp.inf))
                o = o + _bdot_nt(later, u_t)
                st = st + u_t * jnp.exp(end_last - end_i)
            else:
                st = st + u_t
        s_scr[h] = st
        y_ref[:, sl] = _rms_norm_gate(o, gain_ref[:, sl], hg_ref[:, sl]).astype(BF16)

    @pl.when(c == pl.num_programs(1) - 1)
    def _():
        for h in range(HEADS):
            st_ref[0, h] = s_scr[h].T


def _hg_sample_body(hq_ref, hf_ref, hi_ref, hg_ref, lbt_ref, gain_ref, sin_ref_state,
                    y_ref, st_ref, *, ts):
    rows = SAMPLE_BB * ts
    causal, qi, ki, ke, tot, _ = _hg_prepare(hq_ref, hf_ref, lbt_ref, rows, ts)
    etot = jnp.exp(tot)
    v = hi_ref[...]
    for h in range(HEADS):
        sl = slice(h * DH, (h + 1) * DH)
        att = jnp.where(causal, _bdot_nt(qi[:, sl], ki[:, sl]), 0.0)
        o_intra = _bdot(att, v[:, sl])
        outs = []
        for j in range(SAMPLE_BB):
            rs = slice(j * ts, (j + 1) * ts)
            s = sin_ref_state[j, h]
            outs.append(o_intra[rs] + _bdot(qi[rs, sl], s))
            scale = jnp.broadcast_to(etot[j * ts:j * ts + 1, sl], (DH, DH)).T
            new_state = s * scale + _bdot_tn(ke[rs, sl], v[rs, sl])
            for slot in range(st_ref.shape[0]):
                st_ref[slot, j, h] = new_state
        o = jnp.concatenate(outs, axis=0)
        y_ref[:, sl] = _rms_norm_gate(o, gain_ref[:, sl], hg_ref[:, sl]).astype(BF16)


def _hgrn(proj, state, layer, lbt, gain, b, t, nb, ts, prev_s):
    n_p = b * t
    nc = t // RET_CHUNK
    prow = lambda bi, c: bi * nc + c
    y_p, st_p = pl.pallas_call(
        _hg_prompt_body,
        grid=(b, nc),
        in_specs=[_proj_spec(RET_CHUNK, COL_HG_Q, prow), _proj_spec(RET_CHUNK, COL_HG_F, prow),
                  _proj_spec(RET_CHUNK, COL_HG_I, prow), _proj_spec(RET_CHUNK, COL_HG_G, prow),
                  pl.BlockSpec((None, SUBLANES, HW), lambda bi, c: (layer, 0, 0)),
                  pl.BlockSpec((None, 1, HW), lambda bi, c: (layer, 0, 0))],
        out_specs=[pl.BlockSpec((RET_CHUNK, HW), lambda bi, c: (prow(bi, c), 0)),
                   pl.BlockSpec((1, HEADS, DH, DH), lambda bi, c: (bi, 0, 0, 0))],
        out_shape=[jax.ShapeDtypeStruct((n_p, HW), BF16),
                   jax.ShapeDtypeStruct((b, HEADS, DH, DH), F32)],
        scratch_shapes=[pltpu.VMEM((HEADS, DH, DH), F32)],
        compiler_params=_params(("arbitrary", "arbitrary")),
        name="hgrn_prompt",
    )(proj, proj, proj, proj, lbt, gain)

    rows = SAMPLE_BB * ts
    base = n_p // rows
    srow = lambda i: base + i
    y_s, st_s = _sample_state_call(
        functools.partial(_hg_sample_body, ts=ts), (nb // SAMPLE_BB,),
        [_proj_spec(rows, COL_HG_Q, srow), _proj_spec(rows, COL_HG_F, srow),
         _proj_spec(rows, COL_HG_I, srow), _proj_spec(rows, COL_HG_G, srow),
         pl.BlockSpec((None, SUBLANES, HW), lambda i: (layer, 0, 0)),
         pl.BlockSpec((None, 1, HW), lambda i: (layer, 0, 0)),
         pl.BlockSpec((None, SAMPLE_BB, HEADS, DH, DH), lambda i: (layer, i, 0, 0, 0))],
        (proj, proj, proj, proj, lbt, gain, state),
        jax.ShapeDtypeStruct((nb * ts, HW), BF16), pl.BlockSpec((rows, HW), lambda i: (i, 0)),
        layer, nb, prev_s, "hgrn_sample")
    return (y_p, y_s), st_p, st_s


def _softmax_rows(s):
    e = jnp.exp(s - jnp.max(s, axis=-1, keepdims=True))
    return e / jnp.sum(e, axis=-1, keepdims=True)


def _xa_prompt_body(q_ref, k_ref, v_ref, y_ref):
    for h in range(HEADS):
        sl = slice(h * DH, (h + 1) * DH)
        a = _softmax_rows(_bdot_nt(q_ref[:, sl] * (DH ** -0.5), k_ref[:, sl]))
        y_ref[:, sl] = _bdot(a, v_ref[:, sl]).astype(BF16)


def _xa_sample_body(q_ref, k_ref, v_ref, y_ref, *, ts):
    n_mem = k_ref.shape[1] // HEADS
    pairs = [(j, h) for j in range(SAMPLE_BB) for h in range(HEADS)]
    q = q_ref[...] * (DH ** -0.5)
    scores = [_bdot_nt(q[j * ts:(j + 1) * ts, h * DH:(h + 1) * DH], k_ref[j, pl.ds(h, n_mem, stride=HEADS), :])
              for j, h in pairs]
    a = _softmax_rows(jnp.concatenate(scores, axis=0))
    for n, (j, h) in enumerate(pairs):
        y = _bdot(a[n * ts:(n + 1) * ts], v_ref[j, pl.ds(h, n_mem, stride=HEADS), :])
        y_ref[j * ts:(j + 1) * ts, h * DH:(h + 1) * DH] = y.astype(BF16)


def _cross_attention(proj, kv_p, cache_k, cache_v, layer, b, t, nb, ts):
    n_p = b * t
    n_mem = kv_p.shape[0] // b
    tq = _pick(t, (512, 256, 128))
    nq = t // tq
    y_p = pl.pallas_call(
        _xa_prompt_body,
        grid=(b, nq),
        in_specs=[_proj_spec(tq, COL_XA_Q, lambda bi, qi: bi * nq + qi),
                  pl.BlockSpec((n_mem, HW), lambda bi, qi: (bi, 0)),
                  pl.BlockSpec((n_mem, HW), lambda bi, qi: (bi, 1))],
        out_specs=pl.BlockSpec((tq, HW), lambda bi, qi: (bi * nq + qi, 0)),
        out_shape=jax.ShapeDtypeStruct((n_p, HW), BF16),
        compiler_params=_params(("arbitrary", "arbitrary")),
        name="xattn_prompt",
    )(proj, kv_p, kv_p)

    rows = SAMPLE_BB * ts
    base = n_p // rows
    cache_k = cache_k.reshape(DEPTH, nb, n_mem * HEADS, DH)
    cache_v = cache_v.reshape(DEPTH, nb, n_mem * HEADS, DH)
    kv_spec = pl.BlockSpec((None, SAMPLE_BB, n_mem * HEADS, DH), lambda i: (layer, i, 0, 0))
    y_s = pl.pallas_call(
        functools.partial(_xa_sample_body, ts=ts),
        grid=(nb // SAMPLE_BB,),
        in_specs=[_proj_spec(rows, COL_XA_Q, lambda i: base + i), kv_spec, kv_spec],
        out_specs=pl.BlockSpec((rows, HW), lambda i: (i, 0)),
        out_shape=jax.ShapeDtypeStruct((nb * ts, HW), BF16),
        compiler_params=_params(("arbitrary",)),
        name="xattn_sample",
    )(proj, cache_k, cache_v)
    return (y_p, y_s)


def _layer_norm(tv, g, b):
    mu = jnp.mean(tv, axis=-1, keepdims=True)
    var = jnp.mean(jnp.square(tv - mu), axis=-1, keepdims=True)
    return (tv - mu) * lax.rsqrt(var + LN_EPS) * g + b


def _merge_body(yrp_ref, yrs_ref, yhp_ref, yhs_ref, yxp_ref, yxs_ref, g0_ref, g1_ref, g2_ref, xp_ref, xs_ref,
                wr_ref, wh_ref, wx_ref, wo_ref, lg_ref, lb_ref,
                x1_ref, x1b_ref, x1t_ref, wr_s, wh_s, wx_s, wo_s, *, prompt_tiles):
    @pl.when(pl.program_id(0) == 0)
    def _():
        wr_s[...] = wr_ref[...].astype(BF16)
        wh_s[...] = wh_ref[...].astype(BF16)
        wx_s[...] = wx_ref[...].astype(BF16)
        wo_s[...] = wo_ref[...].astype(BF16)

    is_prompt = pl.program_id(0) < prompt_tiles

    def branch(yp_ref, ys_ref, w_s, gate_ref):
        y = jnp.where(is_prompt, yp_ref[...], ys_ref[...])
        return gate_ref[...].astype(F32) * jnp.dot(y, w_s[...], preferred_element_type=F32)

    m = (branch(yrp_ref, yrs_ref, wr_s, g0_ref) + branch(yhp_ref, yhs_ref, wh_s, g1_ref)
         + branch(yxp_ref, yxs_ref, wx_s, g2_ref))
    hmix = jnp.dot(m.astype(BF16), wo_s[...], preferred_element_type=F32)
    x = jnp.where(is_prompt, xp_ref[...], xs_ref[...])
    x1 = _layer_norm(DN_ALPHA * x + hmix, lg_ref[...], lb_ref[...])
    x1_ref[...] = x1
    x1b_ref[...] = x1.astype(BF16)
    tm = x1.shape[0]
    bits = lax.bitcast_convert_type(x1.astype(BF16).astype(F32), U32)
    half = D_MODEL // 2
    packed = (bits[:, :half] >> 16) | (bits[:, half:] & jnp.uint32(0xFFFF0000))
    for s in range(PACKED_TILES):
        x1t_ref[pl.ds(s, tm, stride=PACKED_TILES), :] = packed[:, s * LANES:(s + 1) * LANES]


def _merge(ys, gates, x, w_up_ret, w_up_hgrn, w_up_xattn, w_out, ln_g, ln_b, layer):
    n_p = ys[0].shape[0]
    nt = n_p + ys[1].shape[0]
    tm = _pick(nt, (256, 128))
    assert n_p % tm == 0 and ys[1].shape[0] % tm == 0
    p_tiles = n_p // tm
    row = lambda i: (i, 0)
    p_map = lambda i: (jnp.minimum(i, p_tiles - 1), 0)
    s_map = lambda i: (jnp.maximum(i - p_tiles, 0), 0)
    y_specs = [pl.BlockSpec((tm, HW), p_map), pl.BlockSpec((tm, HW), s_map)] * 3
    wspec = lambda k: pl.BlockSpec((None, k, D_MODEL), lambda i: (layer, 0, 0))
    vec = pl.BlockSpec((None, 1, D_MODEL), lambda i: (layer, 0, 0))
    gate_specs = [pl.BlockSpec((tm, D_MODEL), lambda i, c=c: (i, c)) for c in range(3)]
    return pl.pallas_call(
        functools.partial(_merge_body, prompt_tiles=p_tiles),
        grid=(nt // tm,),
        in_specs=y_specs + gate_specs + [pl.BlockSpec((tm, D_MODEL), p_map), pl.BlockSpec((tm, D_MODEL), s_map),
                  wspec(HW), wspec(HW), wspec(HW), wspec(D_MODEL), vec, vec],
        out_specs=[pl.BlockSpec((tm, D_MODEL), row), pl.BlockSpec((tm, D_MODEL), row),
                   pl.BlockSpec((tm * PACKED_TILES, LANES), row)],
        out_shape=[jax.ShapeDtypeStruct((nt, D_MODEL), F32),
                   jax.ShapeDtypeStruct((nt, D_MODEL), BF16),
                   jax.ShapeDtypeStruct((nt * PACKED_TILES, LANES), U32)],
        scratch_shapes=[pltpu.VMEM((HW, D_MODEL), BF16)] * 3 + [pltpu.VMEM((D_MODEL, D_MODEL), BF16)],
        compiler_params=_params(("arbitrary",)),
        name="merge_out_ln1",
    )(*ys, *([gates] * 3), *x, w_up_ret, w_up_hgrn, w_up_xattn, w_out, ln_g, ln_b)


def _router_body(x_ref, wt_ref, b_ref, eidx_ref, wn_ref):
    tm = x_ref.shape[0]
    x = x_ref[...]
    w = wt_ref[...]
    xh = x.astype(BF16)
    xl = (x - xh.astype(F32)).astype(BF16)
    wh = w.astype(BF16)
    wl = (w - wh.astype(F32)).astype(BF16)
    logits = _bdot_nt(wh, xh) + (_bdot_nt(wh, xl) + _bdot_nt(wl, xh))
    s = jax.nn.sigmoid(logits)
    sel = s + b_ref[...]
    neg = -jnp.inf
    groups = [sel[g * GROUP_SIZE:(g + 1) * GROUP_SIZE, :] for g in range(N_GROUPS)]
    ie = lax.broadcasted_iota(I32, (GROUP_SIZE, tm), 0).astype(F32)
    rows = []
    for blk in groups:
        m1 = jnp.max(blk, axis=0, keepdims=True)
        first = jnp.min(jnp.where(blk == m1, ie, float(GROUP_SIZE)), axis=0, keepdims=True)
        rows.append(m1 + jnp.max(jnp.where(ie == first, neg, blk), axis=0, keepdims=True))
    gscore = jnp.concatenate(rows, axis=0)
    ig = lax.broadcasted_iota(I32, gscore.shape, 0).astype(F32)
    gmask = jnp.zeros(gscore.shape, F32)
    for _ in range(TOPK_GROUPS):
        m = jnp.max(gscore, axis=0, keepdims=True)
        gi = jnp.min(jnp.where(gscore == m, ig, float(N_GROUPS)), axis=0, keepdims=True)
        hit = ig == gi
        gmask = jnp.where(hit, 1.0, gmask)
        gscore = jnp.where(hit, neg, gscore)
    masked = jnp.concatenate([jnp.where(gmask[g:g + 1, :] > 0.5, blk, neg)
                              for g, blk in enumerate(groups)], axis=0)
    ix = lax.broadcasted_iota(I32, masked.shape, 0).astype(F32)
    idxs, ws = [], []
    for _ in range(TOP_K):
        m = jnp.max(masked, axis=0, keepdims=True)
        ei = jnp.min(jnp.where(masked == m, ix, float(N_EXPERTS)), axis=0, keepdims=True)
        hit = ix == ei
        idxs.append(ei)
        ws.append(jnp.sum(jnp.where(hit, s, 0.0), axis=0, keepdims=True))
        masked = jnp.where(hit, neg, masked)
    wsum = ws[0]
    for w in ws[1:]:
        wsum = wsum + w
    pad = [jnp.zeros((1, tm), F32)] * (SUBLANES - TOP_K)
    eidx_ref[...] = jnp.concatenate(idxs + pad, axis=0).astype(I32)
    wn_ref[...] = jnp.concatenate([w / wsum * ROUTED_SCALE for w in ws] + pad, axis=0)


def _router(x1, w_router_t, b_router, layer):
    nt = x1.shape[0]
    tm = _pick(nt, (512, 256, 128))
    return pl.pallas_call(
        _router_body,
        grid=(nt // tm,),
        in_specs=[pl.BlockSpec((tm, D_MODEL), lambda i: (i, 0)),
                  pl.BlockSpec((None, N_EXPERTS, D_MODEL), lambda i: (layer, 0, 0)),
                  pl.BlockSpec((None, N_EXPERTS, 1), lambda i: (layer, 0, 0))],
        out_specs=[pl.BlockSpec((SUBLANES, tm), lambda i: (0, i))] * 2,
        out_shape=[jax.ShapeDtypeStruct((SUBLANES, nt), I32),
                   jax.ShapeDtypeStruct((SUBLANES, nt), F32)],
        compiler_params=_params(("arbitrary",)),
        name="moe_router",
    )(x1, w_router_t, b_router)


def _positions_body(eidx_ref, pos_ref, cnt_ref, off_ref, base_scr, off_scr):
    phase = pl.program_id(0)
    i = pl.program_id(1)
    tp = eidx_ref.shape[1]
    ix = lax.broadcasted_iota(I32, (N_EXPERTS, tp), 0)
    eidx = eidx_ref[...]
    member = jnp.zeros((N_EXPERTS, tp), F32)
    for k in range(TOP_K):
        member = member + (ix == eidx[k:k + 1, :]).astype(F32)
    tile_cnt = jnp.sum(member, axis=1, keepdims=True)

    @pl.when((phase == 0) & (i == 0))
    def _():
        base_scr[...] = jnp.zeros_like(base_scr)

    @pl.when((phase == 1) & (i == 0))
    def _():
        cnt = base_scr[...]
        er = lax.broadcasted_iota(I32, (N_EXPERTS, N_EXPERTS), 0)
        ec = lax.broadcasted_iota(I32, (N_EXPERTS, N_EXPERTS), 1)
        off = jnp.dot((ec < er).astype(F32), cnt, precision=HIGHEST, preferred_element_type=F32)
        off_scr[...] = off
        cnt_ref[...] = cnt
        off_ref[...] = off
        base_scr[...] = jnp.zeros_like(base_scr)

    @pl.when(phase == 1)
    def _():
        tr = lax.broadcasted_iota(I32, (tp, tp), 0)
        tc = lax.broadcasted_iota(I32, (tp, tp), 1)
        before = jnp.dot(member.astype(BF16), (tr < tc).astype(BF16), preferred_element_type=F32)
        where_to = before + (off_scr[...] + base_scr[...])[:, 0:1]
        rows = [jnp.sum(jnp.where(ix == eidx[k:k + 1, :], where_to, 0.0), axis=0, keepdims=True)
                for k in range(TOP_K)]
        rows += [jnp.zeros((1, tp), F32)] * (SUBLANES - TOP_K)
        pos_ref[...] = jnp.concatenate(rows, axis=0).astype(I32)

    base_scr[...] = base_scr[...] + tile_cnt


def _positions(eidx):
    nt = eidx.shape[1]
    tp = _pick(nt, (512, 256, 128))
    const = lambda p, i: (0, 0)
    return pl.pallas_call(
        _positions_body,
        grid=(2, nt // tp),
        in_specs=[pl.BlockSpec((SUBLANES, tp), lambda p, i: (0, i))],
        out_specs=[pl.BlockSpec((SUBLANES, tp), lambda p, i: (0, i * p)),
                   pl.BlockSpec((N_EXPERTS, LANES), const), pl.BlockSpec((N_EXPERTS, LANES), const)],
        out_shape=[jax.ShapeDtypeStruct((SUBLANES, nt), I32),
                   jax.ShapeDtypeStruct((N_EXPERTS, LANES), F32),
                   jax.ShapeDtypeStruct((N_EXPERTS, LANES), F32)],
        scratch_shapes=[pltpu.VMEM((N_EXPERTS, LANES), F32), pltpu.VMEM((N_EXPERTS, LANES), F32)],
        compiler_params=_params(("arbitrary", "arbitrary")),
        name="moe_positions",
    )(eidx)


T_TILE, T_EXPERT, T_LO, T_HI, T_FRESH, T_NEWEXP = range(6)

def _table_body(cnt_ref, off_ref, tbl_ref, *, tile_rows):
    te = float(tile_rows)
    n = tbl_ref.shape[1]
    cnt = cnt_ref[...]
    off = off_ref[...]
    first = jnp.floor(off * (1.0 / te))
    last = jnp.floor((off + cnt - 1.0) * (1.0 / te))
    nst = jnp.where(cnt > 0.0, last - first + 1.0, 0.0)
    er = lax.broadcasted_iota(I32, (N_EXPERTS, N_EXPERTS), 0)
    ec = lax.broadcasted_iota(I32, (N_EXPERTS, N_EXPERTS), 1)
    s_end = jnp.dot((ec <= er).astype(F32), nst, precision=HIGHEST, preferred_element_type=F32)
    s_beg = s_end - nst
    total = s_end[N_EXPERTS - 1:N_EXPERTS, 0:1]
    sidx = lax.broadcasted_iota(I32, (1, n), 1).astype(F32)
    s = jnp.minimum(sidx, total - 1.0)
    e_s = jnp.sum((s_end[:, 0:1] <= s).astype(F32), axis=0, keepdims=True)
    hot = lax.broadcasted_iota(I32, (N_EXPERTS, n), 0).astype(F32) == e_s

    def pick(col):
        return jnp.sum(jnp.where(hot, col[:, 0:1], 0.0), axis=0, keepdims=True)

    tile = pick(first) + s - pick(s_beg)
    valid = sidx < total
    o, c = pick(off), pick(cnt)
    lo = jnp.where(valid, jnp.maximum(o, tile * te), 0.0)
    hi = jnp.where(valid, jnp.minimum(o + c, (tile + 1.0) * te), 0.0)
    head = sidx == 0.0
    fresh = jnp.where((tile != pltpu.roll(tile, 1, 1)) | head, 1.0, 0.0)
    newexp = jnp.where((e_s != pltpu.roll(e_s, 1, 1)) | head, 1.0, 0.0)
    pad = [jnp.zeros((1, n), F32)] * (SUBLANES - 6)
    tbl_ref[...] = jnp.concatenate([tile, e_s, lo, hi, fresh, newexp] + pad, axis=0).astype(I32)


def _step_table(cnt, off, n_rows, te):
    n_steps = n_rows // te + N_EXPERTS
    width = -(-n_steps // LANES) * LANES
    tbl = pl.pallas_call(
        functools.partial(_table_body, tile_rows=te),
        out_shape=jax.ShapeDtypeStruct((SUBLANES, width), I32),
        name="moe_step_table",
    )(cnt, off)
    return tbl, n_steps


def _wait_tile_rows(like_src, dst_rows_ref, sem_ref):
    rows = like_src.shape[0]
    for _ in range(TOP_K):
        pltpu.make_async_copy(like_src, dst_rows_ref.at[pl.ds(0, rows)], sem_ref).wait()


def _dispatch_body(pos_ref, xt_ref, xs_ref, pos_s, sem_p, sem):
    td = pos_ref.shape[1]
    cp = pltpu.make_async_copy(pos_ref, pos_s, sem_p)
    cp.start()
    cp.wait()

    def issue(g, carry):
        for u in range(ISSUE_UNROLL):
            r = g * ISSUE_UNROLL + u
            for k in range(TOP_K):
                pltpu.make_async_copy(xt_ref.at[r], xs_ref.at[pos_s[k, r]], sem).start(priority=k % 2)
        return carry

    lax.fori_loop(0, td // ISSUE_UNROLL, issue, 0)
    _wait_tile_rows(xt_ref, xs_ref, sem)


def _dispatch(pos, x1t):
    nt = x1t.shape[0]
    td = _pick(nt, (512, 256, 128))
    return pl.pallas_call(
        _dispatch_body,
        grid=(nt // td,),
        in_specs=[pl.BlockSpec((SUBLANES, td), lambda i: (0, i)),
                  pl.BlockSpec((td,) + x1t.shape[1:], lambda i: (i, 0, 0))],
        out_specs=pl.BlockSpec(memory_space=pl.ANY),
        out_shape=jax.ShapeDtypeStruct((nt * TOP_K,) + x1t.shape[1:], x1t.dtype),
        scratch_shapes=[pltpu.SMEM((SUBLANES, td), I32), pltpu.SemaphoreType.DMA, pltpu.SemaphoreType.DMA],
        compiler_params=_params(("arbitrary",)),
        name="moe_dispatch",
    )(pos, x1t)


def _experts_body(tbl_ref, xs_ref, wg_ref, wu_ref, wd_ref, ye_ref, wg_s, wu_s, wd_s):
    s = pl.program_id(0)
    te = xs_ref.shape[0] // PACKED_TILES
    lo = tbl_ref[T_LO, s]
    hi = tbl_ref[T_HI, s]

    @pl.when(tbl_ref[T_NEWEXP, s] == 1)
    def _():
        wg_s[...] = wg_ref[...].astype(BF16)
        wu_s[...] = wu_ref[...].astype(BF16)
        wd_s[...] = wd_ref[...].astype(BF16)

    @pl.when(tbl_ref[T_FRESH, s] == 1)
    def _():
        ye_ref[...] = jnp.zeros_like(ye_ref)

    @pl.when(hi > lo)
    def _():
        words = [xs_ref[pl.ds(t, te, stride=PACKED_TILES), :] for t in range(PACKED_TILES)]
        low = [lax.bitcast_convert_type(w << 16, F32).astype(BF16) for w in words]
        high = [lax.bitcast_convert_type(w & jnp.uint32(0xFFFF0000), F32).astype(BF16) for w in words]
        x = jnp.concatenate(low + high, axis=-1)
        g = jnp.dot(x, wg_s[...], preferred_element_type=F32)
        u = jnp.dot(x, wu_s[...], preferred_element_type=F32)
        y = jnp.dot((_silu(g) * u).astype(BF16), wd_s[...], preferred_element_type=F32)
        row = tbl_ref[T_TILE, s] * te + lax.broadcasted_iota(I32, (te, LANES), 0)
        mine = (row >= lo) & (row < hi)
        for t in range(ROW_TILES):
            sl = pl.ds(t, te, stride=ROW_TILES)
            ye_ref[sl, :] = jnp.where(mine, y[:, t * LANES:(t + 1) * LANES], ye_ref[sl, :])


def _experts(tbl, n_steps, te, xs, w_gate, w_up, w_down, layer):
    n_rows = xs.shape[0] // PACKED_TILES
    tile_map = lambda s, tbl: (tbl[T_TILE, s], 0)
    w_map = lambda s, tbl: (layer, tbl[T_EXPERT, s], 0, 0)
    w_in_spec = pl.BlockSpec((None, None, D_MODEL, D_EXPERT), w_map)
    w_dn_spec = pl.BlockSpec((None, None, D_EXPERT, D_MODEL), w_map)
    return pl.pallas_call(
        _experts_body,
        grid_spec=pltpu.PrefetchScalarGridSpec(
            num_scalar_prefetch=1,
            grid=(n_steps,),
            in_specs=[pl.BlockSpec((te * PACKED_TILES, LANES), tile_map), w_in_spec, w_in_spec, w_dn_spec],
            out_specs=pl.BlockSpec((te * ROW_TILES, LANES), tile_map),
            scratch_shapes=[pltpu.VMEM((D_MODEL, D_EXPERT), BF16), pltpu.VMEM((D_MODEL, D_EXPERT), BF16),
                            pltpu.VMEM((D_EXPERT, D_MODEL), BF16)]),
        out_shape=jax.ShapeDtypeStruct((n_rows * ROW_TILES, LANES), F32),
        compiler_params=_params(("arbitrary",)),
        name="moe_experts",
    )(tbl, xs, w_gate, w_up, w_down)


def _combine_body(pos_ref, wn_ref, ye_ref, x1_ref, x1b_ref, wsg_ref, wsu_ref, wsd_ref, lg_ref, lb_ref,
                  x2p_ref, x2s_ref, x2b_ref, pos_s, buf, wsg_s, wsu_s, wsd_s, sem_p, sem, *, prompt_tiles):
    i = pl.program_id(0)
    n = pl.num_programs(0)
    tc = wn_ref.shape[1]
    slot = i % 2
    tile_rows = tc * ROW_TILES

    def request(tile, into):
        cp = pltpu.make_async_copy(pos_ref.at[tile], pos_s, sem_p)
        cp.start()
        cp.wait()

        def issue(g, carry):
            for u in range(ISSUE_UNROLL):
                r = g * ISSUE_UNROLL + u
                for k in range(TOP_K):
                    at = pl.multiple_of(((into * TOP_K + k) * tc + r) * ROW_TILES, ROW_TILES)
                    pltpu.make_async_copy(ye_ref.at[pos_s[k, r]], buf.at[pl.ds(at, ROW_TILES)],
                                          sem.at[into]).start(priority=k % 2)
            return carry

        lax.fori_loop(0, tc // ISSUE_UNROLL, issue, 0)

    @pl.when(i == 0)
    def _():
        wsg_s[...] = wsg_ref[...].astype(BF16)
        wsu_s[...] = wsu_ref[...].astype(BF16)
        wsd_s[...] = wsd_ref[...].astype(BF16)
        request(0, 0)

    @pl.when(i + 1 < n)
    def _():
        request(i + 1, 1 - slot)

    xb = x1b_ref[...]
    hs = _silu(jnp.dot(xb, wsg_s[...], preferred_element_type=F32)) * jnp.dot(xb, wsu_s[...], preferred_element_type=F32)
    shared = jnp.dot(hs.astype(BF16), wsd_s[...], preferred_element_type=F32)
    for k in range(TOP_K):
        pltpu.make_async_copy(buf.at[pl.ds(0, tile_rows)], buf.at[pl.ds(tile_rows, tile_rows)], sem.at[slot]).wait()

    w = wn_ref[...]
    acc = [None] * ROW_TILES
    for k in range(TOP_K):
        wcol = jnp.concatenate([jnp.broadcast_to(w[k:k + 1, c * LANES:(c + 1) * LANES], (LANES, LANES)).T
                                for c in range(tc // LANES)], axis=0)
        base = (slot * TOP_K + k) * tile_rows
        for t in range(ROW_TILES):
            term = wcol * buf[pl.ds(base + t, tc, stride=ROW_TILES), :]
            acc[t] = term if acc[t] is None else acc[t] + term
    routed = jnp.concatenate(acc, axis=-1)
    x2 = _layer_norm(DN_ALPHA * x1_ref[...] + (routed + shared), lg_ref[...], lb_ref[...])
    x2b_ref[...] = x2.astype(BF16)

    @pl.when(i < prompt_tiles)
    def _():
        x2p_ref[...] = x2

    @pl.when(i >= prompt_tiles)
    def _():
        x2s_ref[...] = x2


def _combine(pos, wn, ye, x1, x1b, w_s_gate, w_s_up, w_s_down, ln_g, ln_b, layer, n_p):
    nt = x1.shape[0]
    tc = _pick(nt, (256, 128))
    assert n_p % tc == 0
    n_tiles = nt // tc
    p_tiles = n_p // tc
    d_sh = w_s_gate.shape[2]
    pos3 = pos.reshape(SUBLANES, n_tiles, tc).transpose(1, 0, 2)
    row = lambda i: (i, 0)
    vec = pl.BlockSpec((None, 1, D_MODEL), lambda i: (layer, 0, 0))
    return pl.pallas_call(
        functools.partial(_combine_body, prompt_tiles=p_tiles),
        grid=(n_tiles,),
        in_specs=[pl.BlockSpec((n_tiles, SUBLANES, tc), lambda i: (0, 0, 0)),
                  pl.BlockSpec((SUBLANES, tc), lambda i: (0, i)),
                  pl.BlockSpec(memory_space=pl.ANY),
                  pl.BlockSpec((tc, D_MODEL), row), pl.BlockSpec((tc, D_MODEL), row),
                  pl.BlockSpec((None, D_MODEL, d_sh), lambda i: (layer, 0, 0)),
                  pl.BlockSpec((None, D_MODEL, d_sh), lambda i: (layer, 0, 0)),
                  pl.BlockSpec((None, d_sh, D_MODEL), lambda i: (layer, 0, 0)), vec, vec],
        out_specs=[pl.BlockSpec((tc, D_MODEL), lambda i: (jnp.minimum(i, p_tiles - 1), 0)),
                   pl.BlockSpec((tc, D_MODEL), lambda i: (jnp.maximum(i - p_tiles, 0), 0)),
                   pl.BlockSpec((tc, D_MODEL), row)],
        out_shape=[jax.ShapeDtypeStruct((n_p, D_MODEL), F32), jax.ShapeDtypeStruct((nt - n_p, D_MODEL), F32),
                   jax.ShapeDtypeStruct((nt, D_MODEL), BF16)],
        scratch_shapes=[pltpu.SMEM((SUBLANES, tc), I32),
                        pltpu.VMEM((2 * TOP_K * tc * ROW_TILES, LANES), F32),
                        pltpu.VMEM((D_MODEL, d_sh), BF16), pltpu.VMEM((D_MODEL, d_sh), BF16),
                        pltpu.VMEM((d_sh, D_MODEL), BF16),
                        pltpu.SemaphoreType.DMA, pltpu.SemaphoreType.DMA((2,))],
        compiler_params=_params(("arbitrary",)),
        name="moe_combine_ln2",
    )(pos3, wn, ye, x1, x1b, w_s_gate, w_s_up, w_s_down, ln_g, ln_b)


def _rope_tables(t, pos0):
    inv = 1.0 / (ROPE_BASE ** (jnp.arange(0, DH, 2, dtype=F32) / DH))
    ang = (jnp.arange(t, dtype=F32) + pos0)[:, None] * inv[None, :]
    cos, sin = jnp.cos(ang), jnp.sin(ang)
    return jnp.concatenate([cos, cos], axis=-1), jnp.concatenate([-sin, sin], axis=-1)


def kernel(x_prompt, x_sample, mem_prompt, state_ret, state_hgrn, cache_mem_k, cache_mem_v, w_in, w_up_ret, w_up_hgrn, w_up_xattn, w_out, w_mem_kv, ret_norm_g, hgrn_norm_g, lb_logits, ln1_g, ln1_b, ln2_g, ln2_b, w_router, b_router, w_e_gate, w_e_up, w_e_down, w_s_gate, w_s_up, w_s_down):
    b, t, d = x_prompt.shape
    nb, ts, _ = x_sample.shape
    n_mem = mem_prompt.shape[1]
    assert d == D_MODEL and t % RET_CHUNK == 0 and nb % SAMPLE_BB == 0
    assert ts & (ts - 1) == 0 and HG_CHUNK % ts == 0 and RET_CHUNK % ts == 0
    n_p, n_s = b * t, nb * ts
    nt = n_p + n_s
    assert n_p % (SAMPLE_BB * ts) == 0

    lb_cum = jnp.cumsum(jax.nn.softmax(lb_logits.astype(F32), axis=0), axis=0)
    lbs = lb_cum - lb_cum[0:1]
    lbt = jnp.stack([jnp.log(lbs), jnp.log1p(-lbs), 1.0 - lbs] + [jnp.zeros_like(lbs)] * (SUBLANES - 3), axis=1)
    gl = jnp.broadcast_to(jnp.log1p(-jnp.exp2(-5.0 - jnp.arange(HEADS, dtype=F32)))[:, None], (HEADS, DH))
    cos_p, sin_p = _rope_tables(t, 0)
    cos_s, sin_s = _rope_tables(ts, PAST_LEN)
    cos_s, sin_s = jnp.tile(cos_s, (SAMPLE_BB, 1)), jnp.tile(sin_s, (SAMPLE_BB, 1))
    vec3 = lambda a: a.reshape(DEPTH, 1, -1)
    w_router_t = jnp.swapaxes(w_router, 1, 2)
    b_router3 = b_router.reshape(DEPTH, N_EXPERTS, 1)
    mem2 = mem_prompt.reshape(b * n_mem, d)

    x = (x_prompt.reshape(n_p, d), x_sample.reshape(n_s, d))
    xb = jnp.concatenate([x[0].astype(BF16), x[1].astype(BF16)], axis=0)
    tm_proj = _pick(nt, (1024, 512, 128))
    te = _pick(nt * TOP_K, EXPERT_TILES)
    outs = {k: [] for k in ("ret_p", "hg_p", "mk", "mv")}
    ret_s = hg_s = None
    for l in range(DEPTH):
        mix_tiles = COL_GATES * HW // PROJ_TILE_N
        proj = _matmul(xb, w_in, l, tm_proj, PROJ_TILE_N, n=COL_GATES * HW)
        gates = _matmul(xb, w_in, l, tm_proj, PROJ_TILE_N, first_tile=mix_tiles, n=3 * D_MODEL, gate=True)
        kv_p = _matmul(mem2, w_mem_kv, l, _pick(b * n_mem, (1024, 512, 256)), 2 * HW)
        yr, ret_p, ret_s = _retention(proj, state_ret, l, cos_p, sin_p, cos_s, sin_s, gl,
                                      vec3(ret_norm_g), b, t, nb, ts, ret_s)
        yh, hg_p, hg_s = _hgrn(proj, state_hgrn, l, lbt, vec3(hgrn_norm_g), b, t, nb, ts, hg_s)
        yx = _cross_attention(proj, kv_p, cache_mem_k, cache_mem_v, l, b, t, nb, ts)
        x1, x1b, x1t = _merge((*yr, *yh, *yx), gates, x, w_up_ret, w_up_hgrn, w_up_xattn, w_out,
                              vec3(ln1_g), vec3(ln1_b), l)
        eidx, wn = _router(x1, w_router_t, b_router3, l)
        pos, cnt, off = _positions(eidx)
        tbl, n_steps = _step_table(cnt, off, nt * TOP_K, te)
        xs = _dispatch(pos, x1t.reshape(nt, PACKED_TILES, LANES))
        ye = _experts(tbl, n_steps, te, xs.reshape(-1, LANES), w_e_gate, w_e_up, w_e_down, l)
        ye = ye.reshape(-1, ROW_TILES, LANES)
        x_p, x_s, xb = _combine(pos, wn, ye, x1, x1b, w_s_gate, w_s_up, w_s_down,
                                vec3(ln2_g), vec3(ln2_b), l, n_p)
        x = (x_p, x_s)
        outs["ret_p"].append(ret_p)
        outs["hg_p"].append(hg_p)
        outs["mk"].append(kv_p[:, :HW].reshape(b, n_mem, HEADS, DH))
        outs["mv"].append(kv_p[:, HW:].reshape(b, n_mem, HEADS, DH))
    return (x[0].reshape(b, t, d), x[1].reshape(nb, ts, d),
            jnp.stack(outs["ret_p"]), jnp.stack(outs["hg_p"]), jnp.stack(outs["mk"]), jnp.stack(outs["mv"]),
            ret_s, hg_s)
```

```python
import functools

import jax
import jax.numpy as jnp
from jax import lax
from jax.experimental import pallas as pl
from jax.experimental.pallas import tpu as pltpu

F32 = jnp.float32
BF16 = jnp.bfloat16
I32 = jnp.int32
HIGHEST = lax.Precision.HIGHEST

D_MODEL = 1024
DEPTH = 2
PAST_LEN = 16384
HEADS = 4
DH = 128
HW = HEADS * DH
RET_CHUNK = 128
HG_CHUNK = 16
ROPE_BASE = 10000.0
N_EXPERTS = 64
N_GROUPS = 8
GROUP_SIZE = N_EXPERTS // N_GROUPS
TOPK_GROUPS = 4
TOP_K = 6
D_EXPERT = 256
ROUTED_SCALE = 2.5
LN_EPS = 1e-5
DN_ALPHA = (2 * DEPTH) ** 0.25
N_IN = 9 * HW + 3 * D_MODEL
COL_RET_Q, COL_RET_K, COL_RET_V, COL_RET_G = 0, 1, 2, 3
COL_HG_Q, COL_HG_F, COL_HG_I, COL_HG_G = 4, 5, 6, 7
COL_XA_Q = 8
COL_GATES = 9
LANES = 128
SUBLANES = 8
ROW_TILES = D_MODEL // LANES
PACKED_TILES = ROW_TILES // 2
U32 = jnp.uint32
SAMPLE_BB = 8
EXPERT_TILES = (512, 256)
ISSUE_UNROLL = 8
PROJ_TILE_N = 1536
VMEM_LIMIT = 56 * 1024 * 1024


def _params(sem):
    return pltpu.CompilerParams(dimension_semantics=sem, vmem_limit_bytes=VMEM_LIMIT)


def _bdot(a, b):
    return jnp.dot(a.astype(BF16), b.astype(BF16), preferred_element_type=F32)


def _bdot_nt(a, b):
    return lax.dot_general(a.astype(BF16), b.astype(BF16), (((1,), (1,)), ((), ())),
                           preferred_element_type=F32)


def _bdot_tn(a, b):
    return lax.dot_general(a.astype(BF16), b.astype(BF16), (((0,), (0,)), ((), ())),
                           preferred_element_type=F32)


def _silu(x):
    return x * jax.nn.sigmoid(x)


def _pick(n, prefs):
    for p in prefs:
        if n % p == 0:
            return p
    raise ValueError(f"no tile for {n}")


def _mm_body(x_ref, w_ref, o_ref, wb_ref, *, gate):
    @pl.when(pl.program_id(1) == 0)
    def _():
        wb_ref[...] = w_ref[...].astype(BF16)

    acc = jnp.dot(x_ref[...].astype(BF16), wb_ref[...], preferred_element_type=F32)
    o_ref[...] = (jax.nn.sigmoid(acc) if gate else acc).astype(o_ref.dtype)


def _matmul(x, w, layer, tm, tn, first_tile=0, n=None, gate=False):
    m, k = x.shape
    n = w.shape[2] if n is None else n
    return pl.pallas_call(
        functools.partial(_mm_body, gate=gate),
        grid=(n // tn, m // tm),
        in_specs=[pl.BlockSpec((tm, k), lambda j, i: (i, 0)),
                  pl.BlockSpec((None, k, tn), lambda j, i: (layer, 0, first_tile + j))],
        out_specs=pl.BlockSpec((tm, tn), lambda j, i: (i, j)),
        out_shape=jax.ShapeDtypeStruct((m, n), BF16 if gate else F32),
        scratch_shapes=[pltpu.VMEM((k, tn), BF16)],
        compiler_params=_params(("arbitrary", "arbitrary")),
        name="dense_matmul",
    )(x, w)


def _rotary(x, cos, sin_signed):
    return x * cos + pltpu.roll(x, DH // 2, 1) * sin_signed


def _group_norm_gate(o, gain, gate):
    mu = jnp.mean(o, axis=-1, keepdims=True)
    var = jnp.mean(jnp.square(o - mu), axis=-1, keepdims=True)
    return (o - mu) * lax.rsqrt(var + LN_EPS) * gain * _silu(gate)


def _ret_prompt_body(q_ref, k_ref, v_ref, g_ref, cos_ref, sin_ref, gl_ref, gain_ref,
                     y_ref, st_ref, s_scr, intra_scr, qdec_scr, kdec_scr):
    c = pl.program_id(1)
    ch = RET_CHUNK

    @pl.when((pl.program_id(0) == 0) & (c == 0))
    def _():
        ri = lax.broadcasted_iota(I32, (ch, ch), 0)
        ci = lax.broadcasted_iota(I32, (ch, ch), 1)
        rel = (ri - ci).astype(F32)
        idx = lax.broadcasted_iota(I32, (ch, DH), 0).astype(F32)
        for h in range(HEADS):
            gl = gl_ref[h:h + 1, :]
            intra_scr[h] = jnp.where(rel >= 0, jnp.exp(gl * rel), 0.0)
            qdec_scr[h] = jnp.exp(gl * (idx + 1.0))
            kdec_scr[h] = jnp.exp(gl * (ch - 1.0 - idx))

    @pl.when(c == 0)
    def _():
        s_scr[...] = jnp.zeros_like(s_scr)

    cos = cos_ref[...]
    sin = sin_ref[...]
    for h in range(HEADS):
        sl = slice(h * DH, (h + 1) * DH)
        gl = gl_ref[h:h + 1, :]
        qr = _rotary(q_ref[:, sl], cos, sin)
        kr = _rotary(k_ref[:, sl], cos, sin) * (DH ** -0.5)
        v = v_ref[:, sl]
        att = _bdot_nt(qr, kr) * intra_scr[h]
        s = s_scr[h]
        o = _bdot(att, v) + _bdot(qr, s) * qdec_scr[h]
        s_scr[h] = s * jnp.exp(gl * float(ch)) + _bdot_tn(kr * kdec_scr[h], v)
        y_ref[:, sl] = _group_norm_gate(o, gain_ref[:, sl], g_ref[:, sl]).astype(BF16)

    @pl.when(c == pl.num_programs(1) - 1)
    def _():
        st_ref[0] = s_scr[...]


def _ret_sample_body(q_ref, k_ref, v_ref, g_ref, cos_ref, sin_ref, gl_ref, gain_ref, sin_ref_state,
                     y_ref, st_ref, *, ts):
    rows = SAMPLE_BB * ts
    shift = ts.bit_length() - 1
    cos = cos_ref[...]
    sin = sin_ref[...]
    ri = lax.broadcasted_iota(I32, (rows, rows), 0)
    ci = lax.broadcasted_iota(I32, (rows, rows), 1)
    rel = (ri - ci).astype(F32)
    mask = ((ri >> shift) == (ci >> shift)) & (ri >= ci)
    idx = (lax.broadcasted_iota(I32, (rows, DH), 0) & (ts - 1)).astype(F32)
    for h in range(HEADS):
        sl = slice(h * DH, (h + 1) * DH)
        gl = gl_ref[h:h + 1, :]
        qr = _rotary(q_ref[:, sl], cos, sin)
        kr = _rotary(k_ref[:, sl], cos, sin) * (DH ** -0.5)
        v = v_ref[:, sl]
        intra = jnp.where(mask, jnp.exp(gl[:, :rows] * rel), 0.0)
        o_intra = _bdot(_bdot_nt(qr, kr) * intra, v)
        q_dec = jnp.exp(gl * (idx + 1.0))
        kd = kr * jnp.exp(gl * (ts - 1.0 - idx))
        c_dec = jnp.exp(gl * float(ts))
        outs = []
        for j in range(SAMPLE_BB):
            rs = slice(j * ts, (j + 1) * ts)
            s = sin_ref_state[j, h]
            outs.append(o_intra[rs] + _bdot(qr[rs], s) * q_dec[rs])
            new_state = s * c_dec + _bdot_tn(kd[rs], v[rs])
            for slot in range(st_ref.shape[0]):
                st_ref[slot, j, h] = new_state
        o = jnp.concatenate(outs, axis=0)
        y_ref[:, sl] = _group_norm_gate(o, gain_ref[:, sl], g_ref[:, sl]).astype(BF16)


def _proj_spec(rows, col, row_map):
    return pl.BlockSpec((rows, HW), lambda *a: (row_map(*a), col))


def _sample_state_call(body, grid, in_specs, args, y_shape, y_spec, layer, nb, prev, name):
    st_shape = jax.ShapeDtypeStruct((DEPTH, nb, HEADS, DH, DH), F32)
    slots = DEPTH if prev is None else 1
    st_spec = pl.BlockSpec((slots, SAMPLE_BB, HEADS, DH, DH), lambda i: (layer, i, 0, 0, 0))
    aliases = {}
    if prev is not None:
        n_in = len(args)
        inner = body
        body = lambda *refs: inner(*refs[:n_in], *refs[n_in + 1:])
        in_specs = in_specs + [pl.BlockSpec(memory_space=pl.ANY)]
        args = args + (prev,)
        aliases = {n_in: 1}
    return pl.pallas_call(
        body, grid=grid, in_specs=in_specs, out_specs=[y_spec, st_spec], out_shape=[y_shape, st_shape],
        input_output_aliases=aliases, compiler_params=_params(("arbitrary",)), name=name,
    )(*args)


def _retention(proj, state, layer, cos_p, sin_p, cos_s, sin_s, gl, gain, b, t, nb, ts, prev_s):
    n_p = b * t
    nc = t // RET_CHUNK
    prow = lambda bi, c: bi * nc + c
    const2 = lambda *a: (0, 0)
    y_p, st_p = pl.pallas_call(
        _ret_prompt_body,
        grid=(b, nc),
        in_specs=[_proj_spec(RET_CHUNK, COL_RET_Q, prow), _proj_spec(RET_CHUNK, COL_RET_K, prow),
                  _proj_spec(RET_CHUNK, COL_RET_V, prow), _proj_spec(RET_CHUNK, COL_RET_G, prow),
                  pl.BlockSpec((RET_CHUNK, DH), lambda bi, c: (c, 0)),
                  pl.BlockSpec((RET_CHUNK, DH), lambda bi, c: (c, 0)),
                  pl.BlockSpec((HEADS, DH), const2),
                  pl.BlockSpec((None, 1, HW), lambda bi, c: (layer, 0, 0))],
        out_specs=[pl.BlockSpec((RET_CHUNK, HW), lambda bi, c: (prow(bi, c), 0)),
                   pl.BlockSpec((1, HEADS, DH, DH), lambda bi, c: (bi, 0, 0, 0))],
        out_shape=[jax.ShapeDtypeStruct((n_p, HW), BF16),
                   jax.ShapeDtypeStruct((b, HEADS, DH, DH), F32)],
        scratch_shapes=[pltpu.VMEM((HEADS, DH, DH), F32), pltpu.VMEM((HEADS, RET_CHUNK, RET_CHUNK), F32),
                        pltpu.VMEM((HEADS, RET_CHUNK, DH), F32), pltpu.VMEM((HEADS, RET_CHUNK, DH), F32)],
        compiler_params=_params(("arbitrary", "arbitrary")),
        name="retention_prompt",
    )(proj, proj, proj, proj, cos_p, sin_p, gl, gain)

    rows = SAMPLE_BB * ts
    base = n_p // rows
    srow = lambda i: base + i
    y_s, st_s = _sample_state_call(
        functools.partial(_ret_sample_body, ts=ts), (nb // SAMPLE_BB,),
        [_proj_spec(rows, COL_RET_Q, srow), _proj_spec(rows, COL_RET_K, srow),
         _proj_spec(rows, COL_RET_V, srow), _proj_spec(rows, COL_RET_G, srow),
         pl.BlockSpec((rows, DH), const2), pl.BlockSpec((rows, DH), const2),
         pl.BlockSpec((HEADS, DH), const2),
         pl.BlockSpec((None, 1, HW), lambda i: (layer, 0, 0)),
         pl.BlockSpec((None, SAMPLE_BB, HEADS, DH, DH), lambda i: (layer, i, 0, 0, 0))],
        (proj, proj, proj, proj, cos_s, sin_s, gl, gain, state),
        jax.ShapeDtypeStruct((nb * ts, HW), BF16), pl.BlockSpec((rows, HW), lambda i: (i, 0)),
        layer, nb, prev_s, "retention_sample")
    return (y_p, y_s), st_p, st_s


def _mask_sums(masks, x):
    hi = x.astype(BF16)
    rest = x - hi.astype(F32)
    mid = rest.astype(BF16)
    lo = (rest - mid.astype(F32)).astype(BF16)
    m = jnp.concatenate([mk.astype(BF16) for mk in masks], axis=0)
    dot = functools.partial(jnp.dot, preferred_element_type=F32)
    out = dot(m, hi) + (dot(m, mid) + dot(m, lo))
    rows = masks[0].shape[0]
    return [out[i * rows:(i + 1) * rows] for i in range(len(masks))]


def _hg_prepare(hq_ref, hf_ref, lbt_ref, rows, chunk, with_prefix=False):
    shift = chunk.bit_length() - 1
    ri = lax.broadcasted_iota(I32, (rows, rows), 0)
    ci = lax.broadcasted_iota(I32, (rows, rows), 1)
    same = (ri >> shift) == (ci >> shift)
    causal = same & (ci <= ri)
    z = hf_ref[...]
    log_lb = lbt_ref[0:1, :]
    log_1m_lb = lbt_ref[1:2, :]
    one_m_lb = lbt_ref[2:3, :]
    log_sig = jnp.minimum(z, 0.0) - jnp.log(1.0 + jnp.exp(-jnp.abs(z)))
    bterm = log_1m_lb + log_sig
    logf = jnp.maximum(log_lb, bterm) + jnp.log(1.0 + jnp.exp(-jnp.abs(log_lb - bterm)))
    kh = one_m_lb * jax.nn.sigmoid(-z)
    qh = _silu(hq_ref[...]) * (DH ** -0.5)
    masks = [causal, same] + ([(ci >> shift) < (ri >> shift)] if with_prefix else [])
    cum, tot, *pre = _mask_sums(masks, logf)
    qi = qh * jnp.exp(cum)
    ki = kh * jnp.exp(-cum)
    ke = kh * jnp.exp(tot - cum)
    return causal, qi, ki, ke, tot, (pre[0] if with_prefix else None)


def _rms_norm_gate(o, gain, gate):
    return o * lax.rsqrt(jnp.mean(jnp.square(o), axis=-1, keepdims=True) + LN_EPS) * gain * _silu(gate)


def _hg_prompt_body(hq_ref, hf_ref, hi_ref, hg_ref, lbt_ref, gain_ref, y_ref, st_ref, s_scr):
    c = pl.program_id(1)

    @pl.when(c == 0)
    def _():
        s_scr[...] = jnp.zeros_like(s_scr)

    rows = RET_CHUNK
    n_sub = rows // HG_CHUNK
    shift = HG_CHUNK.bit_length() - 1
    causal, qi, ki, ke, tot, pre = _hg_prepare(hq_ref, hf_ref, lbt_ref, rows, HG_CHUNK, with_prefix=True)
    sub = lax.broadcasted_iota(I32, (rows, DH), 0) >> shift
    v = hi_ref[...]
    for h in range(HEADS):
        sl = slice(h * DH, (h + 1) * DH)
        q_h, ke_h, v_h, pre_h = qi[:, sl], ke[:, sl], v[:, sl], pre[:, sl]
        att = jnp.where(causal, _bdot_nt(q_h, ki[:, sl]), 0.0)
        st0 = s_scr[h]
        o = _bdot(att, v_h) + _bdot_nt(q_h * jnp.exp(pre_h), st0)
        end_last = pre_h[rows - 1:rows] + tot[rows - 1:rows, sl]
        st = st0 * jnp.exp(end_last)
        for i in range(n_sub):
            rs = slice(i * HG_CHUNK, (i + 1) * HG_CHUNK)
            u_t = _bdot_tn(v_h[rs], ke_h[rs])
            if i + 1 < n_sub:
                end_i = pre_h[(i + 1) * HG_CHUNK:(i + 1) * HG_CHUNK + 1]
                later = q_h * jnp.exp(jnp.where(sub > i, pre_h - end_i, -jnp.inf))
                o = o + _bdot_nt(later, u_t)
                st = st + u_t * jnp.exp(end_last - end_i)
            else:
                st = st + u_t
        s_scr[h] = st
        y_ref[:, sl] = _rms_norm_gate(o, gain_ref[:, sl], hg_ref[:, sl]).astype(BF16)

    @pl.when(c == pl.num_programs(1) - 1)
    def _():
        for h in range(HEADS):
            st_ref[0, h] = s_scr[h].T


def _hg_sample_body(hq_ref, hf_ref, hi_ref, hg_ref, lbt_ref, gain_ref, sin_ref_state,
                    y_ref, st_ref, *, ts):
    rows = SAMPLE_BB * ts
    causal, qi, ki, ke, tot, _ = _hg_prepare(hq_ref, hf_ref, lbt_ref, rows, ts)
    etot = jnp.exp(tot)
    v = hi_ref[...]
    for h in range(HEADS):
        sl = slice(h * DH, (h + 1) * DH)
        att = jnp.where(causal, _bdot_nt(qi[:, sl], ki[:, sl]), 0.0)
        o_intra = _bdot(att, v[:, sl])
        outs = []
        for j in range(SAMPLE_BB):
            rs = slice(j * ts, (j + 1) * ts)
            s = sin_ref_state[j, h]
            outs.append(o_intra[rs] + _bdot(qi[rs, sl], s))
            scale = jnp.broadcast_to(etot[j * ts:j * ts + 1, sl], (DH, DH)).T
            new_state = s * scale + _bdot_tn(ke[rs, sl], v[rs, sl])
            for slot in range(st_ref.shape[0]):
                st_ref[slot, j, h] = new_state
        o = jnp.concatenate(outs, axis=0)
        y_ref[:, sl] = _rms_norm_gate(o, gain_ref[:, sl], hg_ref[:, sl]).astype(BF16)


def _hgrn(proj, state, layer, lbt, gain, b, t, nb, ts, prev_s):
    n_p = b * t
    nc = t // RET_CHUNK
    prow = lambda bi, c: bi * nc + c
    y_p, st_p = pl.pallas_call(
        _hg_prompt_body,
        grid=(b, nc),
        in_specs=[_proj_spec(RET_CHUNK, COL_HG_Q, prow), _proj_spec(RET_CHUNK, COL_HG_F, prow),
                  _proj_spec(RET_CHUNK, COL_HG_I, prow), _proj_spec(RET_CHUNK, COL_HG_G, prow),
                  pl.BlockSpec((None, SUBLANES, HW), lambda bi, c: (layer, 0, 0)),
                  pl.BlockSpec((None, 1, HW), lambda bi, c: (layer, 0, 0))],
        out_specs=[pl.BlockSpec((RET_CHUNK, HW), lambda bi, c: (prow(bi, c), 0)),
                   pl.BlockSpec((1, HEADS, DH, DH), lambda bi, c: (bi, 0, 0, 0))],
        out_shape=[jax.ShapeDtypeStruct((n_p, HW), BF16),
                   jax.ShapeDtypeStruct((b, HEADS, DH, DH), F32)],
        scratch_shapes=[pltpu.VMEM((HEADS, DH, DH), F32)],
        compiler_params=_params(("arbitrary", "arbitrary")),
        name="hgrn_prompt",
    )(proj, proj, proj, proj, lbt, gain)

    rows = SAMPLE_BB * ts
    base = n_p // rows
    srow = lambda i: base + i
    y_s, st_s = _sample_state_call(
        functools.partial(_hg_sample_body, ts=ts), (nb // SAMPLE_BB,),
        [_proj_spec(rows, COL_HG_Q, srow), _proj_spec(rows, COL_HG_F, srow),
         _proj_spec(rows, COL_HG_I, srow), _proj_spec(rows, COL_HG_G, srow),
         pl.BlockSpec((None, SUBLANES, HW), lambda i: (layer, 0, 0)),
         pl.BlockSpec((None, 1, HW), lambda i: (layer, 0, 0)),
         pl.BlockSpec((None, SAMPLE_BB, HEADS, DH, DH), lambda i: (layer, i, 0, 0, 0))],
        (proj, proj, proj, proj, lbt, gain, state),
        jax.ShapeDtypeStruct((nb * ts, HW), BF16), pl.BlockSpec((rows, HW), lambda i: (i, 0)),
        layer, nb, prev_s, "hgrn_sample")
    return (y_p, y_s), st_p, st_s


def _softmax_rows(s):
    e = jnp.exp(s - jnp.max(s, axis=-1, keepdims=True))
    return e / jnp.sum(e, axis=-1, keepdims=True)


def _xa_prompt_body(q_ref, k_ref, v_ref, y_ref):
    for h in range(HEADS):
        sl = slice(h * DH, (h + 1) * DH)
        a = _softmax_rows(_bdot_nt(q_ref[:, sl] * (DH ** -0.5), k_ref[:, sl]))
        y_ref[:, sl] = _bdot(a, v_ref[:, sl]).astype(BF16)


def _xa_sample_body(q_ref, k_ref, v_ref, y_ref, *, ts):
    n_mem = k_ref.shape[1] // HEADS
    pairs = [(j, h) for j in range(SAMPLE_BB) for h in range(HEADS)]
    q = q_ref[...] * (DH ** -0.5)
    scores = [_bdot_nt(q[j * ts:(j + 1) * ts, h * DH:(h + 1) * DH], k_ref[j, pl.ds(h, n_mem, stride=HEADS), :])
              for j, h in pairs]
    a = _softmax_rows(jnp.concatenate(scores, axis=0))
    for n, (j, h) in enumerate(pairs):
        y = _bdot(a[n * ts:(n + 1) * ts], v_ref[j, pl.ds(h, n_mem, stride=HEADS), :])
        y_ref[j * ts:(j + 1) * ts, h * DH:(h + 1) * DH] = y.astype(BF16)


def _cross_attention(proj, kv_p, cache_k, cache_v, layer, b, t, nb, ts):
    n_p = b * t
    n_mem = kv_p.shape[0] // b
    tq = _pick(t, (512, 256, 128))
    nq = t // tq
    y_p = pl.pallas_call(
        _xa_prompt_body,
        grid=(b, nq),
        in_specs=[_proj_spec(tq, COL_XA_Q, lambda bi, qi: bi * nq + qi),
                  pl.BlockSpec((n_mem, HW), lambda bi, qi: (bi, 0)),
                  pl.BlockSpec((n_mem, HW), lambda bi, qi: (bi, 1))],
        out_specs=pl.BlockSpec((tq, HW), lambda bi, qi: (bi * nq + qi, 0)),
        out_shape=jax.ShapeDtypeStruct((n_p, HW), BF16),
        compiler_params=_params(("arbitrary", "arbitrary")),
        name="xattn_prompt",
    )(proj, kv_p, kv_p)

    rows = SAMPLE_BB * ts
    base = n_p // rows
    cache_k = cache_k.reshape(DEPTH, nb, n_mem * HEADS, DH)
    cache_v = cache_v.reshape(DEPTH, nb, n_mem * HEADS, DH)
    kv_spec = pl.BlockSpec((None, SAMPLE_BB, n_mem * HEADS, DH), lambda i: (layer, i, 0, 0))
    y_s = pl.pallas_call(
        functools.partial(_xa_sample_body, ts=ts),
        grid=(nb // SAMPLE_BB,),
        in_specs=[_proj_spec(rows, COL_XA_Q, lambda i: base + i), kv_spec, kv_spec],
        out_specs=pl.BlockSpec((rows, HW), lambda i: (i, 0)),
        out_shape=jax.ShapeDtypeStruct((nb * ts, HW), BF16),
        compiler_params=_params(("arbitrary",)),
        name="xattn_sample",
    )(proj, cache_k, cache_v)
    return (y_p, y_s)


def _layer_norm(tv, g, b):
    mu = jnp.mean(tv, axis=-1, keepdims=True)
    var = jnp.mean(jnp.square(tv - mu), axis=-1, keepdims=True)
    return (tv - mu) * lax.rsqrt(var + LN_EPS) * g + b


def _merge_body(yrp_ref, yrs_ref, yhp_ref, yhs_ref, yxp_ref, yxs_ref, g0_ref, g1_ref, g2_ref, xp_ref, xs_ref,
                wr_ref, wh_ref, wx_ref, wo_ref, lg_ref, lb_ref,
                x1_ref, x1b_ref, x1t_ref, wr_s, wh_s, wx_s, wo_s, *, prompt_tiles):
    @pl.when(pl.program_id(0) == 0)
    def _():
        wr_s[...] = wr_ref[...].astype(BF16)
        wh_s[...] = wh_ref[...].astype(BF16)
        wx_s[...] = wx_ref[...].astype(BF16)
        wo_s[...] = wo_ref[...].astype(BF16)

    is_prompt = pl.program_id(0) < prompt_tiles

    def branch(yp_ref, ys_ref, w_s, gate_ref):
        y = jnp.where(is_prompt, yp_ref[...], ys_ref[...])
        return gate_ref[...].astype(F32) * jnp.dot(y, w_s[...], preferred_element_type=F32)

    m = (branch(yrp_ref, yrs_ref, wr_s, g0_ref) + branch(yhp_ref, yhs_ref, wh_s, g1_ref)
         + branch(yxp_ref, yxs_ref, wx_s, g2_ref))
    hmix = jnp.dot(m.astype(BF16), wo_s[...], preferred_element_type=F32)
    x = jnp.where(is_prompt, xp_ref[...], xs_ref[...])
    x1 = _layer_norm(DN_ALPHA * x + hmix, lg_ref[...], lb_ref[...])
    x1_ref[...] = x1
    x1b_ref[...] = x1.astype(BF16)
    tm = x1.shape[0]
    bits = lax.bitcast_convert_type(x1.astype(BF16).astype(F32), U32)
    half = D_MODEL // 2
    packed = (bits[:, :half] >> 16) | (bits[:, half:] & jnp.uint32(0xFFFF0000))
    for s in range(PACKED_TILES):
        x1t_ref[pl.ds(s, tm, stride=PACKED_TILES), :] = packed[:, s * LANES:(s + 1) * LANES]


def _merge(ys, gates, x, w_up_ret, w_up_hgrn, w_up_xattn, w_out, ln_g, ln_b, layer):
    n_p = ys[0].shape[0]
    nt = n_p + ys[1].shape[0]
    tm = _pick(nt, (256, 128))
    assert n_p % tm == 0 and ys[1].shape[0] % tm == 0
    p_tiles = n_p // tm
    row = lambda i: (i, 0)
    p_map = lambda i: (jnp.minimum(i, p_tiles - 1), 0)
    s_map = lambda i: (jnp.maximum(i - p_tiles, 0), 0)
    y_specs = [pl.BlockSpec((tm, HW), p_map), pl.BlockSpec((tm, HW), s_map)] * 3
    wspec = lambda k: pl.BlockSpec((None, k, D_MODEL), lambda i: (layer, 0, 0))
    vec = pl.BlockSpec((None, 1, D_MODEL), lambda i: (layer, 0, 0))
    gate_specs = [pl.BlockSpec((tm, D_MODEL), lambda i, c=c: (i, c)) for c in range(3)]
    return pl.pallas_call(
        functools.partial(_merge_body, prompt_tiles=p_tiles),
        grid=(nt // tm,),
        in_specs=y_specs + gate_specs + [pl.BlockSpec((tm, D_MODEL), p_map), pl.BlockSpec((tm, D_MODEL), s_map),
                  wspec(HW), wspec(HW), wspec(HW), wspec(D_MODEL), vec, vec],
        out_specs=[pl.BlockSpec((tm, D_MODEL), row), pl.BlockSpec((tm, D_MODEL), row),
                   pl.BlockSpec((tm * PACKED_TILES, LANES), row)],
        out_shape=[jax.ShapeDtypeStruct((nt, D_MODEL), F32),
                   jax.ShapeDtypeStruct((nt, D_MODEL), BF16),
                   jax.ShapeDtypeStruct((nt * PACKED_TILES, LANES), U32)],
        scratch_shapes=[pltpu.VMEM((HW, D_MODEL), BF16)] * 3 + [pltpu.VMEM((D_MODEL, D_MODEL), BF16)],
        compiler_params=_params(("arbitrary",)),
        name="merge_out_ln1",
    )(*ys, *([gates] * 3), *x, w_up_ret, w_up_hgrn, w_up_xattn, w_out, ln_g, ln_b)


def _router_body(x_ref, wt_ref, b_ref, eidx_ref, wn_ref):
    tm = x_ref.shape[0]
    x = x_ref[...]
    w = wt_ref[...]
    xh = x.astype(BF16)
    xl = (x - xh.astype(F32)).astype(BF16)
    wh = w.astype(BF16)
    wl = (w - wh.astype(F32)).astype(BF16)
    logits = _bdot_nt(wh, xh) + (_bdot_nt(wh, xl) + _bdot_nt(wl, xh))
    s = jax.nn.sigmoid(logits)
    sel = s + b_ref[...]
    neg = -jnp.inf
    groups = [sel[g * GROUP_SIZE:(g + 1) * GROUP_SIZE, :] for g in range(N_GROUPS)]
    ie = lax.broadcasted_iota(I32, (GROUP_SIZE, tm), 0).astype(F32)
    rows = []
    for blk in groups:
        m1 = jnp.max(blk, axis=0, keepdims=True)
        first = jnp.min(jnp.where(blk == m1, ie, float(GROUP_SIZE)), axis=0, keepdims=True)
        rows.append(m1 + jnp.max(jnp.where(ie == first, neg, blk), axis=0, keepdims=True))
    gscore = jnp.concatenate(rows, axis=0)
    ig = lax.broadcasted_iota(I32, gscore.shape, 0).astype(F32)
    gmask = jnp.zeros(gscore.shape, F32)
    for _ in range(TOPK_GROUPS):
        m = jnp.max(gscore, axis=0, keepdims=True)
        gi = jnp.min(jnp.where(gscore == m, ig, float(N_GROUPS)), axis=0, keepdims=True)
        hit = ig == gi
        gmask = jnp.where(hit, 1.0, gmask)
        gscore = jnp.where(hit, neg, gscore)
    masked = jnp.concatenate([jnp.where(gmask[g:g + 1, :] > 0.5, blk, neg)
                              for g, blk in enumerate(groups)], axis=0)
    ix = lax.broadcasted_iota(I32, masked.shape, 0).astype(F32)
    idxs, ws = [], []
    for _ in range(TOP_K):
        m = jnp.max(masked, axis=0, keepdims=True)
        ei = jnp.min(jnp.where(masked == m, ix, float(N_EXPERTS)), axis=0, keepdims=True)
        hit = ix == ei
        idxs.append(ei)
        ws.append(jnp.sum(jnp.where(hit, s, 0.0), axis=0, keepdims=True))
        masked = jnp.where(hit, neg, masked)
    wsum = ws[0]
    for w in ws[1:]:
        wsum = wsum + w
    pad = [jnp.zeros((1, tm), F32)] * (SUBLANES - TOP_K)
    eidx_ref[...] = jnp.concatenate(idxs + pad, axis=0).astype(I32)
    wn_ref[...] = jnp.concatenate([w / wsum * ROUTED_SCALE for w in ws] + pad, axis=0)


def _router(x1, w_router_t, b_router, layer):
    nt = x1.shape[0]
    tm = _pick(nt, (512, 256, 128))
    return pl.pallas_call(
        _router_body,
        grid=(nt // tm,),
        in_specs=[pl.BlockSpec((tm, D_MODEL), lambda i: (i, 0)),
                  pl.BlockSpec((None, N_EXPERTS, D_MODEL), lambda i: (layer, 0, 0)),
                  pl.BlockSpec((None, N_EXPERTS, 1), lambda i: (layer, 0, 0))],
        out_specs=[pl.BlockSpec((SUBLANES, tm), lambda i: (0, i))] * 2,
        out_shape=[jax.ShapeDtypeStruct((SUBLANES, nt), I32),
                   jax.ShapeDtypeStruct((SUBLANES, nt), F32)],
        compiler_params=_params(("arbitrary",)),
        name="moe_router",
    )(x1, w_router_t, b_router)


def _positions_body(eidx_ref, pos_ref, cnt_ref, off_ref, base_scr, off_scr):
    phase = pl.program_id(0)
    i = pl.program_id(1)
    tp = eidx_ref.shape[1]
    ix = lax.broadcasted_iota(I32, (N_EXPERTS, tp), 0)
    eidx = eidx_ref[...]
    member = jnp.zeros((N_EXPERTS, tp), F32)
    for k in range(TOP_K):
        member = member + (ix == eidx[k:k + 1, :]).astype(F32)
    tile_cnt = jnp.sum(member, axis=1, keepdims=True)

    @pl.when((phase == 0) & (i == 0))
    def _():
        base_scr[...] = jnp.zeros_like(base_scr)

    @pl.when((phase == 1) & (i == 0))
    def _():
        cnt = base_scr[...]
        er = lax.broadcasted_iota(I32, (N_EXPERTS, N_EXPERTS), 0)
        ec = lax.broadcasted_iota(I32, (N_EXPERTS, N_EXPERTS), 1)
        off = jnp.dot((ec < er).astype(F32), cnt, precision=HIGHEST, preferred_element_type=F32)
        off_scr[...] = off
        cnt_ref[...] = cnt
        off_ref[...] = off
        base_scr[...] = jnp.zeros_like(base_scr)

    @pl.when(phase == 1)
    def _():
        tr = lax.broadcasted_iota(I32, (tp, tp), 0)
        tc = lax.broadcasted_iota(I32, (tp, tp), 1)
        before = jnp.dot(member.astype(BF16), (tr < tc).astype(BF16), preferred_element_type=F32)
        where_to = before + (off_scr[...] + base_scr[...])[:, 0:1]
        rows = [jnp.sum(jnp.where(ix == eidx[k:k + 1, :], where_to, 0.0), axis=0, keepdims=True)
                for k in range(TOP_K)]
        rows += [jnp.zeros((1, tp), F32)] * (SUBLANES - TOP_K)
        pos_ref[...] = jnp.concatenate(rows, axis=0).astype(I32)

    base_scr[...] = base_scr[...] + tile_cnt


def _positions(eidx):
    nt = eidx.shape[1]
    tp = _pick(nt, (512, 256, 128))
    const = lambda p, i: (0, 0)
    return pl.pallas_call(
        _positions_body,
        grid=(2, nt // tp),
        in_specs=[pl.BlockSpec((SUBLANES, tp), lambda p, i: (0, i))],
        out_specs=[pl.BlockSpec((SUBLANES, tp), lambda p, i: (0, i * p)),
                   pl.BlockSpec((N_EXPERTS, LANES), const), pl.BlockSpec((N_EXPERTS, LANES), const)],
        out_shape=[jax.ShapeDtypeStruct((SUBLANES, nt), I32),
                   jax.ShapeDtypeStruct((N_EXPERTS, LANES), F32),
                   jax.ShapeDtypeStruct((N_EXPERTS, LANES), F32)],
        scratch_shapes=[pltpu.VMEM((N_EXPERTS, LANES), F32), pltpu.VMEM((N_EXPERTS, LANES), F32)],
        compiler_params=_params(("arbitrary", "arbitrary")),
        name="moe_positions",
    )(eidx)


T_TILE, T_EXPERT, T_LO, T_HI, T_FRESH, T_NEWEXP = range(6)

def _table_body(cnt_ref, off_ref, tbl_ref, *, tile_rows):
    te = float(tile_rows)
    n = tbl_ref.shape[1]
    cnt = cnt_ref[...]
    off = off_ref[...]
    first = jnp.floor(off * (1.0 / te))
    last = jnp.floor((off + cnt - 1.0) * (1.0 / te))
    nst = jnp.where(cnt > 0.0, last - first + 1.0, 0.0)
    er = lax.broadcasted_iota(I32, (N_EXPERTS, N_EXPERTS), 0)
    ec = lax.broadcasted_iota(I32, (N_EXPERTS, N_EXPERTS), 1)
    s_end = jnp.dot((ec <= er).astype(F32), nst, precision=HIGHEST, preferred_element_type=F32)
    s_beg = s_end - nst
    total = s_end[N_EXPERTS - 1:N_EXPERTS, 0:1]
    sidx = lax.broadcasted_iota(I32, (1, n), 1).astype(F32)
    s = jnp.minimum(sidx, total - 1.0)
    e_s = jnp.sum((s_end[:, 0:1] <= s).astype(F32), axis=0, keepdims=True)
    hot = lax.broadcasted_iota(I32, (N_EXPERTS, n), 0).astype(F32) == e_s

    def pick(col):
        return jnp.sum(jnp.where(hot, col[:, 0:1], 0.0), axis=0, keepdims=True)

    tile = pick(first) + s - pick(s_beg)
    valid = sidx < total
    o, c = pick(off), pick(cnt)
    lo = jnp.where(valid, jnp.maximum(o, tile * te), 0.0)
    hi = jnp.where(valid, jnp.minimum(o + c, (tile + 1.0) * te), 0.0)
    head = sidx == 0.0
    fresh = jnp.where((tile != pltpu.roll(tile, 1, 1)) | head, 1.0, 0.0)
    newexp = jnp.where((e_s != pltpu.roll(e_s, 1, 1)) | head, 1.0, 0.0)
    pad = [jnp.zeros((1, n), F32)] * (SUBLANES - 6)
    tbl_ref[...] = jnp.concatenate([tile, e_s, lo, hi, fresh, newexp] + pad, axis=0).astype(I32)


def _step_table(cnt, off, n_rows, te):
    n_steps = n_rows // te + N_EXPERTS
    width = -(-n_steps // LANES) * LANES
    tbl = pl.pallas_call(
        functools.partial(_table_body, tile_rows=te),
        out_shape=jax.ShapeDtypeStruct((SUBLANES, width), I32),
        name="moe_step_table",
    )(cnt, off)
    return tbl, n_steps


def _wait_tile_rows(like_src, dst_rows_ref, sem_ref):
    rows = like_src.shape[0]
    for _ in range(TOP_K):
        pltpu.make_async_copy(like_src, dst_rows_ref.at[pl.ds(0, rows)], sem_ref).wait()


def _dispatch_body(pos_ref, xt_ref, xb_ref, wsg_ref, wsu_ref, wsd_ref, xs_ref, sh_ref,
                   pos_s, wsg_s, wsu_s, wsd_s, sem_p, sem):
    td = pos_ref.shape[1]

    @pl.when(pl.program_id(0) == 0)
    def _():
        wsg_s[...] = wsg_ref[...].astype(BF16)
        wsu_s[...] = wsu_ref[...].astype(BF16)
        wsd_s[...] = wsd_ref[...].astype(BF16)

    cp = pltpu.make_async_copy(pos_ref, pos_s, sem_p)
    cp.start()
    cp.wait()

    def issue(g, carry):
        for u in range(ISSUE_UNROLL):
            r = g * ISSUE_UNROLL + u
            for k in range(TOP_K):
                pltpu.make_async_copy(xt_ref.at[r], xs_ref.at[pos_s[k, r]], sem).start(priority=k % 2)
        return carry

    half = td // 2
    groups = half // ISSUE_UNROLL
    for part in range(2):
        lax.fori_loop(part * groups, (part + 1) * groups, issue, 0)
        rows = slice(part * half, (part + 1) * half)
        xb = xb_ref[rows, :]
        hs = (_silu(jnp.dot(xb, wsg_s[...], preferred_element_type=F32))
              * jnp.dot(xb, wsu_s[...], preferred_element_type=F32))
        sh_ref[rows, :] = jnp.dot(hs.astype(BF16), wsd_s[...], preferred_element_type=F32)
    _wait_tile_rows(xt_ref, xs_ref, sem)


def _dispatch(pos, x1t, x1b, w_s_gate, w_s_up, w_s_down, layer):
    nt = x1t.shape[0]
    td = _pick(nt, (512, 256, 128))
    d_sh = w_s_gate.shape[2]
    w_in_spec = pl.BlockSpec((None, D_MODEL, d_sh), lambda i: (layer, 0, 0))
    return pl.pallas_call(
        _dispatch_body,
        grid=(nt // td,),
        in_specs=[pl.BlockSpec((SUBLANES, td), lambda i: (0, i)),
                  pl.BlockSpec((td,) + x1t.shape[1:], lambda i: (i, 0, 0)),
                  pl.BlockSpec((td, D_MODEL), lambda i: (i, 0)),
                  w_in_spec, w_in_spec, pl.BlockSpec((None, d_sh, D_MODEL), lambda i: (layer, 0, 0))],
        out_specs=[pl.BlockSpec(memory_space=pl.ANY), pl.BlockSpec((td, D_MODEL), lambda i: (i, 0))],
        out_shape=[jax.ShapeDtypeStruct((nt * TOP_K,) + x1t.shape[1:], x1t.dtype),
                   jax.ShapeDtypeStruct((nt, D_MODEL), F32)],
        scratch_shapes=[pltpu.SMEM((SUBLANES, td), I32),
                        pltpu.VMEM((D_MODEL, d_sh), BF16), pltpu.VMEM((D_MODEL, d_sh), BF16),
                        pltpu.VMEM((d_sh, D_MODEL), BF16),
                        pltpu.SemaphoreType.DMA, pltpu.SemaphoreType.DMA],
        compiler_params=_params(("arbitrary",)),
        name="moe_dispatch",
    )(pos, x1t, x1b, w_s_gate, w_s_up, w_s_down)


def _experts_body(tbl_ref, xs_ref, wg_ref, wu_ref, wd_ref, ye_ref, wg_s, wu_s, wd_s):
    s = pl.program_id(0)
    te = xs_ref.shape[0] // PACKED_TILES
    lo = tbl_ref[T_LO, s]
    hi = tbl_ref[T_HI, s]

    @pl.when(tbl_ref[T_NEWEXP, s] == 1)
    def _():
        wg_s[...] = wg_ref[...].astype(BF16)
        wu_s[...] = wu_ref[...].astype(BF16)
        wd_s[...] = wd_ref[...].astype(BF16)

    @pl.when(tbl_ref[T_FRESH, s] == 1)
    def _():
        ye_ref[...] = jnp.zeros_like(ye_ref)

    @pl.when(hi > lo)
    def _():
        words = [xs_ref[pl.ds(t, te, stride=PACKED_TILES), :] for t in range(PACKED_TILES)]
        low = [lax.bitcast_convert_type(w << 16, F32).astype(BF16) for w in words]
        high = [lax.bitcast_convert_type(w & jnp.uint32(0xFFFF0000), F32).astype(BF16) for w in words]
        x = jnp.concatenate(low + high, axis=-1)
        g = jnp.dot(x, wg_s[...], preferred_element_type=F32)
        u = jnp.dot(x, wu_s[...], preferred_element_type=F32)
        y = jnp.dot((_silu(g) * u).astype(BF16), wd_s[...], preferred_element_type=F32)
        row = tbl_ref[T_TILE, s] * te + lax.broadcasted_iota(I32, (te, LANES), 0)
        mine = (row >= lo) & (row < hi)
        for t in range(ROW_TILES):
            sl = pl.ds(t, te, stride=ROW_TILES)
            ye_ref[sl, :] = jnp.where(mine, y[:, t * LANES:(t + 1) * LANES], ye_ref[sl, :])


def _experts(tbl, n_steps, te, xs, w_gate, w_up, w_down, layer):
    n_rows = xs.shape[0] // PACKED_TILES
    tile_map = lambda s, tbl: (tbl[T_TILE, s], 0)
    w_map = lambda s, tbl: (layer, tbl[T_EXPERT, s], 0, 0)
    w_in_spec = pl.BlockSpec((None, None, D_MODEL, D_EXPERT), w_map)
    w_dn_spec = pl.BlockSpec((None, None, D_EXPERT, D_MODEL), w_map)
    return pl.pallas_call(
        _experts_body,
        grid_spec=pltpu.PrefetchScalarGridSpec(
            num_scalar_prefetch=1,
            grid=(n_steps,),
            in_specs=[pl.BlockSpec((te * PACKED_TILES, LANES), tile_map), w_in_spec, w_in_spec, w_dn_spec],
            out_specs=pl.BlockSpec((te * ROW_TILES, LANES), tile_map),
            scratch_shapes=[pltpu.VMEM((D_MODEL, D_EXPERT), BF16), pltpu.VMEM((D_MODEL, D_EXPERT), BF16),
                            pltpu.VMEM((D_EXPERT, D_MODEL), BF16)]),
        out_shape=jax.ShapeDtypeStruct((n_rows * ROW_TILES, LANES), F32),
        compiler_params=_params(("arbitrary",)),
        name="moe_experts",
    )(tbl, xs, w_gate, w_up, w_down)


def _combine_body(pos_ref, wn_ref, ye_ref, x1_ref, sh_ref, lg_ref, lb_ref,
                  x2p_ref, x2s_ref, x2b_ref, pos_s, buf, sem_p, sem, *, prompt_tiles):
    i = pl.program_id(0)
    n = pl.num_programs(0)
    tc = wn_ref.shape[1]
    slot = i % 2
    tile_rows = tc * ROW_TILES

    def request(tile, into):
        cp = pltpu.make_async_copy(pos_ref.at[tile], pos_s, sem_p)
        cp.start()
        cp.wait()

        def issue(g, carry):
            for u in range(ISSUE_UNROLL):
                r = g * ISSUE_UNROLL + u
                for k in range(TOP_K):
                    at = pl.multiple_of(((into * TOP_K + k) * tc + r) * ROW_TILES, ROW_TILES)
                    pltpu.make_async_copy(ye_ref.at[pos_s[k, r]], buf.at[pl.ds(at, ROW_TILES)],
                                          sem.at[into]).start(priority=k % 2)
            return carry

        lax.fori_loop(0, tc // ISSUE_UNROLL, issue, 0)

    @pl.when(i == 0)
    def _():
        request(0, 0)

    @pl.when(i + 1 < n)
    def _():
        request(i + 1, 1 - slot)

    for k in range(TOP_K):
        pltpu.make_async_copy(buf.at[pl.ds(0, tile_rows)], buf.at[pl.ds(tile_rows, tile_rows)], sem.at[slot]).wait()

    w = wn_ref[...]
    acc = [None] * ROW_TILES
    for k in range(TOP_K):
        wcol = jnp.concatenate([jnp.broadcast_to(w[k:k + 1, c * LANES:(c + 1) * LANES], (LANES, LANES)).T
                                for c in range(tc // LANES)], axis=0)
        base = (slot * TOP_K + k) * tile_rows
        for t in range(ROW_TILES):
            term = wcol * buf[pl.ds(base + t, tc, stride=ROW_TILES), :]
            acc[t] = term if acc[t] is None else acc[t] + term
    routed = jnp.concatenate(acc, axis=-1)
    x2 = _layer_norm(DN_ALPHA * x1_ref[...] + (routed + sh_ref[...]), lg_ref[...], lb_ref[...])
    x2b_ref[...] = x2.astype(BF16)

    @pl.when(i < prompt_tiles)
    def _():
        x2p_ref[...] = x2

    @pl.when(i >= prompt_tiles)
    def _():
        x2s_ref[...] = x2


def _combine(pos, wn, ye, x1, shared, ln_g, ln_b, layer, n_p):
    nt = x1.shape[0]
    tc = _pick(nt, (256, 128))
    assert n_p % tc == 0
    n_tiles = nt // tc
    p_tiles = n_p // tc
    pos3 = pos.reshape(SUBLANES, n_tiles, tc).transpose(1, 0, 2)
    row = lambda i: (i, 0)
    vec = pl.BlockSpec((None, 1, D_MODEL), lambda i: (layer, 0, 0))
    return pl.pallas_call(
        functools.partial(_combine_body, prompt_tiles=p_tiles),
        grid=(n_tiles,),
        in_specs=[pl.BlockSpec((n_tiles, SUBLANES, tc), lambda i: (0, 0, 0)),
                  pl.BlockSpec((SUBLANES, tc), lambda i: (0, i)),
                  pl.BlockSpec(memory_space=pl.ANY),
                  pl.BlockSpec((tc, D_MODEL), row), pl.BlockSpec((tc, D_MODEL), row), vec, vec],
        out_specs=[pl.BlockSpec((tc, D_MODEL), lambda i: (jnp.minimum(i, p_tiles - 1), 0)),
                   pl.BlockSpec((tc, D_MODEL), lambda i: (jnp.maximum(i - p_tiles, 0), 0)),
                   pl.BlockSpec((tc, D_MODEL), row)],
        out_shape=[jax.ShapeDtypeStruct((n_p, D_MODEL), F32), jax.ShapeDtypeStruct((nt - n_p, D_MODEL), F32),
                   jax.ShapeDtypeStruct((nt, D_MODEL), BF16)],
        scratch_shapes=[pltpu.SMEM((SUBLANES, tc), I32),
                        pltpu.VMEM((2 * TOP_K * tc * ROW_TILES, LANES), F32),
                        pltpu.SemaphoreType.DMA, pltpu.SemaphoreType.DMA((2,))],
        compiler_params=_params(("arbitrary",)),
        name="moe_combine_ln2",
    )(pos3, wn, ye, x1, shared, ln_g, ln_b)


def _rope_tables(t, pos0):
    inv = 1.0 / (ROPE_BASE ** (jnp.arange(0, DH, 2, dtype=F32) / DH))
    ang = (jnp.arange(t, dtype=F32) + pos0)[:, None] * inv[None, :]
    cos, sin = jnp.cos(ang), jnp.sin(ang)
    return jnp.concatenate([cos, cos], axis=-1), jnp.concatenate([-sin, sin], axis=-1)


def kernel(x_prompt, x_sample, mem_prompt, state_ret, state_hgrn, cache_mem_k, cache_mem_v, w_in, w_up_ret, w_up_hgrn, w_up_xattn, w_out, w_mem_kv, ret_norm_g, hgrn_norm_g, lb_logits, ln1_g, ln1_b, ln2_g, ln2_b, w_router, b_router, w_e_gate, w_e_up, w_e_down, w_s_gate, w_s_up, w_s_down):
    b, t, d = x_prompt.shape
    nb, ts, _ = x_sample.shape
    n_mem = mem_prompt.shape[1]
    assert d == D_MODEL and t % RET_CHUNK == 0 and nb % SAMPLE_BB == 0
    assert ts & (ts - 1) == 0 and HG_CHUNK % ts == 0 and RET_CHUNK % ts == 0
    n_p, n_s = b * t, nb * ts
    nt = n_p + n_s
    assert n_p % (SAMPLE_BB * ts) == 0

    lb_cum = jnp.cumsum(jax.nn.softmax(lb_logits.astype(F32), axis=0), axis=0)
    lbs = lb_cum - lb_cum[0:1]
    lbt = jnp.stack([jnp.log(lbs), jnp.log1p(-lbs), 1.0 - lbs] + [jnp.zeros_like(lbs)] * (SUBLANES - 3), axis=1)
    gl = jnp.broadcast_to(jnp.log1p(-jnp.exp2(-5.0 - jnp.arange(HEADS, dtype=F32)))[:, None], (HEADS, DH))
    cos_p, sin_p = _rope_tables(t, 0)
    cos_s, sin_s = _rope_tables(ts, PAST_LEN)
    cos_s, sin_s = jnp.tile(cos_s, (SAMPLE_BB, 1)), jnp.tile(sin_s, (SAMPLE_BB, 1))
    vec3 = lambda a: a.reshape(DEPTH, 1, -1)
    w_router_t = jnp.swapaxes(w_router, 1, 2)
    b_router3 = b_router.reshape(DEPTH, N_EXPERTS, 1)
    mem2 = mem_prompt.reshape(b * n_mem, d)

    x = (x_prompt.reshape(n_p, d), x_sample.reshape(n_s, d))
    xb = jnp.concatenate([x[0].astype(BF16), x[1].astype(BF16)], axis=0)
    tm_proj = _pick(nt, (1024, 512, 128))
    te = _pick(nt * TOP_K, EXPERT_TILES)
    outs = {k: [] for k in ("ret_p", "hg_p", "mk", "mv")}
    ret_s = hg_s = None
    for l in range(DEPTH):
        mix_tiles = COL_GATES * HW // PROJ_TILE_N
        proj = _matmul(xb, w_in, l, tm_proj, PROJ_TILE_N, n=COL_GATES * HW)
        gates = _matmul(xb, w_in, l, tm_proj, PROJ_TILE_N, first_tile=mix_tiles, n=3 * D_MODEL, gate=True)
        kv_p = _matmul(mem2, w_mem_kv, l, _pick(b * n_mem, (1024, 512, 256)), 2 * HW)
        yr, ret_p, ret_s = _retention(proj, state_ret, l, cos_p, sin_p, cos_s, sin_s, gl,
                                      vec3(ret_norm_g), b, t, nb, ts, ret_s)
        yh, hg_p, hg_s = _hgrn(proj, state_hgrn, l, lbt, vec3(hgrn_norm_g), b, t, nb, ts, hg_s)
        yx = _cross_attention(proj, kv_p, cache_mem_k, cache_mem_v, l, b, t, nb, ts)
        x1, x1b, x1t = _merge((*yr, *yh, *yx), gates, x, w_up_ret, w_up_hgrn, w_up_xattn, w_out,
                              vec3(ln1_g), vec3(ln1_b), l)
        eidx, wn = _router(x1, w_router_t, b_router3, l)
        pos, cnt, off = _positions(eidx)
        tbl, n_steps = _step_table(cnt, off, nt * TOP_K, te)
        xs, shared = _dispatch(pos, x1t.reshape(nt, PACKED_TILES, LANES), x1b, w_s_gate, w_s_up, w_s_down, l)
        ye = _experts(tbl, n_steps, te, xs.reshape(-1, LANES), w_e_gate, w_e_up, w_e_down, l)
        ye = ye.reshape(-1, ROW_TILES, LANES)
        x_p, x_s, xb = _combine(pos, wn, ye, x1, shared, vec3(ln2_g), vec3(ln2_b), l, n_p)
        x = (x_p, x_s)
        outs["ret_p"].append(ret_p)
        outs["hg_p"].append(hg_p)
        outs["mk"].append(kv_p[:, :HW].reshape(b, n_mem, HEADS, DH))
        outs["mv"].append(kv_p[:, HW:].reshape(b, n_mem, HEADS, DH))
    return (x[0].reshape(b, t, d), x[1].reshape(nb, ts, d),
            jnp.stack(outs["ret_p"]), jnp.stack(outs["hg_p"]), jnp.stack(outs["mk"]), jnp.stack(outs["mv"]),
            ret_s, hg_s)
```

```python
import functools

import jax
import jax.numpy as jnp
from jax import lax
from jax.experimental import pallas as pl
from jax.experimental.pallas import tpu as pltpu

F32 = jnp.float32
BF16 = jnp.bfloat16
I32 = jnp.int32
HIGHEST = lax.Precision.HIGHEST

D_MODEL = 1024
DEPTH = 2
PAST_LEN = 16384
HEADS = 4
DH = 128
HW = HEADS * DH
RET_CHUNK = 128
HG_CHUNK = 16
ROPE_BASE = 10000.0
N_EXPERTS = 64
N_GROUPS = 8
GROUP_SIZE = N_EXPERTS // N_GROUPS
TOPK_GROUPS = 4
TOP_K = 6
D_EXPERT = 256
ROUTED_SCALE = 2.5
LN_EPS = 1e-5
DN_ALPHA = (2 * DEPTH) ** 0.25
N_IN = 9 * HW + 3 * D_MODEL
COL_RET_Q, COL_RET_K, COL_RET_V, COL_RET_G = 0, 1, 2, 3
COL_HG_Q, COL_HG_F, COL_HG_I, COL_HG_G = 4, 5, 6, 7
COL_XA_Q = 8
COL_GATES = 9
LANES = 128
SUBLANES = 8
ROW_TILES = D_MODEL // LANES
PACKED_TILES = ROW_TILES // 2
U32 = jnp.uint32
SAMPLE_BB = 8
EXPERT_TILES = (512, 256)
ISSUE_UNROLL = 8
PROJ_TILE_N = 1536
VMEM_LIMIT = 56 * 1024 * 1024


def _params(sem):
    return pltpu.CompilerParams(dimension_semantics=sem, vmem_limit_bytes=VMEM_LIMIT)


def _bdot(a, b):
    return jnp.dot(a.astype(BF16), b.astype(BF16), preferred_element_type=F32)


def _bdot_nt(a, b):
    return lax.dot_general(a.astype(BF16), b.astype(BF16), (((1,), (1,)), ((), ())),
                           preferred_element_type=F32)


def _bdot_tn(a, b):
    return lax.dot_general(a.astype(BF16), b.astype(BF16), (((0,), (0,)), ((), ())),
                           preferred_element_type=F32)


def _silu(x):
    return x * jax.nn.sigmoid(x)


def _pick(n, prefs):
    for p in prefs:
        if n % p == 0:
            return p
    raise ValueError(f"no tile for {n}")


def _mm_body(x_ref, w_ref, o_ref, wb_ref, *, gate):
    @pl.when(pl.program_id(1) == 0)
    def _():
        wb_ref[...] = w_ref[...].astype(BF16)

    acc = jnp.dot(x_ref[...].astype(BF16), wb_ref[...], preferred_element_type=F32)
    o_ref[...] = (jax.nn.sigmoid(acc) if gate else acc).astype(o_ref.dtype)


def _matmul(x, w, layer, tm, tn, first_tile=0, n=None, gate=False):
    m, k = x.shape
    n = w.shape[2] if n is None else n
    return pl.pallas_call(
        functools.partial(_mm_body, gate=gate),
        grid=(n // tn, m // tm),
        in_specs=[pl.BlockSpec((tm, k), lambda j, i: (i, 0)),
                  pl.BlockSpec((None, k, tn), lambda j, i: (layer, 0, first_tile + j))],
        out_specs=pl.BlockSpec((tm, tn), lambda j, i: (i, j)),
        out_shape=jax.ShapeDtypeStruct((m, n), BF16 if gate else F32),
        scratch_shapes=[pltpu.VMEM((k, tn), BF16)],
        compiler_params=_params(("arbitrary", "arbitrary")),
        name="dense_matmul",
    )(x, w)


def _rotary(x, cos, sin_signed):
    return x * cos + pltpu.roll(x, DH // 2, 1) * sin_signed


def _group_norm_gate(o, gain, gate):
    mu = jnp.mean(o, axis=-1, keepdims=True)
    var = jnp.mean(jnp.square(o - mu), axis=-1, keepdims=True)
    return (o - mu) * lax.rsqrt(var + LN_EPS) * gain * _silu(gate)


def _ret_prompt_body(q_ref, k_ref, v_ref, g_ref, cos_ref, sin_ref, gl_ref, gain_ref,
                     y_ref, st_ref, s_scr, intra_scr, qdec_scr, kdec_scr):
    c = pl.program_id(1)
    ch = RET_CHUNK

    @pl.when((pl.program_id(0) == 0) & (c == 0))
    def _():
        ri = lax.broadcasted_iota(I32, (ch, ch), 0)
        ci = lax.broadcasted_iota(I32, (ch, ch), 1)
        rel = (ri - ci).astype(F32)
        idx = lax.broadcasted_iota(I32, (ch, DH), 0).astype(F32)
        for h in range(HEADS):
            gl = gl_ref[h:h + 1, :]
            intra_scr[h] = jnp.where(rel >= 0, jnp.exp(gl * rel), 0.0)
            qdec_scr[h] = jnp.exp(gl * (idx + 1.0))
            kdec_scr[h] = jnp.exp(gl * (ch - 1.0 - idx))

    @pl.when(c == 0)
    def _():
        s_scr[...] = jnp.zeros_like(s_scr)

    cos = cos_ref[...]
    sin = sin_ref[...]
    for h in range(HEADS):
        sl = slice(h * DH, (h + 1) * DH)
        gl = gl_ref[h:h + 1, :]
        qr = _rotary(q_ref[:, sl], cos, sin)
        kr = _rotary(k_ref[:, sl], cos, sin) * (DH ** -0.5)
        v = v_ref[:, sl]
        att = _bdot_nt(qr, kr) * intra_scr[h]
        s = s_scr[h]
        o = _bdot(att, v) + _bdot(qr, s) * qdec_scr[h]
        s_scr[h] = s * jnp.exp(gl * float(ch)) + _bdot_tn(kr * kdec_scr[h], v)
        y_ref[:, sl] = _group_norm_gate(o, gain_ref[:, sl], g_ref[:, sl]).astype(BF16)

    @pl.when(c == pl.num_programs(1) - 1)
    def _():
        st_ref[0] = s_scr[...]


def _ret_sample_body(q_ref, k_ref, v_ref, g_ref, cos_ref, sin_ref, gl_ref, gain_ref, sin_ref_state,
                     y_ref, st_ref, *, ts):
    rows = SAMPLE_BB * ts
    shift = ts.bit_length() - 1
    cos = cos_ref[...]
    sin = sin_ref[...]
    ri = lax.broadcasted_iota(I32, (rows, rows), 0)
    ci = lax.broadcasted_iota(I32, (rows, rows), 1)
    rel = (ri - ci).astype(F32)
    mask = ((ri >> shift) == (ci >> shift)) & (ri >= ci)
    idx = (lax.broadcasted_iota(I32, (rows, DH), 0) & (ts - 1)).astype(F32)
    for h in range(HEADS):
        sl = slice(h * DH, (h + 1) * DH)
        gl = gl_ref[h:h + 1, :]
        qr = _rotary(q_ref[:, sl], cos, sin)
        kr = _rotary(k_ref[:, sl], cos, sin) * (DH ** -0.5)
        v = v_ref[:, sl]
        intra = jnp.where(mask, jnp.exp(gl[:, :rows] * rel), 0.0)
        o_intra = _bdot(_bdot_nt(qr, kr) * intra, v)
        q_dec = jnp.exp(gl * (idx + 1.0))
        kd = kr * jnp.exp(gl * (ts - 1.0 - idx))
        c_dec = jnp.exp(gl * float(ts))
        outs = []
        for j in range(SAMPLE_BB):
            rs = slice(j * ts, (j + 1) * ts)
            s = sin_ref_state[j, h]
            outs.append(o_intra[rs] + _bdot(qr[rs], s) * q_dec[rs])
            new_state = s * c_dec + _bdot_tn(kd[rs], v[rs])
            for slot in range(st_ref.shape[0]):
                st_ref[slot, j, h] = new_state
        o = jnp.concatenate(outs, axis=0)
        y_ref[:, sl] = _group_norm_gate(o, gain_ref[:, sl], g_ref[:, sl]).astype(BF16)


def _proj_spec(rows, col, row_map):
    return pl.BlockSpec((rows, HW), lambda *a: (row_map(*a), col))


def _sample_state_call(body, grid, in_specs, args, y_shape, y_spec, layer, nb, prev, name):
    st_shape = jax.ShapeDtypeStruct((DEPTH, nb, HEADS, DH, DH), F32)
    slots = DEPTH if prev is None else 1
    st_spec = pl.BlockSpec((slots, SAMPLE_BB, HEADS, DH, DH), lambda i: (layer, i, 0, 0, 0))
    aliases = {}
    if prev is not None:
        n_in = len(args)
        inner = body
        body = lambda *refs: inner(*refs[:n_in], *refs[n_in + 1:])
        in_specs = in_specs + [pl.BlockSpec(memory_space=pl.ANY)]
        args = args + (prev,)
        aliases = {n_in: 1}
    return pl.pallas_call(
        body, grid=grid, in_specs=in_specs, out_specs=[y_spec, st_spec], out_shape=[y_shape, st_shape],
        input_output_aliases=aliases, compiler_params=_params(("arbitrary",)), name=name,
    )(*args)


def _retention(proj, state, layer, cos_p, sin_p, cos_s, sin_s, gl, gain, b, t, nb, ts, prev_s):
    n_p = b * t
    nc = t // RET_CHUNK
    prow = lambda bi, c: bi * nc + c
    const2 = lambda *a: (0, 0)
    y_p, st_p = pl.pallas_call(
        _ret_prompt_body,
        grid=(b, nc),
        in_specs=[_proj_spec(RET_CHUNK, COL_RET_Q, prow), _proj_spec(RET_CHUNK, COL_RET_K, prow),
                  _proj_spec(RET_CHUNK, COL_RET_V, prow), _proj_spec(RET_CHUNK, COL_RET_G, prow),
                  pl.BlockSpec((RET_CHUNK, DH), lambda bi, c: (c, 0)),
                  pl.BlockSpec((RET_CHUNK, DH), lambda bi, c: (c, 0)),
                  pl.BlockSpec((HEADS, DH), const2),
                  pl.BlockSpec((None, 1, HW), lambda bi, c: (layer, 0, 0))],
        out_specs=[pl.BlockSpec((RET_CHUNK, HW), lambda bi, c: (prow(bi, c), 0)),
                   pl.BlockSpec((1, HEADS, DH, DH), lambda bi, c: (bi, 0, 0, 0))],
        out_shape=[jax.ShapeDtypeStruct((n_p, HW), BF16),
                   jax.ShapeDtypeStruct((b, HEADS, DH, DH), F32)],
        scratch_shapes=[pltpu.VMEM((HEADS, DH, DH), F32), pltpu.VMEM((HEADS, RET_CHUNK, RET_CHUNK), F32),
                        pltpu.VMEM((HEADS, RET_CHUNK, DH), F32), pltpu.VMEM((HEADS, RET_CHUNK, DH), F32)],
        compiler_params=_params(("arbitrary", "arbitrary")),
        name="retention_prompt",
    )(proj, proj, proj, proj, cos_p, sin_p, gl, gain)

    rows = SAMPLE_BB * ts
    base = n_p // rows
    srow = lambda i: base + i
    y_s, st_s = _sample_state_call(
        functools.partial(_ret_sample_body, ts=ts), (nb // SAMPLE_BB,),
        [_proj_spec(rows, COL_RET_Q, srow), _proj_spec(rows, COL_RET_K, srow),
         _proj_spec(rows, COL_RET_V, srow), _proj_spec(rows, COL_RET_G, srow),
         pl.BlockSpec((rows, DH), const2), pl.BlockSpec((rows, DH), const2),
         pl.BlockSpec((HEADS, DH), const2),
         pl.BlockSpec((None, 1, HW), lambda i: (layer, 0, 0)),
         pl.BlockSpec((None, SAMPLE_BB, HEADS, DH, DH), lambda i: (layer, i, 0, 0, 0))],
        (proj, proj, proj, proj, cos_s, sin_s, gl, gain, state),
        jax.ShapeDtypeStruct((nb * ts, HW), BF16), pl.BlockSpec((rows, HW), lambda i: (i, 0)),
        layer, nb, prev_s, "retention_sample")
    return (y_p, y_s), st_p, st_s


def _mask_sums(masks, x):
    hi = x.astype(BF16)
    rest = x - hi.astype(F32)
    mid = rest.astype(BF16)
    lo = (rest - mid.astype(F32)).astype(BF16)
    m = jnp.concatenate([mk.astype(BF16) for mk in masks], axis=0)
    dot = functools.partial(jnp.dot, preferred_element_type=F32)
    out = dot(m, hi) + (dot(m, mid) + dot(m, lo))
    rows = masks[0].shape[0]
    return [out[i * rows:(i + 1) * rows] for i in range(len(masks))]


def _hg_prepare(hq_ref, hf_ref, lbt_ref, rows, chunk, with_prefix=False):
    shift = chunk.bit_length() - 1
    ri = lax.broadcasted_iota(I32, (rows, rows), 0)
    ci = lax.broadcasted_iota(I32, (rows, rows), 1)
    same = (ri >> shift) == (ci >> shift)
    causal = same & (ci <= ri)
    z = hf_ref[...]
    log_lb = lbt_ref[0:1, :]
    log_1m_lb = lbt_ref[1:2, :]
    one_m_lb = lbt_ref[2:3, :]
    log_sig = jnp.minimum(z, 0.0) - jnp.log(1.0 + jnp.exp(-jnp.abs(z)))
    bterm = log_1m_lb + log_sig
    logf = jnp.maximum(log_lb, bterm) + jnp.log(1.0 + jnp.exp(-jnp.abs(log_lb - bterm)))
    kh = one_m_lb * jax.nn.sigmoid(-z)
    qh = _silu(hq_ref[...]) * (DH ** -0.5)
    masks = [causal, same] + ([(ci >> shift) < (ri >> shift)] if with_prefix else [])
    cum, tot, *pre = _mask_sums(masks, logf)
    qi = qh * jnp.exp(cum)
    ki = kh * jnp.exp(-cum)
    ke = kh * jnp.exp(tot - cum)
    return causal, qi, ki, ke, tot, (pre[0] if with_prefix else None)


def _rms_norm_gate(o, gain, gate):
    return o * lax.rsqrt(jnp.mean(jnp.square(o), axis=-1, keepdims=True) + LN_EPS) * gain * _silu(gate)


def _hg_prompt_body(hq_ref, hf_ref, hi_ref, hg_ref, lbt_ref, gain_ref, y_ref, st_ref, s_scr):
    c = pl.program_id(1)

    @pl.when(c == 0)
    def _():
        s_scr[...] = jnp.zeros_like(s_scr)

    rows = RET_CHUNK
    n_sub = rows // HG_CHUNK
    shift = HG_CHUNK.bit_length() - 1
    causal, qi, ki, ke, tot, pre = _hg_prepare(hq_ref, hf_ref, lbt_ref, rows, HG_CHUNK, with_prefix=True)
    sub = lax.broadcasted_iota(I32, (rows, DH), 0) >> shift
    v = hi_ref[...]
    for h in range(HEADS):
        sl = slice(h * DH, (h + 1) * DH)
        q_h, ke_h, v_h, pre_h = qi[:, sl], ke[:, sl], v[:, sl], pre[:, sl]
        att = jnp.where(causal, _bdot_nt(q_h, ki[:, sl]), 0.0)
        st0 = s_scr[h]
        o = _bdot(att, v_h) + _bdot_nt(q_h * jnp.exp(pre_h), st0)
        end_last = pre_h[rows - 1:rows] + tot[rows - 1:rows, sl]
        st = st0 * jnp.exp(end_last)
        for i in range(n_sub):
            rs = slice(i * HG_CHUNK, (i + 1) * HG_CHUNK)
            u_t = _bdot_tn(v_h[rs], ke_h[rs])
            if i + 1 < n_sub:
                end_i = pre_h[(i + 1) * HG_CHUNK:(i + 1) * HG_CHUNK + 1]
                later = q_h * jnp.exp(jnp.where(sub > i, pre_h - end_i, -jnp.inf))
                o = o + _bdot_nt(later, u_t)
                st = st + u_t * jnp.exp(end_last - end_i)
            else:
                st = st + u_t
        s_scr[h] = st
        y_ref[:, sl] = _rms_norm_gate(o, gain_ref[:, sl], hg_ref[:, sl]).astype(BF16)

    @pl.when(c == pl.num_programs(1) - 1)
    def _():
        for h in range(HEADS):
            st_ref[0, h] = s_scr[h].T


def _hg_sample_body(hq_ref, hf_ref, hi_ref, hg_ref, lbt_ref, gain_ref, sin_ref_state,
                    y_ref, st_ref, *, ts):
    rows = SAMPLE_BB * ts
    causal, qi, ki, ke, tot, _ = _hg_prepare(hq_ref, hf_ref, lbt_ref, rows, ts)
    etot = jnp.exp(tot)
    v = hi_ref[...]
    for h in range(HEADS):
        sl = slice(h * DH, (h + 1) * DH)
        att = jnp.where(causal, _bdot_nt(qi[:, sl], ki[:, sl]), 0.0)
        o_intra = _bdot(att, v[:, sl])
        outs = []
        for j in range(SAMPLE_BB):
            rs = slice(j * ts, (j + 1) * ts)
            s = sin_ref_state[j, h]
            outs.append(o_intra[rs] + _bdot(qi[rs, sl], s))
            scale = jnp.broadcast_to(etot[j * ts:j * ts + 1, sl], (DH, DH)).T
            new_state = s * scale + _bdot_tn(ke[rs, sl], v[rs, sl])
            for slot in range(st_ref.shape[0]):
                st_ref[slot, j, h] = new_state
        o = jnp.concatenate(outs, axis=0)
        y_ref[:, sl] = _rms_norm_gate(o, gain_ref[:, sl], hg_ref[:, sl]).astype(BF16)


def _hgrn(proj, state, layer, lbt, gain, b, t, nb, ts, prev_s):
    n_p = b * t
    nc = t // RET_CHUNK
    prow = lambda bi, c: bi * nc + c
    y_p, st_p = pl.pallas_call(
        _hg_prompt_body,
        grid=(b, nc),
        in_specs=[_proj_spec(RET_CHUNK, COL_HG_Q, prow), _proj_spec(RET_CHUNK, COL_HG_F, prow),
                  _proj_spec(RET_CHUNK, COL_HG_I, prow), _proj_spec(RET_CHUNK, COL_HG_G, prow),
                  pl.BlockSpec((None, SUBLANES, HW), lambda bi, c: (layer, 0, 0)),
                  pl.BlockSpec((None, 1, HW), lambda bi, c: (layer, 0, 0))],
        out_specs=[pl.BlockSpec((RET_CHUNK, HW), lambda bi, c: (prow(bi, c), 0)),
                   pl.BlockSpec((1, HEADS, DH, DH), lambda bi, c: (bi, 0, 0, 0))],
        out_shape=[jax.ShapeDtypeStruct((n_p, HW), BF16),
                   jax.ShapeDtypeStruct((b, HEADS, DH, DH), F32)],
        scratch_shapes=[pltpu.VMEM((HEADS, DH, DH), F32)],
        compiler_params=_params(("arbitrary", "arbitrary")),
        name="hgrn_prompt",
    )(proj, proj, proj, proj, lbt, gain)

    rows = SAMPLE_BB * ts
    base = n_p // rows
    srow = lambda i: base + i
    y_s, st_s = _sample_state_call(
        functools.partial(_hg_sample_body, ts=ts), (nb // SAMPLE_BB,),
        [_proj_spec(rows, COL_HG_Q, srow), _proj_spec(rows, COL_HG_F, srow),
         _proj_spec(rows, COL_HG_I, srow), _proj_spec(rows, COL_HG_G, srow),
         pl.BlockSpec((None, SUBLANES, HW), lambda i: (layer, 0, 0)),
         pl.BlockSpec((None, 1, HW), lambda i: (layer, 0, 0)),
         pl.BlockSpec((None, SAMPLE_BB, HEADS, DH, DH), lambda i: (layer, i, 0, 0, 0))],
        (proj, proj, proj, proj, lbt, gain, state),
        jax.ShapeDtypeStruct((nb * ts, HW), BF16), pl.BlockSpec((rows, HW), lambda i: (i, 0)),
        layer, nb, prev_s, "hgrn_sample")
    return (y_p, y_s), st_p, st_s


def _softmax_rows(s):
    e = jnp.exp(s - jnp.max(s, axis=-1, keepdims=True))
    return e / jnp.sum(e, axis=-1, keepdims=True)


def _xa_prompt_body(q_ref, k_ref, v_ref, y_ref):
    for h in range(HEADS):
        sl = slice(h * DH, (h + 1) * DH)
        a = _softmax_rows(_bdot_nt(q_ref[:, sl] * (DH ** -0.5), k_ref[:, sl]))
        y_ref[:, sl] = _bdot(a, v_ref[:, sl]).astype(BF16)


def _xa_sample_body(q_ref, k_ref, v_ref, y_ref, *, ts):
    n_mem = k_ref.shape[1] // HEADS
    pairs = [(j, h) for j in range(SAMPLE_BB) for h in range(HEADS)]
    q = q_ref[...] * (DH ** -0.5)
    scores = [_bdot_nt(q[j * ts:(j + 1) * ts, h * DH:(h + 1) * DH], k_ref[j, pl.ds(h, n_mem, stride=HEADS), :])
              for j, h in pairs]
    a = _softmax_rows(jnp.concatenate(scores, axis=0))
    for n, (j, h) in enumerate(pairs):
        y = _bdot(a[n * ts:(n + 1) * ts], v_ref[j, pl.ds(h, n_mem, stride=HEADS), :])
        y_ref[j * ts:(j + 1) * ts, h * DH:(h + 1) * DH] = y.astype(BF16)


def _cross_attention(proj, kv_p, cache_k, cache_v, layer, b, t, nb, ts):
    n_p = b * t
    n_mem = kv_p.shape[0] // b
    tq = _pick(t, (512, 256, 128))
    nq = t // tq
    y_p = pl.pallas_call(
        _xa_prompt_body,
        grid=(b, nq),
        in_specs=[_proj_spec(tq, COL_XA_Q, lambda bi, qi: bi * nq + qi),
                  pl.BlockSpec((n_mem, HW), lambda bi, qi: (bi, 0)),
                  pl.BlockSpec((n_mem, HW), lambda bi, qi: (bi, 1))],
        out_specs=pl.BlockSpec((tq, HW), lambda bi, qi: (bi * nq + qi, 0)),
        out_shape=jax.ShapeDtypeStruct((n_p, HW), BF16),
        compiler_params=_params(("arbitrary", "arbitrary")),
        name="xattn_prompt",
    )(proj, kv_p, kv_p)

    rows = SAMPLE_BB * ts
    base = n_p // rows
    cache_k = cache_k.reshape(DEPTH, nb, n_mem * HEADS, DH)
    cache_v = cache_v.reshape(DEPTH, nb, n_mem * HEADS, DH)
    kv_spec = pl.BlockSpec((None, SAMPLE_BB, n_mem * HEADS, DH), lambda i: (layer, i, 0, 0))
    y_s = pl.pallas_call(
        functools.partial(_xa_sample_body, ts=ts),
        grid=(nb // SAMPLE_BB,),
        in_specs=[_proj_spec(rows, COL_XA_Q, lambda i: base + i), kv_spec, kv_spec],
        out_specs=pl.BlockSpec((rows, HW), lambda i: (i, 0)),
        out_shape=jax.ShapeDtypeStruct((nb * ts, HW), BF16),
        compiler_params=_params(("arbitrary",)),
        name="xattn_sample",
    )(proj, cache_k, cache_v)
    return (y_p, y_s)


def _layer_norm(tv, g, b):
    mu = jnp.mean(tv, axis=-1, keepdims=True)
    var = jnp.mean(jnp.square(tv - mu), axis=-1, keepdims=True)
    return (tv - mu) * lax.rsqrt(var + LN_EPS) * g + b


def _merge_body(yrp_ref, yrs_ref, yhp_ref, yhs_ref, yxp_ref, yxs_ref, g0_ref, g1_ref, g2_ref, xp_ref, xs_ref,
                wr_ref, wh_ref, wx_ref, wo_ref, lg_ref, lb_ref,
                x1_ref, x1b_ref, x1t_ref, wr_s, wh_s, wx_s, wo_s, *, prompt_tiles):
    @pl.when(pl.program_id(0) == 0)
    def _():
        wr_s[...] = wr_ref[...].astype(BF16)
        wh_s[...] = wh_ref[...].astype(BF16)
        wx_s[...] = wx_ref[...].astype(BF16)
        wo_s[...] = wo_ref[...].astype(BF16)

    is_prompt = pl.program_id(0) < prompt_tiles

    def branch(yp_ref, ys_ref, w_s, gate_ref):
        y = jnp.where(is_prompt, yp_ref[...], ys_ref[...])
        return gate_ref[...].astype(F32) * jnp.dot(y, w_s[...], preferred_element_type=F32)

    m = (branch(yrp_ref, yrs_ref, wr_s, g0_ref) + branch(yhp_ref, yhs_ref, wh_s, g1_ref)
         + branch(yxp_ref, yxs_ref, wx_s, g2_ref))
    hmix = jnp.dot(m.astype(BF16), wo_s[...], preferred_element_type=F32)
    x = jnp.where(is_prompt, xp_ref[...], xs_ref[...])
    x1 = _layer_norm(DN_ALPHA * x + hmix, lg_ref[...], lb_ref[...])
    x1_ref[...] = x1
    x1b_ref[...] = x1.astype(BF16)
    tm = x1.shape[0]
    bits = lax.bitcast_convert_type(x1.astype(BF16).astype(F32), U32)
    half = D_MODEL // 2
    packed = (bits[:, :half] >> 16) | (bits[:, half:] & jnp.uint32(0xFFFF0000))
    for s in range(PACKED_TILES):
        x1t_ref[pl.ds(s, tm, stride=PACKED_TILES), :] = packed[:, s * LANES:(s + 1) * LANES]


def _merge(ys, gates, x, w_up_ret, w_up_hgrn, w_up_xattn, w_out, ln_g, ln_b, layer):
    n_p = ys[0].shape[0]
    nt = n_p + ys[1].shape[0]
    tm = _pick(nt, (256, 128))
    assert n_p % tm == 0 and ys[1].shape[0] % tm == 0
    p_tiles = n_p // tm
    row = lambda i: (i, 0)
    p_map = lambda i: (jnp.minimum(i, p_tiles - 1), 0)
    s_map = lambda i: (jnp.maximum(i - p_tiles, 0), 0)
    y_specs = [pl.BlockSpec((tm, HW), p_map), pl.BlockSpec((tm, HW), s_map)] * 3
    wspec = lambda k: pl.BlockSpec((None, k, D_MODEL), lambda i: (layer, 0, 0))
    vec = pl.BlockSpec((None, 1, D_MODEL), lambda i: (layer, 0, 0))
    gate_specs = [pl.BlockSpec((tm, D_MODEL), lambda i, c=c: (i, c)) for c in range(3)]
    return pl.pallas_call(
        functools.partial(_merge_body, prompt_tiles=p_tiles),
        grid=(nt // tm,),
        in_specs=y_specs + gate_specs + [pl.BlockSpec((tm, D_MODEL), p_map), pl.BlockSpec((tm, D_MODEL), s_map),
                  wspec(HW), wspec(HW), wspec(HW), wspec(D_MODEL), vec, vec],
        out_specs=[pl.BlockSpec((tm, D_MODEL), row), pl.BlockSpec((tm, D_MODEL), row),
                   pl.BlockSpec((tm * PACKED_TILES, LANES), row)],
        out_shape=[jax.ShapeDtypeStruct((nt, D_MODEL), F32),
                   jax.ShapeDtypeStruct((nt, D_MODEL), BF16),
                   jax.ShapeDtypeStruct((nt * PACKED_TILES, LANES), U32)],
        scratch_shapes=[pltpu.VMEM((HW, D_MODEL), BF16)] * 3 + [pltpu.VMEM((D_MODEL, D_MODEL), BF16)],
        compiler_params=_params(("arbitrary",)),
        name="merge_out_ln1",
    )(*ys, *([gates] * 3), *x, w_up_ret, w_up_hgrn, w_up_xattn, w_out, ln_g, ln_b)


def _router_body(x_ref, wt_ref, b_ref, eidx_ref, wn_ref):
    tm = x_ref.shape[0]
    x = x_ref[...]
    w = wt_ref[...]
    xh = x.astype(BF16)
    xl = (x - xh.astype(F32)).astype(BF16)
    wh = w.astype(BF16)
    wl = (w - wh.astype(F32)).astype(BF16)
    logits = _bdot_nt(wh, xh) + (_bdot_nt(wh, xl) + _bdot_nt(wl, xh))
    s = jax.nn.sigmoid(logits)
    sel = s + b_ref[...]
    neg = -jnp.inf
    groups = [sel[g * GROUP_SIZE:(g + 1) * GROUP_SIZE, :] for g in range(N_GROUPS)]
    ie = lax.broadcasted_iota(I32, (GROUP_SIZE, tm), 0).astype(F32)
    rows = []
    for blk in groups:
        m1 = jnp.max(blk, axis=0, keepdims=True)
        first = jnp.min(jnp.where(blk == m1, ie, float(GROUP_SIZE)), axis=0, keepdims=True)
        rows.append(m1 + jnp.max(jnp.where(ie == first, neg, blk), axis=0, keepdims=True))
    gscore = jnp.concatenate(rows, axis=0)
    ig = lax.broadcasted_iota(I32, gscore.shape, 0).astype(F32)
    gmask = jnp.zeros(gscore.shape, F32)
    for _ in range(TOPK_GROUPS):
        m = jnp.max(gscore, axis=0, keepdims=True)
        gi = jnp.min(jnp.where(gscore == m, ig, float(N_GROUPS)), axis=0, keepdims=True)
        hit = ig == gi
        gmask = jnp.where(hit, 1.0, gmask)
        gscore = jnp.where(hit, neg, gscore)
    masked = jnp.concatenate([jnp.where(gmask[g:g + 1, :] > 0.5, blk, neg)
                              for g, blk in enumerate(groups)], axis=0)
    ix = lax.broadcasted_iota(I32, masked.shape, 0).astype(F32)
    idxs, ws = [], []
    for _ in range(TOP_K):
        m = jnp.max(masked, axis=0, keepdims=True)
        ei = jnp.min(jnp.where(masked == m, ix, float(N_EXPERTS)), axis=0, keepdims=True)
        hit = ix == ei
        idxs.append(ei)
        ws.append(jnp.sum(jnp.where(hit, s, 0.0), axis=0, keepdims=True))
        masked = jnp.where(hit, neg, masked)
    wsum = ws[0]
    for w in ws[1:]:
        wsum = wsum + w
    pad = [jnp.zeros((1, tm), F32)] * (SUBLANES - TOP_K)
    eidx_ref[...] = jnp.concatenate(idxs + pad, axis=0).astype(I32)
    wn_ref[...] = jnp.concatenate([w / wsum * ROUTED_SCALE for w in ws] + pad, axis=0)


def _router(x1, w_router_t, b_router, layer):
    nt = x1.shape[0]
    tm = _pick(nt, (512, 256, 128))
    return pl.pallas_call(
        _router_body,
        grid=(nt // tm,),
        in_specs=[pl.BlockSpec((tm, D_MODEL), lambda i: (i, 0)),
                  pl.BlockSpec((None, N_EXPERTS, D_MODEL), lambda i: (layer, 0, 0)),
                  pl.BlockSpec((None, N_EXPERTS, 1), lambda i: (layer, 0, 0))],
        out_specs=[pl.BlockSpec((SUBLANES, tm), lambda i: (0, i))] * 2,
        out_shape=[jax.ShapeDtypeStruct((SUBLANES, nt), I32),
                   jax.ShapeDtypeStruct((SUBLANES, nt), F32)],
        compiler_params=_params(("arbitrary",)),
        name="moe_router",
    )(x1, w_router_t, b_router)


def _positions_body(eidx_ref, pos_ref, cnt_ref, off_ref, base_scr, off_scr):
    phase = pl.program_id(0)
    i = pl.program_id(1)
    tp = eidx_ref.shape[1]
    ix = lax.broadcasted_iota(I32, (N_EXPERTS, tp), 0)
    eidx = eidx_ref[...]
    member = jnp.zeros((N_EXPERTS, tp), F32)
    for k in range(TOP_K):
        member = member + (ix == eidx[k:k + 1, :]).astype(F32)
    tile_cnt = jnp.sum(member, axis=1, keepdims=True)

    @pl.when((phase == 0) & (i == 0))
    def _():
        base_scr[...] = jnp.zeros_like(base_scr)

    @pl.when((phase == 1) & (i == 0))
    def _():
        cnt = base_scr[...]
        er = lax.broadcasted_iota(I32, (N_EXPERTS, N_EXPERTS), 0)
        ec = lax.broadcasted_iota(I32, (N_EXPERTS, N_EXPERTS), 1)
        off = jnp.dot((ec < er).astype(F32), cnt, precision=HIGHEST, preferred_element_type=F32)
        off_scr[...] = off
        cnt_ref[...] = cnt
        off_ref[...] = off
        base_scr[...] = jnp.zeros_like(base_scr)

    @pl.when(phase == 1)
    def _():
        tr = lax.broadcasted_iota(I32, (tp, tp), 0)
        tc = lax.broadcasted_iota(I32, (tp, tp), 1)
        before = jnp.dot(member.astype(BF16), (tr < tc).astype(BF16), preferred_element_type=F32)
        where_to = before + (off_scr[...] + base_scr[...])[:, 0:1]
        rows = [jnp.sum(jnp.where(ix == eidx[k:k + 1, :], where_to, 0.0), axis=0, keepdims=True)
                for k in range(TOP_K)]
        rows += [jnp.zeros((1, tp), F32)] * (SUBLANES - TOP_K)
        pos_ref[...] = jnp.concatenate(rows, axis=0).astype(I32)

    base_scr[...] = base_scr[...] + tile_cnt


def _positions(eidx):
    nt = eidx.shape[1]
    tp = _pick(nt, (512, 256, 128))
    const = lambda p, i: (0, 0)
    return pl.pallas_call(
        _positions_body,
        grid=(2, nt // tp),
        in_specs=[pl.BlockSpec((SUBLANES, tp), lambda p, i: (0, i))],
        out_specs=[pl.BlockSpec((SUBLANES, tp), lambda p, i: (0, i * p)),
                   pl.BlockSpec((N_EXPERTS, LANES), const), pl.BlockSpec((N_EXPERTS, LANES), const)],
        out_shape=[jax.ShapeDtypeStruct((SUBLANES, nt), I32),
                   jax.ShapeDtypeStruct((N_EXPERTS, LANES), F32),
                   jax.ShapeDtypeStruct((N_EXPERTS, LANES), F32)],
        scratch_shapes=[pltpu.VMEM((N_EXPERTS, LANES), F32), pltpu.VMEM((N_EXPERTS, LANES), F32)],
        compiler_params=_params(("arbitrary", "arbitrary")),
        name="moe_positions",
    )(eidx)


T_TILE, T_EXPERT, T_LO, T_HI, T_FRESH, T_NEWEXP = range(6)

def _table_body(cnt_ref, off_ref, tbl_ref, *, tile_rows):
    te = float(tile_rows)
    n = tbl_ref.shape[1]
    cnt = cnt_ref[...]
    off = off_ref[...]
    first = jnp.floor(off * (1.0 / te))
    last = jnp.floor((off + cnt - 1.0) * (1.0 / te))
    nst = jnp.where(cnt > 0.0, last - first + 1.0, 0.0)
    er = lax.broadcasted_iota(I32, (N_EXPERTS, N_EXPERTS), 0)
    ec = lax.broadcasted_iota(I32, (N_EXPERTS, N_EXPERTS), 1)
    s_end = jnp.dot((ec <= er).astype(F32), nst, precision=HIGHEST, preferred_element_type=F32)
    s_beg = s_end - nst
    total = s_end[N_EXPERTS - 1:N_EXPERTS, 0:1]
    sidx = lax.broadcasted_iota(I32, (1, n), 1).astype(F32)
    s = jnp.minimum(sidx, total - 1.0)
    e_s = jnp.sum((s_end[:, 0:1] <= s).astype(F32), axis=0, keepdims=True)
    hot = lax.broadcasted_iota(I32, (N_EXPERTS, n), 0).astype(F32) == e_s

    def pick(col):
        return jnp.sum(jnp.where(hot, col[:, 0:1], 0.0), axis=0, keepdims=True)

    tile = pick(first) + s - pick(s_beg)
    valid = sidx < total
    o, c = pick(off), pick(cnt)
    lo = jnp.where(valid, jnp.maximum(o, tile * te), 0.0)
    hi = jnp.where(valid, jnp.minimum(o + c, (tile + 1.0) * te), 0.0)
    head = sidx == 0.0
    fresh = jnp.where((tile != pltpu.roll(tile, 1, 1)) | head, 1.0, 0.0)
    newexp = jnp.where((e_s != pltpu.roll(e_s, 1, 1)) | head, 1.0, 0.0)
    pad = [jnp.zeros((1, n), F32)] * (SUBLANES - 6)
    tbl_ref[...] = jnp.concatenate([tile, e_s, lo, hi, fresh, newexp] + pad, axis=0).astype(I32)


def _step_table(cnt, off, n_rows, te):
    n_steps = n_rows // te + N_EXPERTS
    width = -(-n_steps // LANES) * LANES
    tbl = pl.pallas_call(
        functools.partial(_table_body, tile_rows=te),
        out_shape=jax.ShapeDtypeStruct((SUBLANES, width), I32),
        name="moe_step_table",
    )(cnt, off)
    return tbl, n_steps


def _wait_tile_rows(like_src, dst_rows_ref, sem_ref):
    rows = like_src.shape[0]
    for _ in range(TOP_K):
        pltpu.make_async_copy(like_src, dst_rows_ref.at[pl.ds(0, rows)], sem_ref).wait()


def _dispatch_body(pos_ref, xt_ref, xb_ref, wsg_ref, wsu_ref, wsd_ref, xs_ref, sh_ref,
                   pos_s, wsg_s, wsu_s, wsd_s, sem_p, sem):
    td = pos_ref.shape[1]

    @pl.when(pl.program_id(0) == 0)
    def _():
        wsg_s[...] = wsg_ref[...].astype(BF16)
        wsu_s[...] = wsu_ref[...].astype(BF16)
        wsd_s[...] = wsd_ref[...].astype(BF16)

    cp = pltpu.make_async_copy(pos_ref, pos_s, sem_p)
    cp.start()
    cp.wait()

    def issue(g, carry):
        for u in range(ISSUE_UNROLL):
            r = g * ISSUE_UNROLL + u
            for k in range(TOP_K):
                pltpu.make_async_copy(xt_ref.at[r], xs_ref.at[pos_s[k, r]], sem).start(priority=k % 2)
        return carry

    half = td // 2
    groups = td // ISSUE_UNROLL
    bounds = (0, groups // 4, groups // 4 + (groups - groups // 4) // 2, groups)
    for part in range(3):
        lax.fori_loop(bounds[part], bounds[part + 1], issue, 0)
        if part < 2:
            rows = slice(part * half, (part + 1) * half)
            xb = xb_ref[rows, :]
            hs = (_silu(jnp.dot(xb, wsg_s[...], preferred_element_type=F32))
                  * jnp.dot(xb, wsu_s[...], preferred_element_type=F32))
            sh_ref[rows, :] = jnp.dot(hs.astype(BF16), wsd_s[...], preferred_element_type=F32)
    _wait_tile_rows(xt_ref, xs_ref, sem)


def _dispatch(pos, x1t, x1b, w_s_gate, w_s_up, w_s_down, layer):
    nt = x1t.shape[0]
    td = _pick(nt, (512, 256, 128))
    d_sh = w_s_gate.shape[2]
    w_in_spec = pl.BlockSpec((None, D_MODEL, d_sh), lambda i: (layer, 0, 0))
    return pl.pallas_call(
        _dispatch_body,
        grid=(nt // td,),
        in_specs=[pl.BlockSpec((SUBLANES, td), lambda i: (0, i)),
                  pl.BlockSpec((td,) + x1t.shape[1:], lambda i: (i, 0, 0)),
                  pl.BlockSpec((td, D_MODEL), lambda i: (i, 0)),
                  w_in_spec, w_in_spec, pl.BlockSpec((None, d_sh, D_MODEL), lambda i: (layer, 0, 0))],
        out_specs=[pl.BlockSpec(memory_space=pl.ANY), pl.BlockSpec((td, D_MODEL), lambda i: (i, 0))],
        out_shape=[jax.ShapeDtypeStruct((nt * TOP_K,) + x1t.shape[1:], x1t.dtype),
                   jax.ShapeDtypeStruct((nt, D_MODEL), F32)],
        scratch_shapes=[pltpu.SMEM((SUBLANES, td), I32),
                        pltpu.VMEM((D_MODEL, d_sh), BF16), pltpu.VMEM((D_MODEL, d_sh), BF16),
                        pltpu.VMEM((d_sh, D_MODEL), BF16),
                        pltpu.SemaphoreType.DMA, pltpu.SemaphoreType.DMA],
        compiler_params=_params(("arbitrary",)),
        name="moe_dispatch",
    )(pos, x1t, x1b, w_s_gate, w_s_up, w_s_down)


def _experts_body(tbl_ref, xs_ref, wg_ref, wu_ref, wd_ref, ye_ref, wg_s, wu_s, wd_s):
    s = pl.program_id(0)
    te = xs_ref.shape[0] // PACKED_TILES
    lo = tbl_ref[T_LO, s]
    hi = tbl_ref[T_HI, s]

    @pl.when(tbl_ref[T_NEWEXP, s] == 1)
    def _():
        wg_s[...] = wg_ref[...].astype(BF16)
        wu_s[...] = wu_ref[...].astype(BF16)
        wd_s[...] = wd_ref[...].astype(BF16)

    @pl.when(tbl_ref[T_FRESH, s] == 1)
    def _():
        ye_ref[...] = jnp.zeros_like(ye_ref)

    @pl.when(hi > lo)
    def _():
        words = [xs_ref[pl.ds(t, te, stride=PACKED_TILES), :] for t in range(PACKED_TILES)]
        low = [lax.bitcast_convert_type(w << 16, F32).astype(BF16) for w in words]
        high = [lax.bitcast_convert_type(w & jnp.uint32(0xFFFF0000), F32).astype(BF16) for w in words]
        x = jnp.concatenate(low + high, axis=-1)
        g = jnp.dot(x, wg_s[...], preferred_element_type=F32)
        u = jnp.dot(x, wu_s[...], preferred_element_type=F32)
        y = jnp.dot((_silu(g) * u).astype(BF16), wd_s[...], preferred_element_type=F32)
        row = tbl_ref[T_TILE, s] * te + lax.broadcasted_iota(I32, (te, LANES), 0)
        mine = (row >= lo) & (row < hi)
        for t in range(ROW_TILES):
            sl = pl.ds(t, te, stride=ROW_TILES)
            ye_ref[sl, :] = jnp.where(mine, y[:, t * LANES:(t + 1) * LANES], ye_ref[sl, :])


def _experts(tbl, n_steps, te, xs, w_gate, w_up, w_down, layer):
    n_rows = xs.shape[0] // PACKED_TILES
    tile_map = lambda s, tbl: (tbl[T_TILE, s], 0)
    w_map = lambda s, tbl: (layer, tbl[T_EXPERT, s], 0, 0)
    w_in_spec = pl.BlockSpec((None, None, D_MODEL, D_EXPERT), w_map)
    w_dn_spec = pl.BlockSpec((None, None, D_EXPERT, D_MODEL), w_map)
    return pl.pallas_call(
        _experts_body,
        grid_spec=pltpu.PrefetchScalarGridSpec(
            num_scalar_prefetch=1,
            grid=(n_steps,),
            in_specs=[pl.BlockSpec((te * PACKED_TILES, LANES), tile_map), w_in_spec, w_in_spec, w_dn_spec],
            out_specs=pl.BlockSpec((te * ROW_TILES, LANES), tile_map),
            scratch_shapes=[pltpu.VMEM((D_MODEL, D_EXPERT), BF16), pltpu.VMEM((D_MODEL, D_EXPERT), BF16),
                            pltpu.VMEM((D_EXPERT, D_MODEL), BF16)]),
        out_shape=jax.ShapeDtypeStruct((n_rows * ROW_TILES, LANES), F32),
        compiler_params=_params(("arbitrary",)),
        name="moe_experts",
    )(tbl, xs, w_gate, w_up, w_down)


def _combine_body(pos_ref, wn_ref, ye_ref, x1_ref, sh_ref, lg_ref, lb_ref,
                  x2p_ref, x2s_ref, x2b_ref, pos_s, buf, sem_p, sem, *, prompt_tiles):
    i = pl.program_id(0)
    n = pl.num_programs(0)
    tc = wn_ref.shape[1]
    slot = i % 2
    tile_rows = tc * ROW_TILES

    groups = tc // ISSUE_UNROLL
    has_next = i + 1 < n

    def load_positions(tile):
        cp = pltpu.make_async_copy(pos_ref.at[tile], pos_s, sem_p)
        cp.start()
        cp.wait()

    def request(into, first_group, end_group):
        def issue(g, carry):
            for u in range(ISSUE_UNROLL):
                r = g * ISSUE_UNROLL + u
                for k in range(TOP_K):
                    at = pl.multiple_of(((into * TOP_K + k) * tc + r) * ROW_TILES, ROW_TILES)
                    pltpu.make_async_copy(ye_ref.at[pos_s[k, r]], buf.at[pl.ds(at, ROW_TILES)],
                                          sem.at[into]).start(priority=k % 2)
            return carry

        lax.fori_loop(first_group, end_group, issue, 0)

    @pl.when(i == 0)
    def _():
        load_positions(0)
        request(0, 0, groups)

    @pl.when(has_next)
    def _():
        load_positions(i + 1)
        request(1 - slot, 0, groups // 2)

    for k in range(TOP_K):
        pltpu.make_async_copy(buf.at[pl.ds(0, tile_rows)], buf.at[pl.ds(tile_rows, tile_rows)], sem.at[slot]).wait()

    w = wn_ref[...]
    acc = [None] * ROW_TILES
    for k in range(TOP_K):
        wcol = jnp.concatenate([jnp.broadcast_to(w[k:k + 1, c * LANES:(c + 1) * LANES], (LANES, LANES)).T
                                for c in range(tc // LANES)], axis=0)
        base = (slot * TOP_K + k) * tile_rows
        for t in range(ROW_TILES):
            term = wcol * buf[pl.ds(base + t, tc, stride=ROW_TILES), :]
            acc[t] = term if acc[t] is None else acc[t] + term
    routed = jnp.concatenate(acc, axis=-1)

    @pl.when(has_next)
    def _():
        request(1 - slot, groups // 2, groups)

    x2 = _layer_norm(DN_ALPHA * x1_ref[...] + (routed + sh_ref[...]), lg_ref[...], lb_ref[...])
    x2b_ref[...] = x2.astype(BF16)

    @pl.when(i < prompt_tiles)
    def _():
        x2p_ref[...] = x2

    @pl.when(i >= prompt_tiles)
    def _():
        x2s_ref[...] = x2


def _combine(pos, wn, ye, x1, shared, ln_g, ln_b, layer, n_p):
    nt = x1.shape[0]
    tc = _pick(nt, (256, 128))
    assert n_p % tc == 0
    n_tiles = nt // tc
    p_tiles = n_p // tc
    pos3 = pos.reshape(SUBLANES, n_tiles, tc).transpose(1, 0, 2)
    row = lambda i: (i, 0)
    vec = pl.BlockSpec((None, 1, D_MODEL), lambda i: (layer, 0, 0))
    return pl.pallas_call(
        functools.partial(_combine_body, prompt_tiles=p_tiles),
        grid=(n_tiles,),
        in_specs=[pl.BlockSpec((n_tiles, SUBLANES, tc), lambda i: (0, 0, 0)),
                  pl.BlockSpec((SUBLANES, tc), lambda i: (0, i)),
                  pl.BlockSpec(memory_space=pl.ANY),
                  pl.BlockSpec((tc, D_MODEL), row), pl.BlockSpec((tc, D_MODEL), row), vec, vec],
        out_specs=[pl.BlockSpec((tc, D_MODEL), lambda i: (jnp.minimum(i, p_tiles - 1), 0)),
                   pl.BlockSpec((tc, D_MODEL), lambda i: (jnp.maximum(i - p_tiles, 0), 0)),
                   pl.BlockSpec((tc, D_MODEL), row)],
        out_shape=[jax.ShapeDtypeStruct((n_p, D_MODEL), F32), jax.ShapeDtypeStruct((nt - n_p, D_MODEL), F32),
                   jax.ShapeDtypeStruct((nt, D_MODEL), BF16)],
        scratch_shapes=[pltpu.SMEM((SUBLANES, tc), I32),
                        pltpu.VMEM((2 * TOP_K * tc * ROW_TILES, LANES), F32),
                        pltpu.SemaphoreType.DMA, pltpu.SemaphoreType.DMA((2,))],
        compiler_params=_params(("arbitrary",)),
        name="moe_combine_ln2",
    )(pos3, wn, ye, x1, shared, ln_g, ln_b)


def _rope_tables(t, pos0):
    inv = 1.0 / (ROPE_BASE ** (jnp.arange(0, DH, 2, dtype=F32) / DH))
    ang = (jnp.arange(t, dtype=F32) + pos0)[:, None] * inv[None, :]
    cos, sin = jnp.cos(ang), jnp.sin(ang)
    return jnp.concatenate([cos, cos], axis=-1), jnp.concatenate([-sin, sin], axis=-1)


def kernel(x_prompt, x_sample, mem_prompt, state_ret, state_hgrn, cache_mem_k, cache_mem_v, w_in, w_up_ret, w_up_hgrn, w_up_xattn, w_out, w_mem_kv, ret_norm_g, hgrn_norm_g, lb_logits, ln1_g, ln1_b, ln2_g, ln2_b, w_router, b_router, w_e_gate, w_e_up, w_e_down, w_s_gate, w_s_up, w_s_down):
    b, t, d = x_prompt.shape
    nb, ts, _ = x_sample.shape
    n_mem = mem_prompt.shape[1]
    assert d == D_MODEL and t % RET_CHUNK == 0 and nb % SAMPLE_BB == 0
    assert ts & (ts - 1) == 0 and HG_CHUNK % ts == 0 and RET_CHUNK % ts == 0
    n_p, n_s = b * t, nb * ts
    nt = n_p + n_s
    assert n_p % (SAMPLE_BB * ts) == 0

    lb_cum = jnp.cumsum(jax.nn.softmax(lb_logits.astype(F32), axis=0), axis=0)
    lbs = lb_cum - lb_cum[0:1]
    lbt = jnp.stack([jnp.log(lbs), jnp.log1p(-lbs), 1.0 - lbs] + [jnp.zeros_like(lbs)] * (SUBLANES - 3), axis=1)
    gl = jnp.broadcast_to(jnp.log1p(-jnp.exp2(-5.0 - jnp.arange(HEADS, dtype=F32)))[:, None], (HEADS, DH))
    cos_p, sin_p = _rope_tables(t, 0)
    cos_s, sin_s = _rope_tables(ts, PAST_LEN)
    cos_s, sin_s = jnp.tile(cos_s, (SAMPLE_BB, 1)), jnp.tile(sin_s, (SAMPLE_BB, 1))
    vec3 = lambda a: a.reshape(DEPTH, 1, -1)
    w_router_t = jnp.swapaxes(w_router, 1, 2)
    b_router3 = b_router.reshape(DEPTH, N_EXPERTS, 1)
    mem2 = mem_prompt.reshape(b * n_mem, d)

    x = (x_prompt.reshape(n_p, d), x_sample.reshape(n_s, d))
    xb = jnp.concatenate([x[0].astype(BF16), x[1].astype(BF16)], axis=0)
    tm_proj = _pick(nt, (1024, 512, 128))
    te = _pick(nt * TOP_K, EXPERT_TILES)
    outs = {k: [] for k in ("ret_p", "hg_p", "mk", "mv")}
    ret_s = hg_s = None
    for l in range(DEPTH):
        mix_tiles = COL_GATES * HW // PROJ_TILE_N
        proj = _matmul(xb, w_in, l, tm_proj, PROJ_TILE_N, n=COL_GATES * HW)
        gates = _matmul(xb, w_in, l, tm_proj, PROJ_TILE_N, first_tile=mix_tiles, n=3 * D_MODEL, gate=True)
        kv_p = _matmul(mem2, w_mem_kv, l, _pick(b * n_mem, (1024, 512, 256)), 2 * HW)
        yr, ret_p, ret_s = _retention(proj, state_ret, l, cos_p, sin_p, cos_s, sin_s, gl,
                                      vec3(ret_norm_g), b, t, nb, ts, ret_s)
        yh, hg_p, hg_s = _hgrn(proj, state_hgrn, l, lbt, vec3(hgrn_norm_g), b, t, nb, ts, hg_s)
        yx = _cross_attention(proj, kv_p, cache_mem_k, cache_mem_v, l, b, t, nb, ts)
        x1, x1b, x1t = _merge((*yr, *yh, *yx), gates, x, w_up_ret, w_up_hgrn, w_up_xattn, w_out,
                              vec3(ln1_g), vec3(ln1_b), l)
        eidx, wn = _router(x1, w_router_t, b_router3, l)
        pos, cnt, off = _positions(eidx)
        tbl, n_steps = _step_table(cnt, off, nt * TOP_K, te)
        xs, shared = _dispatch(pos, x1t.reshape(nt, PACKED_TILES, LANES), x1b, w_s_gate, w_s_up, w_s_down, l)
        ye = _experts(tbl, n_steps, te, xs.reshape(-1, LANES), w_e_gate, w_e_up, w_e_down, l)
        ye = ye.reshape(-1, ROW_TILES, LANES)
        x_p, x_s, xb = _combine(pos, wn, ye, x1, shared, vec3(ln2_g), vec3(ln2_b), l, n_p)
        x = (x_p, x_s)
        outs["ret_p"].append(ret_p)
        outs["hg_p"].append(hg_p)
        outs["mk"].append(kv_p[:, :HW].reshape(b, n_mem, HEADS, DH))
        outs["mv"].append(kv_p[:, HW:].reshape(b, n_mem, HEADS, DH))
    return (x[0].reshape(b, t, d), x[1].reshape(nb, ts, d),
            jnp.stack(outs["ret_p"]), jnp.stack(outs["hg_p"]), jnp.stack(outs["mk"]), jnp.stack(outs["mv"]),
            ret_s, hg_s)
```

```python
import functools

import jax
import jax.numpy as jnp
from jax import lax
from jax.experimental import pallas as pl
from jax.experimental.pallas import tpu as pltpu

F32 = jnp.float32
BF16 = jnp.bfloat16
I32 = jnp.int32
HIGHEST = lax.Precision.HIGHEST

D_MODEL = 1024
DEPTH = 2
PAST_LEN = 16384
HEADS = 4
DH = 128
HW = HEADS * DH
RET_CHUNK = 128
HG_CHUNK = 16
ROPE_BASE = 10000.0
N_EXPERTS = 64
N_GROUPS = 8
GROUP_SIZE = N_EXPERTS // N_GROUPS
TOPK_GROUPS = 4
TOP_K = 6
D_EXPERT = 256
ROUTED_SCALE = 2.5
LN_EPS = 1e-5
DN_ALPHA = (2 * DEPTH) ** 0.25
N_IN = 9 * HW + 3 * D_MODEL
COL_RET_Q, COL_RET_K, COL_RET_V, COL_RET_G = 0, 1, 2, 3
COL_HG_Q, COL_HG_F, COL_HG_I, COL_HG_G = 4, 5, 6, 7
COL_XA_Q = 8
COL_GATES = 9
LANES = 128
SUBLANES = 8
ROW_TILES = D_MODEL // LANES
PACKED_TILES = ROW_TILES // 2
U32 = jnp.uint32
SAMPLE_BB = 8
EXPERT_TILES = (512, 256)
ISSUE_UNROLL = 8
PROJ_TILE_N = 1536
VMEM_LIMIT = 56 * 1024 * 1024


def _params(sem):
    return pltpu.CompilerParams(dimension_semantics=sem, vmem_limit_bytes=VMEM_LIMIT)


def _bdot(a, b):
    return jnp.dot(a.astype(BF16), b.astype(BF16), preferred_element_type=F32)


def _bdot_nt(a, b):
    return lax.dot_general(a.astype(BF16), b.astype(BF16), (((1,), (1,)), ((), ())),
                           preferred_element_type=F32)


def _bdot_tn(a, b):
    return lax.dot_general(a.astype(BF16), b.astype(BF16), (((0,), (0,)), ((), ())),
                           preferred_element_type=F32)


def _silu(x):
    return x * jax.nn.sigmoid(x)


def _pick(n, prefs):
    for p in prefs:
        if n % p == 0:
            return p
    raise ValueError(f"no tile for {n}")


def _mm_body(x_ref, w_ref, o_ref, wb_ref, *, gate):
    @pl.when(pl.program_id(1) == 0)
    def _():
        wb_ref[...] = w_ref[...].astype(BF16)

    acc = jnp.dot(x_ref[...].astype(BF16), wb_ref[...], preferred_element_type=F32)
    o_ref[...] = (jax.nn.sigmoid(acc) if gate else acc).astype(o_ref.dtype)


def _matmul(x, w, layer, tm, tn, first_tile=0, n=None, gate=False):
    m, k = x.shape
    n = w.shape[2] if n is None else n
    return pl.pallas_call(
        functools.partial(_mm_body, gate=gate),
        grid=(n // tn, m // tm),
        in_specs=[pl.BlockSpec((tm, k), lambda j, i: (i, 0)),
                  pl.BlockSpec((None, k, tn), lambda j, i: (layer, 0, first_tile + j))],
        out_specs=pl.BlockSpec((tm, tn), lambda j, i: (i, j)),
        out_shape=jax.ShapeDtypeStruct((m, n), BF16 if gate else F32),
        scratch_shapes=[pltpu.VMEM((k, tn), BF16)],
        compiler_params=_params(("arbitrary", "arbitrary")),
        name="dense_matmul",
    )(x, w)


def _rotary(x, cos, sin_signed):
    return x * cos + pltpu.roll(x, DH // 2, 1) * sin_signed


def _group_norm_gate(o, gain, gate):
    mu = jnp.mean(o, axis=-1, keepdims=True)
    var = jnp.mean(jnp.square(o - mu), axis=-1, keepdims=True)
    return (o - mu) * lax.rsqrt(var + LN_EPS) * gain * _silu(gate)


def _ret_prompt_body(q_ref, k_ref, v_ref, g_ref, cos_ref, sin_ref, gl_ref, gain_ref,
                     y_ref, st_ref, s_scr, intra_scr, qdec_scr, kdec_scr):
    c = pl.program_id(1)
    ch = RET_CHUNK

    @pl.when((pl.program_id(0) == 0) & (c == 0))
    def _():
        ri = lax.broadcasted_iota(I32, (ch, ch), 0)
        ci = lax.broadcasted_iota(I32, (ch, ch), 1)
        rel = (ri - ci).astype(F32)
        idx = lax.broadcasted_iota(I32, (ch, DH), 0).astype(F32)
        for h in range(HEADS):
            gl = gl_ref[h:h + 1, :]
            intra_scr[h] = jnp.where(rel >= 0, jnp.exp(gl * rel), 0.0)
            qdec_scr[h] = jnp.exp(gl * (idx + 1.0))
            kdec_scr[h] = jnp.exp(gl * (ch - 1.0 - idx))

    @pl.when(c == 0)
    def _():
        s_scr[...] = jnp.zeros_like(s_scr)

    cos = cos_ref[...]
    sin = sin_ref[...]
    for h in range(HEADS):
        sl = slice(h * DH, (h + 1) * DH)
        gl = gl_ref[h:h + 1, :]
        qr = _rotary(q_ref[:, sl], cos, sin)
        kr = _rotary(k_ref[:, sl], cos, sin) * (DH ** -0.5)
        v = v_ref[:, sl]
        att = _bdot_nt(qr, kr) * intra_scr[h]
        s = s_scr[h]
        o = _bdot(att, v) + _bdot(qr, s) * qdec_scr[h]
        s_scr[h] = s * jnp.exp(gl * float(ch)) + _bdot_tn(kr * kdec_scr[h], v)
        y_ref[:, sl] = _group_norm_gate(o, gain_ref[:, sl], g_ref[:, sl]).astype(BF16)

    @pl.when(c == pl.num_programs(1) - 1)
    def _():
        st_ref[0] = s_scr[...]


def _ret_sample_body(q_ref, k_ref, v_ref, g_ref, cos_ref, sin_ref, gl_ref, gain_ref, sin_ref_state,
                     y_ref, st_ref, *, ts):
    rows = SAMPLE_BB * ts
    shift = ts.bit_length() - 1
    cos = cos_ref[...]
    sin = sin_ref[...]
    ri = lax.broadcasted_iota(I32, (rows, rows), 0)
    ci = lax.broadcasted_iota(I32, (rows, rows), 1)
    rel = (ri - ci).astype(F32)
    mask = ((ri >> shift) == (ci >> shift)) & (ri >= ci)
    idx = (lax.broadcasted_iota(I32, (rows, DH), 0) & (ts - 1)).astype(F32)
    for h in range(HEADS):
        sl = slice(h * DH, (h + 1) * DH)
        gl = gl_ref[h:h + 1, :]
        qr = _rotary(q_ref[:, sl], cos, sin)
        kr = _rotary(k_ref[:, sl], cos, sin) * (DH ** -0.5)
        v = v_ref[:, sl]
        intra = jnp.where(mask, jnp.exp(gl[:, :rows] * rel), 0.0)
        o_intra = _bdot(_bdot_nt(qr, kr) * intra, v)
        q_dec = jnp.exp(gl * (idx + 1.0))
        kd = kr * jnp.exp(gl * (ts - 1.0 - idx))
        c_dec = jnp.exp(gl * float(ts))
        outs = []
        for j in range(SAMPLE_BB):
            rs = slice(j * ts, (j + 1) * ts)
            s = sin_ref_state[j, h]
            outs.append(o_intra[rs] + _bdot(qr[rs], s) * q_dec[rs])
            new_state = s * c_dec + _bdot_tn(kd[rs], v[rs])
            for slot in range(st_ref.shape[0]):
                st_ref[slot, j, h] = new_state
        o = jnp.concatenate(outs, axis=0)
        y_ref[:, sl] = _group_norm_gate(o, gain_ref[:, sl], g_ref[:, sl]).astype(BF16)


def _proj_spec(rows, col, row_map):
    return pl.BlockSpec((rows, HW), lambda *a: (row_map(*a), col))


def _sample_state_call(body, grid, in_specs, args, y_shape, y_spec, layer, nb, prev, name):
    st_shape = jax.ShapeDtypeStruct((DEPTH, nb, HEADS, DH, DH), F32)
    slots = DEPTH if prev is None else 1
    st_spec = pl.BlockSpec((slots, SAMPLE_BB, HEADS, DH, DH), lambda i: (layer, i, 0, 0, 0))
    aliases = {}
    if prev is not None:
        n_in = len(args)
        inner = body
        body = lambda *refs: inner(*refs[:n_in], *refs[n_in + 1:])
        in_specs = in_specs + [pl.BlockSpec(memory_space=pl.ANY)]
        args = args + (prev,)
        aliases = {n_in: 1}
    return pl.pallas_call(
        body, grid=grid, in_specs=in_specs, out_specs=[y_spec, st_spec], out_shape=[y_shape, st_shape],
        input_output_aliases=aliases, compiler_params=_params(("arbitrary",)), name=name,
    )(*args)


def _retention(proj, state, layer, cos_p, sin_p, cos_s, sin_s, gl, gain, b, t, nb, ts, prev_s):
    n_p = b * t
    nc = t // RET_CHUNK
    prow = lambda bi, c: bi * nc + c
    const2 = lambda *a: (0, 0)
    y_p, st_p = pl.pallas_call(
        _ret_prompt_body,
        grid=(b, nc),
        in_specs=[_proj_spec(RET_CHUNK, COL_RET_Q, prow), _proj_spec(RET_CHUNK, COL_RET_K, prow),
                  _proj_spec(RET_CHUNK, COL_RET_V, prow), _proj_spec(RET_CHUNK, COL_RET_G, prow),
                  pl.BlockSpec((RET_CHUNK, DH), lambda bi, c: (c, 0)),
                  pl.BlockSpec((RET_CHUNK, DH), lambda bi, c: (c, 0)),
                  pl.BlockSpec((HEADS, DH), const2),
                  pl.BlockSpec((None, 1, HW), lambda bi, c: (layer, 0, 0))],
        out_specs=[pl.BlockSpec((RET_CHUNK, HW), lambda bi, c: (prow(bi, c), 0)),
                   pl.BlockSpec((1, HEADS, DH, DH), lambda bi, c: (bi, 0, 0, 0))],
        out_shape=[jax.ShapeDtypeStruct((n_p, HW), BF16),
                   jax.ShapeDtypeStruct((b, HEADS, DH, DH), F32)],
        scratch_shapes=[pltpu.VMEM((HEADS, DH, DH), F32), pltpu.VMEM((HEADS, RET_CHUNK, RET_CHUNK), F32),
                        pltpu.VMEM((HEADS, RET_CHUNK, DH), F32), pltpu.VMEM((HEADS, RET_CHUNK, DH), F32)],
        compiler_params=_params(("arbitrary", "arbitrary")),
        name="retention_prompt",
    )(proj, proj, proj, proj, cos_p, sin_p, gl, gain)

    rows = SAMPLE_BB * ts
    base = n_p // rows
    srow = lambda i: base + i
    y_s, st_s = _sample_state_call(
        functools.partial(_ret_sample_body, ts=ts), (nb // SAMPLE_BB,),
        [_proj_spec(rows, COL_RET_Q, srow), _proj_spec(rows, COL_RET_K, srow),
         _proj_spec(rows, COL_RET_V, srow), _proj_spec(rows, COL_RET_G, srow),
         pl.BlockSpec((rows, DH), const2), pl.BlockSpec((rows, DH), const2),
         pl.BlockSpec((HEADS, DH), const2),
         pl.BlockSpec((None, 1, HW), lambda i: (layer, 0, 0)),
         pl.BlockSpec((None, SAMPLE_BB, HEADS, DH, DH), lambda i: (layer, i, 0, 0, 0))],
        (proj, proj, proj, proj, cos_s, sin_s, gl, gain, state),
        jax.ShapeDtypeStruct((nb * ts, HW), BF16), pl.BlockSpec((rows, HW), lambda i: (i, 0)),
        layer, nb, prev_s, "retention_sample")
    return (y_p, y_s), st_p, st_s


def _mask_sums(masks, x):
    hi = x.astype(BF16)
    rest = x - hi.astype(F32)
    mid = rest.astype(BF16)
    lo = (rest - mid.astype(F32)).astype(BF16)
    m = jnp.concatenate([mk.astype(BF16) for mk in masks], axis=0)
    dot = functools.partial(jnp.dot, preferred_element_type=F32)
    out = dot(m, hi) + (dot(m, mid) + dot(m, lo))
    rows = masks[0].shape[0]
    return [out[i * rows:(i + 1) * rows] for i in range(len(masks))]


def _hg_prepare(hq_ref, hf_ref, lbt_ref, rows, chunk, with_prefix=False):
    shift = chunk.bit_length() - 1
    ri = lax.broadcasted_iota(I32, (rows, rows), 0)
    ci = lax.broadcasted_iota(I32, (rows, rows), 1)
    same = (ri >> shift) == (ci >> shift)
    causal = same & (ci <= ri)
    z = hf_ref[...]
    log_lb = lbt_ref[0:1, :]
    log_1m_lb = lbt_ref[1:2, :]
    one_m_lb = lbt_ref[2:3, :]
    log_sig = jnp.minimum(z, 0.0) - jnp.log(1.0 + jnp.exp(-jnp.abs(z)))
    bterm = log_1m_lb + log_sig
    logf = jnp.maximum(log_lb, bterm) + jnp.log(1.0 + jnp.exp(-jnp.abs(log_lb - bterm)))
    kh = one_m_lb * jax.nn.sigmoid(-z)
    qh = _silu(hq_ref[...]) * (DH ** -0.5)
    masks = [causal, same] + ([(ci >> shift) < (ri >> shift)] if with_prefix else [])
    cum, tot, *pre = _mask_sums(masks, logf)
    qi = qh * jnp.exp(cum)
    ki = kh * jnp.exp(-cum)
    ke = kh * jnp.exp(tot - cum)
    return causal, qi, ki, ke, tot, (pre[0] if with_prefix else None)


def _rms_norm_gate(o, gain, gate):
    return o * lax.rsqrt(jnp.mean(jnp.square(o), axis=-1, keepdims=True) + LN_EPS) * gain * _silu(gate)


def _hg_prompt_body(hq_ref, hf_ref, hi_ref, hg_ref, lbt_ref, gain_ref, y_ref, st_ref, s_scr):
    c = pl.program_id(1)

    @pl.when(c == 0)
    def _():
        s_scr[...] = jnp.zeros_like(s_scr)

    rows = RET_CHUNK
    n_sub = rows // HG_CHUNK
    shift = HG_CHUNK.bit_length() - 1
    causal, qi, ki, ke, tot, pre = _hg_prepare(hq_ref, hf_ref, lbt_ref, rows, HG_CHUNK, with_prefix=True)
    sub = lax.broadcasted_iota(I32, (rows, DH), 0) >> shift
    v = hi_ref[...]
    for h in range(HEADS):
        sl = slice(h * DH, (h + 1) * DH)
        q_h, ke_h, v_h, pre_h = qi[:, sl], ke[:, sl], v[:, sl], pre[:, sl]
        att = jnp.where(causal, _bdot_nt(q_h, ki[:, sl]), 0.0)
        st0 = s_scr[h]
        o = _bdot(att, v_h) + _bdot_nt(q_h * jnp.exp(pre_h), st0)
        end_last = pre_h[rows - 1:rows] + tot[rows - 1:rows, sl]
        st = st0 * jnp.exp(end_last)
        for i in range(n_sub):
            rs = slice(i * HG_CHUNK, (i + 1) * HG_CHUNK)
            u_t = _bdot_tn(v_h[rs], ke_h[rs])
            if i + 1 < n_sub:
                end_i = pre_h[(i + 1) * HG_CHUNK:(i + 1) * HG_CHUNK + 1]
                later = q_h * jnp.exp(jnp.where(sub > i, pre_h - end_i, -jnp.inf))
                o = o + _bdot_nt(later, u_t)
                st = st + u_t * jnp.exp(end_last - end_i)
            else:
                st = st + u_t
        s_scr[h] = st
        y_ref[:, sl] = _rms_norm_gate(o, gain_ref[:, sl], hg_ref[:, sl]).astype(BF16)

    @pl.when(c == pl.num_programs(1) - 1)
    def _():
        for h in range(HEADS):
            st_ref[0, h] = s_scr[h].T


def _hg_sample_body(hq_ref, hf_ref, hi_ref, hg_ref, lbt_ref, gain_ref, sin_ref_state,
                    y_ref, st_ref, *, ts):
    rows = SAMPLE_BB * ts
    causal, qi, ki, ke, tot, _ = _hg_prepare(hq_ref, hf_ref, lbt_ref, rows, ts)
    etot = jnp.exp(tot)
    v = hi_ref[...]
    for h in range(HEADS):
        sl = slice(h * DH, (h + 1) * DH)
        att = jnp.where(causal, _bdot_nt(qi[:, sl], ki[:, sl]), 0.0)
        o_intra = _bdot(att, v[:, sl])
        outs = []
        for j in range(SAMPLE_BB):
            rs = slice(j * ts, (j + 1) * ts)
            s = sin_ref_state[j, h]
            outs.append(o_intra[rs] + _bdot(qi[rs, sl], s))
            scale = jnp.broadcast_to(etot[j * ts:j * ts + 1, sl], (DH, DH)).T
            new_state = s * scale + _bdot_tn(ke[rs, sl], v[rs, sl])
            for slot in range(st_ref.shape[0]):
                st_ref[slot, j, h] = new_state
        o = jnp.concatenate(outs, axis=0)
        y_ref[:, sl] = _rms_norm_gate(o, gain_ref[:, sl], hg_ref[:, sl]).astype(BF16)


def _hgrn(proj, state, layer, lbt, gain, b, t, nb, ts, prev_s):
    n_p = b * t
    nc = t // RET_CHUNK
    prow = lambda bi, c: bi * nc + c
    y_p, st_p = pl.pallas_call(
        _hg_prompt_body,
        grid=(b, nc),
        in_specs=[_proj_spec(RET_CHUNK, COL_HG_Q, prow), _proj_spec(RET_CHUNK, COL_HG_F, prow),
                  _proj_spec(RET_CHUNK, COL_HG_I, prow), _proj_spec(RET_CHUNK, COL_HG_G, prow),
                  pl.BlockSpec((None, SUBLANES, HW), lambda bi, c: (layer, 0, 0)),
                  pl.BlockSpec((None, 1, HW), lambda bi, c: (layer, 0, 0))],
        out_specs=[pl.BlockSpec((RET_CHUNK, HW), lambda bi, c: (prow(bi, c), 0)),
                   pl.BlockSpec((1, HEADS, DH, DH), lambda bi, c: (bi, 0, 0, 0))],
        out_shape=[jax.ShapeDtypeStruct((n_p, HW), BF16),
                   jax.ShapeDtypeStruct((b, HEADS, DH, DH), F32)],
        scratch_shapes=[pltpu.VMEM((HEADS, DH, DH), F32)],
        compiler_params=_params(("arbitrary", "arbitrary")),
        name="hgrn_prompt",
    )(proj, proj, proj, proj, lbt, gain)

    rows = SAMPLE_BB * ts
    base = n_p // rows
    srow = lambda i: base + i
    y_s, st_s = _sample_state_call(
        functools.partial(_hg_sample_body, ts=ts), (nb // SAMPLE_BB,),
        [_proj_spec(rows, COL_HG_Q, srow), _proj_spec(rows, COL_HG_F, srow),
         _proj_spec(rows, COL_HG_I, srow), _proj_spec(rows, COL_HG_G, srow),
         pl.BlockSpec((None, SUBLANES, HW), lambda i: (layer, 0, 0)),
         pl.BlockSpec((None, 1, HW), lambda i: (layer, 0, 0)),
         pl.BlockSpec((None, SAMPLE_BB, HEADS, DH, DH), lambda i: (layer, i, 0, 0, 0))],
        (proj, proj, proj, proj, lbt, gain, state),
        jax.ShapeDtypeStruct((nb * ts, HW), BF16), pl.BlockSpec((rows, HW), lambda i: (i, 0)),
        layer, nb, prev_s, "hgrn_sample")
    return (y_p, y_s), st_p, st_s


def _softmax_rows(s):
    e = jnp.exp(s - jnp.max(s, axis=-1, keepdims=True))
    return e / jnp.sum(e, axis=-1, keepdims=True)


def _xa_prompt_body(q_ref, k_ref, v_ref, y_ref):
    for h in range(HEADS):
        sl = slice(h * DH, (h + 1) * DH)
        a = _softmax_rows(_bdot_nt(q_ref[:, sl] * (DH ** -0.5), k_ref[:, sl]))
        y_ref[:, sl] = _bdot(a, v_ref[:, sl]).astype(BF16)


def _xa_sample_body(q_ref, k_ref, v_ref, y_ref, *, ts):
    n_mem = k_ref.shape[1] // HEADS
    pairs = [(j, h) for j in range(SAMPLE_BB) for h in range(HEADS)]
    q = q_ref[...] * (DH ** -0.5)
    scores = [_bdot_nt(q[j * ts:(j + 1) * ts, h * DH:(h + 1) * DH], k_ref[j, pl.ds(h, n_mem, stride=HEADS), :])
              for j, h in pairs]
    a = _softmax_rows(jnp.concatenate(scores, axis=0))
    for n, (j, h) in enumerate(pairs):
        y = _bdot(a[n * ts:(n + 1) * ts], v_ref[j, pl.ds(h, n_mem, stride=HEADS), :])
        y_ref[j * ts:(j + 1) * ts, h * DH:(h + 1) * DH] = y.astype(BF16)


def _cross_attention(proj, kv_p, cache_k, cache_v, layer, b, t, nb, ts):
    n_p = b * t
    n_mem = kv_p.shape[0] // b
    tq = _pick(t, (512, 256, 128))
    nq = t // tq
    y_p = pl.pallas_call(
        _xa_prompt_body,
        grid=(b, nq),
        in_specs=[_proj_spec(tq, COL_XA_Q, lambda bi, qi: bi * nq + qi),
                  pl.BlockSpec((n_mem, HW), lambda bi, qi: (bi, 0)),
                  pl.BlockSpec((n_mem, HW), lambda bi, qi: (bi, 1))],
        out_specs=pl.BlockSpec((tq, HW), lambda bi, qi: (bi * nq + qi, 0)),
        out_shape=jax.ShapeDtypeStruct((n_p, HW), BF16),
        compiler_params=_params(("arbitrary", "arbitrary")),
        name="xattn_prompt",
    )(proj, kv_p, kv_p)

    rows = SAMPLE_BB * ts
    base = n_p // rows
    cache_k = cache_k.reshape(DEPTH, nb, n_mem * HEADS, DH)
    cache_v = cache_v.reshape(DEPTH, nb, n_mem * HEADS, DH)
    kv_spec = pl.BlockSpec((None, SAMPLE_BB, n_mem * HEADS, DH), lambda i: (layer, i, 0, 0))
    y_s = pl.pallas_call(
        functools.partial(_xa_sample_body, ts=ts),
        grid=(nb // SAMPLE_BB,),
        in_specs=[_proj_spec(rows, COL_XA_Q, lambda i: base + i), kv_spec, kv_spec],
        out_specs=pl.BlockSpec((rows, HW), lambda i: (i, 0)),
        out_shape=jax.ShapeDtypeStruct((nb * ts, HW), BF16),
        compiler_params=_params(("arbitrary",)),
        name="xattn_sample",
    )(proj, cache_k, cache_v)
    return (y_p, y_s)


def _layer_norm(tv, g, b):
    mu = jnp.mean(tv, axis=-1, keepdims=True)
    var = jnp.mean(jnp.square(tv - mu), axis=-1, keepdims=True)
    return (tv - mu) * lax.rsqrt(var + LN_EPS) * g + b


def _merge_body(yrp_ref, yrs_ref, yhp_ref, yhs_ref, yxp_ref, yxs_ref, g0_ref, g1_ref, g2_ref, xp_ref, xs_ref,
                wr_ref, wh_ref, wx_ref, wo_ref, lg_ref, lb_ref,
                x1_ref, x1b_ref, x1t_ref, wr_s, wh_s, wx_s, wo_s, *, prompt_tiles):
    @pl.when(pl.program_id(0) == 0)
    def _():
        wr_s[...] = wr_ref[...].astype(BF16)
        wh_s[...] = wh_ref[...].astype(BF16)
        wx_s[...] = wx_ref[...].astype(BF16)
        wo_s[...] = wo_ref[...].astype(BF16)

    is_prompt = pl.program_id(0) < prompt_tiles

    def branch(yp_ref, ys_ref, w_s, gate_ref):
        y = jnp.where(is_prompt, yp_ref[...], ys_ref[...])
        return gate_ref[...].astype(F32) * jnp.dot(y, w_s[...], preferred_element_type=F32)

    m = (branch(yrp_ref, yrs_ref, wr_s, g0_ref) + branch(yhp_ref, yhs_ref, wh_s, g1_ref)
         + branch(yxp_ref, yxs_ref, wx_s, g2_ref))
    hmix = jnp.dot(m.astype(BF16), wo_s[...], preferred_element_type=F32)
    x = jnp.where(is_prompt, xp_ref[...], xs_ref[...])
    x1 = _layer_norm(DN_ALPHA * x + hmix, lg_ref[...], lb_ref[...])
    x1_ref[...] = x1
    x1b_ref[...] = x1.astype(BF16)
    tm = x1.shape[0]
    bits = lax.bitcast_convert_type(x1.astype(BF16).astype(F32), U32)
    half = D_MODEL // 2
    packed = (bits[:, :half] >> 16) | (bits[:, half:] & jnp.uint32(0xFFFF0000))
    for s in range(PACKED_TILES):
        x1t_ref[pl.ds(s, tm, stride=PACKED_TILES), :] = packed[:, s * LANES:(s + 1) * LANES]


def _merge(ys, gates, x, w_up_ret, w_up_hgrn, w_up_xattn, w_out, ln_g, ln_b, layer):
    n_p = ys[0].shape[0]
    nt = n_p + ys[1].shape[0]
    tm = _pick(nt, (256, 128))
    assert n_p % tm == 0 and ys[1].shape[0] % tm == 0
    p_tiles = n_p // tm
    row = lambda i: (i, 0)
    p_map = lambda i: (jnp.minimum(i, p_tiles - 1), 0)
    s_map = lambda i: (jnp.maximum(i - p_tiles, 0), 0)
    y_specs = [pl.BlockSpec((tm, HW), p_map), pl.BlockSpec((tm, HW), s_map)] * 3
    wspec = lambda k: pl.BlockSpec((None, k, D_MODEL), lambda i: (layer, 0, 0))
    vec = pl.BlockSpec((None, 1, D_MODEL), lambda i: (layer, 0, 0))
    gate_specs = [pl.BlockSpec((tm, D_MODEL), lambda i, c=c: (i, c)) for c in range(3)]
    return pl.pallas_call(
        functools.partial(_merge_body, prompt_tiles=p_tiles),
        grid=(nt // tm,),
        in_specs=y_specs + gate_specs + [pl.BlockSpec((tm, D_MODEL), p_map), pl.BlockSpec((tm, D_MODEL), s_map),
                  wspec(HW), wspec(HW), wspec(HW), wspec(D_MODEL), vec, vec],
        out_specs=[pl.BlockSpec((tm, D_MODEL), row), pl.BlockSpec((tm, D_MODEL), row),
                   pl.BlockSpec((tm * PACKED_TILES, LANES), row)],
        out_shape=[jax.ShapeDtypeStruct((nt, D_MODEL), F32),
                   jax.ShapeDtypeStruct((nt, D_MODEL), BF16),
                   jax.ShapeDtypeStruct((nt * PACKED_TILES, LANES), U32)],
        scratch_shapes=[pltpu.VMEM((HW, D_MODEL), BF16)] * 3 + [pltpu.VMEM((D_MODEL, D_MODEL), BF16)],
        compiler_params=_params(("arbitrary",)),
        name="merge_out_ln1",
    )(*ys, *([gates] * 3), *x, w_up_ret, w_up_hgrn, w_up_xattn, w_out, ln_g, ln_b)


def _router_body(x_ref, wt_ref, b_ref, eidx_ref, wn_ref):
    tm = x_ref.shape[0]
    x = x_ref[...]
    w = wt_ref[...]
    xh = x.astype(BF16)
    xl = (x - xh.astype(F32)).astype(BF16)
    wh = w.astype(BF16)
    wl = (w - wh.astype(F32)).astype(BF16)
    logits = _bdot_nt(wh, xh) + (_bdot_nt(wh, xl) + _bdot_nt(wl, xh))
    s = jax.nn.sigmoid(logits)
    sel = s + b_ref[...]
    neg = -jnp.inf
    groups = [sel[g * GROUP_SIZE:(g + 1) * GROUP_SIZE, :] for g in range(N_GROUPS)]
    ie = lax.broadcasted_iota(I32, (GROUP_SIZE, tm), 0).astype(F32)
    rows = []
    for blk in groups:
        m1 = jnp.max(blk, axis=0, keepdims=True)
        first = jnp.min(jnp.where(blk == m1, ie, float(GROUP_SIZE)), axis=0, keepdims=True)
        rows.append(m1 + jnp.max(jnp.where(ie == first, neg, blk), axis=0, keepdims=True))
    gscore = jnp.concatenate(rows, axis=0)
    ig = lax.broadcasted_iota(I32, gscore.shape, 0).astype(F32)
    gmask = jnp.zeros(gscore.shape, F32)
    for _ in range(TOPK_GROUPS):
        m = jnp.max(gscore, axis=0, keepdims=True)
        gi = jnp.min(jnp.where(gscore == m, ig, float(N_GROUPS)), axis=0, keepdims=True)
        hit = ig == gi
        gmask = jnp.where(hit, 1.0, gmask)
        gscore = jnp.where(hit, neg, gscore)
    masked = jnp.concatenate([jnp.where(gmask[g:g + 1, :] > 0.5, blk, neg)
                              for g, blk in enumerate(groups)], axis=0)
    ix = lax.broadcasted_iota(I32, masked.shape, 0).astype(F32)
    idxs, ws = [], []
    for _ in range(TOP_K):
        m = jnp.max(masked, axis=0, keepdims=True)
        ei = jnp.min(jnp.where(masked == m, ix, float(N_EXPERTS)), axis=0, keepdims=True)
        hit = ix == ei
        idxs.append(ei)
        ws.append(jnp.sum(jnp.where(hit, s, 0.0), axis=0, keepdims=True))
        masked = jnp.where(hit, neg, masked)
    wsum = ws[0]
    for w in ws[1:]:
        wsum = wsum + w
    pad = [jnp.zeros((1, tm), F32)] * (SUBLANES - TOP_K)
    eidx_ref[...] = jnp.concatenate(idxs + pad, axis=0).astype(I32)
    wn_ref[...] = jnp.concatenate([w / wsum * ROUTED_SCALE for w in ws] + pad, axis=0)


def _router(x1, w_router_t, b_router, layer):
    nt = x1.shape[0]
    tm = _pick(nt, (512, 256, 128))
    return pl.pallas_call(
        _router_body,
        grid=(nt // tm,),
        in_specs=[pl.BlockSpec((tm, D_MODEL), lambda i: (i, 0)),
                  pl.BlockSpec((None, N_EXPERTS, D_MODEL), lambda i: (layer, 0, 0)),
                  pl.BlockSpec((None, N_EXPERTS, 1), lambda i: (layer, 0, 0))],
        out_specs=[pl.BlockSpec((SUBLANES, tm), lambda i: (0, i))] * 2,
        out_shape=[jax.ShapeDtypeStruct((SUBLANES, nt), I32),
                   jax.ShapeDtypeStruct((SUBLANES, nt), F32)],
        compiler_params=_params(("arbitrary",)),
        name="moe_router",
    )(x1, w_router_t, b_router)


def _positions_body(eidx_ref, pos_ref, cnt_ref, off_ref, base_scr, off_scr):
    phase = pl.program_id(0)
    i = pl.program_id(1)
    tp = eidx_ref.shape[1]
    ix = lax.broadcasted_iota(I32, (N_EXPERTS, tp), 0)
    eidx = eidx_ref[...]
    member = jnp.zeros((N_EXPERTS, tp), F32)
    for k in range(TOP_K):
        member = member + (ix == eidx[k:k + 1, :]).astype(F32)
    tile_cnt = jnp.sum(member, axis=1, keepdims=True)

    @pl.when((phase == 0) & (i == 0))
    def _():
        base_scr[...] = jnp.zeros_like(base_scr)

    @pl.when((phase == 1) & (i == 0))
    def _():
        cnt = base_scr[...]
        er = lax.broadcasted_iota(I32, (N_EXPERTS, N_EXPERTS), 0)
        ec = lax.broadcasted_iota(I32, (N_EXPERTS, N_EXPERTS), 1)
        off = jnp.dot((ec < er).astype(F32), cnt, precision=HIGHEST, preferred_element_type=F32)
        off_scr[...] = off
        cnt_ref[...] = cnt
        off_ref[...] = off
        base_scr[...] = jnp.zeros_like(base_scr)

    @pl.when(phase == 1)
    def _():
        tr = lax.broadcasted_iota(I32, (tp, tp), 0)
        tc = lax.broadcasted_iota(I32, (tp, tp), 1)
        before = jnp.dot(member.astype(BF16), (tr < tc).astype(BF16), preferred_element_type=F32)
        where_to = before + (off_scr[...] + base_scr[...])[:, 0:1]
        rows = [jnp.sum(jnp.where(ix == eidx[k:k + 1, :], where_to, 0.0), axis=0, keepdims=True)
                for k in range(TOP_K)]
        rows += [jnp.zeros((1, tp), F32)] * (SUBLANES - TOP_K)
        pos_ref[...] = jnp.concatenate(rows, axis=0).astype(I32)

    base_scr[...] = base_scr[...] + tile_cnt


def _positions(eidx):
    nt = eidx.shape[1]
    tp = _pick(nt, (512, 256, 128))
    const = lambda p, i: (0, 0)
    return pl.pallas_call(
        _positions_body,
        grid=(2, nt // tp),
        in_specs=[pl.BlockSpec((SUBLANES, tp), lambda p, i: (0, i))],
        out_specs=[pl.BlockSpec((SUBLANES, tp), lambda p, i: (0, i * p)),
                   pl.BlockSpec((N_EXPERTS, LANES), const), pl.BlockSpec((N_EXPERTS, LANES), const)],
        out_shape=[jax.ShapeDtypeStruct((SUBLANES, nt), I32),
                   jax.ShapeDtypeStruct((N_EXPERTS, LANES), F32),
                   jax.ShapeDtypeStruct((N_EXPERTS, LANES), F32)],
        scratch_shapes=[pltpu.VMEM((N_EXPERTS, LANES), F32), pltpu.VMEM((N_EXPERTS, LANES), F32)],
        compiler_params=_params(("arbitrary", "arbitrary")),
        name="moe_positions",
    )(eidx)


T_TILE, T_EXPERT, T_LO, T_HI, T_FRESH, T_NEWEXP = range(6)

def _table_body(cnt_ref, off_ref, tbl_ref, *, tile_rows):
    te = float(tile_rows)
    n = tbl_ref.shape[1]
    cnt = cnt_ref[...]
    off = off_ref[...]
    first = jnp.floor(off * (1.0 / te))
    last = jnp.floor((off + cnt - 1.0) * (1.0 / te))
    nst = jnp.where(cnt > 0.0, last - first + 1.0, 0.0)
    er = lax.broadcasted_iota(I32, (N_EXPERTS, N_EXPERTS), 0)
    ec = lax.broadcasted_iota(I32, (N_EXPERTS, N_EXPERTS), 1)
    s_end = jnp.dot((ec <= er).astype(F32), nst, precision=HIGHEST, preferred_element_type=F32)
    s_beg = s_end - nst
    total = s_end[N_EXPERTS - 1:N_EXPERTS, 0:1]
    sidx = lax.broadcasted_iota(I32, (1, n), 1).astype(F32)
    s = jnp.minimum(sidx, total - 1.0)
    e_s = jnp.sum((s_end[:, 0:1] <= s).astype(F32), axis=0, keepdims=True)
    hot = lax.broadcasted_iota(I32, (N_EXPERTS, n), 0).astype(F32) == e_s

    def pick(col):
        return jnp.sum(jnp.where(hot, col[:, 0:1], 0.0), axis=0, keepdims=True)

    tile = pick(first) + s - pick(s_beg)
    valid = sidx < total
    o, c = pick(off), pick(cnt)
    lo = jnp.where(valid, jnp.maximum(o, tile * te), 0.0)
    hi = jnp.where(valid, jnp.minimum(o + c, (tile + 1.0) * te), 0.0)
    head = sidx == 0.0
    fresh = jnp.where((tile != pltpu.roll(tile, 1, 1)) | head, 1.0, 0.0)
    newexp = jnp.where((e_s != pltpu.roll(e_s, 1, 1)) | head, 1.0, 0.0)
    pad = [jnp.zeros((1, n), F32)] * (SUBLANES - 6)
    tbl_ref[...] = jnp.concatenate([tile, e_s, lo, hi, fresh, newexp] + pad, axis=0).astype(I32)


def _step_table(cnt, off, n_rows, te):
    n_steps = n_rows // te + N_EXPERTS
    width = -(-n_steps // LANES) * LANES
    tbl = pl.pallas_call(
        functools.partial(_table_body, tile_rows=te),
        out_shape=jax.ShapeDtypeStruct((SUBLANES, width), I32),
        name="moe_step_table",
    )(cnt, off)
    return tbl, n_steps


def _wait_tile_rows(like_src, dst_rows_ref, sem_ref):
    rows = like_src.shape[0]
    for _ in range(TOP_K):
        pltpu.make_async_copy(like_src, dst_rows_ref.at[pl.ds(0, rows)], sem_ref).wait()


def _dispatch_body(pos_ref, xt_ref, xb_ref, wsg_ref, wsu_ref, wsd_ref, xs_ref, sh_ref,
                   pos_s, wsg_s, wsu_s, wsd_s, sem_p, sem):
    td = pos_ref.shape[1]

    @pl.when(pl.program_id(0) == 0)
    def _():
        wsg_s[...] = wsg_ref[...].astype(BF16)
        wsu_s[...] = wsu_ref[...].astype(BF16)
        wsd_s[...] = wsd_ref[...].astype(BF16)

    cp = pltpu.make_async_copy(pos_ref, pos_s, sem_p)
    cp.start()
    cp.wait()

    def issue(g, carry):
        for u in range(ISSUE_UNROLL):
            r = g * ISSUE_UNROLL + u
            for k in range(TOP_K):
                pltpu.make_async_copy(xt_ref.at[r], xs_ref.at[pos_s[k, r]], sem).start(priority=k % 2)
        return carry

    half = td // 2
    groups = half // ISSUE_UNROLL
    for part in range(2):
        lax.fori_loop(part * groups, (part + 1) * groups, issue, 0)
        rows = slice(part * half, (part + 1) * half)
        xb = xb_ref[rows, :]
        hs = (_silu(jnp.dot(xb, wsg_s[...], preferred_element_type=F32))
              * jnp.dot(xb, wsu_s[...], preferred_element_type=F32))
        sh_ref[rows, :] = jnp.dot(hs.astype(BF16), wsd_s[...], preferred_element_type=F32)
    _wait_tile_rows(xt_ref, xs_ref, sem)


def _dispatch(pos, x1t, x1b, w_s_gate, w_s_up, w_s_down, layer):
    nt = x1t.shape[0]
    td = _pick(nt, (1024, 512, 256, 128))
    d_sh = w_s_gate.shape[2]
    w_in_spec = pl.BlockSpec((None, D_MODEL, d_sh), lambda i: (layer, 0, 0))
    return pl.pallas_call(
        _dispatch_body,
        grid=(nt // td,),
        in_specs=[pl.BlockSpec((SUBLANES, td), lambda i: (0, i)),
                  pl.BlockSpec((td,) + x1t.shape[1:], lambda i: (i, 0, 0)),
                  pl.BlockSpec((td, D_MODEL), lambda i: (i, 0)),
                  w_in_spec, w_in_spec, pl.BlockSpec((None, d_sh, D_MODEL), lambda i: (layer, 0, 0))],
        out_specs=[pl.BlockSpec(memory_space=pl.ANY), pl.BlockSpec((td, D_MODEL), lambda i: (i, 0))],
        out_shape=[jax.ShapeDtypeStruct((nt * TOP_K,) + x1t.shape[1:], x1t.dtype),
                   jax.ShapeDtypeStruct((nt, D_MODEL), F32)],
        scratch_shapes=[pltpu.SMEM((SUBLANES, td), I32),
                        pltpu.VMEM((D_MODEL, d_sh), BF16), pltpu.VMEM((D_MODEL, d_sh), BF16),
                        pltpu.VMEM((d_sh, D_MODEL), BF16),
                        pltpu.SemaphoreType.DMA, pltpu.SemaphoreType.DMA],
        compiler_params=_params(("arbitrary",)),
        name="moe_dispatch",
    )(pos, x1t, x1b, w_s_gate, w_s_up, w_s_down)


def _experts_body(tbl_ref, xs_ref, wg_ref, wu_ref, wd_ref, ye_ref, wg_s, wu_s, wd_s):
    s = pl.program_id(0)
    te = xs_ref.shape[0] // PACKED_TILES
    lo = tbl_ref[T_LO, s]
    hi = tbl_ref[T_HI, s]

    @pl.when(tbl_ref[T_NEWEXP, s] == 1)
    def _():
        wg_s[...] = wg_ref[...].astype(BF16)
        wu_s[...] = wu_ref[...].astype(BF16)
        wd_s[...] = wd_ref[...].astype(BF16)

    @pl.when(tbl_ref[T_FRESH, s] == 1)
    def _():
        ye_ref[...] = jnp.zeros_like(ye_ref)

    @pl.when(hi > lo)
    def _():
        words = [xs_ref[pl.ds(t, te, stride=PACKED_TILES), :] for t in range(PACKED_TILES)]
        low = [lax.bitcast_convert_type(w << 16, F32).astype(BF16) for w in words]
        high = [lax.bitcast_convert_type(w & jnp.uint32(0xFFFF0000), F32).astype(BF16) for w in words]
        x = jnp.concatenate(low + high, axis=-1)
        g = jnp.dot(x, wg_s[...], preferred_element_type=F32)
        u = jnp.dot(x, wu_s[...], preferred_element_type=F32)
        y = jnp.dot((_silu(g) * u).astype(BF16), wd_s[...], preferred_element_type=F32)
        row = tbl_ref[T_TILE, s] * te + lax.broadcasted_iota(I32, (te, LANES), 0)
        mine = (row >= lo) & (row < hi)
        for t in range(ROW_TILES):
            sl = pl.ds(t, te, stride=ROW_TILES)
            ye_ref[sl, :] = jnp.where(mine, y[:, t * LANES:(t + 1) * LANES], ye_ref[sl, :])


def _experts(tbl, n_steps, te, xs, w_gate, w_up, w_down, layer):
    n_rows = xs.shape[0] // PACKED_TILES
    tile_map = lambda s, tbl: (tbl[T_TILE, s], 0)
    w_map = lambda s, tbl: (layer, tbl[T_EXPERT, s], 0, 0)
    w_in_spec = pl.BlockSpec((None, None, D_MODEL, D_EXPERT), w_map)
    w_dn_spec = pl.BlockSpec((None, None, D_EXPERT, D_MODEL), w_map)
    return pl.pallas_call(
        _experts_body,
        grid_spec=pltpu.PrefetchScalarGridSpec(
            num_scalar_prefetch=1,
            grid=(n_steps,),
            in_specs=[pl.BlockSpec((te * PACKED_TILES, LANES), tile_map), w_in_spec, w_in_spec, w_dn_spec],
            out_specs=pl.BlockSpec((te * ROW_TILES, LANES), tile_map),
            scratch_shapes=[pltpu.VMEM((D_MODEL, D_EXPERT), BF16), pltpu.VMEM((D_MODEL, D_EXPERT), BF16),
                            pltpu.VMEM((D_EXPERT, D_MODEL), BF16)]),
        out_shape=jax.ShapeDtypeStruct((n_rows * ROW_TILES, LANES), F32),
        compiler_params=_params(("arbitrary",)),
        name="moe_experts",
    )(tbl, xs, w_gate, w_up, w_down)


def _combine_body(pos_ref, wn_ref, ye_ref, x1_ref, sh_ref, lg_ref, lb_ref,
                  x2p_ref, x2s_ref, x2b_ref, pos_s, buf, sem_p, sem, *, prompt_tiles):
    i = pl.program_id(0)
    n = pl.num_programs(0)
    tc = wn_ref.shape[1]
    slot = i % 2
    tile_rows = tc * ROW_TILES

    def request(tile, into):
        cp = pltpu.make_async_copy(pos_ref.at[tile], pos_s, sem_p)
        cp.start()
        cp.wait()

        def issue(g, carry):
            for u in range(ISSUE_UNROLL):
                r = g * ISSUE_UNROLL + u
                for k in range(TOP_K):
                    at = pl.multiple_of(((into * TOP_K + k) * tc + r) * ROW_TILES, ROW_TILES)
                    pltpu.make_async_copy(ye_ref.at[pos_s[k, r]], buf.at[pl.ds(at, ROW_TILES)],
                                          sem.at[into]).start(priority=k % 2)
            return carry

        lax.fori_loop(0, tc // ISSUE_UNROLL, issue, 0)

    @pl.when(i == 0)
    def _():
        request(0, 0)

    @pl.when(i + 1 < n)
    def _():
        request(i + 1, 1 - slot)

    for k in range(TOP_K):
        pltpu.make_async_copy(buf.at[pl.ds(0, tile_rows)], buf.at[pl.ds(tile_rows, tile_rows)], sem.at[slot]).wait()

    w = wn_ref[...]
    acc = [None] * ROW_TILES
    for k in range(TOP_K):
        wcol = jnp.concatenate([jnp.broadcast_to(w[k:k + 1, c * LANES:(c + 1) * LANES], (LANES, LANES)).T
                                for c in range(tc // LANES)], axis=0)
        base = (slot * TOP_K + k) * tile_rows
        for t in range(ROW_TILES):
            term = wcol * buf[pl.ds(base + t, tc, stride=ROW_TILES), :]
            acc[t] = term if acc[t] is None else acc[t] + term
    routed = jnp.concatenate(acc, axis=-1)
    x2 = _layer_norm(DN_ALPHA * x1_ref[...] + (routed + sh_ref[...]), lg_ref[...], lb_ref[...])
    x2b_ref[...] = x2.astype(BF16)

    @pl.when(i < prompt_tiles)
    def _():
        x2p_ref[...] = x2

    @pl.when(i >= prompt_tiles)
    def _():
        x2s_ref[...] = x2


def _combine(pos, wn, ye, x1, shared, ln_g, ln_b, layer, n_p):
    nt = x1.shape[0]
    tc = _pick(nt, (256, 128))
    assert n_p % tc == 0
    n_tiles = nt // tc
    p_tiles = n_p // tc
    pos3 = pos.reshape(SUBLANES, n_tiles, tc).transpose(1, 0, 2)
    row = lambda i: (i, 0)
    vec = pl.BlockSpec((None, 1, D_MODEL), lambda i: (layer, 0, 0))
    return pl.pallas_call(
        functools.partial(_combine_body, prompt_tiles=p_tiles),
        grid=(n_tiles,),
        in_specs=[pl.BlockSpec((n_tiles, SUBLANES, tc), lambda i: (0, 0, 0)),
                  pl.BlockSpec((SUBLANES, tc), lambda i: (0, i)),
                  pl.BlockSpec(memory_space=pl.ANY),
                  pl.BlockSpec((tc, D_MODEL), row), pl.BlockSpec((tc, D_MODEL), row), vec, vec],
        out_specs=[pl.BlockSpec((tc, D_MODEL), lambda i: (jnp.minimum(i, p_tiles - 1), 0)),
                   pl.BlockSpec((tc, D_MODEL), lambda i: (jnp.maximum(i - p_tiles, 0), 0)),
                   pl.BlockSpec((tc, D_MODEL), row)],
        out_shape=[jax.ShapeDtypeStruct((n_p, D_MODEL), F32), jax.ShapeDtypeStruct((nt - n_p, D_MODEL), F32),
                   jax.ShapeDtypeStruct((nt, D_MODEL), BF16)],
        scratch_shapes=[pltpu.SMEM((SUBLANES, tc), I32),
                        pltpu.VMEM((2 * TOP_K * tc * ROW_TILES, LANES), F32),
                        pltpu.SemaphoreType.DMA, pltpu.SemaphoreType.DMA((2,))],
        compiler_params=_params(("arbitrary",)),
        name="moe_combine_ln2",
    )(pos3, wn, ye, x1, shared, ln_g, ln_b)


def _rope_tables(t, pos0):
    inv = 1.0 / (ROPE_BASE ** (jnp.arange(0, DH, 2, dtype=F32) / DH))
    ang = (jnp.arange(t, dtype=F32) + pos0)[:, None] * inv[None, :]
    cos, sin = jnp.cos(ang), jnp.sin(ang)
    return jnp.concatenate([cos, cos], axis=-1), jnp.concatenate([-sin, sin], axis=-1)


def kernel(x_prompt, x_sample, mem_prompt, state_ret, state_hgrn, cache_mem_k, cache_mem_v, w_in, w_up_ret, w_up_hgrn, w_up_xattn, w_out, w_mem_kv, ret_norm_g, hgrn_norm_g, lb_logits, ln1_g, ln1_b, ln2_g, ln2_b, w_router, b_router, w_e_gate, w_e_up, w_e_down, w_s_gate, w_s_up, w_s_down):
    b, t, d = x_prompt.shape
    nb, ts, _ = x_sample.shape
    n_mem = mem_prompt.shape[1]
    assert d == D_MODEL and t % RET_CHUNK == 0 and nb % SAMPLE_BB == 0
    assert ts & (ts - 1) == 0 and HG_CHUNK % ts == 0 and RET_CHUNK % ts == 0
    n_p, n_s = b * t, nb * ts
    nt = n_p + n_s
    assert n_p % (SAMPLE_BB * ts) == 0

    lb_cum = jnp.cumsum(jax.nn.softmax(lb_logits.astype(F32), axis=0), axis=0)
    lbs = lb_cum - lb_cum[0:1]
    lbt = jnp.stack([jnp.log(lbs), jnp.log1p(-lbs), 1.0 - lbs] + [jnp.zeros_like(lbs)] * (SUBLANES - 3), axis=1)
    gl = jnp.broadcast_to(jnp.log1p(-jnp.exp2(-5.0 - jnp.arange(HEADS, dtype=F32)))[:, None], (HEADS, DH))
    cos_p, sin_p = _rope_tables(t, 0)
    cos_s, sin_s = _rope_tables(ts, PAST_LEN)
    cos_s, sin_s = jnp.tile(cos_s, (SAMPLE_BB, 1)), jnp.tile(sin_s, (SAMPLE_BB, 1))
    vec3 = lambda a: a.reshape(DEPTH, 1, -1)
    w_router_t = jnp.swapaxes(w_router, 1, 2)
    b_router3 = b_router.reshape(DEPTH, N_EXPERTS, 1)
    mem2 = mem_prompt.reshape(b * n_mem, d)

    x = (x_prompt.reshape(n_p, d), x_sample.reshape(n_s, d))
    xb = jnp.concatenate([x[0].astype(BF16), x[1].astype(BF16)], axis=0)
    tm_proj = _pick(nt, (1024, 512, 128))
    te = _pick(nt * TOP_K, EXPERT_TILES)
    outs = {k: [] for k in ("ret_p", "hg_p", "mk", "mv")}
    ret_s = hg_s = None
    for l in range(DEPTH):
        mix_tiles = COL_GATES * HW // PROJ_TILE_N
        proj = _matmul(xb, w_in, l, tm_proj, PROJ_TILE_N, n=COL_GATES * HW)
        gates = _matmul(xb, w_in, l, tm_proj, PROJ_TILE_N, first_tile=mix_tiles, n=3 * D_MODEL, gate=True)
        kv_p = _matmul(mem2, w_mem_kv, l, _pick(b * n_mem, (1024, 512, 256)), 2 * HW)
        yr, ret_p, ret_s = _retention(proj, state_ret, l, cos_p, sin_p, cos_s, sin_s, gl,
                                      vec3(ret_norm_g), b, t, nb, ts, ret_s)
        yh, hg_p, hg_s = _hgrn(proj, state_hgrn, l, lbt, vec3(hgrn_norm_g), b, t, nb, ts, hg_s)
        yx = _cross_attention(proj, kv_p, cache_mem_k, cache_mem_v, l, b, t, nb, ts)
        x1, x1b, x1t = _merge((*yr, *yh, *yx), gates, x, w_up_ret, w_up_hgrn, w_up_xattn, w_out,
                              vec3(ln1_g), vec3(ln1_b), l)
        eidx, wn = _router(x1, w_router_t, b_router3, l)
        pos, cnt, off = _positions(eidx)
        tbl, n_steps = _step_table(cnt, off, nt * TOP_K, te)
        xs, shared = _dispatch(pos, x1t.reshape(nt, PACKED_TILES, LANES), x1b, w_s_gate, w_s_up, w_s_down, l)
        ye = _experts(tbl, n_steps, te, xs.reshape(-1, LANES), w_e_gate, w_e_up, w_e_down, l)
        ye = ye.reshape(-1, ROW_TILES, LANES)
        x_p, x_s, xb = _combine(pos, wn, ye, x1, shared, vec3(ln2_g), vec3(ln2_b), l, n_p)
        x = (x_p, x_s)
        outs["ret_p"].append(ret_p)
        outs["hg_p"].append(hg_p)
        outs["mk"].append(kv_p[:, :HW].reshape(b, n_mem, HEADS, DH))
        outs["mv"].append(kv_p[:, HW:].reshape(b, n_mem, HEADS, DH))
    return (x[0].reshape(b, t, d), x[1].reshape(nb, ts, d),
            jnp.stack(outs["ret_p"]), jnp.stack(outs["hg_p"]), jnp.stack(outs["mk"]), jnp.stack(outs["mv"]),
            ret_s, hg_s)
```

```python
import functools

import jax
import jax.numpy as jnp
from jax import lax
from jax.experimental import pallas as pl
from jax.experimental.pallas import tpu as pltpu

F32 = jnp.float32
BF16 = jnp.bfloat16
I32 = jnp.int32
HIGHEST = lax.Precision.HIGHEST

D_MODEL = 1024
DEPTH = 2
PAST_LEN = 16384
HEADS = 4
DH = 128
HW = HEADS * DH
RET_CHUNK = 128
HG_CHUNK = 16
ROPE_BASE = 10000.0
N_EXPERTS = 64
N_GROUPS = 8
GROUP_SIZE = N_EXPERTS // N_GROUPS
TOPK_GROUPS = 4
TOP_K = 6
D_EXPERT = 256
ROUTED_SCALE = 2.5
LN_EPS = 1e-5
DN_ALPHA = (2 * DEPTH) ** 0.25
N_IN = 9 * HW + 3 * D_MODEL
COL_RET_Q, COL_RET_K, COL_RET_V, COL_RET_G = 0, 1, 2, 3
COL_HG_Q, COL_HG_F, COL_HG_I, COL_HG_G = 4, 5, 6, 7
COL_XA_Q = 8
COL_GATES = 9
LANES = 128
SUBLANES = 8
ROW_TILES = D_MODEL // LANES
PACKED_TILES = ROW_TILES // 2
U32 = jnp.uint32
SAMPLE_BB = 8
EXPERT_TILES = (512, 256)
ISSUE_UNROLL = 8
PROJ_TILE_N = 1536
VMEM_LIMIT = 56 * 1024 * 1024


def _params(sem):
    return pltpu.CompilerParams(dimension_semantics=sem, vmem_limit_bytes=VMEM_LIMIT)


def _bdot(a, b):
    return jnp.dot(a.astype(BF16), b.astype(BF16), preferred_element_type=F32)


def _bdot_nt(a, b):
    return lax.dot_general(a.astype(BF16), b.astype(BF16), (((1,), (1,)), ((), ())),
                           preferred_element_type=F32)


def _bdot_tn(a, b):
    return lax.dot_general(a.astype(BF16), b.astype(BF16), (((0,), (0,)), ((), ())),
                           preferred_element_type=F32)


def _silu(x):
    return x * jax.nn.sigmoid(x)


def _pick(n, prefs):
    for p in prefs:
        if n % p == 0:
            return p
    raise ValueError(f"no tile for {n}")


def _mm_body(x_ref, w_ref, o_ref, wb_ref, *, gate):
    @pl.when(pl.program_id(1) == 0)
    def _():
        wb_ref[...] = w_ref[...].astype(BF16)

    acc = jnp.dot(x_ref[...].astype(BF16), wb_ref[...], preferred_element_type=F32)
    o_ref[...] = (jax.nn.sigmoid(acc) if gate else acc).astype(o_ref.dtype)


def _matmul(x, w, layer, tm, tn, first_tile=0, n=None, gate=False):
    m, k = x.shape
    n = w.shape[2] if n is None else n
    return pl.pallas_call(
        functools.partial(_mm_body, gate=gate),
        grid=(n // tn, m // tm),
        in_specs=[pl.BlockSpec((tm, k), lambda j, i: (i, 0)),
                  pl.BlockSpec((None, k, tn), lambda j, i: (layer, 0, first_tile + j))],
        out_specs=pl.BlockSpec((tm, tn), lambda j, i: (i, j)),
        out_shape=jax.ShapeDtypeStruct((m, n), BF16 if gate else F32),
        scratch_shapes=[pltpu.VMEM((k, tn), BF16)],
        compiler_params=_params(("arbitrary", "arbitrary")),
        name="dense_matmul",
    )(x, w)


def _rotary(x, cos, sin_signed):
    return x * cos + pltpu.roll(x, DH // 2, 1) * sin_signed


def _group_norm_gate(o, gain, gate):
    mu = jnp.mean(o, axis=-1, keepdims=True)
    var = jnp.mean(jnp.square(o - mu), axis=-1, keepdims=True)
    return (o - mu) * lax.rsqrt(var + LN_EPS) * gain * _silu(gate)


def _ret_prompt_body(q_ref, k_ref, v_ref, g_ref, cos_ref, sin_ref, gl_ref, gain_ref,
                     y_ref, st_ref, s_scr, intra_scr, qdec_scr, kdec_scr):
    c = pl.program_id(1)
    ch = RET_CHUNK

    @pl.when((pl.program_id(0) == 0) & (c == 0))
    def _():
        ri = lax.broadcasted_iota(I32, (ch, ch), 0)
        ci = lax.broadcasted_iota(I32, (ch, ch), 1)
        rel = (ri - ci).astype(F32)
        idx = lax.broadcasted_iota(I32, (ch, DH), 0).astype(F32)
        for h in range(HEADS):
            gl = gl_ref[h:h + 1, :]
            intra_scr[h] = jnp.where(rel >= 0, jnp.exp(gl * rel), 0.0)
            qdec_scr[h] = jnp.exp(gl * (idx + 1.0))
            kdec_scr[h] = jnp.exp(gl * (ch - 1.0 - idx))

    @pl.when(c == 0)
    def _():
        s_scr[...] = jnp.zeros_like(s_scr)

    cos = cos_ref[...]
    sin = sin_ref[...]
    for h in range(HEADS):
        sl = slice(h * DH, (h + 1) * DH)
        gl = gl_ref[h:h + 1, :]
        qr = _rotary(q_ref[:, sl], cos, sin)
        kr = _rotary(k_ref[:, sl], cos, sin) * (DH ** -0.5)
        v = v_ref[:, sl]
        att = _bdot_nt(qr, kr) * intra_scr[h]
        s = s_scr[h]
        o = _bdot(att, v) + _bdot(qr, s) * qdec_scr[h]
        s_scr[h] = s * jnp.exp(gl * float(ch)) + _bdot_tn(kr * kdec_scr[h], v)
        y_ref[:, sl] = _group_norm_gate(o, gain_ref[:, sl], g_ref[:, sl]).astype(BF16)

    @pl.when(c == pl.num_programs(1) - 1)
    def _():
        st_ref[0] = s_scr[...]


def _ret_sample_body(q_ref, k_ref, v_ref, g_ref, cos_ref, sin_ref, gl_ref, gain_ref, sin_ref_state,
                     y_ref, st_ref, *, ts):
    rows = SAMPLE_BB * ts
    shift = ts.bit_length() - 1
    cos = cos_ref[...]
    sin = sin_ref[...]
    ri = lax.broadcasted_iota(I32, (rows, rows), 0)
    ci = lax.broadcasted_iota(I32, (rows, rows), 1)
    rel = (ri - ci).astype(F32)
    mask = ((ri >> shift) == (ci >> shift)) & (ri >= ci)
    idx = (lax.broadcasted_iota(I32, (rows, DH), 0) & (ts - 1)).astype(F32)
    for h in range(HEADS):
        sl = slice(h * DH, (h + 1) * DH)
        gl = gl_ref[h:h + 1, :]
        qr = _rotary(q_ref[:, sl], cos, sin)
        kr = _rotary(k_ref[:, sl], cos, sin) * (DH ** -0.5)
        v = v_ref[:, sl]
        intra = jnp.where(mask, jnp.exp(gl[:, :rows] * rel), 0.0)
        o_intra = _bdot(_bdot_nt(qr, kr) * intra, v)
        q_dec = jnp.exp(gl * (idx + 1.0))
        kd = kr * jnp.exp(gl * (ts - 1.0 - idx))
        c_dec = jnp.exp(gl * float(ts))
        outs = []
        for j in range(SAMPLE_BB):
            rs = slice(j * ts, (j + 1) * ts)
            s = sin_ref_state[j, h]
            outs.append(o_intra[rs] + _bdot(qr[rs], s) * q_dec[rs])
            new_state = s * c_dec + _bdot_tn(kd[rs], v[rs])
            for slot in range(st_ref.shape[0]):
                st_ref[slot, j, h] = new_state
        o = jnp.concatenate(outs, axis=0)
        y_ref[:, sl] = _group_norm_gate(o, gain_ref[:, sl], g_ref[:, sl]).astype(BF16)


def _proj_spec(rows, col, row_map):
    return pl.BlockSpec((rows, HW), lambda *a: (row_map(*a), col))


def _sample_state_call(body, grid, in_specs, args, y_shape, y_spec, layer, nb, prev, name):
    st_shape = jax.ShapeDtypeStruct((DEPTH, nb, HEADS, DH, DH), F32)
    slots = DEPTH if prev is None else 1
    st_spec = pl.BlockSpec((slots, SAMPLE_BB, HEADS, DH, DH), lambda i: (layer, i, 0, 0, 0))
    aliases = {}
    if prev is not None:
        n_in = len(args)
        inner = body
        body = lambda *refs: inner(*refs[:n_in], *refs[n_in + 1:])
        in_specs = in_specs + [pl.BlockSpec(memory_space=pl.ANY)]
        args = args + (prev,)
        aliases = {n_in: 1}
    return pl.pallas_call(
        body, grid=grid, in_specs=in_specs, out_specs=[y_spec, st_spec], out_shape=[y_shape, st_shape],
        input_output_aliases=aliases, compiler_params=_params(("arbitrary",)), name=name,
    )(*args)


def _retention(proj, state, layer, cos_p, sin_p, cos_s, sin_s, gl, gain, b, t, nb, ts, prev_s):
    n_p = b * t
    nc = t // RET_CHUNK
    prow = lambda bi, c: bi * nc + c
    const2 = lambda *a: (0, 0)
    y_p, st_p = pl.pallas_call(
        _ret_prompt_body,
        grid=(b, nc),
        in_specs=[_proj_spec(RET_CHUNK, COL_RET_Q, prow), _proj_spec(RET_CHUNK, COL_RET_K, prow),
                  _proj_spec(RET_CHUNK, COL_RET_V, prow), _proj_spec(RET_CHUNK, COL_RET_G, prow),
                  pl.BlockSpec((RET_CHUNK, DH), lambda bi, c: (c, 0)),
                  pl.BlockSpec((RET_CHUNK, DH), lambda bi, c: (c, 0)),
                  pl.BlockSpec((HEADS, DH), const2),
                  pl.BlockSpec((None, 1, HW), lambda bi, c: (layer, 0, 0))],
        out_specs=[pl.BlockSpec((RET_CHUNK, HW), lambda bi, c: (prow(bi, c), 0)),
                   pl.BlockSpec((1, HEADS, DH, DH), lambda bi, c: (bi, 0, 0, 0))],
        out_shape=[jax.ShapeDtypeStruct((n_p, HW), BF16),
                   jax.ShapeDtypeStruct((b, HEADS, DH, DH), F32)],
        scratch_shapes=[pltpu.VMEM((HEADS, DH, DH), F32), pltpu.VMEM((HEADS, RET_CHUNK, RET_CHUNK), F32),
                        pltpu.VMEM((HEADS, RET_CHUNK, DH), F32), pltpu.VMEM((HEADS, RET_CHUNK, DH), F32)],
        compiler_params=_params(("arbitrary", "arbitrary")),
        name="retention_prompt",
    )(proj, proj, proj, proj, cos_p, sin_p, gl, gain)

    rows = SAMPLE_BB * ts
    base = n_p // rows
    srow = lambda i: base + i
    y_s, st_s = _sample_state_call(
        functools.partial(_ret_sample_body, ts=ts), (nb // SAMPLE_BB,),
        [_proj_spec(rows, COL_RET_Q, srow), _proj_spec(rows, COL_RET_K, srow),
         _proj_spec(rows, COL_RET_V, srow), _proj_spec(rows, COL_RET_G, srow),
         pl.BlockSpec((rows, DH), const2), pl.BlockSpec((rows, DH), const2),
         pl.BlockSpec((HEADS, DH), const2),
         pl.BlockSpec((None, 1, HW), lambda i: (layer, 0, 0)),
         pl.BlockSpec((None, SAMPLE_BB, HEADS, DH, DH), lambda i: (layer, i, 0, 0, 0))],
        (proj, proj, proj, proj, cos_s, sin_s, gl, gain, state),
        jax.ShapeDtypeStruct((nb * ts, HW), BF16), pl.BlockSpec((rows, HW), lambda i: (i, 0)),
        layer, nb, prev_s, "retention_sample")
    return (y_p, y_s), st_p, st_s


def _mask_sums(masks, x):
    hi = x.astype(BF16)
    rest = x - hi.astype(F32)
    mid = rest.astype(BF16)
    lo = (rest - mid.astype(F32)).astype(BF16)
    m = jnp.concatenate([mk.astype(BF16) for mk in masks], axis=0)
    dot = functools.partial(jnp.dot, preferred_element_type=F32)
    out = dot(m, hi) + (dot(m, mid) + dot(m, lo))
    rows = masks[0].shape[0]
    return [out[i * rows:(i + 1) * rows] for i in range(len(masks))]


def _hg_prepare(hq_ref, hf_ref, lbt_ref, rows, chunk, with_prefix=False):
    shift = chunk.bit_length() - 1
    ri = lax.broadcasted_iota(I32, (rows, rows), 0)
    ci = lax.broadcasted_iota(I32, (rows, rows), 1)
    same = (ri >> shift) == (ci >> shift)
    causal = same & (ci <= ri)
    z = hf_ref[...]
    log_lb = lbt_ref[0:1, :]
    log_1m_lb = lbt_ref[1:2, :]
    one_m_lb = lbt_ref[2:3, :]
    log_sig = jnp.minimum(z, 0.0) - jnp.log(1.0 + jnp.exp(-jnp.abs(z)))
    bterm = log_1m_lb + log_sig
    logf = jnp.maximum(log_lb, bterm) + jnp.log(1.0 + jnp.exp(-jnp.abs(log_lb - bterm)))
    kh = one_m_lb * jax.nn.sigmoid(-z)
    qh = _silu(hq_ref[...]) * (DH ** -0.5)
    masks = [causal, same] + ([(ci >> shift) < (ri >> shift)] if with_prefix else [])
    cum, tot, *pre = _mask_sums(masks, logf)
    qi = qh * jnp.exp(cum)
    ki = kh * jnp.exp(-cum)
    ke = kh * jnp.exp(tot - cum)
    return causal, qi, ki, ke, tot, (pre[0] if with_prefix else None)


def _rms_norm_gate(o, gain, gate):
    return o * lax.rsqrt(jnp.mean(jnp.square(o), axis=-1, keepdims=True) + LN_EPS) * gain * _silu(gate)


def _hg_prompt_body(hq_ref, hf_ref, hi_ref, hg_ref, lbt_ref, gain_ref, y_ref, st_ref, s_scr):
    c = pl.program_id(1)

    @pl.when(c == 0)
    def _():
        s_scr[...] = jnp.zeros_like(s_scr)

    rows = RET_CHUNK
    n_sub = rows // HG_CHUNK
    shift = HG_CHUNK.bit_length() - 1
    causal, qi, ki, ke, tot, pre = _hg_prepare(hq_ref, hf_ref, lbt_ref, rows, HG_CHUNK, with_prefix=True)
    sub = lax.broadcasted_iota(I32, (rows, DH), 0) >> shift
    v = hi_ref[...]
    for h in range(HEADS):
        sl = slice(h * DH, (h + 1) * DH)
        q_h, ke_h, v_h, pre_h = qi[:, sl], ke[:, sl], v[:, sl], pre[:, sl]
        att = jnp.where(causal, _bdot_nt(q_h, ki[:, sl]), 0.0)
        st0 = s_scr[h]
        o = _bdot(att, v_h) + _bdot_nt(q_h * jnp.exp(pre_h), st0)
        end_last = pre_h[rows - 1:rows] + tot[rows - 1:rows, sl]
        st = st0 * jnp.exp(end_last)
        for i in range(n_sub):
            rs = slice(i * HG_CHUNK, (i + 1) * HG_CHUNK)
            u_t = _bdot_tn(v_h[rs], ke_h[rs])
            if i + 1 < n_sub:
                end_i = pre_h[(i + 1) * HG_CHUNK:(i + 1) * HG_CHUNK + 1]
                later = q_h * jnp.exp(jnp.where(sub > i, pre_h - end_i, -jnp.inf))
                o = o + _bdot_nt(later, u_t)
                st = st + u_t * jnp.exp(end_last - end_i)
            else:
                st = st + u_t
        s_scr[h] = st
        y_ref[:, sl] = _rms_norm_gate(o, gain_ref[:, sl], hg_ref[:, sl]).astype(BF16)

    @pl.when(c == pl.num_programs(1) - 1)
    def _():
        for h in range(HEADS):
            st_ref[0, h] = s_scr[h].T


def _hg_sample_body(hq_ref, hf_ref, hi_ref, hg_ref, lbt_ref, gain_ref, sin_ref_state,
                    y_ref, st_ref, *, ts):
    rows = SAMPLE_BB * ts
    causal, qi, ki, ke, tot, _ = _hg_prepare(hq_ref, hf_ref, lbt_ref, rows, ts)
    etot = jnp.exp(tot)
    v = hi_ref[...]
    for h in range(HEADS):
        sl = slice(h * DH, (h + 1) * DH)
        att = jnp.where(causal, _bdot_nt(qi[:, sl], ki[:, sl]), 0.0)
        o_intra = _bdot(att, v[:, sl])
        outs = []
        for j in range(SAMPLE_BB):
            rs = slice(j * ts, (j + 1) * ts)
            s = sin_ref_state[j, h]
            outs.append(o_intra[rs] + _bdot(qi[rs, sl], s))
            scale = jnp.broadcast_to(etot[j * ts:j * ts + 1, sl], (DH, DH)).T
            new_state = s * scale + _bdot_tn(ke[rs, sl], v[rs, sl])
            for slot in range(st_ref.shape[0]):
                st_ref[slot, j, h] = new_state
        o = jnp.concatenate(outs, axis=0)
        y_ref[:, sl] = _rms_norm_gate(o, gain_ref[:, sl], hg_ref[:, sl]).astype(BF16)


def _hgrn(proj, state, layer, lbt, gain, b, t, nb, ts, prev_s):
    n_p = b * t
    nc = t // RET_CHUNK
    prow = lambda bi, c: bi * nc + c
    y_p, st_p = pl.pallas_call(
        _hg_prompt_body,
        grid=(b, nc),
        in_specs=[_proj_spec(RET_CHUNK, COL_HG_Q, prow), _proj_spec(RET_CHUNK, COL_HG_F, prow),
                  _proj_spec(RET_CHUNK, COL_HG_I, prow), _proj_spec(RET_CHUNK, COL_HG_G, prow),
                  pl.BlockSpec((None, SUBLANES, HW), lambda bi, c: (layer, 0, 0)),
                  pl.BlockSpec((None, 1, HW), lambda bi, c: (layer, 0, 0))],
        out_specs=[pl.BlockSpec((RET_CHUNK, HW), lambda bi, c: (prow(bi, c), 0)),
                   pl.BlockSpec((1, HEADS, DH, DH), lambda bi, c: (bi, 0, 0, 0))],
        out_shape=[jax.ShapeDtypeStruct((n_p, HW), BF16),
                   jax.ShapeDtypeStruct((b, HEADS, DH, DH), F32)],
        scratch_shapes=[pltpu.VMEM((HEADS, DH, DH), F32)],
        compiler_params=_params(("arbitrary", "arbitrary")),
        name="hgrn_prompt",
    )(proj, proj, proj, proj, lbt, gain)

    rows = SAMPLE_BB * ts
    base = n_p // rows
    srow = lambda i: base + i
    y_s, st_s = _sample_state_call(
        functools.partial(_hg_sample_body, ts=ts), (nb // SAMPLE_BB,),
        [_proj_spec(rows, COL_HG_Q, srow), _proj_spec(rows, COL_HG_F, srow),
         _proj_spec(rows, COL_HG_I, srow), _proj_spec(rows, COL_HG_G, srow),
         pl.BlockSpec((None, SUBLANES, HW), lambda i: (layer, 0, 0)),
         pl.BlockSpec((None, 1, HW), lambda i: (layer, 0, 0)),
         pl.BlockSpec((None, SAMPLE_BB, HEADS, DH, DH), lambda i: (layer, i, 0, 0, 0))],
        (proj, proj, proj, proj, lbt, gain, state),
        jax.ShapeDtypeStruct((nb * ts, HW), BF16), pl.BlockSpec((rows, HW), lambda i: (i, 0)),
        layer, nb, prev_s, "hgrn_sample")
    return (y_p, y_s), st_p, st_s


def _softmax_rows(s):
    e = jnp.exp(s - jnp.max(s, axis=-1, keepdims=True))
    return e / jnp.sum(e, axis=-1, keepdims=True)


def _xa_prompt_body(q_ref, k_ref, v_ref, y_ref):
    for h in range(HEADS):
        sl = slice(h * DH, (h + 1) * DH)
        a = _softmax_rows(_bdot_nt(q_ref[:, sl] * (DH ** -0.5), k_ref[:, sl]))
        y_ref[:, sl] = _bdot(a, v_ref[:, sl]).astype(BF16)


def _xa_sample_body(q_ref, k_ref, v_ref, y_ref, *, ts):
    n_mem = k_ref.shape[1] // HEADS
    pairs = [(j, h) for j in range(SAMPLE_BB) for h in range(HEADS)]
    q = q_ref[...] * (DH ** -0.5)
    scores = [_bdot_nt(q[j * ts:(j + 1) * ts, h * DH:(h + 1) * DH], k_ref[j, pl.ds(h, n_mem, stride=HEADS), :])
              for j, h in pairs]
    a = _softmax_rows(jnp.concatenate(scores, axis=0))
    for n, (j, h) in enumerate(pairs):
        y = _bdot(a[n * ts:(n + 1) * ts], v_ref[j, pl.ds(h, n_mem, stride=HEADS), :])
        y_ref[j * ts:(j + 1) * ts, h * DH:(h + 1) * DH] = y.astype(BF16)


def _cross_attention(proj, kv_p, cache_k, cache_v, layer, b, t, nb, ts):
    n_p = b * t
    n_mem = kv_p.shape[0] // b
    tq = _pick(t, (512, 256, 128))
    nq = t // tq
    y_p = pl.pallas_call(
        _xa_prompt_body,
        grid=(b, nq),
        in_specs=[_proj_spec(tq, COL_XA_Q, lambda bi, qi: bi * nq + qi),
                  pl.BlockSpec((n_mem, HW), lambda bi, qi: (bi, 0)),
                  pl.BlockSpec((n_mem, HW), lambda bi, qi: (bi, 1))],
        out_specs=pl.BlockSpec((tq, HW), lambda bi, qi: (bi * nq + qi, 0)),
        out_shape=jax.ShapeDtypeStruct((n_p, HW), BF16),
        compiler_params=_params(("arbitrary", "arbitrary")),
        name="xattn_prompt",
    )(proj, kv_p, kv_p)

    rows = SAMPLE_BB * ts
    base = n_p // rows
    cache_k = cache_k.reshape(DEPTH, nb, n_mem * HEADS, DH)
    cache_v = cache_v.reshape(DEPTH, nb, n_mem * HEADS, DH)
    kv_spec = pl.BlockSpec((None, SAMPLE_BB, n_mem * HEADS, DH), lambda i: (layer, i, 0, 0))
    y_s = pl.pallas_call(
        functools.partial(_xa_sample_body, ts=ts),
        grid=(nb // SAMPLE_BB,),
        in_specs=[_proj_spec(rows, COL_XA_Q, lambda i: base + i), kv_spec, kv_spec],
        out_specs=pl.BlockSpec((rows, HW), lambda i: (i, 0)),
        out_shape=jax.ShapeDtypeStruct((nb * ts, HW), BF16),
        compiler_params=_params(("arbitrary",)),
        name="xattn_sample",
    )(proj, cache_k, cache_v)
    return (y_p, y_s)


def _layer_norm(tv, g, b):
    mu = jnp.mean(tv, axis=-1, keepdims=True)
    var = jnp.mean(jnp.square(tv - mu), axis=-1, keepdims=True)
    return (tv - mu) * lax.rsqrt(var + LN_EPS) * g + b


def _merge_body(yrp_ref, yrs_ref, yhp_ref, yhs_ref, yxp_ref, yxs_ref, g0_ref, g1_ref, g2_ref, xp_ref, xs_ref,
                wr_ref, wh_ref, wx_ref, wo_ref, lg_ref, lb_ref,
                x1_ref, x1b_ref, x1t_ref, wr_s, wh_s, wx_s, wo_s, *, prompt_tiles):
    @pl.when(pl.program_id(0) == 0)
    def _():
        wr_s[...] = wr_ref[...].astype(BF16)
        wh_s[...] = wh_ref[...].astype(BF16)
        wx_s[...] = wx_ref[...].astype(BF16)
        wo_s[...] = wo_ref[...].astype(BF16)

    is_prompt = pl.program_id(0) < prompt_tiles

    def branch(yp_ref, ys_ref, w_s, gate_ref):
        y = jnp.where(is_prompt, yp_ref[...], ys_ref[...])
        return gate_ref[...].astype(F32) * jnp.dot(y, w_s[...], preferred_element_type=F32)

    m = (branch(yrp_ref, yrs_ref, wr_s, g0_ref) + branch(yhp_ref, yhs_ref, wh_s, g1_ref)
         + branch(yxp_ref, yxs_ref, wx_s, g2_ref))
    hmix = jnp.dot(m.astype(BF16), wo_s[...], preferred_element_type=F32)
    x = jnp.where(is_prompt, xp_ref[...], xs_ref[...])
    x1 = _layer_norm(DN_ALPHA * x + hmix, lg_ref[...], lb_ref[...])
    x1_ref[...] = x1
    x1b_ref[...] = x1.astype(BF16)
    tm = x1.shape[0]
    bits = lax.bitcast_convert_type(x1.astype(BF16).astype(F32), U32)
    half = D_MODEL // 2
    packed = (bits[:, :half] >> 16) | (bits[:, half:] & jnp.uint32(0xFFFF0000))
    for s in range(PACKED_TILES):
        x1t_ref[pl.ds(s, tm, stride=PACKED_TILES), :] = packed[:, s * LANES:(s + 1) * LANES]


def _merge(ys, gates, x, w_up_ret, w_up_hgrn, w_up_xattn, w_out, ln_g, ln_b, layer):
    n_p = ys[0].shape[0]
    nt = n_p + ys[1].shape[0]
    tm = _pick(nt, (256, 128))
    assert n_p % tm == 0 and ys[1].shape[0] % tm == 0
    p_tiles = n_p // tm
    row = lambda i: (i, 0)
    p_map = lambda i: (jnp.minimum(i, p_tiles - 1), 0)
    s_map = lambda i: (jnp.maximum(i - p_tiles, 0), 0)
    y_specs = [pl.BlockSpec((tm, HW), p_map), pl.BlockSpec((tm, HW), s_map)] * 3
    wspec = lambda k: pl.BlockSpec((None, k, D_MODEL), lambda i: (layer, 0, 0))
    vec = pl.BlockSpec((None, 1, D_MODEL), lambda i: (layer, 0, 0))
    gate_specs = [pl.BlockSpec((tm, D_MODEL), lambda i, c=c: (i, c)) for c in range(3)]
    return pl.pallas_call(
        functools.partial(_merge_body, prompt_tiles=p_tiles),
        grid=(nt // tm,),
        in_specs=y_specs + gate_specs + [pl.BlockSpec((tm, D_MODEL), p_map), pl.BlockSpec((tm, D_MODEL), s_map),
                  wspec(HW), wspec(HW), wspec(HW), wspec(D_MODEL), vec, vec],
        out_specs=[pl.BlockSpec((tm, D_MODEL), row), pl.BlockSpec((tm, D_MODEL), row),
                   pl.BlockSpec((tm * PACKED_TILES, LANES), row)],
        out_shape=[jax.ShapeDtypeStruct((nt, D_MODEL), F32),
                   jax.ShapeDtypeStruct((nt, D_MODEL), BF16),
                   jax.ShapeDtypeStruct((nt * PACKED_TILES, LANES), U32)],
        scratch_shapes=[pltpu.VMEM((HW, D_MODEL), BF16)] * 3 + [pltpu.VMEM((D_MODEL, D_MODEL), BF16)],
        compiler_params=_params(("arbitrary",)),
        name="merge_out_ln1",
    )(*ys, *([gates] * 3), *x, w_up_ret, w_up_hgrn, w_up_xattn, w_out, ln_g, ln_b)


def _router_body(x_ref, wt_ref, b_ref, eidx_ref, wn_ref):
    tm = x_ref.shape[0]
    x = x_ref[...]
    w = wt_ref[...]
    xh = x.astype(BF16)
    xl = (x - xh.astype(F32)).astype(BF16)
    wh = w.astype(BF16)
    wl = (w - wh.astype(F32)).astype(BF16)
    logits = _bdot_nt(wh, xh) + (_bdot_nt(wh, xl) + _bdot_nt(wl, xh))
    s = jax.nn.sigmoid(logits)
    sel = s + b_ref[...]
    neg = -jnp.inf
    groups = [sel[g * GROUP_SIZE:(g + 1) * GROUP_SIZE, :] for g in range(N_GROUPS)]
    ie = lax.broadcasted_iota(I32, (GROUP_SIZE, tm), 0).astype(F32)
    rows = []
    for blk in groups:
        m1 = jnp.max(blk, axis=0, keepdims=True)
        first = jnp.min(jnp.where(blk == m1, ie, float(GROUP_SIZE)), axis=0, keepdims=True)
        rows.append(m1 + jnp.max(jnp.where(ie == first, neg, blk), axis=0, keepdims=True))
    gscore = jnp.concatenate(rows, axis=0)
    ig = lax.broadcasted_iota(I32, gscore.shape, 0).astype(F32)
    gmask = jnp.zeros(gscore.shape, F32)
    for _ in range(TOPK_GROUPS):
        m = jnp.max(gscore, axis=0, keepdims=True)
        gi = jnp.min(jnp.where(gscore == m, ig, float(N_GROUPS)), axis=0, keepdims=True)
        hit = ig == gi
        gmask = jnp.where(hit, 1.0, gmask)
        gscore = jnp.where(hit, neg, gscore)
    masked = jnp.concatenate([jnp.where(gmask[g:g + 1, :] > 0.5, blk, neg)
                              for g, blk in enumerate(groups)], axis=0)
    ix = lax.broadcasted_iota(I32, masked.shape, 0).astype(F32)
    idxs, ws = [], []
    for _ in range(TOP_K):
        m = jnp.max(masked, axis=0, keepdims=True)
        ei = jnp.min(jnp.where(masked == m, ix, float(N_EXPERTS)), axis=0, keepdims=True)
        hit = ix == ei
        idxs.append(ei)
        ws.append(jnp.sum(jnp.where(hit, s, 0.0), axis=0, keepdims=True))
        masked = jnp.where(hit, neg, masked)
    wsum = ws[0]
    for w in ws[1:]:
        wsum = wsum + w
    pad = [jnp.zeros((1, tm), F32)] * (SUBLANES - TOP_K)
    eidx_ref[...] = jnp.concatenate(idxs + pad, axis=0).astype(I32)
    wn_ref[...] = jnp.concatenate([w / wsum * ROUTED_SCALE for w in ws] + pad, axis=0)


def _router(x1, w_router_t, b_router, layer):
    nt = x1.shape[0]
    tm = _pick(nt, (512, 256, 128))
    return pl.pallas_call(
        _router_body,
        grid=(nt // tm,),
        in_specs=[pl.BlockSpec((tm, D_MODEL), lambda i: (i, 0)),
                  pl.BlockSpec((None, N_EXPERTS, D_MODEL), lambda i: (layer, 0, 0)),
                  pl.BlockSpec((None, N_EXPERTS, 1), lambda i: (layer, 0, 0))],
        out_specs=[pl.BlockSpec((SUBLANES, tm), lambda i: (0, i))] * 2,
        out_shape=[jax.ShapeDtypeStruct((SUBLANES, nt), I32),
                   jax.ShapeDtypeStruct((SUBLANES, nt), F32)],
        compiler_params=_params(("arbitrary",)),
        name="moe_router",
    )(x1, w_router_t, b_router)


def _positions_body(eidx_ref, pos_ref, cnt_ref, off_ref, base_scr, off_scr):
    phase = pl.program_id(0)
    i = pl.program_id(1)
    tp = eidx_ref.shape[1]
    ix = lax.broadcasted_iota(I32, (N_EXPERTS, tp), 0)
    eidx = eidx_ref[...]
    member = jnp.zeros((N_EXPERTS, tp), F32)
    for k in range(TOP_K):
        member = member + (ix == eidx[k:k + 1, :]).astype(F32)
    tile_cnt = jnp.sum(member, axis=1, keepdims=True)

    @pl.when((phase == 0) & (i == 0))
    def _():
        base_scr[...] = jnp.zeros_like(base_scr)

    @pl.when((phase == 1) & (i == 0))
    def _():
        cnt = base_scr[...]
        er = lax.broadcasted_iota(I32, (N_EXPERTS, N_EXPERTS), 0)
        ec = lax.broadcasted_iota(I32, (N_EXPERTS, N_EXPERTS), 1)
        off = jnp.dot((ec < er).astype(F32), cnt, precision=HIGHEST, preferred_element_type=F32)
        off_scr[...] = off
        cnt_ref[...] = cnt
        off_ref[...] = off
        base_scr[...] = jnp.zeros_like(base_scr)

    @pl.when(phase == 1)
    def _():
        tr = lax.broadcasted_iota(I32, (tp, tp), 0)
        tc = lax.broadcasted_iota(I32, (tp, tp), 1)
        before = jnp.dot(member.astype(BF16), (tr < tc).astype(BF16), preferred_element_type=F32)
        where_to = before + (off_scr[...] + base_scr[...])[:, 0:1]
        rows = [jnp.sum(jnp.where(ix == eidx[k:k + 1, :], where_to, 0.0), axis=0, keepdims=True)
                for k in range(TOP_K)]
        rows += [jnp.zeros((1, tp), F32)] * (SUBLANES - TOP_K)
        pos_ref[...] = jnp.concatenate(rows, axis=0).astype(I32)

    base_scr[...] = base_scr[...] + tile_cnt


def _positions(eidx):
    nt = eidx.shape[1]
    tp = _pick(nt, (512, 256, 128))
    const = lambda p, i: (0, 0)
    return pl.pallas_call(
        _positions_body,
        grid=(2, nt // tp),
        in_specs=[pl.BlockSpec((SUBLANES, tp), lambda p, i: (0, i))],
        out_specs=[pl.BlockSpec((SUBLANES, tp), lambda p, i: (0, i * p)),
                   pl.BlockSpec((N_EXPERTS, LANES), const), pl.BlockSpec((N_EXPERTS, LANES), const)],
        out_shape=[jax.ShapeDtypeStruct((SUBLANES, nt), I32),
                   jax.ShapeDtypeStruct((N_EXPERTS, LANES), F32),
                   jax.ShapeDtypeStruct((N_EXPERTS, LANES), F32)],
        scratch_shapes=[pltpu.VMEM((N_EXPERTS, LANES), F32), pltpu.VMEM((N_EXPERTS, LANES), F32)],
        compiler_params=_params(("arbitrary", "arbitrary")),
        name="moe_positions",
    )(eidx)


T_TILE, T_EXPERT, T_LO, T_HI, T_FRESH, T_NEWEXP = range(6)

def _table_body(cnt_ref, off_ref, tbl_ref, *, tile_rows):
    te = float(tile_rows)
    n = tbl_ref.shape[1]
    cnt = cnt_ref[...]
    off = off_ref[...]
    first = jnp.floor(off * (1.0 / te))
    last = jnp.floor((off + cnt - 1.0) * (1.0 / te))
    nst = jnp.where(cnt > 0.0, last - first + 1.0, 0.0)
    er = lax.broadcasted_iota(I32, (N_EXPERTS, N_EXPERTS), 0)
    ec = lax.broadcasted_iota(I32, (N_EXPERTS, N_EXPERTS), 1)
    s_end = jnp.dot((ec <= er).astype(F32), nst, precision=HIGHEST, preferred_element_type=F32)
    s_beg = s_end - nst
    total = s_end[N_EXPERTS - 1:N_EXPERTS, 0:1]
    sidx = lax.broadcasted_iota(I32, (1, n), 1).astype(F32)
    s = jnp.minimum(sidx, total - 1.0)
    e_s = jnp.sum((s_end[:, 0:1] <= s).astype(F32), axis=0, keepdims=True)
    hot = lax.broadcasted_iota(I32, (N_EXPERTS, n), 0).astype(F32) == e_s

    def pick(col):
        return jnp.sum(jnp.where(hot, col[:, 0:1], 0.0), axis=0, keepdims=True)

    tile = pick(first) + s - pick(s_beg)
    valid = sidx < total
    o, c = pick(off), pick(cnt)
    lo = jnp.where(valid, jnp.maximum(o, tile * te), 0.0)
    hi = jnp.where(valid, jnp.minimum(o + c, (tile + 1.0) * te), 0.0)
    head = sidx == 0.0
    fresh = jnp.where((tile != pltpu.roll(tile, 1, 1)) | head, 1.0, 0.0)
    newexp = jnp.where((e_s != pltpu.roll(e_s, 1, 1)) | head, 1.0, 0.0)
    pad = [jnp.zeros((1, n), F32)] * (SUBLANES - 6)
    tbl_ref[...] = jnp.concatenate([tile, e_s, lo, hi, fresh, newexp] + pad, axis=0).astype(I32)


def _step_table(cnt, off, n_rows, te):
    n_steps = n_rows // te + N_EXPERTS
    width = -(-n_steps // LANES) * LANES
    tbl = pl.pallas_call(
        functools.partial(_table_body, tile_rows=te),
        out_shape=jax.ShapeDtypeStruct((SUBLANES, width), I32),
        name="moe_step_table",
    )(cnt, off)
    return tbl, n_steps


def _wait_tile_rows(like_src, dst_rows_ref, sem_ref):
    rows = like_src.shape[0]
    for _ in range(TOP_K):
        pltpu.make_async_copy(like_src, dst_rows_ref.at[pl.ds(0, rows)], sem_ref).wait()


def _dispatch_body(pos_ref, xt_ref, xb_ref, wsg_ref, wsu_ref, wsd_ref, xs_ref, sh_ref,
                   pos_s, wsg_s, wsu_s, wsd_s, sem_p, sem):
    td = pos_ref.shape[1]

    @pl.when(pl.program_id(0) == 0)
    def _():
        wsg_s[...] = wsg_ref[...].astype(BF16)
        wsu_s[...] = wsu_ref[...].astype(BF16)
        wsd_s[...] = wsd_ref[...].astype(BF16)

    cp = pltpu.make_async_copy(pos_ref, pos_s, sem_p)
    cp.start()
    cp.wait()

    def issue(g, carry):
        for u in range(ISSUE_UNROLL):
            r = g * ISSUE_UNROLL + u
            for k in range(TOP_K):
                pltpu.make_async_copy(xt_ref.at[r], xs_ref.at[pos_s[k, r]], sem).start(priority=k % 2)
        return carry

    half = td // 2
    groups = half // ISSUE_UNROLL
    for part in range(2):
        lax.fori_loop(part * groups, (part + 1) * groups, issue, 0)
        rows = slice(part * half, (part + 1) * half)
        xb = xb_ref[rows, :]
        hs = (_silu(jnp.dot(xb, wsg_s[...], preferred_element_type=F32))
              * jnp.dot(xb, wsu_s[...], preferred_element_type=F32))
        sh_ref[rows, :] = jnp.dot(hs.astype(BF16), wsd_s[...], preferred_element_type=F32)
    _wait_tile_rows(xt_ref, xs_ref, sem)


def _dispatch(pos, x1t, x1b, w_s_gate, w_s_up, w_s_down, layer):
    nt = x1t.shape[0]
    td = _pick(nt, (1024, 512, 256, 128))
    d_sh = w_s_gate.shape[2]
    w_in_spec = pl.BlockSpec((None, D_MODEL, d_sh), lambda i: (layer, 0, 0))
    return pl.pallas_call(
        _dispatch_body,
        grid=(nt // td,),
        in_specs=[pl.BlockSpec((SUBLANES, td), lambda i: (0, i)),
                  pl.BlockSpec((td,) + x1t.shape[1:], lambda i: (i, 0, 0)),
                  pl.BlockSpec((td, D_MODEL), lambda i: (i, 0)),
                  w_in_spec, w_in_spec, pl.BlockSpec((None, d_sh, D_MODEL), lambda i: (layer, 0, 0))],
        out_specs=[pl.BlockSpec(memory_space=pl.ANY), pl.BlockSpec((td, D_MODEL), lambda i: (i, 0))],
        out_shape=[jax.ShapeDtypeStruct((nt * TOP_K,) + x1t.shape[1:], x1t.dtype),
                   jax.ShapeDtypeStruct((nt, D_MODEL), F32)],
        scratch_shapes=[pltpu.SMEM((SUBLANES, td), I32),
                        pltpu.VMEM((D_MODEL, d_sh), BF16), pltpu.VMEM((D_MODEL, d_sh), BF16),
                        pltpu.VMEM((d_sh, D_MODEL), BF16),
                        pltpu.SemaphoreType.DMA, pltpu.SemaphoreType.DMA],
        compiler_params=_params(("arbitrary",)),
        name="moe_dispatch",
    )(pos, x1t, x1b, w_s_gate, w_s_up, w_s_down)


def _experts_body(tbl_ref, xs_ref, wg_ref, wu_ref, wd_ref, ye_ref, wg_s, wu_s, wd_s):
    s = pl.program_id(0)
    te = xs_ref.shape[0] // PACKED_TILES
    lo = tbl_ref[T_LO, s]
    hi = tbl_ref[T_HI, s]

    @pl.when(tbl_ref[T_NEWEXP, s] == 1)
    def _():
        wg_s[...] = wg_ref[...].astype(BF16)
        wu_s[...] = wu_ref[...].astype(BF16)
        wd_s[...] = wd_ref[...].astype(BF16)

    @pl.when(tbl_ref[T_FRESH, s] == 1)
    def _():
        ye_ref[...] = jnp.zeros_like(ye_ref)

    @pl.when(hi > lo)
    def _():
        words = [xs_ref[pl.ds(t, te, stride=PACKED_TILES), :] for t in range(PACKED_TILES)]
        low = [lax.bitcast_convert_type(w << 16, F32).astype(BF16) for w in words]
        high = [lax.bitcast_convert_type(w & jnp.uint32(0xFFFF0000), F32).astype(BF16) for w in words]
        x = jnp.concatenate(low + high, axis=-1)
        g = jnp.dot(x, wg_s[...], preferred_element_type=F32)
        u = jnp.dot(x, wu_s[...], preferred_element_type=F32)
        y = jnp.dot((_silu(g) * u).astype(BF16), wd_s[...], preferred_element_type=F32)
        row = tbl_ref[T_TILE, s] * te + lax.broadcasted_iota(I32, (te, LANES), 0)
        mine = (row >= lo) & (row < hi)
        for t in range(ROW_TILES):
            sl = pl.ds(t, te, stride=ROW_TILES)
            ye_ref[sl, :] = jnp.where(mine, y[:, t * LANES:(t + 1) * LANES], ye_ref[sl, :])


def _experts(tbl, n_steps, te, xs, w_gate, w_up, w_down, layer):
    n_rows = xs.shape[0] // PACKED_TILES
    tile_map = lambda s, tbl: (tbl[T_TILE, s], 0)
    w_map = lambda s, tbl: (layer, tbl[T_EXPERT, s], 0, 0)
    w_in_spec = pl.BlockSpec((None, None, D_MODEL, D_EXPERT), w_map)
    w_dn_spec = pl.BlockSpec((None, None, D_EXPERT, D_MODEL), w_map)
    return pl.pallas_call(
        _experts_body,
        grid_spec=pltpu.PrefetchScalarGridSpec(
            num_scalar_prefetch=1,
            grid=(n_steps,),
            in_specs=[pl.BlockSpec((te * PACKED_TILES, LANES), tile_map), w_in_spec, w_in_spec, w_dn_spec],
            out_specs=pl.BlockSpec((te * ROW_TILES, LANES), tile_map),
            scratch_shapes=[pltpu.VMEM((D_MODEL, D_EXPERT), BF16), pltpu.VMEM((D_MODEL, D_EXPERT), BF16),
                            pltpu.VMEM((D_EXPERT, D_MODEL), BF16)]),
        out_shape=jax.ShapeDtypeStruct((n_rows * ROW_TILES, LANES), F32),
        compiler_params=_params(("arbitrary",)),
        name="moe_experts",
    )(tbl, xs, w_gate, w_up, w_down)


def _combine_body(pos_ref, wn_ref, ye_ref, x1_ref, sh_ref, lg_ref, lb_ref,
                  x2p_ref, x2s_ref, x2b_ref, pos_s, buf, sem_p, sem, *, prompt_tiles):
    i = pl.program_id(0)
    n = pl.num_programs(0)
    tc = wn_ref.shape[1]
    slot = i % 2
    tile_rows = tc * ROW_TILES

    def request(tile, into):
        cp = pltpu.make_async_copy(pos_ref.at[tile], pos_s, sem_p)
        cp.start()
        cp.wait()

        def issue(g, carry):
            for u in range(ISSUE_UNROLL):
                r = g * ISSUE_UNROLL + u
                for k in range(TOP_K):
                    at = pl.multiple_of(((into * TOP_K + k) * tc + r) * ROW_TILES, ROW_TILES)
                    pltpu.make_async_copy(ye_ref.at[pos_s[k, r]], buf.at[pl.ds(at, ROW_TILES)],
                                          sem.at[into]).start(priority=k % 2)
            return carry

        lax.fori_loop(0, tc // ISSUE_UNROLL, issue, 0)

    @pl.when(i == 0)
    def _():
        request(0, 0)

    @pl.when(i + 1 < n)
    def _():
        request(i + 1, 1 - slot)

    for k in range(TOP_K):
        pltpu.make_async_copy(buf.at[pl.ds(0, tile_rows)], buf.at[pl.ds(tile_rows, tile_rows)], sem.at[slot]).wait()

    w = wn_ref[...]
    acc = [None] * ROW_TILES
    for k in range(TOP_K):
        wcol = jnp.concatenate([jnp.broadcast_to(w[k:k + 1, c * LANES:(c + 1) * LANES], (LANES, LANES)).T
                                for c in range(tc // LANES)], axis=0)
        base = (slot * TOP_K + k) * tile_rows
        for t in range(ROW_TILES):
            term = wcol * buf[pl.ds(base + t, tc, stride=ROW_TILES), :]
            acc[t] = term if acc[t] is None else acc[t] + term
    routed = jnp.concatenate(acc, axis=-1)
    x2 = _layer_norm(DN_ALPHA * x1_ref[...] + (routed + sh_ref[...]), lg_ref[...], lb_ref[...])
    x2b_ref[...] = x2.astype(BF16)

    @pl.when(i < prompt_tiles)
    def _():
        x2p_ref[...] = x2

    @pl.when(i >= prompt_tiles)
    def _():
        x2s_ref[...] = x2


def _combine(pos, wn, ye, x1, shared, ln_g, ln_b, layer, n_p):
    nt = x1.shape[0]
    tc = _pick(nt, (512, 256, 128))
    assert n_p % tc == 0
    n_tiles = nt // tc
    p_tiles = n_p // tc
    pos3 = pos.reshape(SUBLANES, n_tiles, tc).transpose(1, 0, 2)
    row = lambda i: (i, 0)
    vec = pl.BlockSpec((None, 1, D_MODEL), lambda i: (layer, 0, 0))
    return pl.pallas_call(
        functools.partial(_combine_body, prompt_tiles=p_tiles),
        grid=(n_tiles,),
        in_specs=[pl.BlockSpec((n_tiles, SUBLANES, tc), lambda i: (0, 0, 0)),
                  pl.BlockSpec((SUBLANES, tc), lambda i: (0, i)),
                  pl.BlockSpec(memory_space=pl.ANY),
                  pl.BlockSpec((tc, D_MODEL), row), pl.BlockSpec((tc, D_MODEL), row), vec, vec],
        out_specs=[pl.BlockSpec((tc, D_MODEL), lambda i: (jnp.minimum(i, p_tiles - 1), 0)),
                   pl.BlockSpec((tc, D_MODEL), lambda i: (jnp.maximum(i - p_tiles, 0), 0)),
                   pl.BlockSpec((tc, D_MODEL), row)],
        out_shape=[jax.ShapeDtypeStruct((n_p, D_MODEL), F32), jax.ShapeDtypeStruct((nt - n_p, D_MODEL), F32),
                   jax.ShapeDtypeStruct((nt, D_MODEL), BF16)],
        scratch_shapes=[pltpu.SMEM((SUBLANES, tc), I32),
                        pltpu.VMEM((2 * TOP_K * tc * ROW_TILES, LANES), F32),
                        pltpu.SemaphoreType.DMA, pltpu.SemaphoreType.DMA((2,))],
        compiler_params=_params(("arbitrary",)),
        name="moe_combine_ln2",
    )(pos3, wn, ye, x1, shared, ln_g, ln_b)


def _rope_tables(t, pos0):
    inv = 1.0 / (ROPE_BASE ** (jnp.arange(0, DH, 2, dtype=F32) / DH))
    ang = (jnp.arange(t, dtype=F32) + pos0)[:, None] * inv[None, :]
    cos, sin = jnp.cos(ang), jnp.sin(ang)
    return jnp.concatenate([cos, cos], axis=-1), jnp.concatenate([-sin, sin], axis=-1)


def kernel(x_prompt, x_sample, mem_prompt, state_ret, state_hgrn, cache_mem_k, cache_mem_v, w_in, w_up_ret, w_up_hgrn, w_up_xattn, w_out, w_mem_kv, ret_norm_g, hgrn_norm_g, lb_logits, ln1_g, ln1_b, ln2_g, ln2_b, w_router, b_router, w_e_gate, w_e_up, w_e_down, w_s_gate, w_s_up, w_s_down):
    b, t, d = x_prompt.shape
    nb, ts, _ = x_sample.shape
    n_mem = mem_prompt.shape[1]
    assert d == D_MODEL and t % RET_CHUNK == 0 and nb % SAMPLE_BB == 0
    assert ts & (ts - 1) == 0 and HG_CHUNK % ts == 0 and RET_CHUNK % ts == 0
    n_p, n_s = b * t, nb * ts
    nt = n_p + n_s
    assert n_p % (SAMPLE_BB * ts) == 0

    lb_cum = jnp.cumsum(jax.nn.softmax(lb_logits.astype(F32), axis=0), axis=0)
    lbs = lb_cum - lb_cum[0:1]
    lbt = jnp.stack([jnp.log(lbs), jnp.log1p(-lbs), 1.0 - lbs] + [jnp.zeros_like(lbs)] * (SUBLANES - 3), axis=1)
    gl = jnp.broadcast_to(jnp.log1p(-jnp.exp2(-5.0 - jnp.arange(HEADS, dtype=F32)))[:, None], (HEADS, DH))
    cos_p, sin_p = _rope_tables(t, 0)
    cos_s, sin_s = _rope_tables(ts, PAST_LEN)
    cos_s, sin_s = jnp.tile(cos_s, (SAMPLE_BB, 1)), jnp.tile(sin_s, (SAMPLE_BB, 1))
    vec3 = lambda a: a.reshape(DEPTH, 1, -1)
    w_router_t = jnp.swapaxes(w_router, 1, 2)
    b_router3 = b_router.reshape(DEPTH, N_EXPERTS, 1)
    mem2 = mem_prompt.reshape(b * n_mem, d)

    x = (x_prompt.reshape(n_p, d), x_sample.reshape(n_s, d))
    xb = jnp.concatenate([x[0].astype(BF16), x[1].astype(BF16)], axis=0)
    tm_proj = _pick(nt, (1024, 512, 128))
    te = _pick(nt * TOP_K, EXPERT_TILES)
    outs = {k: [] for k in ("ret_p", "hg_p", "mk", "mv")}
    ret_s = hg_s = None
    for l in range(DEPTH):
        mix_tiles = COL_GATES * HW // PROJ_TILE_N
        proj = _matmul(xb, w_in, l, tm_proj, PROJ_TILE_N, n=COL_GATES * HW)
        gates = _matmul(xb, w_in, l, tm_proj, PROJ_TILE_N, first_tile=mix_tiles, n=3 * D_MODEL, gate=True)
        kv_p = _matmul(mem2, w_mem_kv, l, _pick(b * n_mem, (1024, 512, 256)), 2 * HW)
        yr, ret_p, ret_s = _retention(proj, state_ret, l, cos_p, sin_p, cos_s, sin_s, gl,
                                      vec3(ret_norm_g), b, t, nb, ts, ret_s)
        yh, hg_p, hg_s = _hgrn(proj, state_hgrn, l, lbt, vec3(hgrn_norm_g), b, t, nb, ts, hg_s)
        yx = _cross_attention(proj, kv_p, cache_mem_k, cache_mem_v, l, b, t, nb, ts)
        x1, x1b, x1t = _merge((*yr, *yh, *yx), gates, x, w_up_ret, w_up_hgrn, w_up_xattn, w_out,
                              vec3(ln1_g), vec3(ln1_b), l)
        eidx, wn = _router(x1, w_router_t, b_router3, l)
        pos, cnt, off = _positions(eidx)
        tbl, n_steps = _step_table(cnt, off, nt * TOP_K, te)
        xs, shared = _dispatch(pos, x1t.reshape(nt, PACKED_TILES, LANES), x1b, w_s_gate, w_s_up, w_s_down, l)
        ye = _experts(tbl, n_steps, te, xs.reshape(-1, LANES), w_e_gate, w_e_up, w_e_down, l)
        ye = ye.reshape(-1, ROW_TILES, LANES)
        x_p, x_s, xb = _combine(pos, wn, ye, x1, shared, vec3(ln2_g), vec3(ln2_b), l, n_p)
        x = (x_p, x_s)
        outs["ret_p"].append(ret_p)
        outs["hg_p"].append(hg_p)
        outs["mk"].append(kv_p[:, :HW].reshape(b, n_mem, HEADS, DH))
        outs["mv"].append(kv_p[:, HW:].reshape(b, n_mem, HEADS, DH))
    return (x[0].reshape(b, t, d), x[1].reshape(nb, ts, d),
            jnp.stack(outs["ret_p"]), jnp.stack(outs["hg_p"]), jnp.stack(outs["mk"]), jnp.stack(outs["mv"]),
            ret_s, hg_s)
```

```python
import functools

import jax
import jax.numpy as jnp
from jax import lax
from jax.experimental import pallas as pl
from jax.experimental.pallas import tpu as pltpu

F32 = jnp.float32
BF16 = jnp.bfloat16
I32 = jnp.int32
HIGHEST = lax.Precision.HIGHEST

D_MODEL = 1024
DEPTH = 2
PAST_LEN = 16384
HEADS = 4
DH = 128
HW = HEADS * DH
RET_CHUNK = 128
HG_CHUNK = 16
ROPE_BASE = 10000.0
N_EXPERTS = 64
N_GROUPS = 8
GROUP_SIZE = N_EXPERTS // N_GROUPS
TOPK_GROUPS = 4
TOP_K = 6
D_EXPERT = 256
ROUTED_SCALE = 2.5
LN_EPS = 1e-5
DN_ALPHA = (2 * DEPTH) ** 0.25
N_IN = 9 * HW + 3 * D_MODEL
COL_RET_Q, COL_RET_K, COL_RET_V, COL_RET_G = 0, 1, 2, 3
COL_HG_Q, COL_HG_F, COL_HG_I, COL_HG_G = 4, 5, 6, 7
COL_XA_Q = 8
COL_GATES = 9
LANES = 128
SUBLANES = 8
ROW_TILES = D_MODEL // LANES
PACKED_TILES = ROW_TILES // 2
U32 = jnp.uint32
SAMPLE_BB = 8
EXPERT_TILES = (512, 256)
ISSUE_UNROLL = 8
PROJ_TILE_N = 1536
VMEM_LIMIT = 56 * 1024 * 1024


def _params(sem):
    return pltpu.CompilerParams(dimension_semantics=sem, vmem_limit_bytes=VMEM_LIMIT)


def _bdot(a, b):
    return jnp.dot(a.astype(BF16), b.astype(BF16), preferred_element_type=F32)


def _bdot_nt(a, b):
    return lax.dot_general(a.astype(BF16), b.astype(BF16), (((1,), (1,)), ((), ())),
                           preferred_element_type=F32)


def _bdot_tn(a, b):
    return lax.dot_general(a.astype(BF16), b.astype(BF16), (((0,), (0,)), ((), ())),
                           preferred_element_type=F32)


def _silu(x):
    return x * jax.nn.sigmoid(x)


def _pick(n, prefs):
    for p in prefs:
        if n % p == 0:
            return p
    raise ValueError(f"no tile for {n}")


def _mm_body(x_ref, w_ref, o_ref, wb_ref, *, gate):
    @pl.when(pl.program_id(1) == 0)
    def _():
        wb_ref[...] = w_ref[...].astype(BF16)

    acc = jnp.dot(x_ref[...].astype(BF16), wb_ref[...], preferred_element_type=F32)
    o_ref[...] = (jax.nn.sigmoid(acc) if gate else acc).astype(o_ref.dtype)


def _matmul(x, w, layer, tm, tn, first_tile=0, n=None, gate=False):
    m, k = x.shape
    n = w.shape[2] if n is None else n
    return pl.pallas_call(
        functools.partial(_mm_body, gate=gate),
        grid=(n // tn, m // tm),
        in_specs=[pl.BlockSpec((tm, k), lambda j, i: (i, 0)),
                  pl.BlockSpec((None, k, tn), lambda j, i: (layer, 0, first_tile + j))],
        out_specs=pl.BlockSpec((tm, tn), lambda j, i: (i, j)),
        out_shape=jax.ShapeDtypeStruct((m, n), BF16 if gate else F32),
        scratch_shapes=[pltpu.VMEM((k, tn), BF16)],
        compiler_params=_params(("arbitrary", "arbitrary")),
        name="dense_matmul",
    )(x, w)


def _rotary(x, cos, sin_signed):
    return x * cos + pltpu.roll(x, DH // 2, 1) * sin_signed


def _group_norm_gate(o, gain, gate):
    mu = jnp.mean(o, axis=-1, keepdims=True)
    var = jnp.mean(jnp.square(o - mu), axis=-1, keepdims=True)
    return (o - mu) * lax.rsqrt(var + LN_EPS) * gain * _silu(gate)


def _ret_prompt_body(q_ref, k_ref, v_ref, g_ref, cos_ref, sin_ref, gl_ref, gain_ref,
                     y_ref, st_ref, s_scr, intra_scr, qdec_scr, kdec_scr):
    c = pl.program_id(1)
    ch = RET_CHUNK

    @pl.when((pl.program_id(0) == 0) & (c == 0))
    def _():
        ri = lax.broadcasted_iota(I32, (ch, ch), 0)
        ci = lax.broadcasted_iota(I32, (ch, ch), 1)
        rel = (ri - ci).astype(F32)
        idx = lax.broadcasted_iota(I32, (ch, DH), 0).astype(F32)
        for h in range(HEADS):
            gl = gl_ref[h:h + 1, :]
            intra_scr[h] = jnp.where(rel >= 0, jnp.exp(gl * rel), 0.0)
            qdec_scr[h] = jnp.exp(gl * (idx + 1.0))
            kdec_scr[h] = jnp.exp(gl * (ch - 1.0 - idx))

    @pl.when(c == 0)
    def _():
        s_scr[...] = jnp.zeros_like(s_scr)

    cos = cos_ref[...]
    sin = sin_ref[...]
    for h in range(HEADS):
        sl = slice(h * DH, (h + 1) * DH)
        gl = gl_ref[h:h + 1, :]
        qr = _rotary(q_ref[:, sl], cos, sin)
        kr = _rotary(k_ref[:, sl], cos, sin) * (DH ** -0.5)
        v = v_ref[:, sl]
        att = _bdot_nt(qr, kr) * intra_scr[h]
        s = s_scr[h]
        o = _bdot(att, v) + _bdot(qr, s) * qdec_scr[h]
        s_scr[h] = s * jnp.exp(gl * float(ch)) + _bdot_tn(kr * kdec_scr[h], v)
        y_ref[:, sl] = _group_norm_gate(o, gain_ref[:, sl], g_ref[:, sl]).astype(BF16)

    @pl.when(c == pl.num_programs(1) - 1)
    def _():
        st_ref[0] = s_scr[...]


def _ret_sample_body(q_ref, k_ref, v_ref, g_ref, cos_ref, sin_ref, gl_ref, gain_ref, sin_ref_state,
                     y_ref, st_ref, *, ts):
    rows = SAMPLE_BB * ts
    shift = ts.bit_length() - 1
    cos = cos_ref[...]
    sin = sin_ref[...]
    ri = lax.broadcasted_iota(I32, (rows, rows), 0)
    ci = lax.broadcasted_iota(I32, (rows, rows), 1)
    rel = (ri - ci).astype(F32)
    mask = ((ri >> shift) == (ci >> shift)) & (ri >= ci)
    idx = (lax.broadcasted_iota(I32, (rows, DH), 0) & (ts - 1)).astype(F32)
    for h in range(HEADS):
        sl = slice(h * DH, (h + 1) * DH)
        gl = gl_ref[h:h + 1, :]
        qr = _rotary(q_ref[:, sl], cos, sin)
        kr = _rotary(k_ref[:, sl], cos, sin) * (DH ** -0.5)
        v = v_ref[:, sl]
        intra = jnp.where(mask, jnp.exp(gl[:, :rows] * rel), 0.0)
        o_intra = _bdot(_bdot_nt(qr, kr) * intra, v)
        q_dec = jnp.exp(gl * (idx + 1.0))
        kd = kr * jnp.exp(gl * (ts - 1.0 - idx))
        c_dec = jnp.exp(gl * float(ts))
        outs = []
        for j in range(SAMPLE_BB):
            rs = slice(j * ts, (j + 1) * ts)
            s = sin_ref_state[j, h]
            outs.append(o_intra[rs] + _bdot(qr[rs], s) * q_dec[rs])
            new_state = s * c_dec + _bdot_tn(kd[rs], v[rs])
            for slot in range(st_ref.shape[0]):
                st_ref[slot, j, h] = new_state
        o = jnp.concatenate(outs, axis=0)
        y_ref[:, sl] = _group_norm_gate(o, gain_ref[:, sl], g_ref[:, sl]).astype(BF16)


def _proj_spec(rows, col, row_map):
    return pl.BlockSpec((rows, HW), lambda *a: (row_map(*a), col))


def _sample_state_call(body, grid, in_specs, args, y_shape, y_spec, layer, nb, prev, name):
    st_shape = jax.ShapeDtypeStruct((DEPTH, nb, HEADS, DH, DH), F32)
    slots = DEPTH if prev is None else 1
    st_spec = pl.BlockSpec((slots, SAMPLE_BB, HEADS, DH, DH), lambda i: (layer, i, 0, 0, 0))
    aliases = {}
    if prev is not None:
        n_in = len(args)
        inner = body
        body = lambda *refs: inner(*refs[:n_in], *refs[n_in + 1:])
        in_specs = in_specs + [pl.BlockSpec(memory_space=pl.ANY)]
        args = args + (prev,)
        aliases = {n_in: 1}
    return pl.pallas_call(
        body, grid=grid, in_specs=in_specs, out_specs=[y_spec, st_spec], out_shape=[y_shape, st_shape],
        input_output_aliases=aliases, compiler_params=_params(("arbitrary",)), name=name,
    )(*args)


def _retention(proj, state, layer, cos_p, sin_p, cos_s, sin_s, gl, gain, b, t, nb, ts, prev_s):
    n_p = b * t
    nc = t // RET_CHUNK
    prow = lambda bi, c: bi * nc + c
    const2 = lambda *a: (0, 0)
    y_p, st_p = pl.pallas_call(
        _ret_prompt_body,
        grid=(b, nc),
        in_specs=[_proj_spec(RET_CHUNK, COL_RET_Q, prow), _proj_spec(RET_CHUNK, COL_RET_K, prow),
                  _proj_spec(RET_CHUNK, COL_RET_V, prow), _proj_spec(RET_CHUNK, COL_RET_G, prow),
                  pl.BlockSpec((RET_CHUNK, DH), lambda bi, c: (c, 0)),
                  pl.BlockSpec((RET_CHUNK, DH), lambda bi, c: (c, 0)),
                  pl.BlockSpec((HEADS, DH), const2),
                  pl.BlockSpec((None, 1, HW), lambda bi, c: (layer, 0, 0))],
        out_specs=[pl.BlockSpec((RET_CHUNK, HW), lambda bi, c: (prow(bi, c), 0)),
                   pl.BlockSpec((1, HEADS, DH, DH), lambda bi, c: (bi, 0, 0, 0))],
        out_shape=[jax.ShapeDtypeStruct((n_p, HW), BF16),
                   jax.ShapeDtypeStruct((b, HEADS, DH, DH), F32)],
        scratch_shapes=[pltpu.VMEM((HEADS, DH, DH), F32), pltpu.VMEM((HEADS, RET_CHUNK, RET_CHUNK), F32),
                        pltpu.VMEM((HEADS, RET_CHUNK, DH), F32), pltpu.VMEM((HEADS, RET_CHUNK, DH), F32)],
        compiler_params=_params(("arbitrary", "arbitrary")),
        name="retention_prompt",
    )(proj, proj, proj, proj, cos_p, sin_p, gl, gain)

    rows = SAMPLE_BB * ts
    base = n_p // rows
    srow = lambda i: base + i
    y_s, st_s = _sample_state_call(
        functools.partial(_ret_sample_body, ts=ts), (nb // SAMPLE_BB,),
        [_proj_spec(rows, COL_RET_Q, srow), _proj_spec(rows, COL_RET_K, srow),
         _proj_spec(rows, COL_RET_V, srow), _proj_spec(rows, COL_RET_G, srow),
         pl.BlockSpec((rows, DH), const2), pl.BlockSpec((rows, DH), const2),
         pl.BlockSpec((HEADS, DH), const2),
         pl.BlockSpec((None, 1, HW), lambda i: (layer, 0, 0)),
         pl.BlockSpec((None, SAMPLE_BB, HEADS, DH, DH), lambda i: (layer, i, 0, 0, 0))],
        (proj, proj, proj, proj, cos_s, sin_s, gl, gain, state),
        jax.ShapeDtypeStruct((nb * ts, HW), BF16), pl.BlockSpec((rows, HW), lambda i: (i, 0)),
        layer, nb, prev_s, "retention_sample")
    return (y_p, y_s), st_p, st_s


def _mask_sums(masks, x):
    hi = x.astype(BF16)
    rest = x - hi.astype(F32)
    mid = rest.astype(BF16)
    lo = (rest - mid.astype(F32)).astype(BF16)
    m = jnp.concatenate([mk.astype(BF16) for mk in masks], axis=0)
    dot = functools.partial(jnp.dot, preferred_element_type=F32)
    out = dot(m, hi) + (dot(m, mid) + dot(m, lo))
    rows = masks[0].shape[0]
    return [out[i * rows:(i + 1) * rows] for i in range(len(masks))]


def _hg_prepare(hq_ref, hf_ref, lbt_ref, rows, chunk, with_prefix=False):
    shift = chunk.bit_length() - 1
    ri = lax.broadcasted_iota(I32, (rows, rows), 0)
    ci = lax.broadcasted_iota(I32, (rows, rows), 1)
    same = (ri >> shift) == (ci >> shift)
    causal = same & (ci <= ri)
    z = hf_ref[...]
    log_lb = lbt_ref[0:1, :]
    log_1m_lb = lbt_ref[1:2, :]
    one_m_lb = lbt_ref[2:3, :]
    log_sig = jnp.minimum(z, 0.0) - jnp.log(1.0 + jnp.exp(-jnp.abs(z)))
    bterm = log_1m_lb + log_sig
    logf = jnp.maximum(log_lb, bterm) + jnp.log(1.0 + jnp.exp(-jnp.abs(log_lb - bterm)))
    kh = one_m_lb * jax.nn.sigmoid(-z)
    qh = _silu(hq_ref[...]) * (DH ** -0.5)
    masks = [causal, same] + ([(ci >> shift) < (ri >> shift)] if with_prefix else [])
    cum, tot, *pre = _mask_sums(masks, logf)
    qi = qh * jnp.exp(cum)
    ki = kh * jnp.exp(-cum)
    ke = kh * jnp.exp(tot - cum)
    return causal, qi, ki, ke, tot, (pre[0] if with_prefix else None)


def _rms_norm_gate(o, gain, gate):
    return o * lax.rsqrt(jnp.mean(jnp.square(o), axis=-1, keepdims=True) + LN_EPS) * gain * _silu(gate)


def _hg_prompt_body(hq_ref, hf_ref, hi_ref, hg_ref, lbt_ref, gain_ref, y_ref, st_ref, s_scr):
    c = pl.program_id(1)

    @pl.when(c == 0)
    def _():
        s_scr[...] = jnp.zeros_like(s_scr)

    rows = RET_CHUNK
    n_sub = rows // HG_CHUNK
    shift = HG_CHUNK.bit_length() - 1
    causal, qi, ki, ke, tot, pre = _hg_prepare(hq_ref, hf_ref, lbt_ref, rows, HG_CHUNK, with_prefix=True)
    sub = lax.broadcasted_iota(I32, (rows, DH), 0) >> shift
    v = hi_ref[...]
    for h in range(HEADS):
        sl = slice(h * DH, (h + 1) * DH)
        q_h, ke_h, v_h, pre_h = qi[:, sl], ke[:, sl], v[:, sl], pre[:, sl]
        att = jnp.where(causal, _bdot_nt(q_h, ki[:, sl]), 0.0)
        st0 = s_scr[h]
        o = _bdot(att, v_h) + _bdot_nt(q_h * jnp.exp(pre_h), st0)
        end_last = pre_h[rows - 1:rows] + tot[rows - 1:rows, sl]
        st = st0 * jnp.exp(end_last)
        for i in range(n_sub):
            rs = slice(i * HG_CHUNK, (i + 1) * HG_CHUNK)
            u_t = _bdot_tn(v_h[rs], ke_h[rs])
            if i + 1 < n_sub:
                end_i = pre_h[(i + 1) * HG_CHUNK:(i + 1) * HG_CHUNK + 1]
                later = q_h * jnp.exp(jnp.where(sub > i, pre_h - end_i, -jnp.inf))
                o = o + _bdot_nt(later, u_t)
                st = st + u_t * jnp.exp(end_last - end_i)
            else:
                st = st + u_t
        s_scr[h] = st
        y_ref[:, sl] = _rms_norm_gate(o, gain_ref[:, sl], hg_ref[:, sl]).astype(BF16)

    @pl.when(c == pl.num_programs(1) - 1)
    def _():
        for h in range(HEADS):
            st_ref[0, h] = s_scr[h].T


def _hg_sample_body(hq_ref, hf_ref, hi_ref, hg_ref, lbt_ref, gain_ref, sin_ref_state,
                    y_ref, st_ref, *, ts):
    rows = SAMPLE_BB * ts
    causal, qi, ki, ke, tot, _ = _hg_prepare(hq_ref, hf_ref, lbt_ref, rows, ts)
    etot = jnp.exp(tot)
    v = hi_ref[...]
    for h in range(HEADS):
        sl = slice(h * DH, (h + 1) * DH)
        att = jnp.where(causal, _bdot_nt(qi[:, sl], ki[:, sl]), 0.0)
        o_intra = _bdot(att, v[:, sl])
        outs = []
        for j in range(SAMPLE_BB):
            rs = slice(j * ts, (j + 1) * ts)
            s = sin_ref_state[j, h]
            outs.append(o_intra[rs] + _bdot(qi[rs, sl], s))
            scale = jnp.broadcast_to(etot[j * ts:j * ts + 1, sl], (DH, DH)).T
            new_state = s * scale + _bdot_tn(ke[rs, sl], v[rs, sl])
            for slot in range(st_ref.shape[0]):
                st_ref[slot, j, h] = new_state
        o = jnp.concatenate(outs, axis=0)
        y_ref[:, sl] = _rms_norm_gate(o, gain_ref[:, sl], hg_ref[:, sl]).astype(BF16)


def _hgrn(proj, state, layer, lbt, gain, b, t, nb, ts, prev_s):
    n_p = b * t
    nc = t // RET_CHUNK
    prow = lambda bi, c: bi * nc + c
    y_p, st_p = pl.pallas_call(
        _hg_prompt_body,
        grid=(b, nc),
        in_specs=[_proj_spec(RET_CHUNK, COL_HG_Q, prow), _proj_spec(RET_CHUNK, COL_HG_F, prow),
                  _proj_spec(RET_CHUNK, COL_HG_I, prow), _proj_spec(RET_CHUNK, COL_HG_G, prow),
                  pl.BlockSpec((None, SUBLANES, HW), lambda bi, c: (layer, 0, 0)),
                  pl.BlockSpec((None, 1, HW), lambda bi, c: (layer, 0, 0))],
        out_specs=[pl.BlockSpec((RET_CHUNK, HW), lambda bi, c: (prow(bi, c), 0)),
                   pl.BlockSpec((1, HEADS, DH, DH), lambda bi, c: (bi, 0, 0, 0))],
        out_shape=[jax.ShapeDtypeStruct((n_p, HW), BF16),
                   jax.ShapeDtypeStruct((b, HEADS, DH, DH), F32)],
        scratch_shapes=[pltpu.VMEM((HEADS, DH, DH), F32)],
        compiler_params=_params(("arbitrary", "arbitrary")),
        name="hgrn_prompt",
    )(proj, proj, proj, proj, lbt, gain)

    rows = SAMPLE_BB * ts
    base = n_p // rows
    srow = lambda i: base + i
    y_s, st_s = _sample_state_call(
        functools.partial(_hg_sample_body, ts=ts), (nb // SAMPLE_BB,),
        [_proj_spec(rows, COL_HG_Q, srow), _proj_spec(rows, COL_HG_F, srow),
         _proj_spec(rows, COL_HG_I, srow), _proj_spec(rows, COL_HG_G, srow),
         pl.BlockSpec((None, SUBLANES, HW), lambda i: (layer, 0, 0)),
         pl.BlockSpec((None, 1, HW), lambda i: (layer, 0, 0)),
         pl.BlockSpec((None, SAMPLE_BB, HEADS, DH, DH), lambda i: (layer, i, 0, 0, 0))],
        (proj, proj, proj, proj, lbt, gain, state),
        jax.ShapeDtypeStruct((nb * ts, HW), BF16), pl.BlockSpec((rows, HW), lambda i: (i, 0)),
        layer, nb, prev_s, "hgrn_sample")
    return (y_p, y_s), st_p, st_s


def _softmax_rows(s):
    e = jnp.exp(s - jnp.max(s, axis=-1, keepdims=True))
    return e / jnp.sum(e, axis=-1, keepdims=True)


def _xa_prompt_body(q_ref, k_ref, v_ref, y_ref):
    for h in range(HEADS):
        sl = slice(h * DH, (h + 1) * DH)
        a = _softmax_rows(_bdot_nt(q_ref[:, sl] * (DH ** -0.5), k_ref[:, sl]))
        y_ref[:, sl] = _bdot(a, v_ref[:, sl]).astype(BF16)


def _xa_sample_body(q_ref, k_ref, v_ref, y_ref, *, ts):
    n_mem = k_ref.shape[1] // HEADS
    pairs = [(j, h) for j in range(SAMPLE_BB) for h in range(HEADS)]
    q = q_ref[...] * (DH ** -0.5)
    scores = [_bdot_nt(q[j * ts:(j + 1) * ts, h * DH:(h + 1) * DH], k_ref[j, pl.ds(h, n_mem, stride=HEADS), :])
              for j, h in pairs]
    a = _softmax_rows(jnp.concatenate(scores, axis=0))
    for n, (j, h) in enumerate(pairs):
        y = _bdot(a[n * ts:(n + 1) * ts], v_ref[j, pl.ds(h, n_mem, stride=HEADS), :])
        y_ref[j * ts:(j + 1) * ts, h * DH:(h + 1) * DH] = y.astype(BF16)


def _cross_attention(proj, kv_p, cache_k, cache_v, layer, b, t, nb, ts):
    n_p = b * t
    n_mem = kv_p.shape[0] // b
    tq = _pick(t, (512, 256, 128))
    nq = t // tq
    y_p = pl.pallas_call(
        _xa_prompt_body,
        grid=(b, nq),
        in_specs=[_proj_spec(tq, COL_XA_Q, lambda bi, qi: bi * nq + qi),
                  pl.BlockSpec((n_mem, HW), lambda bi, qi: (bi, 0)),
                  pl.BlockSpec((n_mem, HW), lambda bi, qi: (bi, 1))],
        out_specs=pl.BlockSpec((tq, HW), lambda bi, qi: (bi * nq + qi, 0)),
        out_shape=jax.ShapeDtypeStruct((n_p, HW), BF16),
        compiler_params=_params(("arbitrary", "arbitrary")),
        name="xattn_prompt",
    )(proj, kv_p, kv_p)

    rows = SAMPLE_BB * ts
    base = n_p // rows
    cache_k = cache_k.reshape(DEPTH, nb, n_mem * HEADS, DH)
    cache_v = cache_v.reshape(DEPTH, nb, n_mem * HEADS, DH)
    kv_spec = pl.BlockSpec((None, SAMPLE_BB, n_mem * HEADS, DH), lambda i: (layer, i, 0, 0))
    y_s = pl.pallas_call(
        functools.partial(_xa_sample_body, ts=ts),
        grid=(nb // SAMPLE_BB,),
        in_specs=[_proj_spec(rows, COL_XA_Q, lambda i: base + i), kv_spec, kv_spec],
        out_specs=pl.BlockSpec((rows, HW), lambda i: (i, 0)),
        out_shape=jax.ShapeDtypeStruct((nb * ts, HW), BF16),
        compiler_params=_params(("arbitrary",)),
        name="xattn_sample",
    )(proj, cache_k, cache_v)
    return (y_p, y_s)


def _layer_norm(tv, g, b):
    mu = jnp.mean(tv, axis=-1, keepdims=True)
    var = jnp.mean(jnp.square(tv - mu), axis=-1, keepdims=True)
    return (tv - mu) * lax.rsqrt(var + LN_EPS) * g + b


def _merge_body(yrp_ref, yrs_ref, yhp_ref, yhs_ref, yxp_ref, yxs_ref, g0_ref, g1_ref, g2_ref, xp_ref, xs_ref,
                wr_ref, wh_ref, wx_ref, wo_ref, lg_ref, lb_ref,
                x1_ref, x1b_ref, x1t_ref, wr_s, wh_s, wx_s, wo_s, *, prompt_tiles):
    @pl.when(pl.program_id(0) == 0)
    def _():
        wr_s[...] = wr_ref[...].astype(BF16)
        wh_s[...] = wh_ref[...].astype(BF16)
        wx_s[...] = wx_ref[...].astype(BF16)
        wo_s[...] = wo_ref[...].astype(BF16)

    is_prompt = pl.program_id(0) < prompt_tiles

    def branch(yp_ref, ys_ref, w_s, gate_ref):
        y = jnp.where(is_prompt, yp_ref[...], ys_ref[...])
        return gate_ref[...].astype(F32) * jnp.dot(y, w_s[...], preferred_element_type=F32)

    m = (branch(yrp_ref, yrs_ref, wr_s, g0_ref) + branch(yhp_ref, yhs_ref, wh_s, g1_ref)
         + branch(yxp_ref, yxs_ref, wx_s, g2_ref))
    hmix = jnp.dot(m.astype(BF16), wo_s[...], preferred_element_type=F32)
    x = jnp.where(is_prompt, xp_ref[...], xs_ref[...])
    x1 = _layer_norm(DN_ALPHA * x + hmix, lg_ref[...], lb_ref[...])
    x1_ref[...] = x1
    x1b_ref[...] = x1.astype(BF16)
    tm = x1.shape[0]
    bits = lax.bitcast_convert_type(x1.astype(BF16).astype(F32), U32)
    half = D_MODEL // 2
    packed = (bits[:, :half] >> 16) | (bits[:, half:] & jnp.uint32(0xFFFF0000))
    for s in range(PACKED_TILES):
        x1t_ref[pl.ds(s, tm, stride=PACKED_TILES), :] = packed[:, s * LANES:(s + 1) * LANES]


def _merge(ys, gates, x, w_up_ret, w_up_hgrn, w_up_xattn, w_out, ln_g, ln_b, layer):
    n_p = ys[0].shape[0]
    nt = n_p + ys[1].shape[0]
    tm = _pick(nt, (512, 256, 128))
    assert n_p % tm == 0 and ys[1].shape[0] % tm == 0
    p_tiles = n_p // tm
    row = lambda i: (i, 0)
    p_map = lambda i: (jnp.minimum(i, p_tiles - 1), 0)
    s_map = lambda i: (jnp.maximum(i - p_tiles, 0), 0)
    y_specs = [pl.BlockSpec((tm, HW), p_map), pl.BlockSpec((tm, HW), s_map)] * 3
    wspec = lambda k: pl.BlockSpec((None, k, D_MODEL), lambda i: (layer, 0, 0), pipeline_mode=pl.Buffered(1))
    vec = pl.BlockSpec((None, 1, D_MODEL), lambda i: (layer, 0, 0))
    gate_specs = [pl.BlockSpec((tm, D_MODEL), lambda i, c=c: (i, c)) for c in range(3)]
    return pl.pallas_call(
        functools.partial(_merge_body, prompt_tiles=p_tiles),
        grid=(nt // tm,),
        in_specs=y_specs + gate_specs + [pl.BlockSpec((tm, D_MODEL), p_map), pl.BlockSpec((tm, D_MODEL), s_map),
                  wspec(HW), wspec(HW), wspec(HW), wspec(D_MODEL), vec, vec],
        out_specs=[pl.BlockSpec((tm, D_MODEL), row), pl.BlockSpec((tm, D_MODEL), row),
                   pl.BlockSpec((tm * PACKED_TILES, LANES), row)],
        out_shape=[jax.ShapeDtypeStruct((nt, D_MODEL), F32),
                   jax.ShapeDtypeStruct((nt, D_MODEL), BF16),
                   jax.ShapeDtypeStruct((nt * PACKED_TILES, LANES), U32)],
        scratch_shapes=[pltpu.VMEM((HW, D_MODEL), BF16)] * 3 + [pltpu.VMEM((D_MODEL, D_MODEL), BF16)],
        compiler_params=_params(("arbitrary",)),
        name="merge_out_ln1",
    )(*ys, *([gates] * 3), *x, w_up_ret, w_up_hgrn, w_up_xattn, w_out, ln_g, ln_b)


def _router_body(x_ref, wt_ref, b_ref, eidx_ref, wn_ref):
    tm = x_ref.shape[0]
    x = x_ref[...]
    w = wt_ref[...]
    xh = x.astype(BF16)
    xl = (x - xh.astype(F32)).astype(BF16)
    wh = w.astype(BF16)
    wl = (w - wh.astype(F32)).astype(BF16)
    logits = _bdot_nt(wh, xh) + (_bdot_nt(wh, xl) + _bdot_nt(wl, xh))
    s = jax.nn.sigmoid(logits)
    sel = s + b_ref[...]
    neg = -jnp.inf
    groups = [sel[g * GROUP_SIZE:(g + 1) * GROUP_SIZE, :] for g in range(N_GROUPS)]
    ie = lax.broadcasted_iota(I32, (GROUP_SIZE, tm), 0).astype(F32)
    rows = []
    for blk in groups:
        m1 = jnp.max(blk, axis=0, keepdims=True)
        first = jnp.min(jnp.where(blk == m1, ie, float(GROUP_SIZE)), axis=0, keepdims=True)
        rows.append(m1 + jnp.max(jnp.where(ie == first, neg, blk), axis=0, keepdims=True))
    gscore = jnp.concatenate(rows, axis=0)
    ig = lax.broadcasted_iota(I32, gscore.shape, 0).astype(F32)
    gmask = jnp.zeros(gscore.shape, F32)
    for _ in range(TOPK_GROUPS):
        m = jnp.max(gscore, axis=0, keepdims=True)
        gi = jnp.min(jnp.where(gscore == m, ig, float(N_GROUPS)), axis=0, keepdims=True)
        hit = ig == gi
        gmask = jnp.where(hit, 1.0, gmask)
        gscore = jnp.where(hit, neg, gscore)
    masked = jnp.concatenate([jnp.where(gmask[g:g + 1, :] > 0.5, blk, neg)
                              for g, blk in enumerate(groups)], axis=0)
    ix = lax.broadcasted_iota(I32, masked.shape, 0).astype(F32)
    idxs, ws = [], []
    for _ in range(TOP_K):
        m = jnp.max(masked, axis=0, keepdims=True)
        ei = jnp.min(jnp.where(masked == m, ix, float(N_EXPERTS)), axis=0, keepdims=True)
        hit = ix == ei
        idxs.append(ei)
        ws.append(jnp.sum(jnp.where(hit, s, 0.0), axis=0, keepdims=True))
        masked = jnp.where(hit, neg, masked)
    wsum = ws[0]
    for w in ws[1:]:
        wsum = wsum + w
    pad = [jnp.zeros((1, tm), F32)] * (SUBLANES - TOP_K)
    eidx_ref[...] = jnp.concatenate(idxs + pad, axis=0).astype(I32)
    wn_ref[...] = jnp.concatenate([w / wsum * ROUTED_SCALE for w in ws] + pad, axis=0)


def _router(x1, w_router_t, b_router, layer):
    nt = x1.shape[0]
    tm = _pick(nt, (512, 256, 128))
    return pl.pallas_call(
        _router_body,
        grid=(nt // tm,),
        in_specs=[pl.BlockSpec((tm, D_MODEL), lambda i: (i, 0)),
                  pl.BlockSpec((None, N_EXPERTS, D_MODEL), lambda i: (layer, 0, 0)),
                  pl.BlockSpec((None, N_EXPERTS, 1), lambda i: (layer, 0, 0))],
        out_specs=[pl.BlockSpec((SUBLANES, tm), lambda i: (0, i))] * 2,
        out_shape=[jax.ShapeDtypeStruct((SUBLANES, nt), I32),
                   jax.ShapeDtypeStruct((SUBLANES, nt), F32)],
        compiler_params=_params(("arbitrary",)),
        name="moe_router",
    )(x1, w_router_t, b_router)


def _positions_body(eidx_ref, pos_ref, cnt_ref, off_ref, base_scr, off_scr):
    phase = pl.program_id(0)
    i = pl.program_id(1)
    tp = eidx_ref.shape[1]
    ix = lax.broadcasted_iota(I32, (N_EXPERTS, tp), 0)
    eidx = eidx_ref[...]
    member = jnp.zeros((N_EXPERTS, tp), F32)
    for k in range(TOP_K):
        member = member + (ix == eidx[k:k + 1, :]).astype(F32)
    tile_cnt = jnp.sum(member, axis=1, keepdims=True)

    @pl.when((phase == 0) & (i == 0))
    def _():
        base_scr[...] = jnp.zeros_like(base_scr)

    @pl.when((phase == 1) & (i == 0))
    def _():
        cnt = base_scr[...]
        er = lax.broadcasted_iota(I32, (N_EXPERTS, N_EXPERTS), 0)
        ec = lax.broadcasted_iota(I32, (N_EXPERTS, N_EXPERTS), 1)
        off = jnp.dot((ec < er).astype(F32), cnt, precision=HIGHEST, preferred_element_type=F32)
        off_scr[...] = off
        cnt_ref[...] = cnt
        off_ref[...] = off
        base_scr[...] = jnp.zeros_like(base_scr)

    @pl.when(phase == 1)
    def _():
        tr = lax.broadcasted_iota(I32, (tp, tp), 0)
        tc = lax.broadcasted_iota(I32, (tp, tp), 1)
        before = jnp.dot(member.astype(BF16), (tr < tc).astype(BF16), preferred_element_type=F32)
        where_to = before + (off_scr[...] + base_scr[...])[:, 0:1]
        rows = [jnp.sum(jnp.where(ix == eidx[k:k + 1, :], where_to, 0.0), axis=0, keepdims=True)
                for k in range(TOP_K)]
        rows += [jnp.zeros((1, tp), F32)] * (SUBLANES - TOP_K)
        pos_ref[...] = jnp.concatenate(rows, axis=0).astype(I32)

    base_scr[...] = base_scr[...] + tile_cnt


def _positions(eidx):
    nt = eidx.shape[1]
    tp = _pick(nt, (512, 256, 128))
    const = lambda p, i: (0, 0)
    return pl.pallas_call(
        _positions_body,
        grid=(2, nt // tp),
        in_specs=[pl.BlockSpec((SUBLANES, tp), lambda p, i: (0, i))],
        out_specs=[pl.BlockSpec((SUBLANES, tp), lambda p, i: (0, i * p)),
                   pl.BlockSpec((N_EXPERTS, LANES), const), pl.BlockSpec((N_EXPERTS, LANES), const)],
        out_shape=[jax.ShapeDtypeStruct((SUBLANES, nt), I32),
                   jax.ShapeDtypeStruct((N_EXPERTS, LANES), F32),
                   jax.ShapeDtypeStruct((N_EXPERTS, LANES), F32)],
        scratch_shapes=[pltpu.VMEM((N_EXPERTS, LANES), F32), pltpu.VMEM((N_EXPERTS, LANES), F32)],
        compiler_params=_params(("arbitrary", "arbitrary")),
        name="moe_positions",
    )(eidx)


T_TILE, T_EXPERT, T_LO, T_HI, T_FRESH, T_NEWEXP = range(6)

def _table_body(cnt_ref, off_ref, tbl_ref, *, tile_rows):
    te = float(tile_rows)
    n = tbl_ref.shape[1]
    cnt = cnt_ref[...]
    off = off_ref[...]
    first = jnp.floor(off * (1.0 / te))
    last = jnp.floor((off + cnt - 1.0) * (1.0 / te))
    nst = jnp.where(cnt > 0.0, last - first + 1.0, 0.0)
    er = lax.broadcasted_iota(I32, (N_EXPERTS, N_EXPERTS), 0)
    ec = lax.broadcasted_iota(I32, (N_EXPERTS, N_EXPERTS), 1)
    s_end = jnp.dot((ec <= er).astype(F32), nst, precision=HIGHEST, preferred_element_type=F32)
    s_beg = s_end - nst
    total = s_end[N_EXPERTS - 1:N_EXPERTS, 0:1]
    sidx = lax.broadcasted_iota(I32, (1, n), 1).astype(F32)
    s = jnp.minimum(sidx, total - 1.0)
    e_s = jnp.sum((s_end[:, 0:1] <= s).astype(F32), axis=0, keepdims=True)
    hot = lax.broadcasted_iota(I32, (N_EXPERTS, n), 0).astype(F32) == e_s

    def pick(col):
        return jnp.sum(jnp.where(hot, col[:, 0:1], 0.0), axis=0, keepdims=True)

    tile = pick(first) + s - pick(s_beg)
    valid = sidx < total
    o, c = pick(off), pick(cnt)
    lo = jnp.where(valid, jnp.maximum(o, tile * te), 0.0)
    hi = jnp.where(valid, jnp.minimum(o + c, (tile + 1.0) * te), 0.0)
    head = sidx == 0.0
    fresh = jnp.where((tile != pltpu.roll(tile, 1, 1)) | head, 1.0, 0.0)
    newexp = jnp.where((e_s != pltpu.roll(e_s, 1, 1)) | head, 1.0, 0.0)
    pad = [jnp.zeros((1, n), F32)] * (SUBLANES - 6)
    tbl_ref[...] = jnp.concatenate([tile, e_s, lo, hi, fresh, newexp] + pad, axis=0).astype(I32)


def _step_table(cnt, off, n_rows, te):
    n_steps = n_rows // te + N_EXPERTS
    width = -(-n_steps // LANES) * LANES
    tbl = pl.pallas_call(
        functools.partial(_table_body, tile_rows=te),
        out_shape=jax.ShapeDtypeStruct((SUBLANES, width), I32),
        name="moe_step_table",
    )(cnt, off)
    return tbl, n_steps


def _wait_tile_rows(like_src, dst_rows_ref, sem_ref):
    rows = like_src.shape[0]
    for _ in range(TOP_K):
        pltpu.make_async_copy(like_src, dst_rows_ref.at[pl.ds(0, rows)], sem_ref).wait()


def _dispatch_body(pos_ref, xt_ref, xb_ref, wsg_ref, wsu_ref, wsd_ref, xs_ref, sh_ref,
                   pos_s, wsg_s, wsu_s, wsd_s, sem_p, sem):
    td = pos_ref.shape[1]

    @pl.when(pl.program_id(0) == 0)
    def _():
        wsg_s[...] = wsg_ref[...].astype(BF16)
        wsu_s[...] = wsu_ref[...].astype(BF16)
        wsd_s[...] = wsd_ref[...].astype(BF16)

    cp = pltpu.make_async_copy(pos_ref, pos_s, sem_p)
    cp.start()
    cp.wait()

    def issue(g, carry):
        for u in range(ISSUE_UNROLL):
            r = g * ISSUE_UNROLL + u
            for k in range(TOP_K):
                pltpu.make_async_copy(xt_ref.at[r], xs_ref.at[pos_s[k, r]], sem).start(priority=k % 2)
        return carry

    half = td // 2
    groups = half // ISSUE_UNROLL
    for part in range(2):
        lax.fori_loop(part * groups, (part + 1) * groups, issue, 0)
        rows = slice(part * half, (part + 1) * half)
        xb = xb_ref[rows, :]
        hs = (_silu(jnp.dot(xb, wsg_s[...], preferred_element_type=F32))
              * jnp.dot(xb, wsu_s[...], preferred_element_type=F32))
        sh_ref[rows, :] = jnp.dot(hs.astype(BF16), wsd_s[...], preferred_element_type=F32)
    _wait_tile_rows(xt_ref, xs_ref, sem)


def _dispatch(pos, x1t, x1b, w_s_gate, w_s_up, w_s_down, layer):
    nt = x1t.shape[0]
    td = _pick(nt, (1024, 512, 256, 128))
    d_sh = w_s_gate.shape[2]
    w_in_spec = pl.BlockSpec((None, D_MODEL, d_sh), lambda i: (layer, 0, 0))
    return pl.pallas_call(
        _dispatch_body,
        grid=(nt // td,),
        in_specs=[pl.BlockSpec((SUBLANES, td), lambda i: (0, i)),
                  pl.BlockSpec((td,) + x1t.shape[1:], lambda i: (i, 0, 0)),
                  pl.BlockSpec((td, D_MODEL), lambda i: (i, 0)),
                  w_in_spec, w_in_spec, pl.BlockSpec((None, d_sh, D_MODEL), lambda i: (layer, 0, 0))],
        out_specs=[pl.BlockSpec(memory_space=pl.ANY), pl.BlockSpec((td, D_MODEL), lambda i: (i, 0))],
        out_shape=[jax.ShapeDtypeStruct((nt * TOP_K,) + x1t.shape[1:], x1t.dtype),
                   jax.ShapeDtypeStruct((nt, D_MODEL), F32)],
        scratch_shapes=[pltpu.SMEM((SUBLANES, td), I32),
                        pltpu.VMEM((D_MODEL, d_sh), BF16), pltpu.VMEM((D_MODEL, d_sh), BF16),
                        pltpu.VMEM((d_sh, D_MODEL), BF16),
                        pltpu.SemaphoreType.DMA, pltpu.SemaphoreType.DMA],
        compiler_params=_params(("arbitrary",)),
        name="moe_dispatch",
    )(pos, x1t, x1b, w_s_gate, w_s_up, w_s_down)


def _experts_body(tbl_ref, xs_ref, wg_ref, wu_ref, wd_ref, ye_ref, wg_s, wu_s, wd_s):
    s = pl.program_id(0)
    te = xs_ref.shape[0] // PACKED_TILES
    lo = tbl_ref[T_LO, s]
    hi = tbl_ref[T_HI, s]

    @pl.when(tbl_ref[T_NEWEXP, s] == 1)
    def _():
        wg_s[...] = wg_ref[...].astype(BF16)
        wu_s[...] = wu_ref[...].astype(BF16)
        wd_s[...] = wd_ref[...].astype(BF16)

    @pl.when(tbl_ref[T_FRESH, s] == 1)
    def _():
        ye_ref[...] = jnp.zeros_like(ye_ref)

    @pl.when(hi > lo)
    def _():
        words = [xs_ref[pl.ds(t, te, stride=PACKED_TILES), :] for t in range(PACKED_TILES)]
        low = [lax.bitcast_convert_type(w << 16, F32).astype(BF16) for w in words]
        high = [lax.bitcast_convert_type(w & jnp.uint32(0xFFFF0000), F32).astype(BF16) for w in words]
        x = jnp.concatenate(low + high, axis=-1)
        g = jnp.dot(x, wg_s[...], preferred_element_type=F32)
        u = jnp.dot(x, wu_s[...], preferred_element_type=F32)
        y = jnp.dot((_silu(g) * u).astype(BF16), wd_s[...], preferred_element_type=F32)
        row = tbl_ref[T_TILE, s] * te + lax.broadcasted_iota(I32, (te, LANES), 0)
        mine = (row >= lo) & (row < hi)
        for t in range(ROW_TILES):
            sl = pl.ds(t, te, stride=ROW_TILES)
            ye_ref[sl, :] = jnp.where(mine, y[:, t * LANES:(t + 1) * LANES], ye_ref[sl, :])


def _experts(tbl, n_steps, te, xs, w_gate, w_up, w_down, layer):
    n_rows = xs.shape[0] // PACKED_TILES
    tile_map = lambda s, tbl: (tbl[T_TILE, s], 0)
    w_map = lambda s, tbl: (layer, tbl[T_EXPERT, s], 0, 0)
    w_in_spec = pl.BlockSpec((None, None, D_MODEL, D_EXPERT), w_map)
    w_dn_spec = pl.BlockSpec((None, None, D_EXPERT, D_MODEL), w_map)
    return pl.pallas_call(
        _experts_body,
        grid_spec=pltpu.PrefetchScalarGridSpec(
            num_scalar_prefetch=1,
            grid=(n_steps,),
            in_specs=[pl.BlockSpec((te * PACKED_TILES, LANES), tile_map), w_in_spec, w_in_spec, w_dn_spec],
            out_specs=pl.BlockSpec((te * ROW_TILES, LANES), tile_map),
            scratch_shapes=[pltpu.VMEM((D_MODEL, D_EXPERT), BF16), pltpu.VMEM((D_MODEL, D_EXPERT), BF16),
                            pltpu.VMEM((D_EXPERT, D_MODEL), BF16)]),
        out_shape=jax.ShapeDtypeStruct((n_rows * ROW_TILES, LANES), F32),
        compiler_params=_params(("arbitrary",)),
        name="moe_experts",
    )(tbl, xs, w_gate, w_up, w_down)


def _combine_body(pos_ref, wn_ref, ye_ref, x1_ref, sh_ref, lg_ref, lb_ref,
                  x2p_ref, x2s_ref, x2b_ref, pos_s, buf, sem_p, sem, *, prompt_tiles):
    i = pl.program_id(0)
    n = pl.num_programs(0)
    tc = wn_ref.shape[1]
    slot = i % 2
    tile_rows = tc * ROW_TILES

    def request(tile, into):
        cp = pltpu.make_async_copy(pos_ref.at[tile], pos_s, sem_p)
        cp.start()
        cp.wait()

        def issue(g, carry):
            for u in range(ISSUE_UNROLL):
                r = g * ISSUE_UNROLL + u
                for k in range(TOP_K):
                    at = pl.multiple_of(((into * TOP_K + k) * tc + r) * ROW_TILES, ROW_TILES)
                    pltpu.make_async_copy(ye_ref.at[pos_s[k, r]], buf.at[pl.ds(at, ROW_TILES)],
                                          sem.at[into]).start(priority=k % 2)
            return carry

        lax.fori_loop(0, tc // ISSUE_UNROLL, issue, 0)

    @pl.when(i == 0)
    def _():
        request(0, 0)

    @pl.when(i + 1 < n)
    def _():
        request(i + 1, 1 - slot)

    for k in range(TOP_K):
        pltpu.make_async_copy(buf.at[pl.ds(0, tile_rows)], buf.at[pl.ds(tile_rows, tile_rows)], sem.at[slot]).wait()

    w = wn_ref[...]
    acc = [None] * ROW_TILES
    for k in range(TOP_K):
        wcol = jnp.concatenate([jnp.broadcast_to(w[k:k + 1, c * LANES:(c + 1) * LANES], (LANES, LANES)).T
                                for c in range(tc // LANES)], axis=0)
        base = (slot * TOP_K + k) * tile_rows
        for t in range(ROW_TILES):
            term = wcol * buf[pl.ds(base + t, tc, stride=ROW_TILES), :]
            acc[t] = term if acc[t] is None else acc[t] + term
    routed = jnp.concatenate(acc, axis=-1)
    x2 = _layer_norm(DN_ALPHA * x1_ref[...] + (routed + sh_ref[...]), lg_ref[...], lb_ref[...])
    x2b_ref[...] = x2.astype(BF16)

    @pl.when(i < prompt_tiles)
    def _():
        x2p_ref[...] = x2

    @pl.when(i >= prompt_tiles)
    def _():
        x2s_ref[...] = x2


def _combine(pos, wn, ye, x1, shared, ln_g, ln_b, layer, n_p):
    nt = x1.shape[0]
    tc = _pick(nt, (512, 256, 128))
    assert n_p % tc == 0
    n_tiles = nt // tc
    p_tiles = n_p // tc
    pos3 = pos.reshape(SUBLANES, n_tiles, tc).transpose(1, 0, 2)
    row = lambda i: (i, 0)
    vec = pl.BlockSpec((None, 1, D_MODEL), lambda i: (layer, 0, 0))
    return pl.pallas_call(
        functools.partial(_combine_body, prompt_tiles=p_tiles),
        grid=(n_tiles,),
        in_specs=[pl.BlockSpec((n_tiles, SUBLANES, tc), lambda i: (0, 0, 0)),
                  pl.BlockSpec((SUBLANES, tc), lambda i: (0, i)),
                  pl.BlockSpec(memory_space=pl.ANY),
                  pl.BlockSpec((tc, D_MODEL), row), pl.BlockSpec((tc, D_MODEL), row), vec, vec],
        out_specs=[pl.BlockSpec((tc, D_MODEL), lambda i: (jnp.minimum(i, p_tiles - 1), 0)),
                   pl.BlockSpec((tc, D_MODEL), lambda i: (jnp.maximum(i - p_tiles, 0), 0)),
                   pl.BlockSpec((tc, D_MODEL), row)],
        out_shape=[jax.ShapeDtypeStruct((n_p, D_MODEL), F32), jax.ShapeDtypeStruct((nt - n_p, D_MODEL), F32),
                   jax.ShapeDtypeStruct((nt, D_MODEL), BF16)],
        scratch_shapes=[pltpu.SMEM((SUBLANES, tc), I32),
                        pltpu.VMEM((2 * TOP_K * tc * ROW_TILES, LANES), F32),
                        pltpu.SemaphoreType.DMA, pltpu.SemaphoreType.DMA((2,))],
        compiler_params=_params(("arbitrary",)),
        name="moe_combine_ln2",
    )(pos3, wn, ye, x1, shared, ln_g, ln_b)


def _rope_tables(t, pos0):
    inv = 1.0 / (ROPE_BASE ** (jnp.arange(0, DH, 2, dtype=F32) / DH))
    ang = (jnp.arange(t, dtype=F32) + pos0)[:, None] * inv[None, :]
    cos, sin = jnp.cos(ang), jnp.sin(ang)
    return jnp.concatenate([cos, cos], axis=-1), jnp.concatenate([-sin, sin], axis=-1)


def kernel(x_prompt, x_sample, mem_prompt, state_ret, state_hgrn, cache_mem_k, cache_mem_v, w_in, w_up_ret, w_up_hgrn, w_up_xattn, w_out, w_mem_kv, ret_norm_g, hgrn_norm_g, lb_logits, ln1_g, ln1_b, ln2_g, ln2_b, w_router, b_router, w_e_gate, w_e_up, w_e_down, w_s_gate, w_s_up, w_s_down):
    b, t, d = x_prompt.shape
    nb, ts, _ = x_sample.shape
    n_mem = mem_prompt.shape[1]
    assert d == D_MODEL and t % RET_CHUNK == 0 and nb % SAMPLE_BB == 0
    assert ts & (ts - 1) == 0 and HG_CHUNK % ts == 0 and RET_CHUNK % ts == 0
    n_p, n_s = b * t, nb * ts
    nt = n_p + n_s
    assert n_p % (SAMPLE_BB * ts) == 0

    lb_cum = jnp.cumsum(jax.nn.softmax(lb_logits.astype(F32), axis=0), axis=0)
    lbs = lb_cum - lb_cum[0:1]
    lbt = jnp.stack([jnp.log(lbs), jnp.log1p(-lbs), 1.0 - lbs] + [jnp.zeros_like(lbs)] * (SUBLANES - 3), axis=1)
    gl = jnp.broadcast_to(jnp.log1p(-jnp.exp2(-5.0 - jnp.arange(HEADS, dtype=F32)))[:, None], (HEADS, DH))
    cos_p, sin_p = _rope_tables(t, 0)
    cos_s, sin_s = _rope_tables(ts, PAST_LEN)
    cos_s, sin_s = jnp.tile(cos_s, (SAMPLE_BB, 1)), jnp.tile(sin_s, (SAMPLE_BB, 1))
    vec3 = lambda a: a.reshape(DEPTH, 1, -1)
    w_router_t = jnp.swapaxes(w_router, 1, 2)
    b_router3 = b_router.reshape(DEPTH, N_EXPERTS, 1)
    mem2 = mem_prompt.reshape(b * n_mem, d)

    x = (x_prompt.reshape(n_p, d), x_sample.reshape(n_s, d))
    xb = jnp.concatenate([x[0].astype(BF16), x[1].astype(BF16)], axis=0)
    tm_proj = _pick(nt, (1024, 512, 128))
    te = _pick(nt * TOP_K, EXPERT_TILES)
    outs = {k: [] for k in ("ret_p", "hg_p", "mk", "mv")}
    ret_s = hg_s = None
    for l in range(DEPTH):
        mix_tiles = COL_GATES * HW // PROJ_TILE_N
        proj = _matmul(xb, w_in, l, tm_proj, PROJ_TILE_N, n=COL_GATES * HW)
        gates = _matmul(xb, w_in, l, tm_proj, PROJ_TILE_N, first_tile=mix_tiles, n=3 * D_MODEL, gate=True)
        kv_p = _matmul(mem2, w_mem_kv, l, _pick(b * n_mem, (1024, 512, 256)), 2 * HW)
        yr, ret_p, ret_s = _retention(proj, state_ret, l, cos_p, sin_p, cos_s, sin_s, gl,
                                      vec3(ret_norm_g), b, t, nb, ts, ret_s)
        yh, hg_p, hg_s = _hgrn(proj, state_hgrn, l, lbt, vec3(hgrn_norm_g), b, t, nb, ts, hg_s)
        yx = _cross_attention(proj, kv_p, cache_mem_k, cache_mem_v, l, b, t, nb, ts)
        x1, x1b, x1t = _merge((*yr, *yh, *yx), gates, x, w_up_ret, w_up_hgrn, w_up_xattn, w_out,
                              vec3(ln1_g), vec3(ln1_b), l)
        eidx, wn = _router(x1, w_router_t, b_router3, l)
        pos, cnt, off = _positions(eidx)
        tbl, n_steps = _step_table(cnt, off, nt * TOP_K, te)
        xs, shared = _dispatch(pos, x1t.reshape(nt, PACKED_TILES, LANES), x1b, w_s_gate, w_s_up, w_s_down, l)
        ye = _experts(tbl, n_steps, te, xs.reshape(-1, LANES), w_e_gate, w_e_up, w_e_down, l)
        ye = ye.reshape(-1, ROW_TILES, LANES)
        x_p, x_s, xb = _combine(pos, wn, ye, x1, shared, vec3(ln2_g), vec3(ln2_b), l, n_p)
        x = (x_p, x_s)
        outs["ret_p"].append(ret_p)
        outs["hg_p"].append(hg_p)
        outs["mk"].append(kv_p[:, :HW].reshape(b, n_mem, HEADS, DH))
        outs["mv"].append(kv_p[:, HW:].reshape(b, n_mem, HEADS, DH))
    return (x[0].reshape(b, t, d), x[1].reshape(nb, ts, d),
            jnp.stack(outs["ret_p"]), jnp.stack(outs["hg_p"]), jnp.stack(outs["mk"]), jnp.stack(outs["mv"]),
            ret_s, hg_s)
```
